```python
import math
import jax
import jax.numpy as jnp
from jax import lax
import numpy as np

D_MODEL = 1024
BATCH = 16
SEQ = 256
DEPTH = 4
DEC_BATCH = 8
DEC_SEQ = 2048
PAST_LEN = 256

F32 = jnp.float32
GRID_W = 64
ROPE_THETA = 10000.0
NORM_EPS = 1e-6
NEG_BIG = -1e30
LOG_TINY = 1e-30
Q_BLOCK = 128
N_EVEN = (DEPTH + 1) // 2
N_ODD = DEPTH // 2

SWA_HEADS = 8
SWA_KV_HEADS = 2
SWA_GROUP = SWA_HEADS // SWA_KV_HEADS
SWA_HEAD_DIM = 64
SWA_WIDTH = SWA_HEADS * SWA_HEAD_DIM
SWA_WINDOW = 128
SWA_BLOCK = 128

HGRN_HEADS = 4
HGRN_KEY_DIM = 128
HGRN_VAL_DIM = 128
HGRN_WIDTH = HGRN_HEADS * HGRN_KEY_DIM
HGRN_CHUNK = 64

EVEN_SIZES = (SWA_WIDTH, SWA_KV_HEADS * SWA_HEAD_DIM, SWA_KV_HEADS * SWA_HEAD_DIM,
              HGRN_WIDTH, HGRN_WIDTH, HGRN_WIDTH, HGRN_WIDTH, HGRN_WIDTH)
EVEN_SPLIT_POINTS = tuple(sum(EVEN_SIZES[:i + 1]) for i in range(len(EVEN_SIZES) - 1))
EVEN_IN_WIDTH = sum(EVEN_SIZES)
EVEN_MIX_WIDTH = SWA_WIDTH + HGRN_WIDTH

MLA_HEADS = 16
MLA_Q_RANK = 384
MLA_KV_RANK = 256
MLA_NOPE_DIM = 64
MLA_ROPE_DIM = 32
MLA_V_DIM = 64
MLA_QK_DIM = MLA_NOPE_DIM + MLA_ROPE_DIM

MOE_GROUPS = 4
MOE_EXPERTS_PER_GROUP = 8
MOE_EXPERTS = MOE_GROUPS * MOE_EXPERTS_PER_GROUP
MOE_TOP_K = 2
MOE_HIDDEN = 256

kernel_name = 'hybrid_flow_prefix_step'


def rms_norm(x, w):
    xf = x.astype(F32)
    y = xf * lax.rsqrt(jnp.mean(xf * xf, axis=-1, keepdims=True) + NORM_EPS)
    return (y * w.astype(F32)).astype(x.dtype)


def adaln_input(x, w, shift, scale):
    return rms_norm(x, w) * (1 + scale) + shift


def rope_2d(x):
    T = x.shape[1]
    rows = T // GRID_W
    row = jnp.repeat(jnp.arange(rows, dtype=F32), GRID_W)
    col = (jnp.arange(T) % GRID_W).astype(F32)
    half = x.shape[-1] // 2
    nf = half // 2
    inv = jnp.exp(-math.log(ROPE_THETA) * jnp.arange(nf, dtype=F32) / nf)
    bshape = (1, T) + (1,) * (x.ndim - 3) + (nf,)

    def rot(xh, pos):
        ang = (pos[:, None] * inv[None, :]).reshape(bshape)
        cos, sin = jnp.cos(ang), jnp.sin(ang)
        x1 = xh[..., :nf].astype(F32)
        x2 = xh[..., nf:].astype(F32)
        return jnp.concatenate([x1 * cos - x2 * sin, x1 * sin + x2 * cos], axis=-1)

    return jnp.concatenate([rot(x[..., :half], row), rot(x[..., half:], col)], axis=-1).astype(x.dtype)


def dense_attention(q, k, v, scale, sink=None):
    B, T, KV, G, D = q.shape
    nb = T // Q_BLOCK
    qb = jnp.moveaxis(q.reshape(B, nb, Q_BLOCK, KV, G, D), 1, 0)

    def one(qblk):
        s = jnp.einsum('bqkgd,bskd->bkgqs', qblk, k).astype(F32) * scale
        if sink is not None:
            sk = jnp.broadcast_to(sink.astype(F32).reshape(1, KV, G, 1, 1), s.shape[:-1] + (1,))
            p = jax.nn.softmax(jnp.concatenate([sk, s], axis=-1), axis=-1)[..., 1:]
        else:
            p = jax.nn.softmax(s, axis=-1)
        return jnp.einsum('bkgqs,bske->bqkge', p.astype(v.dtype), v)

    o = lax.map(one, qb)
    return jnp.moveaxis(o, 0, 1).reshape(B, T, KV, G, v.shape[-1])


def swa_latent(q, k, v, k_ctx, v_ctx, sink, scale):
    B, T, KV, G, D = q.shape
    E = v.shape[-1]
    nb = T // SWA_BLOCK
    S = k_ctx.shape[1]
    qb = q.reshape(B, nb, SWA_BLOCK, KV, G, D)

    def band(a):
        pad = jnp.zeros((B, SWA_BLOCK) + a.shape[2:], a.dtype)
        ab = jnp.concatenate([pad, a, pad], axis=1).reshape((B, nb + 2, SWA_BLOCK) + a.shape[2:])
        return jnp.concatenate([ab[:, :-2], ab[:, 1:-1], ab[:, 2:]], axis=2)

    kw, vw = band(k), band(v)
    qi = jnp.arange(SWA_BLOCK)[:, None]
    r = jnp.arange(3 * SWA_BLOCK)[None, :]
    rel = qi + SWA_BLOCK - r
    kpos = (jnp.arange(nb)[:, None, None] - 1) * SWA_BLOCK + r[None]
    mask = (jnp.abs(rel)[None] <= SWA_WINDOW) & (kpos >= 0) & (kpos < T)
    s_loc = jnp.einsum('bnqkgd,bnskd->bkgnqs', qb, kw).astype(F32) * scale
    s_loc = jnp.where(mask, s_loc, NEG_BIG)
    s_ctx = jnp.einsum('bnqkgd,bskd->bkgnqs', qb, k_ctx).astype(F32) * scale
    sk = jnp.broadcast_to(sink.astype(F32).reshape(1, KV, G, 1, 1, 1), s_ctx.shape[:-1] + (1,))
    p = jax.nn.softmax(jnp.concatenate([sk, s_ctx, s_loc], axis=-1), axis=-1)
    p_ctx = p[..., 1:1 + S].astype(v.dtype)
    p_loc = p[..., 1 + S:].astype(v.dtype)
    o = (jnp.einsum('bkgnqs,bske->bnqkge', p_ctx, v_ctx)
         + jnp.einsum('bkgnqs,bnske->bnqkge', p_loc, vw))
    return o.reshape(B, T, KV, G, E)


def gla_chunk_scan(q, k, v, logf, s0):
    B, T, H, DK = q.shape
    DV = v.shape[-1]
    C = HGRN_CHUNK
    nc = T // C

    def chunks(a):
        return a.reshape(B, nc, C, H, a.shape[-1]).transpose(1, 0, 3, 2, 4)

    tri = jnp.tril(jnp.ones((C, C), dtype=bool))[:, :, None]

    def step(S, inp):
        qc, kc, vc, gc = inp
        b = jnp.cumsum(gc, axis=2)
        diff = b[:, :, :, None, :] - b[:, :, None, :, :]
        decay = jnp.exp(jnp.where(tri, diff, NEG_BIG))
        att = jnp.einsum('bhtd,bhsd,bhtsd->bhts', qc, kc, decay)
        o = att @ vc + jnp.einsum('bhtd,bhde->bhte', qc * jnp.exp(b), S)
        b_last = b[:, :, -1:, :]
        S = (jnp.exp(b_last[:, :, 0, :])[..., None] * S
             + jnp.einsum('bhsd,bhse->bhde', kc * jnp.exp(b_last - b), vc))
        return S, o

    S, o = lax.scan(step, s0.astype(F32), (chunks(q), chunks(k), chunks(v), chunks(logf)))
    return o.transpose(1, 0, 3, 2, 4).reshape(B, T, H, DV), S


def layer_lower_bounds(p):
    pr = jax.nn.softmax(p.astype(F32), axis=0)
    return jnp.cumsum(pr, axis=0) - pr[0:1]


def even_project(h, w_in, lb_f, lb_b):
    B, T, _ = h.shape
    qa, ka, va, qb, ff, fb, ib, gb = jnp.split(h @ w_in, EVEN_SPLIT_POINTS, axis=-1)
    qa = qa.reshape(B, T, SWA_KV_HEADS, SWA_GROUP, SWA_HEAD_DIM)
    ka = ka.reshape(B, T, SWA_KV_HEADS, SWA_HEAD_DIM)
    va = va.reshape(B, T, SWA_KV_HEADS, SWA_HEAD_DIM)
    hshape = (B, T, HGRN_HEADS, HGRN_KEY_DIM)
    qh = (jax.nn.silu(qb.astype(F32)) * HGRN_KEY_DIM ** -0.5).reshape(hshape)

    def log_forget(f, lb):
        lb = jnp.clip(lb, 0.0, 1.0 - 1e-6)
        return jnp.logaddexp(jnp.log(jnp.maximum(lb, LOG_TINY)),
                             jnp.log1p(-lb) + jax.nn.log_sigmoid(f.astype(F32))).reshape(hshape)

    lf_f = log_forget(ff, lb_f)
    lf_b = log_forget(fb, lb_b)
    iv = ib.astype(F32).reshape(B, T, HGRN_HEADS, HGRN_VAL_DIM)
    return qa, ka, va, qh, lf_f, lf_b, iv, gb


def hgrn_bidir(qh, lf_f, lf_b, iv, s0_f, s0_b):
    flip = lambda a: jnp.flip(a, axis=1)
    o_f, s_f = gla_chunk_scan(qh, -jnp.expm1(lf_f), iv, lf_f, s0_f)
    o_b, s_b = gla_chunk_scan(flip(qh), flip(-jnp.expm1(lf_b)), flip(iv), flip(lf_b), s0_b)
    return o_f + flip(o_b), s_f, s_b


def hgrn_readout(o, g, gnorm_w, dtype):
    B, T = o.shape[:2]
    gate = jax.nn.silu(g.astype(F32)).reshape(o.shape)
    return (rms_norm(o, gnorm_w) * gate).reshape(B, T, HGRN_WIDTH).astype(dtype)


def even_context(h, w_in, w_out, sink, lb_f, lb_b, gnorm_w):
    B, T, _ = h.shape
    qa, ka, va, qh, lf_f, lf_b, iv, gb = even_project(h, w_in, lb_f, lb_b)
    a = dense_attention(qa, ka, va, SWA_HEAD_DIM ** -0.5, sink).reshape(B, T, SWA_WIDTH)
    s0 = jnp.zeros((B, HGRN_HEADS, HGRN_KEY_DIM, HGRN_VAL_DIM), F32)
    o, s_f, s_b = hgrn_bidir(qh, lf_f, lf_b, iv, s0, s0)
    out = jnp.concatenate([a, hgrn_readout(o, gb, gnorm_w, h.dtype)], axis=-1) @ w_out
    return out, ka, va, jnp.stack([s_f, s_b], axis=1).astype(h.dtype)


def even_latent(h, k_ctx, v_ctx, s_ctx, w_in, w_out, sink, lb_f, lb_b, gnorm_w):
    B, T, _ = h.shape
    qa, ka, va, qh, lf_f, lf_b, iv, gb = even_project(h, w_in, lb_f, lb_b)
    a = swa_latent(rope_2d(qa), rope_2d(ka), va, k_ctx, v_ctx, sink,
                   SWA_HEAD_DIM ** -0.5).reshape(B, T, SWA_WIDTH)
    o, _, _ = hgrn_bidir(qh, lf_f, lf_b, iv, s_ctx[:, 0], s_ctx[:, 1])
    return jnp.concatenate([a, hgrn_readout(o, gb, gnorm_w, h.dtype)], axis=-1) @ w_out


def mla_queries(h, w_dq, qn_w, w_uq, rotate):
    B, T, _ = h.shape
    q = (rms_norm(h @ w_dq, qn_w) @ w_uq).reshape(B, T, MLA_HEADS, MLA_QK_DIM)
    if rotate:
        q = jnp.concatenate([q[..., :MLA_NOPE_DIM], rope_2d(q[..., MLA_NOPE_DIM:])], axis=-1)
    return q


def mla_compress_kv(h, w_dkv, kvn_w, rotate):
    kv = h @ w_dkv
    ckv = rms_norm(kv[..., :MLA_KV_RANK], kvn_w)
    kr = kv[..., MLA_KV_RANK:]
    if rotate:
        kr = rope_2d(kr[:, :, None, :])[:, :, 0, :]
    return ckv, kr


def mla_attend(q, ckv, kr, w_ukv, w_o):
    B, T, H, _ = q.shape
    S = ckv.shape[1]
    kvu = (ckv @ w_ukv).reshape(B, S, H, MLA_NOPE_DIM + MLA_V_DIM)
    k = jnp.concatenate([kvu[..., :MLA_NOPE_DIM],
                         jnp.broadcast_to(kr[:, :, None, :], (B, S, H, MLA_ROPE_DIM))], axis=-1)
    v = kvu[..., MLA_NOPE_DIM:]
    o = dense_attention(q[:, :, :, None, :], k, v, MLA_QK_DIM ** -0.5)
    return o.reshape(B, T, H * MLA_V_DIM) @ w_o


def hier_moe(h, wg, bg, we, be, w_gate, w_up, w_down):
    B, T, D = h.shape
    x = h.reshape(B * T, D)
    n = x.shape[0]
    g_prob = jax.nn.softmax((x @ wg).astype(F32) + bg.astype(F32), axis=-1)
    g_w, g_idx = lax.top_k(g_prob, 1)
    e_logits = jnp.einsum('nd,gde->nge', x, we).astype(F32) + be.astype(F32)
    e_sel = jnp.take_along_axis(e_logits, g_idx[:, :, None], axis=1)[:, 0]
    e_w, e_idx = lax.top_k(jax.nn.softmax(e_sel, axis=-1), MOE_TOP_K)
    e_w = e_w / jnp.sum(e_w, axis=-1, keepdims=True)
    in_group = jnp.sum(jax.nn.one_hot(e_idx, MOE_EXPERTS_PER_GROUP, dtype=F32) * e_w[..., None], axis=1)
    comb = (jax.nn.one_hot(g_idx[:, 0], MOE_GROUPS, dtype=F32)[:, :, None]
            * (g_w[:, :, None] * in_group[:, None, :]))
    wg4 = w_gate.reshape(MOE_GROUPS, MOE_EXPERTS_PER_GROUP, D, MOE_HIDDEN)
    wu4 = w_up.reshape(MOE_GROUPS, MOE_EXPERTS_PER_GROUP, D, MOE_HIDDEN)
    wd4 = w_down.reshape(MOE_GROUPS, MOE_EXPERTS_PER_GROUP, MOE_HIDDEN, D)
    y = jnp.zeros((n, D), F32)
    for g in range(MOE_GROUPS):
        a = jax.nn.silu(jnp.einsum('nd,edf->nef', x, wg4[g])) * jnp.einsum('nd,edf->nef', x, wu4[g])
        a = a * comb[:, g, :, None].astype(a.dtype)
        y = y + jnp.einsum('nef,efd->nd', a, wd4[g]).astype(F32)
    return y.astype(h.dtype).reshape(B, T, D)


def setup_inputs(seed: int = 0) -> dict:
    key = jax.random.key(seed)
    ks = iter(jax.random.split(key, 48))
    nrm = lambda shape, s: jax.random.normal(next(ks), shape, F32) * s
    D = D_MODEL
    E, F = MOE_EXPERTS, MOE_HIDDEN
    return {
        'x_prompt': nrm((BATCH, SEQ, D), 1.0),
        'x_sample': nrm((DEC_BATCH, DEC_SEQ, D), 1.0),
        'c': nrm((DEC_BATCH, D), 1.0),
        'cache_swa_k': nrm((DEC_BATCH, N_EVEN, PAST_LEN, SWA_KV_HEADS, SWA_HEAD_DIM), 1.0),
        'cache_swa_v': nrm((DEC_BATCH, N_EVEN, PAST_LEN, SWA_KV_HEADS, SWA_HEAD_DIM), 1.0),
        'state_hgrn': nrm((DEC_BATCH, N_EVEN, 2, HGRN_HEADS, HGRN_KEY_DIM, HGRN_VAL_DIM), 0.3),
        'cache_mla_ckv': nrm((DEC_BATCH, N_ODD, PAST_LEN, MLA_KV_RANK), 1.0),
        'cache_mla_krope': nrm((DEC_BATCH, N_ODD, PAST_LEN, MLA_ROPE_DIM), 1.0),
        'c_ctx': nrm((D,), 1.0),
        'mod_w': nrm((DEPTH, D, 6 * D), 0.5 * D ** -0.5),
        'mod_b': nrm((DEPTH, 6 * D), 0.02),
        'norm1_w': 1.0 + nrm((DEPTH, D), 0.02),
        'norm2_w': 1.0 + nrm((DEPTH, D), 0.02),
        'final_norm_w': 1.0 + nrm((D,), 0.02),
        'even_w_in': nrm((N_EVEN, D, EVEN_IN_WIDTH), D ** -0.5),
        'even_w_out': nrm((N_EVEN, EVEN_MIX_WIDTH, D), EVEN_MIX_WIDTH ** -0.5),
        'swa_sink': nrm((N_EVEN, SWA_HEADS), 0.5),
        'hgrn_lb_fwd': nrm((N_EVEN, HGRN_WIDTH), 1.0),
        'hgrn_lb_bwd': nrm((N_EVEN, HGRN_WIDTH), 1.0),
        'hgrn_gnorm_w': 1.0 + nrm((N_EVEN, HGRN_VAL_DIM), 0.02),
        'mla_w_dq': nrm((N_ODD, D, MLA_Q_RANK), D ** -0.5),
        'mla_qnorm_w': 1.0 + nrm((N_ODD, MLA_Q_RANK), 0.02),
        'mla_w_uq': nrm((N_ODD, MLA_Q_RANK, MLA_HEADS * MLA_QK_DIM), MLA_Q_RANK ** -0.5),
        'mla_w_dkv': nrm((N_ODD, D, MLA_KV_RANK + MLA_ROPE_DIM), D ** -0.5),
        'mla_kvnorm_w': 1.0 + nrm((N_ODD, MLA_KV_RANK), 0.02),
        'mla_w_ukv': nrm((N_ODD, MLA_KV_RANK, MLA_HEADS * (MLA_NOPE_DIM + MLA_V_DIM)), MLA_KV_RANK ** -0.5),
        'mla_w_o': nrm((N_ODD, MLA_HEADS * MLA_V_DIM, D), (MLA_HEADS * MLA_V_DIM) ** -0.5),
        'moe_router_group_w': nrm((DEPTH, D, MOE_GROUPS), D ** -0.5),
        'moe_router_group_b': nrm((DEPTH, MOE_GROUPS), 0.01),
        'moe_router_expert_w': nrm((DEPTH, MOE_GROUPS, D, MOE_EXPERTS_PER_GROUP), D ** -0.5),
        'moe_router_expert_b': nrm((DEPTH, MOE_GROUPS, MOE_EXPERTS_PER_GROUP), 0.01),
        'moe_w_gate': nrm((DEPTH, E, D, F), D ** -0.5),
        'moe_w_up': nrm((DEPTH, E, D, F), D ** -0.5),
        'moe_w_down': nrm((DEPTH, E, F, D), F ** -0.5),
    }


def reference(x_prompt, x_sample, c, cache_swa_k, cache_swa_v, state_hgrn, cache_mla_ckv,
              cache_mla_krope, c_ctx, mod_w, mod_b, norm1_w, norm2_w, final_norm_w,
              even_w_in, even_w_out, swa_sink, hgrn_lb_fwd, hgrn_lb_bwd, hgrn_gnorm_w,
              mla_w_dq, mla_qnorm_w, mla_w_uq, mla_w_dkv, mla_kvnorm_w, mla_w_ukv, mla_w_o,
              moe_router_group_w, moe_router_group_b, moe_router_expert_w, moe_router_expert_b,
              moe_w_gate, moe_w_up, moe_w_down):
    lb_fwd = layer_lower_bounds(hgrn_lb_fwd)
    lb_bwd = layer_lower_bounds(hgrn_lb_bwd)
    ctx, lat = x_prompt, x_sample
    new_k, new_v, new_s, new_ckv, new_kr = [], [], [], [], []
    for l in range(DEPTH):
        j = l // 2
        mc = jnp.split(jax.nn.silu(c_ctx) @ mod_w[l] + mod_b[l], 6, axis=-1)
        ms = [m[:, None, :] for m in jnp.split(jax.nn.silu(c) @ mod_w[l] + mod_b[l], 6, axis=-1)]
        hc = adaln_input(ctx, norm1_w[l], mc[0], mc[1])
        hs = adaln_input(lat, norm1_w[l], ms[0], ms[1])
        if l % 2 == 0:
            oc, kc, vc, sc = even_context(hc, even_w_in[j], even_w_out[j], swa_sink[j],
                                          lb_fwd[j], lb_bwd[j], hgrn_gnorm_w[j])
            os_ = even_latent(hs, cache_swa_k[:, j], cache_swa_v[:, j], state_hgrn[:, j],
                              even_w_in[j], even_w_out[j], swa_sink[j],
                              lb_fwd[j], lb_bwd[j], hgrn_gnorm_w[j])
            new_k.append(kc)
            new_v.append(vc)
            new_s.append(sc)
        else:
            qc = mla_queries(hc, mla_w_dq[j], mla_qnorm_w[j], mla_w_uq[j], False)
            ckv_c, kr_c = mla_compress_kv(hc, mla_w_dkv[j], mla_kvnorm_w[j], False)
            oc = mla_attend(qc, ckv_c, kr_c, mla_w_ukv[j], mla_w_o[j])
            qs = mla_queries(hs, mla_w_dq[j], mla_qnorm_w[j], mla_w_uq[j], True)
            ckv_s, kr_s = mla_compress_kv(hs, mla_w_dkv[j], mla_kvnorm_w[j], True)
            os_ = mla_attend(qs, jnp.concatenate([cache_mla_ckv[:, j], ckv_s], axis=1),
                             jnp.concatenate([cache_mla_krope[:, j], kr_s], axis=1),
                             mla_w_ukv[j], mla_w_o[j])
            new_ckv.append(ckv_c)
            new_kr.append(kr_c)
        ctx = ctx + mc[2] * oc
        lat = lat + ms[2] * os_
        hc = adaln_input(ctx, norm2_w[l], mc[3], mc[4])
        hs = adaln_input(lat, norm2_w[l], ms[3], ms[4])
        ctx = ctx + mc[5] * hier_moe(hc, moe_router_group_w[l], moe_router_group_b[l],
                                     moe_router_expert_w[l], moe_router_expert_b[l],
                                     moe_w_gate[l], moe_w_up[l], moe_w_down[l])
        lat = lat + ms[5] * hier_moe(hs, moe_router_group_w[l], moe_router_group_b[l],
                                     moe_router_expert_w[l], moe_router_expert_b[l],
                                     moe_w_gate[l], moe_w_up[l], moe_w_down[l])
    y_prompt = rms_norm(ctx, final_norm_w)
    y_sample = rms_norm(lat, final_norm_w)
    return (y_prompt, y_sample, jnp.stack(new_k, axis=1), jnp.stack(new_v, axis=1),
            jnp.stack(new_s, axis=1), jnp.stack(new_ckv, axis=1), jnp.stack(new_kr, axis=1))
```

```python
import functools
import math

import numpy as np
import jax
import jax.numpy as jnp
from jax import lax
from jax.experimental import pallas as pl
from jax.experimental.pallas import tpu as pltpu

F32, BF16, I32 = jnp.float32, jnp.bfloat16, jnp.int32

D_MODEL = 1024
DEPTH = 4
GRID_W = 64
ROPE_THETA = 10000.0
NORM_EPS = 1e-6
NEG_BIG = -1e30
LOG_TINY = 1e-30
N_EVEN = (DEPTH + 1) // 2
N_ODD = DEPTH // 2

SWA_HEADS = 8
SWA_KV_HEADS = 2
SWA_GROUP = SWA_HEADS // SWA_KV_HEADS
SWA_HEAD_DIM = 64
SWA_WIDTH = SWA_HEADS * SWA_HEAD_DIM
SWA_KV_WIDTH = SWA_KV_HEADS * SWA_HEAD_DIM
SWA_WINDOW = 128
SWA_BLOCK = 128

HGRN_HEADS = 4
HGRN_KEY_DIM = 128
HGRN_VAL_DIM = 128
HGRN_WIDTH = HGRN_HEADS * HGRN_KEY_DIM
HGRN_CHUNK = 128
HGRN_SUB = 8
HGRN_LEVELS = (8, 16, 32, 64)

EVEN_IN_WIDTH = SWA_WIDTH + 2 * SWA_KV_WIDTH + 5 * HGRN_WIDTH
LANE = 128
COL_Q, COL_K, COL_V = 0, SWA_WIDTH // LANE, (SWA_WIDTH + SWA_KV_WIDTH) // LANE
COL_HGRN = (SWA_WIDTH + 2 * SWA_KV_WIDTH) // LANE

MLA_HEADS = 16
MLA_Q_RANK = 384
MLA_KV_RANK = 256
MLA_NOPE_DIM = 64
MLA_ROPE_DIM = 32
MLA_V_DIM = 64
MLA_QK_DIM = MLA_NOPE_DIM + MLA_ROPE_DIM
MLA_DOWN_WIDTH = MLA_Q_RANK + MLA_KV_RANK + LANE

MOE_GROUPS = 4
MOE_EPG = 8
MOE_EXPERTS = MOE_GROUPS * MOE_EPG
MOE_HIDDEN = 256

TM = 256
TME = 256
VMEM_LIMIT = 48 * 1024 * 1024


def _params(*sem):
    return pltpu.CompilerParams(dimension_semantics=sem, vmem_limit_bytes=VMEM_LIMIT)


def _dot(a, b):
    return jnp.dot(a, b, preferred_element_type=F32)


def _dot_nt(a, b):
    return lax.dot_general(a, b, (((1,), (1,)), ((), ())), preferred_element_type=F32)


def _split2(a):
    hi = a.astype(BF16)
    return hi, (a - hi.astype(F32)).astype(BF16)


def _dot_f32ish(a, b):
    ah, al = _split2(a)
    bh, bl = _split2(b)
    return _dot(ah, bh) + (_dot(ah, bl) + _dot(al, bh))


def _silu(x):
    return x * jax.nn.sigmoid(x)


def _normmod(x, nw, shift, scale):
    ms = jnp.mean(x * x, axis=-1, keepdims=True)
    return (x * lax.rsqrt(ms + NORM_EPS) * nw) * (1.0 + scale) + shift


def _mod_kernel(c_ref, w_ref, b_ref, o_ref):
    o_ref[...] = _dot_f32ish(_silu(c_ref[...]), w_ref[...]) + b_ref[...]


def _modulation(cvec, mod_w, mod_b):
    rows = cvec.shape[0]
    nb = 6 * D_MODEL // 1024
    return pl.pallas_call(
        _mod_kernel,
        out_shape=jax.ShapeDtypeStruct((DEPTH, rows, 6 * D_MODEL), F32),
        grid=(DEPTH, nb),
        in_specs=[
            pl.BlockSpec((rows, D_MODEL), lambda l, n: (0, 0)),
            pl.BlockSpec((None, D_MODEL, 1024), lambda l, n: (l, 0, n)),
            pl.BlockSpec((None, 1, 1024), lambda l, n: (l, 0, n)),
        ],
        out_specs=pl.BlockSpec((None, rows, 1024), lambda l, n: (l, 0, n)),
        compiler_params=_params("parallel", "parallel"),
        name="modulation",
    )(cvec, mod_w, mod_b.reshape(DEPTH, 1, 6 * D_MODEL))


class _Rows:
    def __init__(self, nc_b, nc_t, nl_b, nl_t):
        self.nc_b, self.nc_t, self.nl_b, self.nl_t = nc_b, nc_t, nl_b, nl_t
        self.nc = nc_b * nc_t
        self.nl = nl_b * nl_t
        self.n = self.nc + self.nl
        assert self.nc % TM == 0 and nl_t % TM == 0 and self.nc % nl_t == 0
        self.ctx_tiles = self.nc // TM
        self.tiles_per_lat = nl_t // TM
        self.n_tiles = self.n // TM

    def mod_row(self, i):
        return jnp.where(i < self.ctx_tiles, 0, 1 + (i - self.ctx_tiles) // self.tiles_per_lat)

    def mod_spec(self):
        return pl.BlockSpec((None, 6, D_MODEL), lambda i: (self.mod_row(i), 0, 0))

    def pos_block(self, i):
        return jnp.where(i < self.ctx_tiles, 0, 1 + (i - self.ctx_tiles) % self.tiles_per_lat)


def _even_proj_kernel(x_ref, mod_ref, nw_ref, w_ref, o_ref):
    h = _normmod(x_ref[...], nw_ref[...], mod_ref[0:1, :], mod_ref[1:2, :]).astype(BF16)
    step = 256
    for c in range(EVEN_IN_WIDTH // step):
        o_ref[:, c * step:(c + 1) * step] = _dot(h, w_ref[:, c * step:(c + 1) * step])


def _even_proj(rows, x, mods, nw, w_in_bf16):
    return pl.pallas_call(
        _even_proj_kernel,
        out_shape=jax.ShapeDtypeStruct((rows.n, EVEN_IN_WIDTH), F32),
        grid=(rows.n_tiles,),
        in_specs=[
            pl.BlockSpec((TM, D_MODEL), lambda i: (i, 0)),
            rows.mod_spec(),
            pl.BlockSpec((1, D_MODEL), lambda i: (0, 0)),
            pl.BlockSpec((D_MODEL, EVEN_IN_WIDTH), lambda i: (0, 0)),
        ],
        out_specs=pl.BlockSpec((TM, EVEN_IN_WIDTH), lambda i: (i, 0)),
        compiler_params=_params("parallel"),
        name="even_proj",
    )(x, mods, nw.reshape(1, D_MODEL), w_in_bf16)


def _rope_tables(t_len, rot_dim, lane_lo, lane_hi, lead_rows):
    half = rot_dim // 2
    nf = half // 2
    lane = np.arange(LANE)
    d = (lane - lane_lo) % rot_dim
    active = (lane >= lane_lo) & (lane < lane_hi)
    use_col = d >= half
    fidx = d % nf
    first = (d % half) < nf
    pos = jnp.arange(t_len)
    row = (pos // GRID_W).astype(F32)
    col = (pos % GRID_W).astype(F32)
    inv = jnp.exp(-math.log(ROPE_THETA) * jnp.arange(nf, dtype=F32) / nf)
    p = jnp.where(jnp.asarray(use_col)[None, :], col[:, None], row[:, None])
    ang = p * inv[jnp.asarray(fidx)][None, :]
    act = jnp.asarray(active)[None, :]
    cos = jnp.where(act, jnp.cos(ang), 1.0)
    sin = jnp.where(act, jnp.sin(ang), 0.0)
    sin = jnp.where(jnp.asarray(first)[None, :], -sin, sin)
    if lead_rows:
        cos = jnp.concatenate([jnp.ones((lead_rows, LANE), F32), cos], axis=0)
        sin = jnp.concatenate([jnp.zeros((lead_rows, LANE), F32), sin], axis=0)
    return cos, sin


def _rope(x, cos, sin, nf):
    lane = lax.broadcasted_iota(I32, x.shape, 1)
    up = pltpu.roll(x, LANE - nf, axis=1)
    dn = pltpu.roll(x, nf, axis=1)
    partner = jnp.where((lane & nf) == 0, up, dn)
    return x * cos + partner * sin


def _swa_ctx_kernel(sink_ref, q_ref, k_ref, v_ref, o_ref):
    scale = SWA_HEAD_DIM ** -0.5
    k = k_ref[...]
    v = v_ref[...]
    for h in range(SWA_HEADS):
        kv = h // SWA_GROUP
        lo, klo = h * SWA_HEAD_DIM, kv * SWA_HEAD_DIM
        q = q_ref[:, lo:lo + SWA_HEAD_DIM].astype(BF16)
        kk = k[:, klo:klo + SWA_HEAD_DIM].astype(BF16)
        vv = v[:, klo:klo + SWA_HEAD_DIM].astype(BF16)
        s = _dot_nt(q, kk) * scale
        sk = sink_ref[h]
        m = jnp.maximum(jnp.max(s, axis=-1, keepdims=True), sk)
        p = jnp.exp(s - m)
        den = jnp.sum(p, axis=-1, keepdims=True) + jnp.exp(sk - m)
        o_ref[:, lo:lo + SWA_HEAD_DIM] = _dot(p.astype(BF16), vv) / den


def _swa_ctx(rows, proj, sink):
    t = rows.nc_t
    return pl.pallas_call(
        _swa_ctx_kernel,
        out_shape=jax.ShapeDtypeStruct((rows.nc, SWA_WIDTH), F32),
        grid=(rows.nc_b,),
        in_specs=[
            pl.BlockSpec(memory_space=pltpu.SMEM),
            pl.BlockSpec((t, SWA_WIDTH), lambda b: (b, COL_Q)),
            pl.BlockSpec((t, LANE), lambda b: (b, COL_K)),
            pl.BlockSpec((t, LANE), lambda b: (b, COL_V)),
        ],
        out_specs=pl.BlockSpec((t, SWA_WIDTH), lambda b: (b, 0)),
        compiler_params=_params("parallel"),
        name="swa_ctx",
    )(sink, proj, proj, proj)


def _swa_lat_kernel(sink_ref, q_ref, k_ref, v_ref, kc_ref, vc_ref, cos_ref, sin_ref, o_ref, *, n_blocks):
    scale = SWA_HEAD_DIM ** -0.5
    nf = SWA_HEAD_DIM // 4
    n = pl.program_id(1)
    q0 = pl.multiple_of(n * SWA_BLOCK, SWA_BLOCK)
    cq = cos_ref[pl.ds(q0, SWA_BLOCK), :]
    sq = sin_ref[pl.ds(q0, SWA_BLOCK), :]
    qs = [_rope(q_ref[:, g * LANE:(g + 1) * LANE], cq, sq, nf) for g in range(SWA_WIDTH // LANE)]

    qi = lax.broadcasted_iota(I32, (SWA_BLOCK, SWA_BLOCK), 0)
    r = lax.broadcasted_iota(I32, (SWA_BLOCK, SWA_BLOCK), 1)
    kb, vb, mb = [], [], []
    for off in (-1, 0, 1):
        blk = n + off
        valid = (blk >= 0) & (blk < n_blocks)
        st = pl.multiple_of(jnp.clip(blk, 0, n_blocks - 1) * SWA_BLOCK, SWA_BLOCK)
        sl = pl.ds(st, SWA_BLOCK)
        kb.append(_rope(k_ref[sl, :], cos_ref[sl, :], sin_ref[sl, :], nf))
        vb.append(v_ref[sl, :])
        rel = qi - r - off * SWA_BLOCK
        mb.append(jnp.where((jnp.abs(rel) <= SWA_WINDOW) & valid, 1.0, 0.0))
    kloc = jnp.concatenate(kb, axis=0)
    vloc = jnp.concatenate(vb, axis=0)
    mask = jnp.concatenate(mb, axis=1) > 0.5
    kctx = kc_ref[...]
    vctx = vc_ref[...]

    for h in range(SWA_HEADS):
        kv = h // SWA_GROUP
        lo, klo = h * SWA_HEAD_DIM, kv * SWA_HEAD_DIM
        qlo = lo % LANE
        q = qs[lo // LANE][:, qlo:qlo + SWA_HEAD_DIM].astype(BF16)
        s_ctx = _dot_nt(q, kctx[:, klo:klo + SWA_HEAD_DIM].astype(BF16)) * scale
        s_loc = _dot_nt(q, kloc[:, klo:klo + SWA_HEAD_DIM].astype(BF16)) * scale
        s_loc = jnp.where(mask, s_loc, NEG_BIG)
        sk = sink_ref[h]
        m = jnp.maximum(jnp.maximum(jnp.max(s_ctx, axis=-1, keepdims=True),
                                    jnp.max(s_loc, axis=-1, keepdims=True)), sk)
        p_ctx = jnp.exp(s_ctx - m)
        p_loc = jnp.exp(s_loc - m)
        den = (jnp.sum(p_ctx, axis=-1, keepdims=True) + jnp.sum(p_loc, axis=-1, keepdims=True)
               + jnp.exp(sk - m))
        o = (_dot(p_ctx.astype(BF16), vctx[:, klo:klo + SWA_HEAD_DIM].astype(BF16))
             + _dot(p_loc.astype(BF16), vloc[:, klo:klo + SWA_HEAD_DIM].astype(BF16)))
        o_ref[:, lo:lo + SWA_HEAD_DIM] = o / den


def _swa_lat(rows, proj, sink, k_ctx, v_ctx, cos, sin):
    t = rows.nl_t
    n_blocks = t // SWA_BLOCK
    q_base = rows.nc // SWA_BLOCK
    kv_base = rows.nc // t
    s_ctx = k_ctx.shape[1]
    return pl.pallas_call(
        functools.partial(_swa_lat_kernel, n_blocks=n_blocks),
        out_shape=jax.ShapeDtypeStruct((rows.nl, SWA_WIDTH), F32),
        grid=(rows.nl_b, n_blocks),
        in_specs=[
            pl.BlockSpec(memory_space=pltpu.SMEM),
            pl.BlockSpec((SWA_BLOCK, SWA_WIDTH), lambda b, n: (q_base + b * n_blocks + n, COL_Q)),
            pl.BlockSpec((t, LANE), lambda b, n: (kv_base + b, COL_K)),
            pl.BlockSpec((t, LANE), lambda b, n: (kv_base + b, COL_V)),
            pl.BlockSpec((None, s_ctx, LANE), lambda b, n: (b, 0, 0)),
            pl.BlockSpec((None, s_ctx, LANE), lambda b, n: (b, 0, 0)),
            pl.BlockSpec((t, LANE), lambda b, n: (0, 0)),
            pl.BlockSpec((t, LANE), lambda b, n: (0, 0)),
        ],
        out_specs=pl.BlockSpec((SWA_BLOCK, SWA_WIDTH), lambda b, n: (b * n_blocks + n, 0)),
        compiler_params=_params("parallel", "arbitrary"),
        name="swa_lat",
    )(sink, proj, proj, proj, k_ctx, v_ctx, cos, sin)


def _hgrn_consts():
    c = HGRN_CHUNK
    t = np.arange(c)[:, None]
    u = np.arange(c)[None, :]
    mats = [((t // HGRN_SUB) == (u // HGRN_SUB)) & (u <= t)]
    masks = []
    for m in HGRN_LEVELS:
        same = (t // m) == (u // m)
        right = ((t // m) % 2) == 1
        mats.append(same & np.where(right, u <= t, u > t))
        masks.append(right & ((u // m) == (t // m) - 1))
    mats.append(u <= t)
    mats.append(u > t)
    e_f = np.stack(mats).astype(np.float32)
    m_f = np.stack(masks).astype(np.float32)
    e_b = e_f[:, ::-1, ::-1]
    m_b = m_f[:, ::-1, ::-1]
    n_e = e_f.shape[0]
    sel = np.kron(np.eye(c), np.ones((1, HGRN_SUB))).astype(np.float32)
    return (jnp.asarray(e_f.reshape(n_e * c, c), BF16), jnp.asarray(m_f, F32),
            jnp.asarray(e_b.reshape(n_e * c, c), BF16), jnp.asarray(m_b, F32),
            jnp.ones((HGRN_KEY_DIM, LANE), BF16), jnp.asarray(sel, BF16))


def _hgrn_chunk(qb, fr, v, la, l1, e_ref, m_ref, ones_b, sel_b, st, forward):
    c = HGRN_CHUNK
    nsub = c // HGRN_SUB
    q = _silu(qb) * (HGRN_KEY_DIM ** -0.5)
    ls = jnp.minimum(fr, 0.0) - jnp.log1p(jnp.exp(-jnp.abs(fr)))
    y = l1 + ls
    lf = jnp.maximum(la, y) + jnp.log1p(jnp.exp(-jnp.abs(la - y)))
    k = 1.0 - jnp.exp(lf)

    hi = lf.astype(BF16)
    r1 = lf - hi.astype(F32)
    mid = r1.astype(BF16)
    lo = (r1 - mid.astype(F32)).astype(BF16)
    e = e_ref[...]
    x = _dot(e, hi) + (_dot(e, mid) + _dot(e, lo))
    x_diag = x[0:c]
    x_state_q = x[(1 + len(HGRN_LEVELS)) * c:(2 + len(HGRN_LEVELS)) * c]
    x_state_k = x[(2 + len(HGRN_LEVELS)) * c:(3 + len(HGRN_LEVELS)) * c]

    b3 = x_diag.reshape(nsub, HGRN_SUB, HGRN_KEY_DIM)
    q3 = q.reshape(nsub, HGRN_SUB, HGRN_KEY_DIM)
    k3 = k.reshape(nsub, HGRN_SUB, HGRN_KEY_DIM)
    v3 = v.reshape(nsub, HGRN_SUB, HGRN_VAL_DIM)
    s_io = lax.broadcasted_iota(I32, (1, HGRN_SUB, HGRN_KEY_DIM), 1)
    ps = []
    for i in range(HGRN_SUB):
        keep = (s_io <= i) if forward else (s_io >= i)
        dec = jnp.exp(jnp.where(keep, b3[:, i:i + 1, :] - b3, NEG_BIG))
        ps.append((q3[:, i:i + 1, :] * dec) * k3)
    p = jnp.stack(ps, axis=1).reshape(nsub * HGRN_SUB * HGRN_SUB, HGRN_KEY_DIM)
    att = _dot(p.astype(BF16), ones_b)
    av = att.reshape(nsub, HGRN_SUB, HGRN_SUB, HGRN_VAL_DIM) * v3[:, None, :, :]
    o = _dot(sel_b, av.reshape(nsub * HGRN_SUB * HGRN_SUB, HGRN_VAL_DIM).astype(BF16))

    a = jnp.zeros((c, c), F32)
    for li in range(len(HGRN_LEVELS)):
        fac = jnp.exp(x[(1 + li) * c:(2 + li) * c])
        a = a + m_ref[li] * _dot_nt((q * fac).astype(BF16), (k * fac).astype(BF16))
    vb = v.astype(BF16)
    o = o + _dot(a.astype(BF16), vb)

    o = o + _dot_nt((q * jnp.exp(x_state_q)).astype(BF16), st.astype(BF16))
    g = jnp.exp(jnp.sum(lf, axis=0, keepdims=True))
    kc = (k * jnp.exp(x_state_k)).astype(BF16)
    st_new = st * g + _dot(v.T.astype(BF16), kc)
    return o, st_new


def _hgrn_kernel(*refs, n_chunks, has_s0, emit_state):
    (qb_ref, ff_ref, fb_ref, ib_ref, gb_ref, lbp_ref, gw_ref,
     ef_ref, mf_ref, eb_ref, mb_ref, ones_ref, sel_ref) = refs[:13]
    rest = list(refs[13:])
    s0_ref = rest.pop(0) if has_s0 else None
    r_ref = rest.pop(0)
    sout_ref = rest.pop(0) if emit_state else None
    of_scr, st_scr = rest
    c = HGRN_CHUNK
    ones_b = ones_ref[...]
    sel_b = sel_ref[...]
    gw = gw_ref[...]

    def init_state(d):
        if has_s0:
            st_scr[...] = s0_ref[d].T
        else:
            st_scr[...] = jnp.zeros((HGRN_VAL_DIM, HGRN_KEY_DIM), F32)

    init_state(0)

    def fwd_body(ci, carry):
        sl = pl.ds(pl.multiple_of(ci * c, c), c)
        o, st = _hgrn_chunk(qb_ref[sl, :], ff_ref[sl, :], ib_ref[sl, :], lbp_ref[0:1, :], lbp_ref[1:2, :],
                            ef_ref, mf_ref, ones_b, sel_b, st_scr[...], True)
        of_scr[sl, :] = o
        st_scr[...] = st
        return carry

    lax.fori_loop(0, n_chunks, fwd_body, 0)
    if emit_state:
        sout_ref[0] = st_scr[...].T
    init_state(1)

    def bwd_body(i, carry):
        ci = n_chunks - 1 - i
        sl = pl.ds(pl.multiple_of(ci * c, c), c)
        o, st = _hgrn_chunk(qb_ref[sl, :], fb_ref[sl, :], ib_ref[sl, :], lbp_ref[2:3, :], lbp_ref[3:4, :],
                            eb_ref, mb_ref, ones_b, sel_b, st_scr[...], False)
        st_scr[...] = st
        tot = of_scr[sl, :] + o
        ms = jnp.mean(tot * tot, axis=-1, keepdims=True)
        r_ref[sl, :] = (tot * lax.rsqrt(ms + NORM_EPS) * gw) * _silu(gb_ref[sl, :])
        return carry

    lax.fori_loop(0, n_chunks, bwd_body, 0)
    if emit_state:
        sout_ref[1] = st_scr[...].T


def _hgrn(proj, lbp, gw, consts, n_b, t_len, row_block0, n_rows, s0=None, s0_layer=0, emit_state=False):
    n_chunks = t_len // HGRN_CHUNK
    e_f, m_f, e_b, m_b, ones_b, sel_b = consts

    def col(off):
        return pl.BlockSpec((t_len, LANE), lambda b, h: (row_block0 + b, COL_HGRN + off * HGRN_HEADS + h))

    def whole(a):
        nd = a.ndim
        return pl.BlockSpec(a.shape, lambda b, h: (0,) * nd)

    in_specs = [col(0), col(1), col(2), col(3), col(4),
                pl.BlockSpec((None, 4, LANE), lambda b, h: (h, 0, 0)),
                pl.BlockSpec((1, HGRN_VAL_DIM), lambda b, h: (0, 0)),
                whole(e_f), whole(m_f), whole(e_b), whole(m_b), whole(ones_b), whole(sel_b)]
    args = [proj, proj, proj, proj, proj, lbp, gw.reshape(1, HGRN_VAL_DIM), e_f, m_f, e_b, m_b, ones_b, sel_b]
    if s0 is not None:
        in_specs.append(pl.BlockSpec((None, None, 2, None, HGRN_KEY_DIM, HGRN_VAL_DIM),
                                     lambda b, h: (b, s0_layer, 0, h, 0, 0)))
        args.append(s0)
    out_shape = [jax.ShapeDtypeStruct((n_rows, HGRN_WIDTH), F32)]
    out_specs = [pl.BlockSpec((t_len, LANE), lambda b, h: (b, h))]
    if emit_state:
        out_shape.append(jax.ShapeDtypeStruct((n_b, 2, HGRN_HEADS, HGRN_KEY_DIM, HGRN_VAL_DIM), F32))
        out_specs.append(pl.BlockSpec((None, 2, None, HGRN_KEY_DIM, HGRN_VAL_DIM),
                                      lambda b, h: (b, 0, h, 0, 0)))
    return pl.pallas_call(
        functools.partial(_hgrn_kernel, n_chunks=n_chunks, has_s0=s0 is not None, emit_state=emit_state),
        out_shape=out_shape,
        grid=(n_b, HGRN_HEADS),
        in_specs=in_specs,
        out_specs=out_specs,
        scratch_shapes=[pltpu.VMEM((t_len, HGRN_VAL_DIM), F32), pltpu.VMEM((HGRN_VAL_DIM, HGRN_KEY_DIM), F32)],
        compiler_params=_params("parallel", "parallel"),
        name="hgrn_lat" if s0 is not None else "hgrn_ctx",
    )(*args)


def _outproj_kernel(*refs, n_parts):
    a_refs = refs[:n_parts]
    w_refs = refs[n_parts:2 * n_parts]
    x_ref, mod_ref, o_ref = refs[2 * n_parts:]
    acc = _dot(a_refs[0][...].astype(BF16), w_refs[0][...])
    for a_ref, w_ref in zip(a_refs[1:], w_refs[1:]):
        acc = acc + _dot(a_ref[...].astype(BF16), w_ref[...])
    o_ref[...] = x_ref[...] + mod_ref[2:3, :] * acc


def _outproj(rows, parts, weights, x, mods):
    n_parts = len(parts)
    in_specs = [pl.BlockSpec((TM, p.shape[1]), lambda i: (i, 0)) for p in parts]
    in_specs += [pl.BlockSpec(w.shape, lambda i: (0, 0)) for w in weights]
    in_specs += [pl.BlockSpec((TM, D_MODEL), lambda i: (i, 0)), rows.mod_spec()]
    return pl.pallas_call(
        functools.partial(_outproj_kernel, n_parts=n_parts),
        out_shape=jax.ShapeDtypeStruct((rows.n, D_MODEL), F32),
        grid=(rows.n_tiles,),
        in_specs=in_specs,
        out_specs=pl.BlockSpec((TM, D_MODEL), lambda i: (i, 0)),
        compiler_params=_params("parallel"),
        name="outproj",
    )(*parts, *weights, x, mods)


def _mla_proj_kernel(x_ref, mod_ref, nw_ref, wd_ref, qnw_ref, kvnw_ref, wuq_ref, cos_ref, sin_ref,
                     q_ref, ckv_ref, krb_ref):
    nf = MLA_ROPE_DIM // 4
    h = _normmod(x_ref[...], nw_ref[...], mod_ref[0:1, :], mod_ref[1:2, :]).astype(BF16)
    t1 = _dot(h, wd_ref[...])
    qd = t1[:, :MLA_Q_RANK]
    kvd = t1[:, MLA_Q_RANK:MLA_Q_RANK + MLA_KV_RANK]
    cos = cos_ref[...]
    sin = sin_ref[...]
    qn = qd * lax.rsqrt(jnp.mean(qd * qd, axis=-1, keepdims=True) + NORM_EPS) * qnw_ref[...]
    ckv_ref[...] = kvd * lax.rsqrt(jnp.mean(kvd * kvd, axis=-1, keepdims=True) + NORM_EPS) * kvnw_ref[...]
    krb_ref[...] = _rope(t1[:, MLA_Q_RANK + MLA_KV_RANK:], cos, sin, nf)
    qb = qn.astype(BF16)
    scale = MLA_QK_DIM ** -0.5
    for hd in range(MLA_HEADS):
        qh = _dot(qb, wuq_ref[:, hd * LANE:(hd + 1) * LANE])
        q_ref[:, hd * LANE:(hd + 1) * LANE] = (_rope(qh, cos, sin, nf) * scale).astype(BF16)


def _mla_proj(rows, x, mods, nw, wd, qnw, kvnw, wuq, cos, sin):
    return pl.pallas_call(
        _mla_proj_kernel,
        out_shape=[jax.ShapeDtypeStruct((rows.n, MLA_HEADS * LANE), BF16),
                   jax.ShapeDtypeStruct((rows.n, MLA_KV_RANK), F32),
                   jax.ShapeDtypeStruct((rows.n, LANE), F32)],
        grid=(rows.n_tiles,),
        in_specs=[
            pl.BlockSpec((TM, D_MODEL), lambda i: (i, 0)),
            rows.mod_spec(),
            pl.BlockSpec((1, D_MODEL), lambda i: (0, 0)),
            pl.BlockSpec((D_MODEL, MLA_DOWN_WIDTH), lambda i: (0, 0)),
            pl.BlockSpec((1, MLA_Q_RANK), lambda i: (0, 0)),
            pl.BlockSpec((1, MLA_KV_RANK), lambda i: (0, 0)),
            pl.BlockSpec((MLA_Q_RANK, MLA_HEADS * LANE), lambda i: (0, 0)),
            pl.BlockSpec((TM, LANE), lambda i: (rows.pos_block(i), 0)),
            pl.BlockSpec((TM, LANE), lambda i: (rows.pos_block(i), 0)),
        ],
        out_specs=[pl.BlockSpec((TM, MLA_HEADS * LANE), lambda i: (i, 0)),
                   pl.BlockSpec((TM, MLA_KV_RANK), lambda i: (i, 0)),
                   pl.BlockSpec((TM, LANE), lambda i: (i, 0))],
        compiler_params=_params("parallel"),
        name="mla_proj",
    )(x, mods, nw.reshape(1, D_MODEL), wd, qnw.reshape(1, MLA_Q_RANK), kvnw.reshape(1, MLA_KV_RANK), wuq, cos, sin)


MLA_TQ = 256


def _mla_attn_kernel(q_ref, ckv_ref, krb_ref, wk_ref, wv_ref, o_ref, k_scr, v_scr, *, t_len):
    ckv = ckv_ref[...].astype(BF16)
    krb = krb_ref[...]
    for hh in range(2):
        k_scr[...] = (_dot(ckv, wk_ref[hh]) + krb).astype(BF16)
        v_scr[...] = _dot(ckv, wv_ref[hh]).astype(BF16)

        def body(ti, carry, hh=hh):
            sl = pl.ds(pl.multiple_of(ti * MLA_TQ, MLA_TQ), MLA_TQ)
            s = _dot_nt(q_ref[sl, hh * LANE:(hh + 1) * LANE], k_scr[...])
            m = jnp.max(s, axis=-1, keepdims=True)
            p = jnp.exp(s - m)
            den = jnp.sum(p, axis=-1, keepdims=True)
            o = _dot(p.astype(BF16), v_scr[...]) / den
            if hh == 0:
                o_ref[sl, :] = o
            else:
                o_ref[sl, :] = o_ref[sl, :] + o
            return carry

        lax.fori_loop(0, t_len // MLA_TQ, body, 0)


def _mla_attn(q, ckv_all, krb_all, wk, wv, n_b, t_len, q_row_block0, n_rows):
    s_len = ckv_all.shape[1]
    return pl.pallas_call(
        functools.partial(_mla_attn_kernel, t_len=t_len),
        out_shape=jax.ShapeDtypeStruct((n_rows, MLA_HEADS * MLA_V_DIM), F32),
        grid=(n_b, MLA_HEADS // 2),
        in_specs=[
            pl.BlockSpec((t_len, 2 * LANE), lambda b, hp: (q_row_block0 + b, hp)),
            pl.BlockSpec((None, s_len, MLA_KV_RANK), lambda b, hp: (b, 0, 0)),
            pl.BlockSpec((None, s_len, LANE), lambda b, hp: (b, 0, 0)),
            pl.BlockSpec((2, MLA_KV_RANK, LANE), lambda b, hp: (hp, 0, 0)),
            pl.BlockSpec((2, MLA_KV_RANK, LANE), lambda b, hp: (hp, 0, 0)),
        ],
        out_specs=pl.BlockSpec((t_len, LANE), lambda b, hp: (b, hp)),
        scratch_shapes=[pltpu.VMEM((s_len, LANE), BF16), pltpu.VMEM((s_len, LANE), BF16)],
        compiler_params=_params("parallel", "arbitrary"),
        name="mla_attn",
    )(q, ckv_all, krb_all, wk, wv)


META_E1, META_E2, META_W1, META_W2, META_R1, META_R2 = range(6)


def _router_kernel(x_ref, mod_ref, nw_ref, wr_ref, br_ref, tri_ref, xn_ref, meta_ref, cnt_ref, base_scr):
    @pl.when(pl.program_id(0) == 0)
    def _():
        base_scr[...] = jnp.zeros(base_scr.shape, F32)

    xn = _normmod(x_ref[...], nw_ref[...], mod_ref[3:4, :], mod_ref[4:5, :])
    xn_ref[...] = xn
    logits = _dot_f32ish(xn, wr_ref[...]) + br_ref[...]
    lane = lax.broadcasted_iota(I32, logits.shape, 1).astype(F32)
    far = float(LANE)

    def first_argmax(vals, vmax):
        return jnp.min(jnp.where(vals == vmax, lane, far), axis=-1, keepdims=True)

    gl = jnp.where(lane < MOE_GROUPS, logits, NEG_BIG)
    gmax = jnp.max(gl, axis=-1, keepdims=True)
    g_w = 1.0 / jnp.sum(jnp.exp(gl - gmax), axis=-1, keepdims=True)
    g_idx = first_argmax(gl, gmax)
    e_lo = MOE_GROUPS + MOE_EPG * g_idx
    el = jnp.where((lane >= e_lo) & (lane < e_lo + MOE_EPG), logits, NEG_BIG)
    m1 = jnp.max(el, axis=-1, keepdims=True)
    i1 = first_argmax(el, m1)
    el2 = jnp.where(lane == i1, NEG_BIG, el)
    m2 = jnp.max(el2, axis=-1, keepdims=True)
    i2 = first_argmax(el2, m2)
    esum = jnp.sum(jnp.exp(el - m1), axis=-1, keepdims=True)
    p1 = 1.0 / esum
    p2 = jnp.exp(m2 - m1) / esum
    w1 = g_w * (p1 / (p1 + p2))
    w2 = g_w * (p2 / (p1 + p2))
    e1 = i1 - MOE_GROUPS
    e2 = i2 - MOE_GROUPS

    oh1 = lane == e1
    oh2 = lane == e2
    oh = jnp.where(oh1 | oh2, 1.0, 0.0)
    before = _dot(tri_ref[...], oh.astype(BF16)) + base_scr[0:1, :]
    r1 = jnp.sum(jnp.where(oh1, before, 0.0), axis=-1, keepdims=True)
    r2 = jnp.sum(jnp.where(oh2, before, 0.0), axis=-1, keepdims=True)
    base_scr[...] = base_scr[...] + jnp.sum(oh, axis=0, keepdims=True)
    cnt_ref[...] = base_scr[...]

    meta = jnp.zeros(logits.shape, F32)
    for slot, val in ((META_E1, e1), (META_E2, e2), (META_W1, w1), (META_W2, w2), (META_R1, r1), (META_R2, r2)):
        meta = jnp.where(lane == slot, val, meta)
    meta_ref[...] = meta


def _router(rows, x, mods, nw, wr, br, tri):
    return pl.pallas_call(
        _router_kernel,
        out_shape=[jax.ShapeDtypeStruct((rows.n, D_MODEL), F32),
                   jax.ShapeDtypeStruct((rows.n, LANE), F32),
                   jax.ShapeDtypeStruct((8, LANE), F32)],
        grid=(rows.n_tiles,),
        in_specs=[
            pl.BlockSpec((TM, D_MODEL), lambda i: (i, 0)),
            rows.mod_spec(),
            pl.BlockSpec((1, D_MODEL), lambda i: (0, 0)),
            pl.BlockSpec((D_MODEL, LANE), lambda i: (0, 0)),
            pl.BlockSpec((1, LANE), lambda i: (0, 0)),
            pl.BlockSpec((TM, TM), lambda i: (0, 0)),
        ],
        out_specs=[pl.BlockSpec((TM, D_MODEL), lambda i: (i, 0)),
                   pl.BlockSpec((TM, LANE), lambda i: (i, 0)),
                   pl.BlockSpec((8, LANE), lambda i: (0, 0))],
        scratch_shapes=[pltpu.VMEM((8, LANE), F32)],
        compiler_params=_params("arbitrary"),
        name="moe_router",
    )(x, mods, nw.reshape(1, D_MODEL), wr, br, tri)


def _row_copy(src, src_row, dst, dst_row, sem):
    return pltpu.make_async_copy(src.at[pl.ds(src_row, 1), :], dst.at[pl.ds(dst_row, 1), :], sem)


def _dispatch_kernel(fill_ref, pos_ref, xn_ref, xs_hbm, zero_scr, sem, fill_sem, *, n_tiles):
    @pl.when(pl.program_id(0) == 0)
    def _():
        zero_scr[...] = jnp.zeros(zero_scr.shape, F32)

        def fill_copy(t):
            return pltpu.make_async_copy(zero_scr, xs_hbm.at[pl.ds(pl.multiple_of(t * TME, TME), TME), :], fill_sem)

        def fill_start(t, carry):
            @pl.when(fill_ref[t] != 0)
            def _():
                fill_copy(t).start()
            return carry

        def fill_wait(t, carry):
            @pl.when(fill_ref[t] != 0)
            def _():
                fill_copy(t).wait()
            return carry

        lax.fori_loop(0, n_tiles, fill_start, 0)
        lax.fori_loop(0, n_tiles, fill_wait, 0)

    def start(r, carry):
        for k in range(2):
            _row_copy(xn_ref, r, xs_hbm, pos_ref[0, 2 * r + k], sem).start()
        return carry

    def wait(r, carry):
        for k in range(2):
            _row_copy(xn_ref, r, xs_hbm, pos_ref[0, 2 * r + k], sem).wait()
        return carry

    lax.fori_loop(0, TM, start, 0)
    lax.fori_loop(0, TM, wait, 0)


def _dispatch(rows, tile_fill, pos, xn, n_tiles):
    return pl.pallas_call(
        functools.partial(_dispatch_kernel, n_tiles=n_tiles),
        out_shape=jax.ShapeDtypeStruct((n_tiles * TME, D_MODEL), F32),
        grid_spec=pltpu.PrefetchScalarGridSpec(
            num_scalar_prefetch=1,
            grid=(rows.n_tiles,),
            in_specs=[
                pl.BlockSpec((None, 1, 2 * TM), lambda i, fill: (i, 0, 0), memory_space=pltpu.SMEM),
                pl.BlockSpec((TM, D_MODEL), lambda i, fill: (i, 0)),
            ],
            out_specs=pl.BlockSpec(memory_space=pl.ANY),
            scratch_shapes=[pltpu.VMEM((TME, D_MODEL), F32), pltpu.SemaphoreType.DMA(()),
                            pltpu.SemaphoreType.DMA(())],
        ),
        compiler_params=_params("arbitrary"),
        name="moe_dispatch",
    )(tile_fill, pos, xn)


def _ffn_kernel(te_ref, nv_ref, x_ref, wg_ref, wu_ref, wd_ref, y_ref, wg_b, wu_b, wd_b):
    t = pl.program_id(0)
    valid = t < nv_ref[0]
    new_expert = (t == 0) | (te_ref[t] != te_ref[jnp.maximum(t - 1, 0)])

    @pl.when(valid & new_expert)
    def _():
        wg_b[...] = wg_ref[...].astype(BF16)
        wu_b[...] = wu_ref[...].astype(BF16)
        wd_b[...] = wd_ref[...].astype(BF16)

    @pl.when(valid)
    def _():
        x = x_ref[...].astype(BF16)
        a = _silu(_dot(x, wg_b[...])) * _dot(x, wu_b[...])
        y_ref[...] = _dot(a.astype(BF16), wd_b[...])

    @pl.when(jnp.logical_not(valid))
    def _():
        y_ref[...] = jnp.zeros(y_ref.shape, F32)


def _ffn(tile_expert, n_valid, xs, w_gate, w_up, w_down, n_tiles):
    def xmap(t, te, nv):
        return (jnp.minimum(t, nv[0] - 1), 0)

    def wmap(t, te, nv):
        return (te[t], 0, 0)

    return pl.pallas_call(
        _ffn_kernel,
        out_shape=jax.ShapeDtypeStruct((n_tiles * TME, D_MODEL), F32),
        grid_spec=pltpu.PrefetchScalarGridSpec(
            num_scalar_prefetch=2,
            grid=(n_tiles,),
            in_specs=[
                pl.BlockSpec((TME, D_MODEL), xmap),
                pl.BlockSpec((None, D_MODEL, MOE_HIDDEN), wmap),
                pl.BlockSpec((None, D_MODEL, MOE_HIDDEN), wmap),
                pl.BlockSpec((None, MOE_HIDDEN, D_MODEL), wmap),
            ],
            out_specs=pl.BlockSpec((TME, D_MODEL), lambda t, te, nv: (t, 0)),
            scratch_shapes=[pltpu.VMEM((D_MODEL, MOE_HIDDEN), BF16),
                            pltpu.VMEM((D_MODEL, MOE_HIDDEN), BF16),
                            pltpu.VMEM((MOE_HIDDEN, D_MODEL), BF16)],
        ),
        compiler_params=_params("arbitrary"),
        name="moe_ffn",
    )(tile_expert, n_valid, xs, w_gate, w_up, w_down)


def _combine_kernel(pos_ref, x_ref, meta_ref, mod_ref, fnw_ref, ys_hbm, o_ref, buf0, buf1, sem, *, final):
    bufs = (buf0, buf1)

    def start(r, carry):
        for k in range(2):
            _row_copy(ys_hbm, pos_ref[0, 2 * r + k], bufs[k], r, sem).start()
        return carry

    def wait(r, carry):
        for k in range(2):
            _row_copy(ys_hbm, pos_ref[0, 2 * r + k], bufs[k], r, sem).wait()
        return carry

    lax.fori_loop(0, TM, start, 0)
    lax.fori_loop(0, TM, wait, 0)
    meta = meta_ref[...]
    y = meta[:, META_W1:META_W1 + 1] * buf0[...] + meta[:, META_W2:META_W2 + 1] * buf1[...]
    xo = x_ref[...] + mod_ref[5:6, :] * y
    if final:
        xo = xo * lax.rsqrt(jnp.mean(xo * xo, axis=-1, keepdims=True) + NORM_EPS) * fnw_ref[...]
    o_ref[...] = xo


def _combine(rows, pos, x, meta, mods, fnw, ys, final):
    return pl.pallas_call(
        functools.partial(_combine_kernel, final=final),
        out_shape=jax.ShapeDtypeStruct((rows.n, D_MODEL), F32),
        grid=(rows.n_tiles,),
        in_specs=[
            pl.BlockSpec((None, 1, 2 * TM), lambda i: (i, 0, 0), memory_space=pltpu.SMEM),
            pl.BlockSpec((TM, D_MODEL), lambda i: (i, 0)),
            pl.BlockSpec((TM, LANE), lambda i: (i, 0)),
            rows.mod_spec(),
            pl.BlockSpec((1, D_MODEL), lambda i: (0, 0)),
            pl.BlockSpec(memory_space=pl.ANY),
        ],
        out_specs=pl.BlockSpec((TM, D_MODEL), lambda i: (i, 0)),
        scratch_shapes=[pltpu.VMEM((TM, D_MODEL), F32), pltpu.VMEM((TM, D_MODEL), F32),
                        pltpu.SemaphoreType.DMA(())],
        compiler_params=_params("arbitrary"),
        name="moe_combine",
    )(pos, x, meta, mods, fnw.reshape(1, D_MODEL), ys)


def _moe(rows, x, mods, nw, wr, br, tri, w_gate, w_up, w_down, fnw, final):
    n_assign = 2 * rows.n
    n_tiles = n_assign // TME + MOE_EXPERTS
    xn, meta, cnt = _router(rows, x, mods, nw, wr, br, tri)

    counts = cnt[0, :MOE_EXPERTS].astype(I32)
    padded = ((counts + TME - 1) // TME) * TME
    ends = jnp.cumsum(padded)
    starts = ends - padded
    e = meta[:, META_E1:META_E2 + 1].astype(I32)
    rank = meta[:, META_R1:META_R2 + 1].astype(I32)
    pos = (starts[e] + rank).reshape(rows.n_tiles, 1, 2 * TM)
    n_valid = ends[-1] // TME
    tile_first = jnp.arange(n_tiles, dtype=I32) * TME
    tile_start = jnp.minimum(tile_first, ends[-1] - TME)
    tile_expert = jnp.minimum(jnp.searchsorted(ends, tile_start, side="right"), MOE_EXPERTS - 1).astype(I32)
    tile_rows = counts[tile_expert] - (tile_start - starts[tile_expert])
    tile_fill = ((tile_first >= ends[-1]) | (tile_rows < TME)).astype(I32)

    xs = _dispatch(rows, tile_fill, pos, xn, n_tiles)
    ys = _ffn(tile_expert, n_valid.reshape(1).astype(I32), xs, w_gate, w_up, w_down, n_tiles)
    return _combine(rows, pos, x, meta, mods, fnw, ys, final)


def _lower_bound_params(p):
    pr = jax.nn.softmax(p.astype(F32), axis=0)
    lb = jnp.cumsum(pr, axis=0) - pr[0:1]
    lb = jnp.clip(lb, 0.0, 1.0 - 1e-6)
    return jnp.log(jnp.maximum(lb, LOG_TINY)), jnp.log1p(-lb)


def kernel(x_prompt, x_sample, c, cache_swa_k, cache_swa_v, state_hgrn, cache_mla_ckv, cache_mla_krope, c_ctx, mod_w, mod_b, norm1_w, norm2_w, final_norm_w, even_w_in, even_w_out, swa_sink, hgrn_lb_fwd, hgrn_lb_bwd, hgrn_gnorm_w, mla_w_dq, mla_qnorm_w, mla_w_uq, mla_w_dkv, mla_kvnorm_w, mla_w_ukv, mla_w_o, moe_router_group_w, moe_router_group_b, moe_router_expert_w, moe_router_expert_b, moe_w_gate, moe_w_up, moe_w_down):
    nc_b, nc_t, _ = x_prompt.shape
    nl_b, nl_t, _ = x_sample.shape
    rows = _Rows(nc_b, nc_t, nl_b, nl_t)
    past = cache_swa_k.shape[2]

    x = jnp.concatenate([x_prompt.reshape(rows.nc, D_MODEL), x_sample.reshape(rows.nl, D_MODEL)], axis=0)
    mod_rows = 16
    cvec = jnp.concatenate([c_ctx[None, :], c, jnp.zeros((mod_rows - 1 - nl_b, D_MODEL), F32)], axis=0)
    mods_all = _modulation(cvec, mod_w, mod_b).reshape(DEPTH, mod_rows, 6, D_MODEL)

    hconsts = _hgrn_consts()
    la_f, l1_f = _lower_bound_params(hgrn_lb_fwd)
    la_b, l1_b = _lower_bound_params(hgrn_lb_bwd)
    lbp_all = jnp.stack([la_f, l1_f, la_b, l1_b], axis=1).reshape(N_EVEN, 4, HGRN_HEADS, LANE).transpose(0, 2, 1, 3)
    swa_cos, swa_sin = _rope_tables(nl_t, SWA_HEAD_DIM, 0, LANE, 0)
    mla_cos, mla_sin = _rope_tables(nl_t, MLA_ROPE_DIM, MLA_NOPE_DIM, MLA_QK_DIM, TM)
    tri = jnp.asarray(np.tril(np.ones((TM, TM), np.float32), -1), BF16)

    new_k, new_v, new_s, new_ckv, new_kr = [], [], [], [], []
    for l in range(DEPTH):
        j = l // 2
        mods = mods_all[l]
        if l % 2 == 0:
            proj = _even_proj(rows, x, mods, norm1_w[l], even_w_in[j].astype(BF16))
            a_ctx = _swa_ctx(rows, proj, swa_sink[j])
            a_lat = _swa_lat(rows, proj, swa_sink[j],
                             cache_swa_k[:, j].reshape(nl_b, past, SWA_KV_WIDTH),
                             cache_swa_v[:, j].reshape(nl_b, past, SWA_KV_WIDTH), swa_cos, swa_sin)
            r_ctx, s_ctx = _hgrn(proj, lbp_all[j], hgrn_gnorm_w[j], hconsts, nc_b, nc_t, 0, rows.nc,
                                 emit_state=True)
            (r_lat,) = _hgrn(proj, lbp_all[j], hgrn_gnorm_w[j], hconsts, nl_b, nl_t, rows.nc // nl_t, rows.nl,
                             s0=state_hgrn, s0_layer=j)
            w_out = even_w_out[j].astype(BF16)
            x = _outproj(rows, [jnp.concatenate([a_ctx, a_lat], axis=0), jnp.concatenate([r_ctx, r_lat], axis=0)],
                         [w_out[:SWA_WIDTH], w_out[SWA_WIDTH:]], x, mods)
            kv = proj[:rows.nc, SWA_WIDTH:SWA_WIDTH + 2 * SWA_KV_WIDTH]
            new_k.append(kv[:, :SWA_KV_WIDTH].reshape(nc_b, nc_t, SWA_KV_HEADS, SWA_HEAD_DIM))
            new_v.append(kv[:, SWA_KV_WIDTH:].reshape(nc_b, nc_t, SWA_KV_HEADS, SWA_HEAD_DIM))
            new_s.append(s_ctx)
        else:
            wd = jnp.zeros((D_MODEL, MLA_DOWN_WIDTH), F32)
            wd = wd.at[:, :MLA_Q_RANK].set(mla_w_dq[j])
            wd = wd.at[:, MLA_Q_RANK:MLA_Q_RANK + MLA_KV_RANK].set(mla_w_dkv[j][:, :MLA_KV_RANK])
            kr_lo = MLA_Q_RANK + MLA_KV_RANK + MLA_NOPE_DIM
            wd = wd.at[:, kr_lo:kr_lo + MLA_ROPE_DIM].set(mla_w_dkv[j][:, MLA_KV_RANK:])
            wuq = jnp.pad(mla_w_uq[j].reshape(MLA_Q_RANK, MLA_HEADS, MLA_QK_DIM),
                          ((0, 0), (0, 0), (0, LANE - MLA_QK_DIM))).reshape(MLA_Q_RANK, MLA_HEADS * LANE)
            wukv = mla_w_ukv[j].reshape(MLA_KV_RANK, MLA_HEADS, MLA_NOPE_DIM + MLA_V_DIM).transpose(1, 0, 2)
            wk = jnp.pad(wukv[..., :MLA_NOPE_DIM], ((0, 0), (0, 0), (0, LANE - MLA_NOPE_DIM)))
            wv_e = jnp.pad(wukv[..., MLA_NOPE_DIM:], ((0, 0), (0, 0), (0, LANE - MLA_V_DIM)))
            wv_o = jnp.pad(wukv[..., MLA_NOPE_DIM:], ((0, 0), (0, 0), (LANE - MLA_V_DIM, 0)))
            odd = (jnp.arange(MLA_HEADS) % 2 == 1)[:, None, None]
            wv = jnp.where(odd, wv_o, wv_e)
            q, ckv, krb = _mla_proj(rows, x, mods, norm1_w[l], wd.astype(BF16), mla_qnorm_w[j], mla_kvnorm_w[j],
                                    wuq.astype(BF16), mla_cos, mla_sin)
            wk = wk.astype(BF16)
            wv = wv.astype(BF16)
            ckv_c = ckv[:rows.nc].reshape(nc_b, nc_t, MLA_KV_RANK)
            krb_c = krb[:rows.nc].reshape(nc_b, nc_t, LANE)
            o_ctx = _mla_attn(q, ckv_c, krb_c, wk, wv, nc_b, nc_t, 0, rows.nc)
            cache_kr = jnp.pad(cache_mla_krope[:, j], ((0, 0), (0, 0), (MLA_NOPE_DIM, LANE - MLA_QK_DIM)))
            ckv_l = jnp.concatenate([cache_mla_ckv[:, j], ckv[rows.nc:].reshape(nl_b, nl_t, MLA_KV_RANK)], axis=1)
            krb_l = jnp.concatenate([cache_kr, krb[rows.nc:].reshape(nl_b, nl_t, LANE)], axis=1)
            o_lat = _mla_attn(q, ckv_l, krb_l, wk, wv, nl_b, nl_t, rows.nc // nl_t, rows.nl)
            x = _outproj(rows, [jnp.concatenate([o_ctx, o_lat], axis=0)], [mla_w_o[j].astype(BF16)], x, mods)
            new_ckv.append(ckv_c)
            new_kr.append(krb_c[..., MLA_NOPE_DIM:MLA_QK_DIM])

        wr = jnp.zeros((D_MODEL, LANE), F32)
        wr = wr.at[:, :MOE_GROUPS].set(moe_router_group_w[l])
        wr = wr.at[:, MOE_GROUPS:MOE_GROUPS + MOE_EXPERTS].set(
            moe_router_expert_w[l].transpose(1, 0, 2).reshape(D_MODEL, MOE_EXPERTS))
        br = jnp.zeros((1, LANE), F32)
        br = br.at[0, :MOE_GROUPS].set(moe_router_group_b[l])
        br = br.at[0, MOE_GROUPS:MOE_GROUPS + MOE_EXPERTS].set(moe_router_expert_b[l].reshape(MOE_EXPERTS))
        x = _moe(rows, x, mods, norm2_w[l], wr, br, tri, moe_w_gate[l], moe_w_up[l], moe_w_down[l],
                 final_norm_w, final=(l == DEPTH - 1))

    y_prompt = x[:rows.nc].reshape(nc_b, nc_t, D_MODEL)
    y_sample = x[rows.nc:].reshape(nl_b, nl_t, D_MODEL)
    return (y_prompt, y_sample, jnp.stack(new_k, axis=1), jnp.stack(new_v, axis=1), jnp.stack(new_s, axis=1),
            jnp.stack(new_ckv, axis=1), jnp.stack(new_kr, axis=1))
```

```python
import functools
import math

import numpy as np
import jax
import jax.numpy as jnp
from jax import lax
from jax.experimental import pallas as pl
from jax.experimental.pallas import tpu as pltpu

F32, BF16, I32 = jnp.float32, jnp.bfloat16, jnp.int32

D_MODEL = 1024
DEPTH = 4
GRID_W = 64
ROPE_THETA = 10000.0
NORM_EPS = 1e-6
NEG_BIG = -1e30
LOG_TINY = 1e-30
N_EVEN = (DEPTH + 1) // 2
N_ODD = DEPTH // 2

SWA_HEADS = 8
SWA_KV_HEADS = 2
SWA_GROUP = SWA_HEADS // SWA_KV_HEADS
SWA_HEAD_DIM = 64
SWA_WIDTH = SWA_HEADS * SWA_HEAD_DIM
SWA_KV_WIDTH = SWA_KV_HEADS * SWA_HEAD_DIM
SWA_WINDOW = 128
SWA_BLOCK = 128

HGRN_HEADS = 4
HGRN_KEY_DIM = 128
HGRN_VAL_DIM = 128
HGRN_WIDTH = HGRN_HEADS * HGRN_KEY_DIM
HGRN_CHUNK = 128
HGRN_SUB = 8
HGRN_LEVELS = (8, 16, 32, 64)

EVEN_IN_WIDTH = SWA_WIDTH + 2 * SWA_KV_WIDTH + 5 * HGRN_WIDTH
LANE = 128
COL_Q, COL_K, COL_V = 0, SWA_WIDTH // LANE, (SWA_WIDTH + SWA_KV_WIDTH) // LANE
COL_HGRN = (SWA_WIDTH + 2 * SWA_KV_WIDTH) // LANE

MLA_HEADS = 16
MLA_Q_RANK = 384
MLA_KV_RANK = 256
MLA_NOPE_DIM = 64
MLA_ROPE_DIM = 32
MLA_V_DIM = 64
MLA_QK_DIM = MLA_NOPE_DIM + MLA_ROPE_DIM
MLA_DOWN_WIDTH = MLA_Q_RANK + MLA_KV_RANK + LANE

MOE_GROUPS = 4
MOE_EPG = 8
MOE_EXPERTS = MOE_GROUPS * MOE_EPG
MOE_HIDDEN = 256

TM = 512
TME = 256
DMA_UNROLL = 8
VMEM_LIMIT = 48 * 1024 * 1024


def _params(*sem):
    return pltpu.CompilerParams(dimension_semantics=sem, vmem_limit_bytes=VMEM_LIMIT)


def _dot(a, b):
    return jnp.dot(a, b, preferred_element_type=F32)


def _dot_nt(a, b):
    return lax.dot_general(a, b, (((1,), (1,)), ((), ())), preferred_element_type=F32)


def _split2(a):
    hi = a.astype(BF16)
    return hi, (a - hi.astype(F32)).astype(BF16)


def _dot_f32ish(a, b):
    ah, al = _split2(a)
    bh, bl = _split2(b)
    return _dot(ah, bh) + (_dot(ah, bl) + _dot(al, bh))


def _silu(x):
    return x * jax.nn.sigmoid(x)


def _normmod(x, nw, shift, scale):
    ms = jnp.mean(x * x, axis=-1, keepdims=True)
    return (x * lax.rsqrt(ms + NORM_EPS) * nw) * (1.0 + scale) + shift


def _mod_kernel(c_ref, w_ref, b_ref, o_ref):
    o_ref[...] = _dot_f32ish(_silu(c_ref[...]), w_ref[...]) + b_ref[...]


def _modulation(cvec, mod_w, mod_b):
    rows = cvec.shape[0]
    nb = 6 * D_MODEL // 1024
    return pl.pallas_call(
        _mod_kernel,
        out_shape=jax.ShapeDtypeStruct((DEPTH, rows, 6 * D_MODEL), F32),
        grid=(DEPTH, nb),
        in_specs=[
            pl.BlockSpec((rows, D_MODEL), lambda l, n: (0, 0)),
            pl.BlockSpec((None, D_MODEL, 1024), lambda l, n: (l, 0, n)),
            pl.BlockSpec((None, 1, 1024), lambda l, n: (l, 0, n)),
        ],
        out_specs=pl.BlockSpec((None, rows, 1024), lambda l, n: (l, 0, n)),
        compiler_params=_params("parallel", "parallel"),
        name="modulation",
    )(cvec, mod_w, mod_b.reshape(DEPTH, 1, 6 * D_MODEL))


class _Rows:
    def __init__(self, nc_b, nc_t, nl_b, nl_t):
        self.nc_b, self.nc_t, self.nl_b, self.nl_t = nc_b, nc_t, nl_b, nl_t
        self.nc = nc_b * nc_t
        self.nl = nl_b * nl_t
        self.n = self.nc + self.nl
        assert self.nc % TM == 0 and nl_t % TM == 0 and self.nc % nl_t == 0
        self.ctx_tiles = self.nc // TM
        self.tiles_per_lat = nl_t // TM
        self.n_tiles = self.n // TM

    def mod_row(self, i):
        return jnp.where(i < self.ctx_tiles, 0, 1 + (i - self.ctx_tiles) // self.tiles_per_lat)

    def mod_spec(self):
        return pl.BlockSpec((None, 6, D_MODEL), lambda i: (self.mod_row(i), 0, 0))

    def pos_block(self, i):
        return jnp.where(i < self.ctx_tiles, 0, 1 + (i - self.ctx_tiles) % self.tiles_per_lat)


def _even_proj_kernel(x_ref, mod_ref, nw_ref, w_ref, o_ref):
    h = _normmod(x_ref[...], nw_ref[...], mod_ref[0:1, :], mod_ref[1:2, :]).astype(BF16)
    step = 256
    for c in range(EVEN_IN_WIDTH // step):
        o_ref[:, c * step:(c + 1) * step] = _dot(h, w_ref[:, c * step:(c + 1) * step])


def _even_proj(rows, x, mods, nw, w_in_bf16):
    return pl.pallas_call(
        _even_proj_kernel,
        out_shape=jax.ShapeDtypeStruct((rows.n, EVEN_IN_WIDTH), F32),
        grid=(rows.n_tiles,),
        in_specs=[
            pl.BlockSpec((TM, D_MODEL), lambda i: (i, 0)),
            rows.mod_spec(),
            pl.BlockSpec((1, D_MODEL), lambda i: (0, 0)),
            pl.BlockSpec((D_MODEL, EVEN_IN_WIDTH), lambda i: (0, 0)),
        ],
        out_specs=pl.BlockSpec((TM, EVEN_IN_WIDTH), lambda i: (i, 0)),
        compiler_params=_params("parallel"),
        name="even_proj",
    )(x, mods, nw.reshape(1, D_MODEL), w_in_bf16)


def _rope_tables(t_len, rot_dim, lane_lo, lane_hi, lead_rows):
    half = rot_dim // 2
    nf = half // 2
    lane = np.arange(LANE)
    d = (lane - lane_lo) % rot_dim
    active = (lane >= lane_lo) & (lane < lane_hi)
    use_col = d >= half
    fidx = d % nf
    first = (d % half) < nf
    pos = jnp.arange(t_len)
    row = (pos // GRID_W).astype(F32)
    col = (pos % GRID_W).astype(F32)
    inv = jnp.exp(-math.log(ROPE_THETA) * jnp.arange(nf, dtype=F32) / nf)
    p = jnp.where(jnp.asarray(use_col)[None, :], col[:, None], row[:, None])
    ang = p * inv[jnp.asarray(fidx)][None, :]
    act = jnp.asarray(active)[None, :]
    cos = jnp.where(act, jnp.cos(ang), 1.0)
    sin = jnp.where(act, jnp.sin(ang), 0.0)
    sin = jnp.where(jnp.asarray(first)[None, :], -sin, sin)
    if lead_rows:
        cos = jnp.concatenate([jnp.ones((lead_rows, LANE), F32), cos], axis=0)
        sin = jnp.concatenate([jnp.zeros((lead_rows, LANE), F32), sin], axis=0)
    return cos, sin


def _rope(x, cos, sin, nf):
    lane = lax.broadcasted_iota(I32, x.shape, 1)
    up = pltpu.roll(x, LANE - nf, axis=1)
    dn = pltpu.roll(x, nf, axis=1)
    partner = jnp.where((lane & nf) == 0, up, dn)
    return x * cos + partner * sin


def _swa_ctx_kernel(sink_ref, q_ref, k_ref, v_ref, o_ref):
    scale = SWA_HEAD_DIM ** -0.5
    k = k_ref[...]
    v = v_ref[...]
    for h in range(SWA_HEADS):
        kv = h // SWA_GROUP
        lo, klo = h * SWA_HEAD_DIM, kv * SWA_HEAD_DIM
        q = q_ref[:, lo:lo + SWA_HEAD_DIM].astype(BF16)
        kk = k[:, klo:klo + SWA_HEAD_DIM].astype(BF16)
        vv = v[:, klo:klo + SWA_HEAD_DIM].astype(BF16)
        s = _dot_nt(q, kk) * scale
        sk = sink_ref[h]
        m = jnp.maximum(jnp.max(s, axis=-1, keepdims=True), sk)
        p = jnp.exp(s - m)
        den = jnp.sum(p, axis=-1, keepdims=True) + jnp.exp(sk - m)
        o_ref[:, lo:lo + SWA_HEAD_DIM] = _dot(p.astype(BF16), vv) / den


def _swa_ctx(rows, proj, sink):
    t = rows.nc_t
    return pl.pallas_call(
        _swa_ctx_kernel,
        out_shape=jax.ShapeDtypeStruct((rows.nc, SWA_WIDTH), F32),
        grid=(rows.nc_b,),
        in_specs=[
            pl.BlockSpec(memory_space=pltpu.SMEM),
            pl.BlockSpec((t, SWA_WIDTH), lambda b: (b, COL_Q)),
            pl.BlockSpec((t, LANE), lambda b: (b, COL_K)),
            pl.BlockSpec((t, LANE), lambda b: (b, COL_V)),
        ],
        out_specs=pl.BlockSpec((t, SWA_WIDTH), lambda b: (b, 0)),
        compiler_params=_params("parallel"),
        name="swa_ctx",
    )(sink, proj, proj, proj)


def _swa_lat_kernel(sink_ref, q_ref, k_ref, v_ref, kc_ref, vc_ref, cos_ref, sin_ref, o_ref, *, n_blocks):
    scale = SWA_HEAD_DIM ** -0.5
    nf = SWA_HEAD_DIM // 4
    n = pl.program_id(1)
    q0 = pl.multiple_of(n * SWA_BLOCK, SWA_BLOCK)
    cq = cos_ref[pl.ds(q0, SWA_BLOCK), :]
    sq = sin_ref[pl.ds(q0, SWA_BLOCK), :]
    qs = [_rope(q_ref[:, g * LANE:(g + 1) * LANE], cq, sq, nf) for g in range(SWA_WIDTH // LANE)]

    qi = lax.broadcasted_iota(I32, (SWA_BLOCK, SWA_BLOCK), 0)
    r = lax.broadcasted_iota(I32, (SWA_BLOCK, SWA_BLOCK), 1)
    kb, vb, mb = [], [], []
    for off in (-1, 0, 1):
        blk = n + off
        valid = (blk >= 0) & (blk < n_blocks)
        st = pl.multiple_of(jnp.clip(blk, 0, n_blocks - 1) * SWA_BLOCK, SWA_BLOCK)
        sl = pl.ds(st, SWA_BLOCK)
        kb.append(_rope(k_ref[sl, :], cos_ref[sl, :], sin_ref[sl, :], nf))
        vb.append(v_ref[sl, :])
        rel = qi - r - off * SWA_BLOCK
        mb.append(jnp.where((jnp.abs(rel) <= SWA_WINDOW) & valid, 1.0, 0.0))
    kloc = jnp.concatenate(kb, axis=0)
    vloc = jnp.concatenate(vb, axis=0)
    mask = jnp.concatenate(mb, axis=1) > 0.5
    kctx = kc_ref[...]
    vctx = vc_ref[...]

    for h in range(SWA_HEADS):
        kv = h // SWA_GROUP
        lo, klo = h * SWA_HEAD_DIM, kv * SWA_HEAD_DIM
        qlo = lo % LANE
        q = qs[lo // LANE][:, qlo:qlo + SWA_HEAD_DIM].astype(BF16)
        s_ctx = _dot_nt(q, kctx[:, klo:klo + SWA_HEAD_DIM].astype(BF16)) * scale
        s_loc = _dot_nt(q, kloc[:, klo:klo + SWA_HEAD_DIM].astype(BF16)) * scale
        s_loc = jnp.where(mask, s_loc, NEG_BIG)
        sk = sink_ref[h]
        m = jnp.maximum(jnp.maximum(jnp.max(s_ctx, axis=-1, keepdims=True),
                                    jnp.max(s_loc, axis=-1, keepdims=True)), sk)
        p_ctx = jnp.exp(s_ctx - m)
        p_loc = jnp.exp(s_loc - m)
        den = (jnp.sum(p_ctx, axis=-1, keepdims=True) + jnp.sum(p_loc, axis=-1, keepdims=True)
               + jnp.exp(sk - m))
        o = (_dot(p_ctx.astype(BF16), vctx[:, klo:klo + SWA_HEAD_DIM].astype(BF16))
             + _dot(p_loc.astype(BF16), vloc[:, klo:klo + SWA_HEAD_DIM].astype(BF16)))
        o_ref[:, lo:lo + SWA_HEAD_DIM] = o / den


def _swa_lat(rows, proj, sink, k_ctx, v_ctx, cos, sin):
    t = rows.nl_t
    n_blocks = t // SWA_BLOCK
    q_base = rows.nc // SWA_BLOCK
    kv_base = rows.nc // t
    s_ctx = k_ctx.shape[1]
    return pl.pallas_call(
        functools.partial(_swa_lat_kernel, n_blocks=n_blocks),
        out_shape=jax.ShapeDtypeStruct((rows.nl, SWA_WIDTH), F32),
        grid=(rows.nl_b, n_blocks),
        in_specs=[
            pl.BlockSpec(memory_space=pltpu.SMEM),
            pl.BlockSpec((SWA_BLOCK, SWA_WIDTH), lambda b, n: (q_base + b * n_blocks + n, COL_Q)),
            pl.BlockSpec((t, LANE), lambda b, n: (kv_base + b, COL_K)),
            pl.BlockSpec((t, LANE), lambda b, n: (kv_base + b, COL_V)),
            pl.BlockSpec((None, s_ctx, LANE), lambda b, n: (b, 0, 0)),
            pl.BlockSpec((None, s_ctx, LANE), lambda b, n: (b, 0, 0)),
            pl.BlockSpec((t, LANE), lambda b, n: (0, 0)),
            pl.BlockSpec((t, LANE), lambda b, n: (0, 0)),
        ],
        out_specs=pl.BlockSpec((SWA_BLOCK, SWA_WIDTH), lambda b, n: (b * n_blocks + n, 0)),
        compiler_params=_params("parallel", "arbitrary"),
        name="swa_lat",
    )(sink, proj, proj, proj, k_ctx, v_ctx, cos, sin)


def _hgrn_consts():
    c = HGRN_CHUNK
    t = np.arange(c)[:, None]
    u = np.arange(c)[None, :]
    tri_f = (u <= t).astype(np.float32)
    masks = []
    for m in HGRN_LEVELS:
        right = ((t // m) % 2) == 1
        masks.append(right & ((u // m) == (t // m) - 1))
    m_f = np.stack(masks).astype(np.float32)
    sel = np.kron(np.eye(c), np.ones((1, HGRN_SUB))).astype(np.float32)
    return (jnp.asarray(tri_f, BF16), jnp.asarray(m_f, F32),
            jnp.asarray(tri_f[::-1, ::-1], BF16), jnp.asarray(m_f[:, ::-1, ::-1], F32),
            jnp.ones((HGRN_KEY_DIM, LANE), BF16), jnp.asarray(sel, BF16))


def _hgrn_chunk(qb, fr, v, la, l1, tri_ref, m_ref, ones_b, sel_b, st, forward):
    c = HGRN_CHUNK
    nsub = c // HGRN_SUB
    q = _silu(qb) * (HGRN_KEY_DIM ** -0.5)
    ls = jnp.minimum(fr, 0.0) - jnp.log1p(jnp.exp(-jnp.abs(fr)))
    y = l1 + ls
    lf = jnp.maximum(la, y) + jnp.log1p(jnp.exp(-jnp.abs(la - y)))
    k = 1.0 - jnp.exp(lf)

    hi = lf.astype(BF16)
    r1 = lf - hi.astype(F32)
    mid = r1.astype(BF16)
    lo = (r1 - mid.astype(F32)).astype(BF16)
    tri = tri_ref[...]
    b = _dot(tri, hi) + (_dot(tri, mid) + _dot(tri, lo))
    total = jnp.sum(lf, axis=0, keepdims=True)

    b3 = b.reshape(nsub, HGRN_SUB, HGRN_KEY_DIM)
    q3 = q.reshape(nsub, HGRN_SUB, HGRN_KEY_DIM)
    k3 = k.reshape(nsub, HGRN_SUB, HGRN_KEY_DIM)
    v3 = v.reshape(nsub, HGRN_SUB, HGRN_VAL_DIM)
    s_io = lax.broadcasted_iota(I32, (1, HGRN_SUB, HGRN_KEY_DIM), 1)
    ps = []
    for i in range(HGRN_SUB):
        keep = (s_io <= i) if forward else (s_io >= i)
        dec = jnp.exp(jnp.where(keep, b3[:, i:i + 1, :] - b3, NEG_BIG))
        ps.append((q3[:, i:i + 1, :] * dec) * k3)
    p = jnp.stack(ps, axis=1).reshape(nsub * HGRN_SUB * HGRN_SUB, HGRN_KEY_DIM)
    att = _dot(p.astype(BF16), ones_b)
    av = att.reshape(nsub, HGRN_SUB, HGRN_SUB, HGRN_VAL_DIM) * v3[:, None, :, :]
    o = _dot(sel_b, av.reshape(nsub * HGRN_SUB * HGRN_SUB, HGRN_VAL_DIM).astype(BF16))

    a = jnp.zeros((c, c), F32)
    for li, m in enumerate(HGRN_LEVELS):
        pieces = []
        for pair in range(c // (2 * m)):
            r = 2 * pair * m + (m - 1 if forward else m)
            pieces.append(jnp.broadcast_to(b[r:r + 1, :], (2 * m, HGRN_KEY_DIM)))
        bnd = pieces[0] if len(pieces) == 1 else jnp.concatenate(pieces, axis=0)
        fac = jnp.exp(-jnp.abs(b - bnd))
        a = a + m_ref[li] * _dot_nt((q * fac).astype(BF16), (k * fac).astype(BF16))
    vb = v.astype(BF16)
    o = o + _dot(a.astype(BF16), vb)

    o = o + _dot_nt((q * jnp.exp(b)).astype(BF16), st.astype(BF16))
    g = jnp.exp(total)
    kc = (k * jnp.exp(total - b)).astype(BF16)
    st_new = st * g + _dot(v.T.astype(BF16), kc)
    return o, st_new


def _hgrn_kernel(*refs, n_chunks, has_s0, emit_state):
    (qb_ref, ff_ref, fb_ref, ib_ref, gb_ref, lbp_ref, gw_ref,
     ef_ref, mf_ref, eb_ref, mb_ref, ones_ref, sel_ref) = refs[:13]
    rest = list(refs[13:])
    s0_ref = rest.pop(0) if has_s0 else None
    r_ref = rest.pop(0)
    sout_ref = rest.pop(0) if emit_state else None
    of_scr, ob_scr, stf_scr, stb_scr = rest
    c = HGRN_CHUNK
    ones_b = ones_ref[...]
    sel_b = sel_ref[...]
    gw = gw_ref[...]

    for d, st_scr in enumerate((stf_scr, stb_scr)):
        if has_s0:
            st_scr[...] = s0_ref[d].T
        else:
            st_scr[...] = jnp.zeros((HGRN_VAL_DIM, HGRN_KEY_DIM), F32)

    def sweep(i, carry):
        slf = pl.ds(pl.multiple_of(i * c, c), c)
        slb = pl.ds(pl.multiple_of((n_chunks - 1 - i) * c, c), c)
        o_f, st_f = _hgrn_chunk(qb_ref[slf, :], ff_ref[slf, :], ib_ref[slf, :], lbp_ref[0:1, :], lbp_ref[1:2, :],
                                ef_ref, mf_ref, ones_b, sel_b, stf_scr[...], True)
        o_b, st_b = _hgrn_chunk(qb_ref[slb, :], fb_ref[slb, :], ib_ref[slb, :], lbp_ref[2:3, :], lbp_ref[3:4, :],
                                eb_ref, mb_ref, ones_b, sel_b, stb_scr[...], False)
        of_scr[slf, :] = o_f
        ob_scr[slb, :] = o_b
        stf_scr[...] = st_f
        stb_scr[...] = st_b
        return carry

    lax.fori_loop(0, n_chunks, sweep, 0, unroll=2)
    if emit_state:
        sout_ref[0] = stf_scr[...].T
        sout_ref[1] = stb_scr[...].T

    def readout(ci, carry):
        sl = pl.ds(pl.multiple_of(ci * c, c), c)
        tot = of_scr[sl, :] + ob_scr[sl, :]
        ms = jnp.mean(tot * tot, axis=-1, keepdims=True)
        r_ref[sl, :] = (tot * lax.rsqrt(ms + NORM_EPS) * gw) * _silu(gb_ref[sl, :])
        return carry

    lax.fori_loop(0, n_chunks, readout, 0, unroll=2)


def _hgrn(proj, lbp, gw, consts, n_b, t_len, row_block0, n_rows, s0=None, s0_layer=0, emit_state=False):
    n_chunks = t_len // HGRN_CHUNK
    e_f, m_f, e_b, m_b, ones_b, sel_b = consts

    def col(off):
        return pl.BlockSpec((t_len, LANE), lambda b, h: (row_block0 + b, COL_HGRN + off * HGRN_HEADS + h))

    def whole(a):
        nd = a.ndim
        return pl.BlockSpec(a.shape, lambda b, h: (0,) * nd)

    in_specs = [col(0), col(1), col(2), col(3), col(4),
                pl.BlockSpec((None, 4, LANE), lambda b, h: (h, 0, 0)),
                pl.BlockSpec((1, HGRN_VAL_DIM), lambda b, h: (0, 0)),
                whole(e_f), whole(m_f), whole(e_b), whole(m_b), whole(ones_b), whole(sel_b)]
    args = [proj, proj, proj, proj, proj, lbp, gw.reshape(1, HGRN_VAL_DIM), e_f, m_f, e_b, m_b, ones_b, sel_b]
    if s0 is not None:
        in_specs.append(pl.BlockSpec((None, None, 2, None, HGRN_KEY_DIM, HGRN_VAL_DIM),
                                     lambda b, h: (b, s0_layer, 0, h, 0, 0)))
        args.append(s0)
    out_shape = [jax.ShapeDtypeStruct((n_rows, HGRN_WIDTH), F32)]
    out_specs = [pl.BlockSpec((t_len, LANE), lambda b, h: (b, h))]
    if emit_state:
        out_shape.append(jax.ShapeDtypeStruct((n_b, 2, HGRN_HEADS, HGRN_KEY_DIM, HGRN_VAL_DIM), F32))
        out_specs.append(pl.BlockSpec((None, 2, None, HGRN_KEY_DIM, HGRN_VAL_DIM),
                                      lambda b, h: (b, 0, h, 0, 0)))
    return pl.pallas_call(
        functools.partial(_hgrn_kernel, n_chunks=n_chunks, has_s0=s0 is not None, emit_state=emit_state),
        out_shape=out_shape,
        grid=(n_b, HGRN_HEADS),
        in_specs=in_specs,
        out_specs=out_specs,
        scratch_shapes=[pltpu.VMEM((t_len, HGRN_VAL_DIM), F32), pltpu.VMEM((t_len, HGRN_VAL_DIM), F32),
                        pltpu.VMEM((HGRN_VAL_DIM, HGRN_KEY_DIM), F32), pltpu.VMEM((HGRN_VAL_DIM, HGRN_KEY_DIM), F32)],
        compiler_params=_params("parallel", "parallel"),
        name="hgrn_lat" if s0 is not None else "hgrn_ctx",
    )(*args)


def _outproj_kernel(*refs, n_parts, ctx_tiles):
    pair_refs = refs[:2 * n_parts]
    w_refs = refs[2 * n_parts:3 * n_parts]
    x_ref, mod_ref, o_ref = refs[3 * n_parts:]
    is_ctx = pl.program_id(0) < ctx_tiles
    acc = None
    for p in range(n_parts):
        a = jnp.where(is_ctx, pair_refs[2 * p][...], pair_refs[2 * p + 1][...]).astype(BF16)
        d = _dot(a, w_refs[p][...])
        acc = d if acc is None else acc + d
    o_ref[...] = x_ref[...] + mod_ref[2:3, :] * acc


def _outproj(rows, pairs, weights, x, mods):
    n_parts = len(pairs)
    ct = rows.ctx_tiles
    in_specs, args = [], []
    for a_ctx, a_lat in pairs:
        in_specs.append(pl.BlockSpec((TM, a_ctx.shape[1]), lambda i: (jnp.minimum(i, ct - 1), 0)))
        in_specs.append(pl.BlockSpec((TM, a_lat.shape[1]), lambda i: (jnp.maximum(i - ct, 0), 0)))
        args += [a_ctx, a_lat]
    in_specs += [pl.BlockSpec(w.shape, lambda i: (0, 0)) for w in weights]
    in_specs += [pl.BlockSpec((TM, D_MODEL), lambda i: (i, 0)), rows.mod_spec()]
    return pl.pallas_call(
        functools.partial(_outproj_kernel, n_parts=n_parts, ctx_tiles=ct),
        out_shape=jax.ShapeDtypeStruct((rows.n, D_MODEL), F32),
        grid=(rows.n_tiles,),
        in_specs=in_specs,
        out_specs=pl.BlockSpec((TM, D_MODEL), lambda i: (i, 0)),
        compiler_params=_params("arbitrary"),
        name="outproj",
    )(*args, *weights, x, mods)


def _mla_proj_kernel(x_ref, mod_ref, nw_ref, wd_ref, qnw_ref, kvnw_ref, wuq_ref, cos_ref, sin_ref,
                     q_ref, ckv_ref, krb_ref):
    nf = MLA_ROPE_DIM // 4
    h = _normmod(x_ref[...], nw_ref[...], mod_ref[0:1, :], mod_ref[1:2, :]).astype(BF16)
    t1 = _dot(h, wd_ref[...])
    qd = t1[:, :MLA_Q_RANK]
    kvd = t1[:, MLA_Q_RANK:MLA_Q_RANK + MLA_KV_RANK]
    cos = cos_ref[...]
    sin = sin_ref[...]
    qn = qd * lax.rsqrt(jnp.mean(qd * qd, axis=-1, keepdims=True) + NORM_EPS) * qnw_ref[...]
    ckv_ref[...] = kvd * lax.rsqrt(jnp.mean(kvd * kvd, axis=-1, keepdims=True) + NORM_EPS) * kvnw_ref[...]
    krb_ref[...] = _rope(t1[:, MLA_Q_RANK + MLA_KV_RANK:], cos, sin, nf)
    qb = qn.astype(BF16)
    scale = MLA_QK_DIM ** -0.5
    for hd in range(MLA_HEADS):
        qh = _dot(qb, wuq_ref[:, hd * LANE:(hd + 1) * LANE])
        q_ref[:, hd * LANE:(hd + 1) * LANE] = (_rope(qh, cos, sin, nf) * scale).astype(BF16)


def _mla_proj(rows, x, mods, nw, wd, qnw, kvnw, wuq, cos, sin):
    return pl.pallas_call(
        _mla_proj_kernel,
        out_shape=[jax.ShapeDtypeStruct((rows.n, MLA_HEADS * LANE), BF16),
                   jax.ShapeDtypeStruct((rows.n, MLA_KV_RANK), F32),
                   jax.ShapeDtypeStruct((rows.n, LANE), F32)],
        grid=(rows.n_tiles,),
        in_specs=[
            pl.BlockSpec((TM, D_MODEL), lambda i: (i, 0)),
            rows.mod_spec(),
            pl.BlockSpec((1, D_MODEL), lambda i: (0, 0)),
            pl.BlockSpec((D_MODEL, MLA_DOWN_WIDTH), lambda i: (0, 0)),
            pl.BlockSpec((1, MLA_Q_RANK), lambda i: (0, 0)),
            pl.BlockSpec((1, MLA_KV_RANK), lambda i: (0, 0)),
            pl.BlockSpec((MLA_Q_RANK, MLA_HEADS * LANE), lambda i: (0, 0)),
            pl.BlockSpec((TM, LANE), lambda i: (rows.pos_block(i), 0)),
            pl.BlockSpec((TM, LANE), lambda i: (rows.pos_block(i), 0)),
        ],
        out_specs=[pl.BlockSpec((TM, MLA_HEADS * LANE), lambda i: (i, 0)),
                   pl.BlockSpec((TM, MLA_KV_RANK), lambda i: (i, 0)),
                   pl.BlockSpec((TM, LANE), lambda i: (i, 0))],
        compiler_params=_params("parallel"),
        name="mla_proj",
    )(x, mods, nw.reshape(1, D_MODEL), wd, qnw.reshape(1, MLA_Q_RANK), kvnw.reshape(1, MLA_KV_RANK), wuq, cos, sin)


MLA_TQ = 256


def _mla_attn_kernel(q_ref, ckv_ref, krb_ref, wk_ref, wv_ref, o_ref, k_scr, v_scr, *, t_len):
    ckv = ckv_ref[...].astype(BF16)
    krb = krb_ref[...]
    for hh in range(2):
        k_scr[hh] = (_dot(ckv, wk_ref[hh]) + krb).astype(BF16)
        v_scr[hh] = _dot(ckv, wv_ref[hh]).astype(BF16)

    def body(ti, carry):
        sl = pl.ds(pl.multiple_of(ti * MLA_TQ, MLA_TQ), MLA_TQ)
        o = None
        for hh in range(2):
            s = _dot_nt(q_ref[sl, hh * LANE:(hh + 1) * LANE], k_scr[hh])
            m = jnp.max(s, axis=-1, keepdims=True)
            p = jnp.exp(s - m)
            den = jnp.sum(p, axis=-1, keepdims=True)
            oh = _dot(p.astype(BF16), v_scr[hh]) / den
            o = oh if o is None else o + oh
        o_ref[sl, :] = o
        return carry

    lax.fori_loop(0, t_len // MLA_TQ, body, 0)


def _mla_attn(q, ckv_all, krb_all, wk, wv, n_b, t_len, q_row_block0, n_rows):
    s_len = ckv_all.shape[1]
    return pl.pallas_call(
        functools.partial(_mla_attn_kernel, t_len=t_len),
        out_shape=jax.ShapeDtypeStruct((n_rows, MLA_HEADS * MLA_V_DIM), F32),
        grid=(n_b, MLA_HEADS // 2),
        in_specs=[
            pl.BlockSpec((t_len, 2 * LANE), lambda b, hp: (q_row_block0 + b, hp)),
            pl.BlockSpec((None, s_len, MLA_KV_RANK), lambda b, hp: (b, 0, 0)),
            pl.BlockSpec((None, s_len, LANE), lambda b, hp: (b, 0, 0)),
            pl.BlockSpec((2, MLA_KV_RANK, LANE), lambda b, hp: (hp, 0, 0)),
            pl.BlockSpec((2, MLA_KV_RANK, LANE), lambda b, hp: (hp, 0, 0)),
        ],
        out_specs=pl.BlockSpec((t_len, LANE), lambda b, hp: (b, hp)),
        scratch_shapes=[pltpu.VMEM((2, s_len, LANE), BF16), pltpu.VMEM((2, s_len, LANE), BF16)],
        compiler_params=_params("parallel", "arbitrary"),
        name="mla_attn",
    )(q, ckv_all, krb_all, wk, wv)


META_E1, META_E2, META_W1, META_W2, META_R1, META_R2 = range(6)


def _router_kernel(x_ref, mod_ref, nw_ref, wr_ref, br_ref, tri_ref, xn_ref, meta_ref, cnt_ref, base_scr):
    @pl.when(pl.program_id(0) == 0)
    def _():
        base_scr[...] = jnp.zeros(base_scr.shape, F32)

    xn = _normmod(x_ref[...], nw_ref[...], mod_ref[3:4, :], mod_ref[4:5, :])
    xn_ref[...] = xn
    logits = _dot_f32ish(xn, wr_ref[...]) + br_ref[...]
    lane = lax.broadcasted_iota(I32, logits.shape, 1).astype(F32)
    far = float(LANE)

    def first_argmax(vals, vmax):
        return jnp.min(jnp.where(vals == vmax, lane, far), axis=-1, keepdims=True)

    gl = jnp.where(lane < MOE_GROUPS, logits, NEG_BIG)
    gmax = jnp.max(gl, axis=-1, keepdims=True)
    g_w = 1.0 / jnp.sum(jnp.exp(gl - gmax), axis=-1, keepdims=True)
    g_idx = first_argmax(gl, gmax)
    e_lo = MOE_GROUPS + MOE_EPG * g_idx
    el = jnp.where((lane >= e_lo) & (lane < e_lo + MOE_EPG), logits, NEG_BIG)
    m1 = jnp.max(el, axis=-1, keepdims=True)
    i1 = first_argmax(el, m1)
    el2 = jnp.where(lane == i1, NEG_BIG, el)
    m2 = jnp.max(el2, axis=-1, keepdims=True)
    i2 = first_argmax(el2, m2)
    esum = jnp.sum(jnp.exp(el - m1), axis=-1, keepdims=True)
    p1 = 1.0 / esum
    p2 = jnp.exp(m2 - m1) / esum
    w1 = g_w * (p1 / (p1 + p2))
    w2 = g_w * (p2 / (p1 + p2))
    e1 = i1 - MOE_GROUPS
    e2 = i2 - MOE_GROUPS

    oh1 = lane == e1
    oh2 = lane == e2
    oh = jnp.where(oh1 | oh2, 1.0, 0.0)
    before = _dot(tri_ref[...], oh.astype(BF16)) + base_scr[0:1, :]
    r1 = jnp.sum(jnp.where(oh1, before, 0.0), axis=-1, keepdims=True)
    r2 = jnp.sum(jnp.where(oh2, before, 0.0), axis=-1, keepdims=True)
    base_scr[...] = base_scr[...] + jnp.sum(oh, axis=0, keepdims=True)
    cnt_ref[...] = base_scr[...]

    meta = jnp.zeros(logits.shape, F32)
    for slot, val in ((META_E1, e1), (META_E2, e2), (META_W1, w1), (META_W2, w2), (META_R1, r1), (META_R2, r2)):
        meta = jnp.where(lane == slot, val, meta)
    meta_ref[...] = meta


def _router(rows, x, mods, nw, wr, br, tri):
    return pl.pallas_call(
        _router_kernel,
        out_shape=[jax.ShapeDtypeStruct((rows.n, D_MODEL), F32),
                   jax.ShapeDtypeStruct((rows.n, LANE), F32),
                   jax.ShapeDtypeStruct((8, LANE), F32)],
        grid=(rows.n_tiles,),
        in_specs=[
            pl.BlockSpec((TM, D_MODEL), lambda i: (i, 0)),
            rows.mod_spec(),
            pl.BlockSpec((1, D_MODEL), lambda i: (0, 0)),
            pl.BlockSpec((D_MODEL, LANE), lambda i: (0, 0)),
            pl.BlockSpec((1, LANE), lambda i: (0, 0)),
            pl.BlockSpec((TM, TM), lambda i: (0, 0)),
        ],
        out_specs=[pl.BlockSpec((TM, D_MODEL), lambda i: (i, 0)),
                   pl.BlockSpec((TM, LANE), lambda i: (i, 0)),
                   pl.BlockSpec((8, LANE), lambda i: (0, 0))],
        scratch_shapes=[pltpu.VMEM((8, LANE), F32)],
        compiler_params=_params("arbitrary"),
        name="moe_router",
    )(x, mods, nw.reshape(1, D_MODEL), wr, br, tri)


def _row_copy(src, src_row, dst, dst_row, sem):
    return pltpu.make_async_copy(src.at[pl.ds(src_row, 1), :], dst.at[pl.ds(dst_row, 1), :], sem)


def _dispatch_kernel(fill_ref, pos_ref, xn_ref, xs_hbm, zero_scr, sem, fill_sem, *, n_tiles):
    @pl.when(pl.program_id(0) == 0)
    def _():
        zero_scr[...] = jnp.zeros(zero_scr.shape, F32)

        def fill_copy(t):
            return pltpu.make_async_copy(zero_scr, xs_hbm.at[pl.ds(pl.multiple_of(t * TME, TME), TME), :], fill_sem)

        def fill_start(t, carry):
            @pl.when(fill_ref[t] != 0)
            def _():
                fill_copy(t).start()
            return carry

        def fill_wait(t, carry):
            @pl.when(fill_ref[t] != 0)
            def _():
                fill_copy(t).wait()
            return carry

        lax.fori_loop(0, n_tiles, fill_start, 0)
        lax.fori_loop(0, n_tiles, fill_wait, 0)

    def start(r, carry):
        for k in range(2):
            _row_copy(xn_ref, r, xs_hbm, pos_ref[0, 2 * r + k], sem).start()
        return carry

    lax.fori_loop(0, TM, start, 0, unroll=DMA_UNROLL)
    for _ in range(2):
        pltpu.make_async_copy(xn_ref, xs_hbm.at[pl.ds(0, TM), :], sem).wait()


def _dispatch(rows, tile_fill, pos, xn, n_tiles):
    return pl.pallas_call(
        functools.partial(_dispatch_kernel, n_tiles=n_tiles),
        out_shape=jax.ShapeDtypeStruct((n_tiles * TME, D_MODEL), F32),
        grid_spec=pltpu.PrefetchScalarGridSpec(
            num_scalar_prefetch=1,
            grid=(rows.n_tiles,),
            in_specs=[
                pl.BlockSpec((None, 1, 2 * TM), lambda i, fill: (i, 0, 0), memory_space=pltpu.SMEM),
                pl.BlockSpec((TM, D_MODEL), lambda i, fill: (i, 0)),
            ],
            out_specs=pl.BlockSpec(memory_space=pl.ANY),
            scratch_shapes=[pltpu.VMEM((TME, D_MODEL), F32), pltpu.SemaphoreType.DMA(()),
                            pltpu.SemaphoreType.DMA(())],
        ),
        compiler_params=_params("arbitrary"),
        name="moe_dispatch",
    )(tile_fill, pos, xn)


def _ffn_kernel(te_ref, nv_ref, x_ref, wg_ref, wu_ref, wd_ref, y_ref, wg_b, wu_b, wd_b):
    t = pl.program_id(0)
    valid = t < nv_ref[0]
    new_expert = (t == 0) | (te_ref[t] != te_ref[jnp.maximum(t - 1, 0)])

    @pl.when(valid & new_expert)
    def _():
        wg_b[...] = wg_ref[...].astype(BF16)
        wu_b[...] = wu_ref[...].astype(BF16)
        wd_b[...] = wd_ref[...].astype(BF16)

    @pl.when(valid)
    def _():
        x = x_ref[...].astype(BF16)
        a = _silu(_dot(x, wg_b[...])) * _dot(x, wu_b[...])
        y_ref[...] = _dot(a.astype(BF16), wd_b[...])

    @pl.when(jnp.logical_not(valid))
    def _():
        y_ref[...] = jnp.zeros(y_ref.shape, F32)


def _ffn(tile_expert, n_valid, xs, w_gate, w_up, w_down, n_tiles):
    def xmap(t, te, nv):
        return (jnp.minimum(t, nv[0] - 1), 0)

    def wmap(t, te, nv):
        return (te[t], 0, 0)

    return pl.pallas_call(
        _ffn_kernel,
        out_shape=jax.ShapeDtypeStruct((n_tiles * TME, D_MODEL), F32),
        grid_spec=pltpu.PrefetchScalarGridSpec(
            num_scalar_prefetch=2,
            grid=(n_tiles,),
            in_specs=[
                pl.BlockSpec((TME, D_MODEL), xmap),
                pl.BlockSpec((None, D_MODEL, MOE_HIDDEN), wmap),
                pl.BlockSpec((None, D_MODEL, MOE_HIDDEN), wmap),
                pl.BlockSpec((None, MOE_HIDDEN, D_MODEL), wmap),
            ],
            out_specs=pl.BlockSpec((TME, D_MODEL), lambda t, te, nv: (t, 0)),
            scratch_shapes=[pltpu.VMEM((D_MODEL, MOE_HIDDEN), BF16),
                            pltpu.VMEM((D_MODEL, MOE_HIDDEN), BF16),
                            pltpu.VMEM((MOE_HIDDEN, D_MODEL), BF16)],
        ),
        compiler_params=_params("arbitrary"),
        name="moe_ffn",
    )(tile_expert, n_valid, xs, w_gate, w_up, w_down)


def _combine_kernel(pos_ref, x_ref, meta_ref, mod_ref, fnw_ref, ys_hbm, *rest, final, ctx_tiles):
    if final:
        o_ctx_ref, o_lat_ref, buf0, buf1, sem = rest
    else:
        o_ref, buf0, buf1, sem = rest
    bufs = (buf0, buf1)

    def start(r, carry):
        for k in range(2):
            _row_copy(ys_hbm, pos_ref[0, 2 * r + k], bufs[k], r, sem).start()
        return carry

    lax.fori_loop(0, TM, start, 0, unroll=DMA_UNROLL)
    for k in range(2):
        pltpu.make_async_copy(ys_hbm.at[pl.ds(0, TM), :], bufs[k], sem).wait()
    meta = meta_ref[...]
    y = meta[:, META_W1:META_W1 + 1] * buf0[...] + meta[:, META_W2:META_W2 + 1] * buf1[...]
    xo = x_ref[...] + mod_ref[5:6, :] * y
    if not final:
        o_ref[...] = xo
        return
    xo = xo * lax.rsqrt(jnp.mean(xo * xo, axis=-1, keepdims=True) + NORM_EPS) * fnw_ref[...]
    is_ctx = pl.program_id(0) < ctx_tiles

    @pl.when(is_ctx)
    def _():
        o_ctx_ref[...] = xo

    @pl.when(jnp.logical_not(is_ctx))
    def _():
        o_lat_ref[...] = xo


def _combine(rows, pos, x, meta, mods, fnw, ys, final):
    ct = rows.ctx_tiles
    if final:
        out_shape = [jax.ShapeDtypeStruct((rows.nc, D_MODEL), F32), jax.ShapeDtypeStruct((rows.nl, D_MODEL), F32)]
        out_specs = [pl.BlockSpec((TM, D_MODEL), lambda i: (jnp.minimum(i, ct - 1), 0)),
                     pl.BlockSpec((TM, D_MODEL), lambda i: (jnp.maximum(i - ct, 0), 0))]
    else:
        out_shape = jax.ShapeDtypeStruct((rows.n, D_MODEL), F32)
        out_specs = pl.BlockSpec((TM, D_MODEL), lambda i: (i, 0))
    return pl.pallas_call(
        functools.partial(_combine_kernel, final=final, ctx_tiles=ct),
        out_shape=out_shape,
        grid=(rows.n_tiles,),
        in_specs=[
            pl.BlockSpec((None, 1, 2 * TM), lambda i: (i, 0, 0), memory_space=pltpu.SMEM),
            pl.BlockSpec((TM, D_MODEL), lambda i: (i, 0)),
            pl.BlockSpec((TM, LANE), lambda i: (i, 0)),
            rows.mod_spec(),
            pl.BlockSpec((1, D_MODEL), lambda i: (0, 0)),
            pl.BlockSpec(memory_space=pl.ANY),
        ],
        out_specs=out_specs,
        scratch_shapes=[pltpu.VMEM((TM, D_MODEL), F32), pltpu.VMEM((TM, D_MODEL), F32),
                        pltpu.SemaphoreType.DMA(())],
        compiler_params=_params("arbitrary"),
        name="moe_combine",
    )(pos, x, meta, mods, fnw.reshape(1, D_MODEL), ys)


def _moe(rows, x, mods, nw, wr, br, tri, w_gate, w_up, w_down, fnw, final):
    n_assign = 2 * rows.n
    n_tiles = n_assign // TME + MOE_EXPERTS
    xn, meta, cnt = _router(rows, x, mods, nw, wr, br, tri)

    counts = cnt[0, :MOE_EXPERTS].astype(I32)
    padded = ((counts + TME - 1) // TME) * TME
    ends = jnp.cumsum(padded)
    starts = ends - padded
    e = meta[:, META_E1:META_E2 + 1].astype(I32)
    rank = meta[:, META_R1:META_R2 + 1].astype(I32)
    pos = (starts[e] + rank).reshape(rows.n_tiles, 1, 2 * TM)
    n_valid = ends[-1] // TME
    tile_first = jnp.arange(n_tiles, dtype=I32) * TME
    tile_start = jnp.minimum(tile_first, ends[-1] - TME)
    tile_expert = jnp.sum((ends[None, :] <= tile_start[:, None]).astype(I32), axis=1)
    tile_expert = jnp.minimum(tile_expert, MOE_EXPERTS - 1)
    tile_rows = counts[tile_expert] - (tile_start - starts[tile_expert])
    tile_fill = ((tile_first >= ends[-1]) | (tile_rows < TME)).astype(I32)

    xs = _dispatch(rows, tile_fill, pos, xn, n_tiles)
    ys = _ffn(tile_expert, n_valid.reshape(1).astype(I32), xs, w_gate, w_up, w_down, n_tiles)
    return _combine(rows, pos, x, meta, mods, fnw, ys, final)


def _lower_bound_params(p):
    pr = jax.nn.softmax(p.astype(F32), axis=0)
    lb = jnp.cumsum(pr, axis=0) - pr[0:1]
    lb = jnp.clip(lb, 0.0, 1.0 - 1e-6)
    return jnp.log(jnp.maximum(lb, LOG_TINY)), jnp.log1p(-lb)


def kernel(x_prompt, x_sample, c, cache_swa_k, cache_swa_v, state_hgrn, cache_mla_ckv, cache_mla_krope, c_ctx, mod_w, mod_b, norm1_w, norm2_w, final_norm_w, even_w_in, even_w_out, swa_sink, hgrn_lb_fwd, hgrn_lb_bwd, hgrn_gnorm_w, mla_w_dq, mla_qnorm_w, mla_w_uq, mla_w_dkv, mla_kvnorm_w, mla_w_ukv, mla_w_o, moe_router_group_w, moe_router_group_b, moe_router_expert_w, moe_router_expert_b, moe_w_gate, moe_w_up, moe_w_down):
    nc_b, nc_t, _ = x_prompt.shape
    nl_b, nl_t, _ = x_sample.shape
    rows = _Rows(nc_b, nc_t, nl_b, nl_t)
    past = cache_swa_k.shape[2]

    x = jnp.concatenate([x_prompt.reshape(rows.nc, D_MODEL), x_sample.reshape(rows.nl, D_MODEL)], axis=0)
    mod_rows = 16
    cvec = jnp.concatenate([c_ctx[None, :], c, jnp.zeros((mod_rows - 1 - nl_b, D_MODEL), F32)], axis=0)
    mods_all = _modulation(cvec, mod_w, mod_b).reshape(DEPTH, mod_rows, 6, D_MODEL)

    hconsts = _hgrn_consts()
    la_f, l1_f = _lower_bound_params(hgrn_lb_fwd)
    la_b, l1_b = _lower_bound_params(hgrn_lb_bwd)
    lbp_all = jnp.stack([la_f, l1_f, la_b, l1_b], axis=1).reshape(N_EVEN, 4, HGRN_HEADS, LANE).transpose(0, 2, 1, 3)
    swa_cos, swa_sin = _rope_tables(nl_t, SWA_HEAD_DIM, 0, LANE, 0)
    mla_cos, mla_sin = _rope_tables(nl_t, MLA_ROPE_DIM, MLA_NOPE_DIM, MLA_QK_DIM, TM)
    tri = jnp.asarray(np.tril(np.ones((TM, TM), np.float32), -1), BF16)

    new_k, new_v, new_s, new_ckv, new_kr = [], [], [], [], []
    for l in range(DEPTH):
        j = l // 2
        mods = mods_all[l]
        if l % 2 == 0:
            proj = _even_proj(rows, x, mods, norm1_w[l], even_w_in[j].astype(BF16))
            a_ctx = _swa_ctx(rows, proj, swa_sink[j])
            a_lat = _swa_lat(rows, proj, swa_sink[j],
                             cache_swa_k[:, j].reshape(nl_b, past, SWA_KV_WIDTH),
                             cache_swa_v[:, j].reshape(nl_b, past, SWA_KV_WIDTH), swa_cos, swa_sin)
            r_ctx, s_ctx = _hgrn(proj, lbp_all[j], hgrn_gnorm_w[j], hconsts, nc_b, nc_t, 0, rows.nc,
                                 emit_state=True)
            (r_lat,) = _hgrn(proj, lbp_all[j], hgrn_gnorm_w[j], hconsts, nl_b, nl_t, rows.nc // nl_t, rows.nl,
                             s0=state_hgrn, s0_layer=j)
            w_out = even_w_out[j].astype(BF16)
            x = _outproj(rows, [(a_ctx, a_lat), (r_ctx, r_lat)], [w_out[:SWA_WIDTH], w_out[SWA_WIDTH:]], x, mods)
            kv = proj[:rows.nc, SWA_WIDTH:SWA_WIDTH + 2 * SWA_KV_WIDTH]
            new_k.append(kv[:, :SWA_KV_WIDTH].reshape(nc_b, nc_t, SWA_KV_HEADS, SWA_HEAD_DIM))
            new_v.append(kv[:, SWA_KV_WIDTH:].reshape(nc_b, nc_t, SWA_KV_HEADS, SWA_HEAD_DIM))
            new_s.append(s_ctx)
        else:
            wd = jnp.zeros((D_MODEL, MLA_DOWN_WIDTH), F32)
            wd = wd.at[:, :MLA_Q_RANK].set(mla_w_dq[j])
            wd = wd.at[:, MLA_Q_RANK:MLA_Q_RANK + MLA_KV_RANK].set(mla_w_dkv[j][:, :MLA_KV_RANK])
            kr_lo = MLA_Q_RANK + MLA_KV_RANK + MLA_NOPE_DIM
            wd = wd.at[:, kr_lo:kr_lo + MLA_ROPE_DIM].set(mla_w_dkv[j][:, MLA_KV_RANK:])
            wuq = jnp.pad(mla_w_uq[j].reshape(MLA_Q_RANK, MLA_HEADS, MLA_QK_DIM),
                          ((0, 0), (0, 0), (0, LANE - MLA_QK_DIM))).reshape(MLA_Q_RANK, MLA_HEADS * LANE)
            wukv = mla_w_ukv[j].reshape(MLA_KV_RANK, MLA_HEADS, MLA_NOPE_DIM + MLA_V_DIM).transpose(1, 0, 2)
            wk = jnp.pad(wukv[..., :MLA_NOPE_DIM], ((0, 0), (0, 0), (0, LANE - MLA_NOPE_DIM)))
            wv_e = jnp.pad(wukv[..., MLA_NOPE_DIM:], ((0, 0), (0, 0), (0, LANE - MLA_V_DIM)))
            wv_o = jnp.pad(wukv[..., MLA_NOPE_DIM:], ((0, 0), (0, 0), (LANE - MLA_V_DIM, 0)))
            odd = (jnp.arange(MLA_HEADS) % 2 == 1)[:, None, None]
            wv = jnp.where(odd, wv_o, wv_e)
            q, ckv, krb = _mla_proj(rows, x, mods, norm1_w[l], wd.astype(BF16), mla_qnorm_w[j], mla_kvnorm_w[j],
                                    wuq.astype(BF16), mla_cos, mla_sin)
            wk = wk.astype(BF16)
            wv = wv.astype(BF16)
            ckv_c = ckv[:rows.nc].reshape(nc_b, nc_t, MLA_KV_RANK)
            krb_c = krb[:rows.nc].reshape(nc_b, nc_t, LANE)
            o_ctx = _mla_attn(q, ckv_c, krb_c, wk, wv, nc_b, nc_t, 0, rows.nc)
            cache_kr = jnp.pad(cache_mla_krope[:, j], ((0, 0), (0, 0), (MLA_NOPE_DIM, LANE - MLA_QK_DIM)))
            ckv_l = jnp.concatenate([cache_mla_ckv[:, j], ckv[rows.nc:].reshape(nl_b, nl_t, MLA_KV_RANK)], axis=1)
            krb_l = jnp.concatenate([cache_kr, krb[rows.nc:].reshape(nl_b, nl_t, LANE)], axis=1)
            o_lat = _mla_attn(q, ckv_l, krb_l, wk, wv, nl_b, nl_t, rows.nc // nl_t, rows.nl)
            x = _outproj(rows, [(o_ctx, o_lat)], [mla_w_o[j].astype(BF16)], x, mods)
            new_ckv.append(ckv_c)
            new_kr.append(krb_c[..., MLA_NOPE_DIM:MLA_QK_DIM])

        wr = jnp.zeros((D_MODEL, LANE), F32)
        wr = wr.at[:, :MOE_GROUPS].set(moe_router_group_w[l])
        wr = wr.at[:, MOE_GROUPS:MOE_GROUPS + MOE_EXPERTS].set(
            moe_router_expert_w[l].transpose(1, 0, 2).reshape(D_MODEL, MOE_EXPERTS))
        br = jnp.zeros((1, LANE), F32)
        br = br.at[0, :MOE_GROUPS].set(moe_router_group_b[l])
        br = br.at[0, MOE_GROUPS:MOE_GROUPS + MOE_EXPERTS].set(moe_router_expert_b[l].reshape(MOE_EXPERTS))
        x = _moe(rows, x, mods, norm2_w[l], wr, br, tri, moe_w_gate[l], moe_w_up[l], moe_w_down[l],
                 final_norm_w, final=(l == DEPTH - 1))

    y_prompt = x[0].reshape(nc_b, nc_t, D_MODEL)
    y_sample = x[1].reshape(nl_b, nl_t, D_MODEL)
    return (y_prompt, y_sample, jnp.stack(new_k, axis=1), jnp.stack(new_v, axis=1), jnp.stack(new_s, axis=1),
            jnp.stack(new_ckv, axis=1), jnp.stack(new_kr, axis=1))
```

```python
import functools
import math

import numpy as np
import jax
import jax.numpy as jnp
from jax import lax
from jax.experimental import pallas as pl
from jax.experimental.pallas import tpu as pltpu

F32, BF16, I32 = jnp.float32, jnp.bfloat16, jnp.int32

D_MODEL = 1024
DEPTH = 4
GRID_W = 64
ROPE_THETA = 10000.0
NORM_EPS = 1e-6
NEG_BIG = -1e30
LOG_TINY = 1e-30
N_EVEN = (DEPTH + 1) // 2
N_ODD = DEPTH // 2

SWA_HEADS = 8
SWA_KV_HEADS = 2
SWA_GROUP = SWA_HEADS // SWA_KV_HEADS
SWA_HEAD_DIM = 64
SWA_WIDTH = SWA_HEADS * SWA_HEAD_DIM
SWA_KV_WIDTH = SWA_KV_HEADS * SWA_HEAD_DIM
SWA_WINDOW = 128
SWA_BLOCK = 128

HGRN_HEADS = 4
HGRN_KEY_DIM = 128
HGRN_VAL_DIM = 128
HGRN_WIDTH = HGRN_HEADS * HGRN_KEY_DIM
HGRN_CHUNK = 128
HGRN_SUB = 8
HGRN_LEVELS = (8, 16, 32, 64)

EVEN_IN_WIDTH = SWA_WIDTH + 2 * SWA_KV_WIDTH + 5 * HGRN_WIDTH
LANE = 128
COL_Q, COL_K, COL_V = 0, SWA_WIDTH // LANE, (SWA_WIDTH + SWA_KV_WIDTH) // LANE
COL_HGRN = (SWA_WIDTH + 2 * SWA_KV_WIDTH) // LANE

MLA_HEADS = 16
MLA_Q_RANK = 384
MLA_KV_RANK = 256
MLA_NOPE_DIM = 64
MLA_ROPE_DIM = 32
MLA_V_DIM = 64
MLA_QK_DIM = MLA_NOPE_DIM + MLA_ROPE_DIM
MLA_DOWN_WIDTH = MLA_Q_RANK + MLA_KV_RANK + LANE

MOE_GROUPS = 4
MOE_EPG = 8
MOE_EXPERTS = MOE_GROUPS * MOE_EPG
MOE_HIDDEN = 256

TM = 512
TME = 256
DMA_UNROLL = 8
VMEM_LIMIT = 48 * 1024 * 1024


def _params(*sem):
    return pltpu.CompilerParams(dimension_semantics=sem, vmem_limit_bytes=VMEM_LIMIT)


def _dot(a, b):
    return jnp.dot(a, b, preferred_element_type=F32)


def _dot_nt(a, b):
    return lax.dot_general(a, b, (((1,), (1,)), ((), ())), preferred_element_type=F32)


def _split2(a):
    hi = a.astype(BF16)
    return hi, (a - hi.astype(F32)).astype(BF16)


def _dot_f32ish(a, b):
    ah, al = _split2(a)
    bh, bl = _split2(b)
    return _dot(ah, bh) + (_dot(ah, bl) + _dot(al, bh))


def _silu(x):
    return x * jax.nn.sigmoid(x)


def _normmod(x, nw, shift, scale):
    ms = jnp.mean(x * x, axis=-1, keepdims=True)
    return (x * lax.rsqrt(ms + NORM_EPS) * nw) * (1.0 + scale) + shift


def _mod_kernel(c_ref, w_ref, b_ref, o_ref):
    o_ref[...] = _dot_f32ish(_silu(c_ref[...]), w_ref[...]) + b_ref[...]


def _modulation(cvec, mod_w, mod_b):
    rows = cvec.shape[0]
    nb = 6 * D_MODEL // 1024
    return pl.pallas_call(
        _mod_kernel,
        out_shape=jax.ShapeDtypeStruct((DEPTH, rows, 6 * D_MODEL), F32),
        grid=(DEPTH, nb),
        in_specs=[
            pl.BlockSpec((rows, D_MODEL), lambda l, n: (0, 0)),
            pl.BlockSpec((None, D_MODEL, 1024), lambda l, n: (l, 0, n)),
            pl.BlockSpec((None, 1, 1024), lambda l, n: (l, 0, n)),
        ],
        out_specs=pl.BlockSpec((None, rows, 1024), lambda l, n: (l, 0, n)),
        compiler_params=_params("parallel", "parallel"),
        name="modulation",
    )(cvec, mod_w, mod_b.reshape(DEPTH, 1, 6 * D_MODEL))


class _Rows:
    def __init__(self, nc_b, nc_t, nl_b, nl_t):
        self.nc_b, self.nc_t, self.nl_b, self.nl_t = nc_b, nc_t, nl_b, nl_t
        self.nc = nc_b * nc_t
        self.nl = nl_b * nl_t
        self.n = self.nc + self.nl
        assert self.nc % TM == 0 and nl_t % TM == 0 and self.nc % nl_t == 0
        self.ctx_tiles = self.nc // TM
        self.tiles_per_lat = nl_t // TM
        self.n_tiles = self.n // TM

    def mod_row(self, i):
        return jnp.where(i < self.ctx_tiles, 0, 1 + (i - self.ctx_tiles) // self.tiles_per_lat)

    def mod_spec(self):
        return pl.BlockSpec((None, 6, D_MODEL), lambda i: (self.mod_row(i), 0, 0))

    def stream_specs(self, x):
        ct = self.ctx_tiles
        if not isinstance(x, tuple):
            return [pl.BlockSpec((TM, x.shape[1]), lambda i: (i, 0))], [x]
        x_ctx, x_lat = x
        return ([pl.BlockSpec((TM, x_ctx.shape[1]), lambda i: (jnp.minimum(i, ct - 1), 0)),
                 pl.BlockSpec((TM, x_lat.shape[1]), lambda i: (jnp.maximum(i - ct, 0), 0))], [x_ctx, x_lat])

    def stream_tile(self, refs):
        if len(refs) == 1:
            return refs[0][...]
        return jnp.where(pl.program_id(0) < self.ctx_tiles, refs[0][...], refs[1][...])

    def pos_block(self, i):
        return jnp.where(i < self.ctx_tiles, 0, 1 + (i - self.ctx_tiles) % self.tiles_per_lat)


def _even_proj_kernel(*refs, rows):
    mod_ref, nw_ref, w_ref, o_ref = refs[-4:]
    x = rows.stream_tile(refs[:-4])
    h = _normmod(x, nw_ref[...], mod_ref[0:1, :], mod_ref[1:2, :]).astype(BF16)
    step = 256
    for c in range(EVEN_IN_WIDTH // step):
        o_ref[:, c * step:(c + 1) * step] = _dot(h, w_ref[:, c * step:(c + 1) * step])


def _even_proj(rows, x, mods, nw, w_in_bf16):
    x_specs, x_args = rows.stream_specs(x)
    return pl.pallas_call(
        functools.partial(_even_proj_kernel, rows=rows),
        out_shape=jax.ShapeDtypeStruct((rows.n, EVEN_IN_WIDTH), F32),
        grid=(rows.n_tiles,),
        in_specs=x_specs + [
            rows.mod_spec(),
            pl.BlockSpec((1, D_MODEL), lambda i: (0, 0)),
            pl.BlockSpec((D_MODEL, EVEN_IN_WIDTH), lambda i: (0, 0)),
        ],
        out_specs=pl.BlockSpec((TM, EVEN_IN_WIDTH), lambda i: (i, 0)),
        compiler_params=_params("arbitrary"),
        name="even_proj",
    )(*x_args, mods, nw.reshape(1, D_MODEL), w_in_bf16)


def _rope_tables(t_len, rot_dim, lane_lo, lane_hi, lead_rows):
    half = rot_dim // 2
    nf = half // 2
    lane = np.arange(LANE)
    d = (lane - lane_lo) % rot_dim
    active = (lane >= lane_lo) & (lane < lane_hi)
    use_col = d >= half
    fidx = d % nf
    first = (d % half) < nf
    pos = jnp.arange(t_len)
    row = (pos // GRID_W).astype(F32)
    col = (pos % GRID_W).astype(F32)
    inv = jnp.exp(-math.log(ROPE_THETA) * jnp.arange(nf, dtype=F32) / nf)
    p = jnp.where(jnp.asarray(use_col)[None, :], col[:, None], row[:, None])
    ang = p * inv[jnp.asarray(fidx)][None, :]
    act = jnp.asarray(active)[None, :]
    cos = jnp.where(act, jnp.cos(ang), 1.0)
    sin = jnp.where(act, jnp.sin(ang), 0.0)
    sin = jnp.where(jnp.asarray(first)[None, :], -sin, sin)
    if lead_rows:
        cos = jnp.concatenate([jnp.ones((lead_rows, LANE), F32), cos], axis=0)
        sin = jnp.concatenate([jnp.zeros((lead_rows, LANE), F32), sin], axis=0)
    return cos, sin


def _rope(x, cos, sin, nf):
    lane = lax.broadcasted_iota(I32, x.shape, 1)
    up = pltpu.roll(x, LANE - nf, axis=1)
    dn = pltpu.roll(x, nf, axis=1)
    partner = jnp.where((lane & nf) == 0, up, dn)
    return x * cos + partner * sin


def _swa_ctx_kernel(sink_ref, q_ref, k_ref, v_ref, o_ref):
    scale = SWA_HEAD_DIM ** -0.5
    k = k_ref[...]
    v = v_ref[...]
    for h in range(SWA_HEADS):
        kv = h // SWA_GROUP
        lo, klo = h * SWA_HEAD_DIM, kv * SWA_HEAD_DIM
        q = q_ref[:, lo:lo + SWA_HEAD_DIM].astype(BF16)
        kk = k[:, klo:klo + SWA_HEAD_DIM].astype(BF16)
        vv = v[:, klo:klo + SWA_HEAD_DIM].astype(BF16)
        s = _dot_nt(q, kk) * scale
        sk = sink_ref[h]
        m = jnp.maximum(jnp.max(s, axis=-1, keepdims=True), sk)
        p = jnp.exp(s - m)
        den = jnp.sum(p, axis=-1, keepdims=True) + jnp.exp(sk - m)
        o_ref[:, lo:lo + SWA_HEAD_DIM] = _dot(p.astype(BF16), vv) / den


def _swa_ctx(rows, proj, sink):
    t = rows.nc_t
    return pl.pallas_call(
        _swa_ctx_kernel,
        out_shape=jax.ShapeDtypeStruct((rows.nc, SWA_WIDTH), F32),
        grid=(rows.nc_b,),
        in_specs=[
            pl.BlockSpec(memory_space=pltpu.SMEM),
            pl.BlockSpec((t, SWA_WIDTH), lambda b: (b, COL_Q)),
            pl.BlockSpec((t, LANE), lambda b: (b, COL_K)),
            pl.BlockSpec((t, LANE), lambda b: (b, COL_V)),
        ],
        out_specs=pl.BlockSpec((t, SWA_WIDTH), lambda b: (b, 0)),
        compiler_params=_params("parallel"),
        name="swa_ctx",
    )(sink, proj, proj, proj)


def _lane_fold(x, op):
    out = x[:, :LANE]
    for i in range(1, x.shape[1] // LANE):
        out = op(out, x[:, i * LANE:(i + 1) * LANE])
    return out


def _swa_lat_kernel(sink_ref, q_ref, k_ref, v_ref, kc_ref, vc_ref, cos_ref, sin_ref, o_ref,
                    kl_scr, vl_scr, kc_scr, vc_scr, *, n_blocks):
    scale = SWA_HEAD_DIM ** -0.5
    nf = SWA_HEAD_DIM // 4
    n = pl.program_id(1)

    @pl.when(n == 0)
    def _():
        kr = _rope(k_ref[...], cos_ref[...], sin_ref[...], nf)
        for kv in range(SWA_KV_HEADS):
            cols = slice(kv * SWA_HEAD_DIM, (kv + 1) * SWA_HEAD_DIM)
            kl_scr[kv] = kr[:, cols].astype(BF16)
            vl_scr[kv] = v_ref[:, cols].astype(BF16)
            kc_scr[kv] = kc_ref[:, cols].astype(BF16)
            vc_scr[kv] = vc_ref[:, cols].astype(BF16)

    q0 = pl.multiple_of(n * SWA_BLOCK, SWA_BLOCK)
    cq = cos_ref[pl.ds(q0, SWA_BLOCK), :]
    sq = sin_ref[pl.ds(q0, SWA_BLOCK), :]
    qs = [_rope(q_ref[:, g * LANE:(g + 1) * LANE], cq, sq, nf) for g in range(SWA_WIDTH // LANE)]

    qi = lax.broadcasted_iota(I32, (SWA_BLOCK, SWA_BLOCK), 0)
    r = lax.broadcasted_iota(I32, (SWA_BLOCK, SWA_BLOCK), 1)
    band, mb = [], []
    for off in (-1, 0, 1):
        blk = n + off
        valid = (blk >= 0) & (blk < n_blocks)
        st = pl.multiple_of(jnp.clip(blk, 0, n_blocks - 1) * SWA_BLOCK, SWA_BLOCK)
        band.append(pl.ds(st, SWA_BLOCK))
        rel = qi - r - off * SWA_BLOCK
        mb.append(jnp.where((jnp.abs(rel) <= SWA_WINDOW) & valid, 1.0, 0.0))
    mask = jnp.concatenate(mb, axis=1) > 0.5

    for kv in range(SWA_KV_HEADS):
        kl = jnp.concatenate([kl_scr[kv, sl, :] for sl in band], axis=0)
        vl = jnp.concatenate([vl_scr[kv, sl, :] for sl in band], axis=0)
        kc = kc_scr[kv]
        vc = vc_scr[kv]
        for g in range(SWA_GROUP):
            h = kv * SWA_GROUP + g
            lo = h * SWA_HEAD_DIM
            q = qs[lo // LANE][:, lo % LANE:lo % LANE + SWA_HEAD_DIM].astype(BF16)
            s_ctx = _dot_nt(q, kc) * scale
            s_loc = jnp.where(mask, _dot_nt(q, kl) * scale, NEG_BIG)
            sk = sink_ref[h]
            m = jnp.max(jnp.maximum(_lane_fold(s_ctx, jnp.maximum), _lane_fold(s_loc, jnp.maximum)),
                        axis=-1, keepdims=True)
            m = jnp.maximum(m, sk)
            p_ctx = jnp.exp(s_ctx - m)
            p_loc = jnp.exp(s_loc - m)
            den = jnp.sum(_lane_fold(p_ctx, jnp.add) + _lane_fold(p_loc, jnp.add), axis=-1, keepdims=True)
            den = den + jnp.exp(sk - m)
            o = _dot(p_ctx.astype(BF16), vc) + _dot(p_loc.astype(BF16), vl)
            o_ref[:, lo:lo + SWA_HEAD_DIM] = o / den


def _swa_lat(rows, proj, sink, k_ctx, v_ctx, cos, sin):
    t = rows.nl_t
    n_blocks = t // SWA_BLOCK
    q_base = rows.nc // SWA_BLOCK
    kv_base = rows.nc // t
    s_ctx = k_ctx.shape[1]
    return pl.pallas_call(
        functools.partial(_swa_lat_kernel, n_blocks=n_blocks),
        out_shape=jax.ShapeDtypeStruct((rows.nl, SWA_WIDTH), F32),
        grid=(rows.nl_b, n_blocks),
        in_specs=[
            pl.BlockSpec(memory_space=pltpu.SMEM),
            pl.BlockSpec((SWA_BLOCK, SWA_WIDTH), lambda b, n: (q_base + b * n_blocks + n, COL_Q)),
            pl.BlockSpec((t, LANE), lambda b, n: (kv_base + b, COL_K)),
            pl.BlockSpec((t, LANE), lambda b, n: (kv_base + b, COL_V)),
            pl.BlockSpec((None, s_ctx, LANE), lambda b, n: (b, 0, 0)),
            pl.BlockSpec((None, s_ctx, LANE), lambda b, n: (b, 0, 0)),
            pl.BlockSpec((t, LANE), lambda b, n: (0, 0)),
            pl.BlockSpec((t, LANE), lambda b, n: (0, 0)),
        ],
        out_specs=pl.BlockSpec((SWA_BLOCK, SWA_WIDTH), lambda b, n: (b * n_blocks + n, 0)),
        scratch_shapes=[pltpu.VMEM((SWA_KV_HEADS, t, SWA_HEAD_DIM), BF16),
                        pltpu.VMEM((SWA_KV_HEADS, t, SWA_HEAD_DIM), BF16),
                        pltpu.VMEM((SWA_KV_HEADS, s_ctx, SWA_HEAD_DIM), BF16),
                        pltpu.VMEM((SWA_KV_HEADS, s_ctx, SWA_HEAD_DIM), BF16)],
        compiler_params=_params("parallel", "arbitrary"),
        name="swa_lat",
    )(sink, proj, proj, proj, k_ctx, v_ctx, cos, sin)


def _hgrn_consts():
    c = HGRN_CHUNK
    t = np.arange(c)[:, None]
    u = np.arange(c)[None, :]
    tri_f = (u <= t).astype(np.float32)
    masks = []
    for m in HGRN_LEVELS:
        right = ((t // m) % 2) == 1
        masks.append(right & ((u // m) == (t // m) - 1))
    m_f = np.stack(masks).astype(np.float32)
    sel = np.kron(np.eye(c), np.ones((1, HGRN_SUB))).astype(np.float32)
    return (jnp.asarray(tri_f, BF16), jnp.asarray(m_f, F32),
            jnp.asarray(tri_f[::-1, ::-1], BF16), jnp.asarray(m_f[:, ::-1, ::-1], F32),
            jnp.ones((HGRN_KEY_DIM, LANE), BF16), jnp.asarray(sel, BF16))


def _hgrn_chunk(qb, fr, v, lbm, oml, tri_ref, m_ref, ones_b, sel_b, st, forward):
    c = HGRN_CHUNK
    nsub = c // HGRN_SUB
    q = _silu(qb) * (HGRN_KEY_DIM ** -0.5)
    f = lbm + oml * jax.nn.sigmoid(fr)
    lf = jnp.log(f)
    k = 1.0 - f

    hi = lf.astype(BF16)
    r1 = lf - hi.astype(F32)
    mid = r1.astype(BF16)
    lo = (r1 - mid.astype(F32)).astype(BF16)
    tri = tri_ref[...]
    b = _dot(tri, hi) + (_dot(tri, mid) + _dot(tri, lo))
    total = jnp.sum(lf, axis=0, keepdims=True)

    b3 = b.reshape(nsub, HGRN_SUB, HGRN_KEY_DIM)
    q3 = q.reshape(nsub, HGRN_SUB, HGRN_KEY_DIM)
    k3 = k.reshape(nsub, HGRN_SUB, HGRN_KEY_DIM)
    v3 = v.reshape(nsub, HGRN_SUB, HGRN_VAL_DIM)
    s_io = lax.broadcasted_iota(I32, (1, HGRN_SUB, HGRN_KEY_DIM), 1)
    ps = []
    for i in range(HGRN_SUB):
        keep = (s_io <= i) if forward else (s_io >= i)
        dec = jnp.exp(jnp.where(keep, b3[:, i:i + 1, :] - b3, NEG_BIG))
        ps.append((q3[:, i:i + 1, :] * dec) * k3)
    p = jnp.stack(ps, axis=1).reshape(nsub * HGRN_SUB * HGRN_SUB, HGRN_KEY_DIM)
    att = _dot(p.astype(BF16), ones_b)
    av = att.reshape(nsub, HGRN_SUB, HGRN_SUB, HGRN_VAL_DIM) * v3[:, None, :, :]
    o = _dot(sel_b, av.reshape(nsub * HGRN_SUB * HGRN_SUB, HGRN_VAL_DIM).astype(BF16))

    a = jnp.zeros((c, c), F32)
    for li, m in enumerate(HGRN_LEVELS):
        pieces = []
        for pair in range(c // (2 * m)):
            r = 2 * pair * m + (m - 1 if forward else m)
            pieces.append(jnp.broadcast_to(b[r:r + 1, :], (2 * m, HGRN_KEY_DIM)))
        bnd = pieces[0] if len(pieces) == 1 else jnp.concatenate(pieces, axis=0)
        fac = jnp.exp(-jnp.abs(b - bnd))
        a = a + m_ref[li] * _dot_nt((q * fac).astype(BF16), (k * fac).astype(BF16))
    vb = v.astype(BF16)
    o = o + _dot(a.astype(BF16), vb)

    o = o + _dot_nt((q * jnp.exp(b)).astype(BF16), st.astype(BF16))
    g = jnp.exp(total)
    kc = (k * jnp.exp(total - b)).astype(BF16)
    st_new = st * g + _dot(v.T.astype(BF16), kc)
    return o, st_new


def _hgrn_kernel(*refs, n_chunks, has_s0, emit_state):
    (qb_ref, ff_ref, fb_ref, ib_ref, gb_ref, lbp_ref, gw_ref,
     ef_ref, mf_ref, eb_ref, mb_ref, ones_ref, sel_ref) = refs[:13]
    rest = list(refs[13:])
    s0_ref = rest.pop(0) if has_s0 else None
    r_ref = rest.pop(0)
    sout_ref = rest.pop(0) if emit_state else None
    of_scr, ob_scr, stf_scr, stb_scr = rest
    c = HGRN_CHUNK
    ones_b = ones_ref[...]
    sel_b = sel_ref[...]
    gw = gw_ref[...]

    for d, st_scr in enumerate((stf_scr, stb_scr)):
        if has_s0:
            st_scr[...] = s0_ref[d].T
        else:
            st_scr[...] = jnp.zeros((HGRN_VAL_DIM, HGRN_KEY_DIM), F32)

    def sweep(i, carry):
        slf = pl.ds(pl.multiple_of(i * c, c), c)
        slb = pl.ds(pl.multiple_of((n_chunks - 1 - i) * c, c), c)
        o_f, st_f = _hgrn_chunk(qb_ref[slf, :], ff_ref[slf, :], ib_ref[slf, :], lbp_ref[0:1, :], lbp_ref[1:2, :],
                                ef_ref, mf_ref, ones_b, sel_b, stf_scr[...], True)
        o_b, st_b = _hgrn_chunk(qb_ref[slb, :], fb_ref[slb, :], ib_ref[slb, :], lbp_ref[2:3, :], lbp_ref[3:4, :],
                                eb_ref, mb_ref, ones_b, sel_b, stb_scr[...], False)
        of_scr[slf, :] = o_f
        ob_scr[slb, :] = o_b
        stf_scr[...] = st_f
        stb_scr[...] = st_b
        return carry

    lax.fori_loop(0, n_chunks, sweep, 0, unroll=2)
    if emit_state:
        sout_ref[0] = stf_scr[...].T
        sout_ref[1] = stb_scr[...].T

    def readout(ci, carry):
        sl = pl.ds(pl.multiple_of(ci * c, c), c)
        tot = of_scr[sl, :] + ob_scr[sl, :]
        ms = jnp.mean(tot * tot, axis=-1, keepdims=True)
        r_ref[sl, :] = (tot * lax.rsqrt(ms + NORM_EPS) * gw) * _silu(gb_ref[sl, :])
        return carry

    lax.fori_loop(0, n_chunks, readout, 0, unroll=2)


def _hgrn(proj, lbp, gw, consts, n_b, t_len, row_block0, n_rows, s0=None, s0_layer=0, emit_state=False):
    n_chunks = t_len // HGRN_CHUNK
    e_f, m_f, e_b, m_b, ones_b, sel_b = consts

    def col(off):
        return pl.BlockSpec((t_len, LANE), lambda b, h: (row_block0 + b, COL_HGRN + off * HGRN_HEADS + h))

    def whole(a):
        nd = a.ndim
        return pl.BlockSpec(a.shape, lambda b, h: (0,) * nd)

    in_specs = [col(0), col(1), col(2), col(3), col(4),
                pl.BlockSpec((None, 4, LANE), lambda b, h: (h, 0, 0)),
                pl.BlockSpec((1, HGRN_VAL_DIM), lambda b, h: (0, 0)),
                whole(e_f), whole(m_f), whole(e_b), whole(m_b), whole(ones_b), whole(sel_b)]
    args = [proj, proj, proj, proj, proj, lbp, gw.reshape(1, HGRN_VAL_DIM), e_f, m_f, e_b, m_b, ones_b, sel_b]
    if s0 is not None:
        in_specs.append(pl.BlockSpec((None, None, 2, None, HGRN_KEY_DIM, HGRN_VAL_DIM),
                                     lambda b, h: (b, s0_layer, 0, h, 0, 0)))
        args.append(s0)
    out_shape = [jax.ShapeDtypeStruct((n_rows, HGRN_WIDTH), F32)]
    out_specs = [pl.BlockSpec((t_len, LANE), lambda b, h: (b, h))]
    if emit_state:
        out_shape.append(jax.ShapeDtypeStruct((n_b, 2, HGRN_HEADS, HGRN_KEY_DIM, HGRN_VAL_DIM), F32))
        out_specs.append(pl.BlockSpec((None, 2, None, HGRN_KEY_DIM, HGRN_VAL_DIM),
                                      lambda b, h: (b, 0, h, 0, 0)))
    return pl.pallas_call(
        functools.partial(_hgrn_kernel, n_chunks=n_chunks, has_s0=s0 is not None, emit_state=emit_state),
        out_shape=out_shape,
        grid=(n_b, HGRN_HEADS),
        in_specs=in_specs,
        out_specs=out_specs,
        scratch_shapes=[pltpu.VMEM((t_len, HGRN_VAL_DIM), F32), pltpu.VMEM((t_len, HGRN_VAL_DIM), F32),
                        pltpu.VMEM((HGRN_VAL_DIM, HGRN_KEY_DIM), F32), pltpu.VMEM((HGRN_VAL_DIM, HGRN_KEY_DIM), F32)],
        compiler_params=_params("parallel", "parallel"),
        name="hgrn_lat" if s0 is not None else "hgrn_ctx",
    )(*args)


def _outproj_kernel(*refs, n_parts, n_x, rows):
    pair_refs = refs[:2 * n_parts]
    w_refs = refs[2 * n_parts:3 * n_parts]
    x_refs = refs[3 * n_parts:3 * n_parts + n_x]
    mod_ref, o_ref = refs[3 * n_parts + n_x:]
    acc = None
    for p in range(n_parts):
        a = rows.stream_tile(pair_refs[2 * p:2 * p + 2]).astype(BF16)
        d = _dot(a, w_refs[p][...])
        acc = d if acc is None else acc + d
    o_ref[...] = rows.stream_tile(x_refs) + mod_ref[2:3, :] * acc


def _outproj(rows, pairs, weights, x, mods):
    in_specs, args = [], []
    for pair in pairs:
        specs, arrs = rows.stream_specs(pair)
        in_specs += specs
        args += arrs
    in_specs += [pl.BlockSpec(w.shape, lambda i: (0, 0)) for w in weights]
    x_specs, x_args = rows.stream_specs(x)
    return pl.pallas_call(
        functools.partial(_outproj_kernel, n_parts=len(pairs), n_x=len(x_args), rows=rows),
        out_shape=jax.ShapeDtypeStruct((rows.n, D_MODEL), F32),
        grid=(rows.n_tiles,),
        in_specs=in_specs + x_specs + [rows.mod_spec()],
        out_specs=pl.BlockSpec((TM, D_MODEL), lambda i: (i, 0)),
        compiler_params=_params("arbitrary"),
        name="outproj",
    )(*args, *weights, *x_args, mods)


def _mla_proj_kernel(x_ref, mod_ref, nw_ref, wd_ref, qnw_ref, kvnw_ref, wuq_ref, cos_ref, sin_ref,
                     q_ref, ckv_ref, krb_ref):
    nf = MLA_ROPE_DIM // 4
    h = _normmod(x_ref[...], nw_ref[...], mod_ref[0:1, :], mod_ref[1:2, :]).astype(BF16)
    t1 = _dot(h, wd_ref[...])
    qd = t1[:, :MLA_Q_RANK]
    kvd = t1[:, MLA_Q_RANK:MLA_Q_RANK + MLA_KV_RANK]
    cos = cos_ref[...]
    sin = sin_ref[...]
    qn = qd * lax.rsqrt(jnp.mean(qd * qd, axis=-1, keepdims=True) + NORM_EPS) * qnw_ref[...]
    ckv_ref[...] = kvd * lax.rsqrt(jnp.mean(kvd * kvd, axis=-1, keepdims=True) + NORM_EPS) * kvnw_ref[...]
    krb_ref[...] = _rope(t1[:, MLA_Q_RANK + MLA_KV_RANK:], cos, sin, nf)
    qb = qn.astype(BF16)
    scale = MLA_QK_DIM ** -0.5
    for hd in range(MLA_HEADS):
        qh = _dot(qb, wuq_ref[:, hd * LANE:(hd + 1) * LANE])
        q_ref[:, hd * LANE:(hd + 1) * LANE] = (_rope(qh, cos, sin, nf) * scale).astype(BF16)


def _mla_proj(rows, x, mods, nw, wd, qnw, kvnw, wuq, cos, sin):
    return pl.pallas_call(
        _mla_proj_kernel,
        out_shape=[jax.ShapeDtypeStruct((rows.n, MLA_HEADS * LANE), BF16),
                   jax.ShapeDtypeStruct((rows.n, MLA_KV_RANK), F32),
                   jax.ShapeDtypeStruct((rows.n, LANE), F32)],
        grid=(rows.n_tiles,),
        in_specs=[
            pl.BlockSpec((TM, D_MODEL), lambda i: (i, 0)),
            rows.mod_spec(),
            pl.BlockSpec((1, D_MODEL), lambda i: (0, 0)),
            pl.BlockSpec((D_MODEL, MLA_DOWN_WIDTH), lambda i: (0, 0)),
            pl.BlockSpec((1, MLA_Q_RANK), lambda i: (0, 0)),
            pl.BlockSpec((1, MLA_KV_RANK), lambda i: (0, 0)),
            pl.BlockSpec((MLA_Q_RANK, MLA_HEADS * LANE), lambda i: (0, 0)),
            pl.BlockSpec((TM, LANE), lambda i: (rows.pos_block(i), 0)),
            pl.BlockSpec((TM, LANE), lambda i: (rows.pos_block(i), 0)),
        ],
        out_specs=[pl.BlockSpec((TM, MLA_HEADS * LANE), lambda i: (i, 0)),
                   pl.BlockSpec((TM, MLA_KV_RANK), lambda i: (i, 0)),
                   pl.BlockSpec((TM, LANE), lambda i: (i, 0))],
        compiler_params=_params("parallel"),
        name="mla_proj",
    )(x, mods, nw.reshape(1, D_MODEL), wd, qnw.reshape(1, MLA_Q_RANK), kvnw.reshape(1, MLA_KV_RANK), wuq, cos, sin)


MLA_TQ = 256
MLA_HPS = 4


def _mla_attn_kernel(q_ref, ckv_ref, krb_ref, wkv_ref, o_ref, k_scr, v_scr, *, t_len):
    tq = min(MLA_TQ, t_len)
    ckv = ckv_ref[...].astype(BF16)
    krb = krb_ref[...]
    for hh in range(MLA_HPS):
        kvh = _dot(ckv, wkv_ref[hh])
        k_scr[hh] = (kvh[:, :LANE] + krb).astype(BF16)
        v_scr[hh] = kvh[:, LANE:].astype(BF16)

    def body(ti, carry):
        sl = pl.ds(pl.multiple_of(ti * tq, tq), tq)
        for pair in range(MLA_HPS // 2):
            o = None
            for hh in (2 * pair, 2 * pair + 1):
                s = _dot_nt(q_ref[sl, hh * LANE:(hh + 1) * LANE], k_scr[hh])
                m = jnp.max(s, axis=-1, keepdims=True)
                p = jnp.exp(s - m)
                den = jnp.sum(p, axis=-1, keepdims=True)
                oh = _dot(p.astype(BF16), v_scr[hh]) / den
                o = oh if o is None else o + oh
            o_ref[sl, pair * LANE:(pair + 1) * LANE] = o
        return carry

    lax.fori_loop(0, t_len // tq, body, 0)


def _mla_attn(q, ckv_all, krb_all, wkv, n_b, t_len, q_row_block0, n_rows):
    s_len = ckv_all.shape[1]
    return pl.pallas_call(
        functools.partial(_mla_attn_kernel, t_len=t_len),
        out_shape=jax.ShapeDtypeStruct((n_rows, MLA_HEADS * MLA_V_DIM), F32),
        grid=(n_b, MLA_HEADS // MLA_HPS),
        in_specs=[
            pl.BlockSpec((t_len, MLA_HPS * LANE), lambda b, hp: (q_row_block0 + b, hp)),
            pl.BlockSpec((None, s_len, MLA_KV_RANK), lambda b, hp: (b, 0, 0)),
            pl.BlockSpec((None, s_len, LANE), lambda b, hp: (b, 0, 0)),
            pl.BlockSpec((MLA_HPS, MLA_KV_RANK, 2 * LANE), lambda b, hp: (hp, 0, 0)),
        ],
        out_specs=pl.BlockSpec((t_len, MLA_HPS // 2 * LANE), lambda b, hp: (b, hp)),
        scratch_shapes=[pltpu.VMEM((MLA_HPS, s_len, LANE), BF16), pltpu.VMEM((MLA_HPS, s_len, LANE), BF16)],
        compiler_params=_params("parallel", "arbitrary"),
        name="mla_attn",
    )(q, ckv_all, krb_all, wkv)


META_E1, META_E2, META_W1, META_W2, META_R1, META_R2 = range(6)


def _router_kernel(x_ref, mod_ref, nw_ref, wr_ref, br_ref, tri_ref, xn_ref, meta_ref, cnt_ref, base_scr):
    @pl.when(pl.program_id(0) == 0)
    def _():
        base_scr[...] = jnp.zeros(base_scr.shape, F32)

    xn = _normmod(x_ref[...], nw_ref[...], mod_ref[3:4, :], mod_ref[4:5, :])
    xn_ref[...] = xn
    logits = _dot_f32ish(xn, wr_ref[...]) + br_ref[...]
    lane = lax.broadcasted_iota(I32, logits.shape, 1).astype(F32)
    far = float(LANE)

    def first_argmax(vals, vmax):
        return jnp.min(jnp.where(vals == vmax, lane, far), axis=-1, keepdims=True)

    gl = jnp.where(lane < MOE_GROUPS, logits, NEG_BIG)
    gmax = jnp.max(gl, axis=-1, keepdims=True)
    g_w = 1.0 / jnp.sum(jnp.exp(gl - gmax), axis=-1, keepdims=True)
    g_idx = first_argmax(gl, gmax)
    e_lo = MOE_GROUPS + MOE_EPG * g_idx
    el = jnp.where((lane >= e_lo) & (lane < e_lo + MOE_EPG), logits, NEG_BIG)
    m1 = jnp.max(el, axis=-1, keepdims=True)
    i1 = first_argmax(el, m1)
    el2 = jnp.where(lane == i1, NEG_BIG, el)
    m2 = jnp.max(el2, axis=-1, keepdims=True)
    i2 = first_argmax(el2, m2)
    esum = jnp.sum(jnp.exp(el - m1), axis=-1, keepdims=True)
    p1 = 1.0 / esum
    p2 = jnp.exp(m2 - m1) / esum
    w1 = g_w * (p1 / (p1 + p2))
    w2 = g_w * (p2 / (p1 + p2))
    e1 = i1 - MOE_GROUPS
    e2 = i2 - MOE_GROUPS

    oh1 = lane == e1
    oh2 = lane == e2
    oh = jnp.where(oh1 | oh2, 1.0, 0.0)
    before = _dot(tri_ref[...], oh.astype(BF16)) + base_scr[0:1, :]
    r1 = jnp.sum(jnp.where(oh1, before, 0.0), axis=-1, keepdims=True)
    r2 = jnp.sum(jnp.where(oh2, before, 0.0), axis=-1, keepdims=True)
    base_scr[...] = base_scr[...] + jnp.sum(oh, axis=0, keepdims=True)
    cnt_ref[...] = base_scr[...]

    meta = jnp.zeros(logits.shape, F32)
    for slot, val in ((META_E1, e1), (META_E2, e2), (META_W1, w1), (META_W2, w2), (META_R1, r1), (META_R2, r2)):
        meta = jnp.where(lane == slot, val, meta)
    meta_ref[...] = meta


def _router(rows, x, mods, nw, wr, br, tri):
    return pl.pallas_call(
        _router_kernel,
        out_shape=[jax.ShapeDtypeStruct((rows.n, D_MODEL), F32),
                   jax.ShapeDtypeStruct((rows.n, LANE), F32),
                   jax.ShapeDtypeStruct((8, LANE), F32)],
        grid=(rows.n_tiles,),
        in_specs=[
            pl.BlockSpec((TM, D_MODEL), lambda i: (i, 0)),
            rows.mod_spec(),
            pl.BlockSpec((1, D_MODEL), lambda i: (0, 0)),
            pl.BlockSpec((D_MODEL, LANE), lambda i: (0, 0)),
            pl.BlockSpec((1, LANE), lambda i: (0, 0)),
            pl.BlockSpec((TM, TM), lambda i: (0, 0)),
        ],
        out_specs=[pl.BlockSpec((TM, D_MODEL), lambda i: (i, 0)),
                   pl.BlockSpec((TM, LANE), lambda i: (i, 0)),
                   pl.BlockSpec((8, LANE), lambda i: (0, 0))],
        scratch_shapes=[pltpu.VMEM((8, LANE), F32)],
        compiler_params=_params("arbitrary"),
        name="moe_router",
    )(x, mods, nw.reshape(1, D_MODEL), wr, br, tri)


def _row_copy(src, src_row, dst, dst_row, sem):
    return pltpu.make_async_copy(src.at[pl.ds(src_row, 1), :], dst.at[pl.ds(dst_row, 1), :], sem)


def _dispatch_kernel(fill_ref, pos_ref, xn_ref, xs_hbm, zero_scr, sem, fill_sem, *, n_tiles):
    @pl.when(pl.program_id(0) == 0)
    def _():
        zero_scr[...] = jnp.zeros(zero_scr.shape, F32)

        def fill_copy(t):
            return pltpu.make_async_copy(zero_scr, xs_hbm.at[pl.ds(pl.multiple_of(t * TME, TME), TME), :], fill_sem)

        def fill_start(t, carry):
            @pl.when(fill_ref[t] != 0)
            def _():
                fill_copy(t).start()
            return carry

        def fill_wait(t, carry):
            @pl.when(fill_ref[t] != 0)
            def _():
                fill_copy(t).wait()
            return carry

        lax.fori_loop(0, n_tiles, fill_start, 0)
        lax.fori_loop(0, n_tiles, fill_wait, 0)

    def start(r, carry):
        for k in range(2):
            _row_copy(xn_ref, r, xs_hbm, pos_ref[0, 2 * r + k], sem).start(priority=k)
        return carry

    lax.fori_loop(0, TM, start, 0, unroll=DMA_UNROLL)
    for _ in range(2):
        pltpu.make_async_copy(xn_ref, xs_hbm.at[pl.ds(0, TM), :], sem).wait()


def _dispatch(rows, tile_fill, pos, xn, n_tiles):
    return pl.pallas_call(
        functools.partial(_dispatch_kernel, n_tiles=n_tiles),
        out_shape=jax.ShapeDtypeStruct((n_tiles * TME, D_MODEL), F32),
        grid_spec=pltpu.PrefetchScalarGridSpec(
            num_scalar_prefetch=1,
            grid=(rows.n_tiles,),
            in_specs=[
                pl.BlockSpec((None, 1, 2 * TM), lambda i, fill: (i, 0, 0), memory_space=pltpu.SMEM),
                pl.BlockSpec((TM, D_MODEL), lambda i, fill: (i, 0)),
            ],
            out_specs=pl.BlockSpec(memory_space=pl.ANY),
            scratch_shapes=[pltpu.VMEM((TME, D_MODEL), F32), pltpu.SemaphoreType.DMA(()),
                            pltpu.SemaphoreType.DMA(())],
        ),
        compiler_params=_params("arbitrary"),
        name="moe_dispatch",
    )(tile_fill, pos, xn)


def _ffn_kernel(te_ref, nv_ref, x_ref, wg_ref, wu_ref, wd_ref, y_ref, wg_b, wu_b, wd_b):
    t = pl.program_id(0)
    valid = t < nv_ref[0]
    new_expert = (t == 0) | (te_ref[t] != te_ref[jnp.maximum(t - 1, 0)])

    @pl.when(valid & new_expert)
    def _():
        wg_b[...] = wg_ref[...].astype(BF16)
        wu_b[...] = wu_ref[...].astype(BF16)
        wd_b[...] = wd_ref[...].astype(BF16)

    @pl.when(valid)
    def _():
        x = x_ref[...].astype(BF16)
        a = _silu(_dot(x, wg_b[...])) * _dot(x, wu_b[...])
        y_ref[...] = _dot(a.astype(BF16), wd_b[...])

    @pl.when(jnp.logical_not(valid))
    def _():
        y_ref[...] = jnp.zeros(y_ref.shape, F32)


def _ffn(tile_expert, n_valid, xs, w_gate, w_up, w_down, layer, n_tiles):
    def xmap(t, te, nv):
        return (jnp.minimum(t, nv[0] - 1), 0)

    def wmap(t, te, nv):
        return (layer, te[t], 0, 0)

    return pl.pallas_call(
        _ffn_kernel,
        out_shape=jax.ShapeDtypeStruct((n_tiles * TME, D_MODEL), F32),
        grid_spec=pltpu.PrefetchScalarGridSpec(
            num_scalar_prefetch=2,
            grid=(n_tiles,),
            in_specs=[
                pl.BlockSpec((TME, D_MODEL), xmap),
                pl.BlockSpec((None, None, D_MODEL, MOE_HIDDEN), wmap),
                pl.BlockSpec((None, None, D_MODEL, MOE_HIDDEN), wmap),
                pl.BlockSpec((None, None, MOE_HIDDEN, D_MODEL), wmap),
            ],
            out_specs=pl.BlockSpec((TME, D_MODEL), lambda t, te, nv: (t, 0)),
            scratch_shapes=[pltpu.VMEM((D_MODEL, MOE_HIDDEN), BF16),
                            pltpu.VMEM((D_MODEL, MOE_HIDDEN), BF16),
                            pltpu.VMEM((MOE_HIDDEN, D_MODEL), BF16)],
        ),
        compiler_params=_params("arbitrary"),
        name="moe_ffn",
    )(tile_expert, n_valid, xs, w_gate, w_up, w_down)


def _combine_kernel(pos_ref, x_ref, meta_ref, mod_ref, fnw_ref, ys_hbm, *rest, final, ctx_tiles):
    if final:
        o_ctx_ref, o_lat_ref, buf0, buf1, sem = rest
    else:
        o_ref, buf0, buf1, sem = rest
    bufs = (buf0, buf1)

    def start(r, carry):
        for k in range(2):
            _row_copy(ys_hbm, pos_ref[0, 2 * r + k], bufs[k], r, sem).start(priority=k)
        return carry

    lax.fori_loop(0, TM, start, 0, unroll=DMA_UNROLL)
    for k in range(2):
        pltpu.make_async_copy(ys_hbm.at[pl.ds(0, TM), :], bufs[k], sem).wait()
    meta = meta_ref[...]
    y = meta[:, META_W1:META_W1 + 1] * buf0[...] + meta[:, META_W2:META_W2 + 1] * buf1[...]
    xo = x_ref[...] + mod_ref[5:6, :] * y
    if not final:
        o_ref[...] = xo
        return
    xo = xo * lax.rsqrt(jnp.mean(xo * xo, axis=-1, keepdims=True) + NORM_EPS) * fnw_ref[...]
    is_ctx = pl.program_id(0) < ctx_tiles

    @pl.when(is_ctx)
    def _():
        o_ctx_ref[...] = xo

    @pl.when(jnp.logical_not(is_ctx))
    def _():
        o_lat_ref[...] = xo


def _combine(rows, pos, x, meta, mods, fnw, ys, final):
    ct = rows.ctx_tiles
    if final:
        out_shape = [jax.ShapeDtypeStruct((rows.nc, D_MODEL), F32), jax.ShapeDtypeStruct((rows.nl, D_MODEL), F32)]
        out_specs = [pl.BlockSpec((TM, D_MODEL), lambda i: (jnp.minimum(i, ct - 1), 0)),
                     pl.BlockSpec((TM, D_MODEL), lambda i: (jnp.maximum(i - ct, 0), 0))]
    else:
        out_shape = jax.ShapeDtypeStruct((rows.n, D_MODEL), F32)
        out_specs = pl.BlockSpec((TM, D_MODEL), lambda i: (i, 0))
    return pl.pallas_call(
        functools.partial(_combine_kernel, final=final, ctx_tiles=ct),
        out_shape=out_shape,
        grid=(rows.n_tiles,),
        in_specs=[
            pl.BlockSpec((None, 1, 2 * TM), lambda i: (i, 0, 0), memory_space=pltpu.SMEM),
            pl.BlockSpec((TM, D_MODEL), lambda i: (i, 0)),
            pl.BlockSpec((TM, LANE), lambda i: (i, 0)),
            rows.mod_spec(),
            pl.BlockSpec((1, D_MODEL), lambda i: (0, 0)),
            pl.BlockSpec(memory_space=pl.ANY),
        ],
        out_specs=out_specs,
        scratch_shapes=[pltpu.VMEM((TM, D_MODEL), F32), pltpu.VMEM((TM, D_MODEL), F32),
                        pltpu.SemaphoreType.DMA(())],
        compiler_params=_params("arbitrary"),
        name="moe_combine",
    )(pos, x, meta, mods, fnw.reshape(1, D_MODEL), ys)


def _moe(rows, x, mods, nw, wr, br, tri, w_gate, w_up, w_down, layer, fnw, final):
    n_assign = 2 * rows.n
    n_tiles = n_assign // TME + MOE_EXPERTS
    xn, meta, cnt = _router(rows, x, mods, nw, wr, br, tri)

    counts = cnt[0, :MOE_EXPERTS].astype(I32)
    padded = ((counts + TME - 1) // TME) * TME
    ends = jnp.cumsum(padded)
    starts = ends - padded
    experts = jnp.arange(MOE_EXPERTS, dtype=I32)
    e = meta[:, META_E1:META_E2 + 1].astype(I32)
    rank = meta[:, META_R1:META_R2 + 1].astype(I32)
    start_of = jnp.sum(jnp.where(e[..., None] == experts, starts, 0), axis=-1)
    pos = (start_of + rank).reshape(rows.n_tiles, 1, 2 * TM)
    n_valid = ends[-1] // TME
    tile_first = jnp.arange(n_tiles, dtype=I32) * TME
    tile_start = jnp.minimum(tile_first, ends[-1] - TME)
    tile_expert = jnp.sum((ends[None, :] <= tile_start[:, None]).astype(I32), axis=1)
    tile_expert = jnp.minimum(tile_expert, MOE_EXPERTS - 1)
    tile_oh = tile_expert[:, None] == experts
    tile_rows = jnp.sum(jnp.where(tile_oh, counts + starts, 0), axis=1) - tile_start
    tile_fill = ((tile_first >= ends[-1]) | (tile_rows < TME)).astype(I32)

    xs = _dispatch(rows, tile_fill, pos, xn, n_tiles)
    ys = _ffn(tile_expert, n_valid.reshape(1).astype(I32), xs, w_gate, w_up, w_down, layer, n_tiles)
    return _combine(rows, pos, x, meta, mods, fnw, ys, final)


def _lower_bound_params(p):
    pr = jax.nn.softmax(p.astype(F32), axis=0)
    lb = jnp.cumsum(pr, axis=0) - pr[0:1]
    lb = jnp.clip(lb, 0.0, 1.0 - 1e-6)
    return jnp.maximum(lb, LOG_TINY), 1.0 - lb


def kernel(x_prompt, x_sample, c, cache_swa_k, cache_swa_v, state_hgrn, cache_mla_ckv, cache_mla_krope, c_ctx, mod_w, mod_b, norm1_w, norm2_w, final_norm_w, even_w_in, even_w_out, swa_sink, hgrn_lb_fwd, hgrn_lb_bwd, hgrn_gnorm_w, mla_w_dq, mla_qnorm_w, mla_w_uq, mla_w_dkv, mla_kvnorm_w, mla_w_ukv, mla_w_o, moe_router_group_w, moe_router_group_b, moe_router_expert_w, moe_router_expert_b, moe_w_gate, moe_w_up, moe_w_down):
    nc_b, nc_t, _ = x_prompt.shape
    nl_b, nl_t, _ = x_sample.shape
    rows = _Rows(nc_b, nc_t, nl_b, nl_t)
    past = cache_swa_k.shape[2]

    x = (x_prompt.reshape(rows.nc, D_MODEL), x_sample.reshape(rows.nl, D_MODEL))
    mod_rows = 16
    cvec = jnp.concatenate([c_ctx[None, :], c, jnp.zeros((mod_rows - 1 - nl_b, D_MODEL), F32)], axis=0)
    mods_all = _modulation(cvec, mod_w, mod_b).reshape(DEPTH, mod_rows, 6, D_MODEL)

    hconsts = _hgrn_consts()
    la_f, l1_f = _lower_bound_params(hgrn_lb_fwd)
    la_b, l1_b = _lower_bound_params(hgrn_lb_bwd)
    lbp_all = jnp.stack([la_f, l1_f, la_b, l1_b], axis=1).reshape(N_EVEN, 4, HGRN_HEADS, LANE).transpose(0, 2, 1, 3)
    swa_cos, swa_sin = _rope_tables(nl_t, SWA_HEAD_DIM, 0, LANE, 0)
    mla_cos, mla_sin = _rope_tables(nl_t, MLA_ROPE_DIM, MLA_NOPE_DIM, MLA_QK_DIM, TM)
    tri = jnp.asarray(np.tril(np.ones((TM, TM), np.float32), -1), BF16)

    new_k, new_v, new_s, new_ckv, new_kr = [], [], [], [], []
    for l in range(DEPTH):
        j = l // 2
        mods = mods_all[l]
        if l % 2 == 0:
            proj = _even_proj(rows, x, mods, norm1_w[l], even_w_in[j].astype(BF16))
            a_ctx = _swa_ctx(rows, proj, swa_sink[j])
            a_lat = _swa_lat(rows, proj, swa_sink[j],
                             cache_swa_k[:, j].reshape(nl_b, past, SWA_KV_WIDTH),
                             cache_swa_v[:, j].reshape(nl_b, past, SWA_KV_WIDTH), swa_cos, swa_sin)
            r_ctx, s_ctx = _hgrn(proj, lbp_all[j], hgrn_gnorm_w[j], hconsts, nc_b, nc_t, 0, rows.nc,
                                 emit_state=True)
            (r_lat,) = _hgrn(proj, lbp_all[j], hgrn_gnorm_w[j], hconsts, nl_b, nl_t, rows.nc // nl_t, rows.nl,
                             s0=state_hgrn, s0_layer=j)
            w_out = even_w_out[j].astype(BF16)
            x = _outproj(rows, [(a_ctx, a_lat), (r_ctx, r_lat)], [w_out[:SWA_WIDTH], w_out[SWA_WIDTH:]], x, mods)
            kv = proj[:rows.nc, SWA_WIDTH:SWA_WIDTH + 2 * SWA_KV_WIDTH]
            new_k.append(kv[:, :SWA_KV_WIDTH].reshape(nc_b, nc_t, SWA_KV_HEADS, SWA_HEAD_DIM))
            new_v.append(kv[:, SWA_KV_WIDTH:].reshape(nc_b, nc_t, SWA_KV_HEADS, SWA_HEAD_DIM))
            new_s.append(s_ctx)
        else:
            wd = jnp.zeros((D_MODEL, MLA_DOWN_WIDTH), F32)
            wd = wd.at[:, :MLA_Q_RANK].set(mla_w_dq[j])
            wd = wd.at[:, MLA_Q_RANK:MLA_Q_RANK + MLA_KV_RANK].set(mla_w_dkv[j][:, :MLA_KV_RANK])
            kr_lo = MLA_Q_RANK + MLA_KV_RANK + MLA_NOPE_DIM
            wd = wd.at[:, kr_lo:kr_lo + MLA_ROPE_DIM].set(mla_w_dkv[j][:, MLA_KV_RANK:])
            wuq = jnp.pad(mla_w_uq[j].reshape(MLA_Q_RANK, MLA_HEADS, MLA_QK_DIM),
                          ((0, 0), (0, 0), (0, LANE - MLA_QK_DIM))).reshape(MLA_Q_RANK, MLA_HEADS * LANE)
            wukv = mla_w_ukv[j].reshape(MLA_KV_RANK, MLA_HEADS, MLA_NOPE_DIM + MLA_V_DIM).transpose(1, 0, 2)
            wk = jnp.pad(wukv[..., :MLA_NOPE_DIM], ((0, 0), (0, 0), (0, LANE - MLA_NOPE_DIM)))
            wv_e = jnp.pad(wukv[..., MLA_NOPE_DIM:], ((0, 0), (0, 0), (0, LANE - MLA_V_DIM)))
            wv_o = jnp.pad(wukv[..., MLA_NOPE_DIM:], ((0, 0), (0, 0), (LANE - MLA_V_DIM, 0)))
            odd = (jnp.arange(MLA_HEADS) % 2 == 1)[:, None, None]
            wv = jnp.where(odd, wv_o, wv_e)
            q, ckv, krb = _mla_proj(rows, x, mods, norm1_w[l], wd.astype(BF16), mla_qnorm_w[j], mla_kvnorm_w[j],
                                    wuq.astype(BF16), mla_cos, mla_sin)
            wkv = jnp.concatenate([wk, wv], axis=-1).astype(BF16)
            ckv_c = ckv[:rows.nc].reshape(nc_b, nc_t, MLA_KV_RANK)
            krb_c = krb[:rows.nc].reshape(nc_b, nc_t, LANE)
            o_ctx = _mla_attn(q, ckv_c, krb_c, wkv, nc_b, nc_t, 0, rows.nc)
            cache_kr = jnp.pad(cache_mla_krope[:, j], ((0, 0), (0, 0), (MLA_NOPE_DIM, LANE - MLA_QK_DIM)))
            ckv_l = jnp.concatenate([cache_mla_ckv[:, j], ckv[rows.nc:].reshape(nl_b, nl_t, MLA_KV_RANK)], axis=1)
            krb_l = jnp.concatenate([cache_kr, krb[rows.nc:].reshape(nl_b, nl_t, LANE)], axis=1)
            o_lat = _mla_attn(q, ckv_l, krb_l, wkv, nl_b, nl_t, rows.nc // nl_t, rows.nl)
            x = _outproj(rows, [(o_ctx, o_lat)], [mla_w_o[j].astype(BF16)], x, mods)
            new_ckv.append(ckv_c)
            new_kr.append(krb_c[..., MLA_NOPE_DIM:MLA_QK_DIM])

        wr = jnp.zeros((D_MODEL, LANE), F32)
        wr = wr.at[:, :MOE_GROUPS].set(moe_router_group_w[l])
        wr = wr.at[:, MOE_GROUPS:MOE_GROUPS + MOE_EXPERTS].set(
            moe_router_expert_w[l].transpose(1, 0, 2).reshape(D_MODEL, MOE_EXPERTS))
        br = jnp.zeros((1, LANE), F32)
        br = br.at[0, :MOE_GROUPS].set(moe_router_group_b[l])
        br = br.at[0, MOE_GROUPS:MOE_GROUPS + MOE_EXPERTS].set(moe_router_expert_b[l].reshape(MOE_EXPERTS))
        x = _moe(rows, x, mods, norm2_w[l], wr, br, tri, moe_w_gate, moe_w_up, moe_w_down, l,
                 final_norm_w, final=(l == DEPTH - 1))

    y_prompt = x[0].reshape(nc_b, nc_t, D_MODEL)
    y_sample = x[1].reshape(nl_b, nl_t, D_MODEL)
    return (y_prompt, y_sample, jnp.stack(new_k, axis=1), jnp.stack(new_v, axis=1), jnp.stack(new_s, axis=1),
            jnp.stack(new_ckv, axis=1), jnp.stack(new_kr, axis=1))
```

```python
import functools
import math

import numpy as np
import jax
import jax.numpy as jnp
from jax import lax
from jax.experimental import pallas as pl
from jax.experimental.pallas import tpu as pltpu

F32, BF16, I32 = jnp.float32, jnp.bfloat16, jnp.int32

D_MODEL = 1024
DEPTH = 4
GRID_W = 64
ROPE_THETA = 10000.0
NORM_EPS = 1e-6
NEG_BIG = -1e30
LOG_TINY = 1e-30
N_EVEN = (DEPTH + 1) // 2
N_ODD = DEPTH // 2

SWA_HEADS = 8
SWA_KV_HEADS = 2
SWA_GROUP = SWA_HEADS // SWA_KV_HEADS
SWA_HEAD_DIM = 64
SWA_WIDTH = SWA_HEADS * SWA_HEAD_DIM
SWA_KV_WIDTH = SWA_KV_HEADS * SWA_HEAD_DIM
SWA_WINDOW = 128
SWA_BLOCK = 128

HGRN_HEADS = 4
HGRN_KEY_DIM = 128
HGRN_VAL_DIM = 128
HGRN_WIDTH = HGRN_HEADS * HGRN_KEY_DIM
HGRN_CHUNK = 128
HGRN_SUB = 8
HGRN_LEVELS = (8, 16, 32, 64)

EVEN_IN_WIDTH = SWA_WIDTH + 2 * SWA_KV_WIDTH + 5 * HGRN_WIDTH
LANE = 128
COL_Q, COL_K, COL_V = 0, SWA_WIDTH // LANE, (SWA_WIDTH + SWA_KV_WIDTH) // LANE
COL_HGRN = (SWA_WIDTH + 2 * SWA_KV_WIDTH) // LANE

MLA_HEADS = 16
MLA_Q_RANK = 384
MLA_KV_RANK = 256
MLA_NOPE_DIM = 64
MLA_ROPE_DIM = 32
MLA_V_DIM = 64
MLA_QK_DIM = MLA_NOPE_DIM + MLA_ROPE_DIM
MLA_DOWN_WIDTH = MLA_Q_RANK + MLA_KV_RANK + LANE

MOE_GROUPS = 4
MOE_EPG = 8
MOE_EXPERTS = MOE_GROUPS * MOE_EPG
MOE_HIDDEN = 256

TM = 512
TME = 512
DMA_UNROLL = 8
VMEM_LIMIT = 48 * 1024 * 1024


def _params(*sem):
    return pltpu.CompilerParams(dimension_semantics=sem, vmem_limit_bytes=VMEM_LIMIT)


def _dot(a, b):
    return jnp.dot(a, b, preferred_element_type=F32)


def _dot_nt(a, b):
    return lax.dot_general(a, b, (((1,), (1,)), ((), ())), preferred_element_type=F32)


def _split2(a):
    hi = a.astype(BF16)
    return hi, (a - hi.astype(F32)).astype(BF16)


def _dot_f32ish(a, b):
    ah, al = _split2(a)
    bh, bl = _split2(b)
    return _dot(ah, bh) + (_dot(ah, bl) + _dot(al, bh))


def _silu(x):
    return x * jax.nn.sigmoid(x)


def _normmod(x, nw, shift, scale):
    ms = jnp.mean(x * x, axis=-1, keepdims=True)
    return (x * lax.rsqrt(ms + NORM_EPS) * nw) * (1.0 + scale) + shift


def _mod_kernel(c_ref, w_ref, b_ref, o_ref):
    o_ref[...] = _dot_f32ish(_silu(c_ref[...]), w_ref[...]) + b_ref[...]


def _modulation(cvec, mod_w, mod_b):
    rows = cvec.shape[0]
    nb = 6 * D_MODEL // 1024
    return pl.pallas_call(
        _mod_kernel,
        out_shape=jax.ShapeDtypeStruct((DEPTH, rows, 6 * D_MODEL), F32),
        grid=(DEPTH, nb),
        in_specs=[
            pl.BlockSpec((rows, D_MODEL), lambda l, n: (0, 0)),
            pl.BlockSpec((None, D_MODEL, 1024), lambda l, n: (l, 0, n)),
            pl.BlockSpec((None, 1, 1024), lambda l, n: (l, 0, n)),
        ],
        out_specs=pl.BlockSpec((None, rows, 1024), lambda l, n: (l, 0, n)),
        compiler_params=_params("parallel", "parallel"),
        name="modulation",
    )(cvec, mod_w, mod_b.reshape(DEPTH, 1, 6 * D_MODEL))


class _Rows:
    def __init__(self, nc_b, nc_t, nl_b, nl_t):
        self.nc_b, self.nc_t, self.nl_b, self.nl_t = nc_b, nc_t, nl_b, nl_t
        self.nc = nc_b * nc_t
        self.nl = nl_b * nl_t
        self.n = self.nc + self.nl
        assert self.nc % TM == 0 and nl_t % TM == 0 and self.nc % nl_t == 0
        self.ctx_tiles = self.nc // TM
        self.tiles_per_lat = nl_t // TM
        self.n_tiles = self.n // TM

    def mod_row(self, i):
        return jnp.where(i < self.ctx_tiles, 0, 1 + (i - self.ctx_tiles) // self.tiles_per_lat)

    def mod_spec(self):
        return pl.BlockSpec((None, 6, D_MODEL), lambda i: (self.mod_row(i), 0, 0))

    def stream_specs(self, x):
        ct = self.ctx_tiles
        if not isinstance(x, tuple):
            return [pl.BlockSpec((TM, x.shape[1]), lambda i: (i, 0))], [x]
        x_ctx, x_lat = x
        return ([pl.BlockSpec((TM, x_ctx.shape[1]), lambda i: (jnp.minimum(i, ct - 1), 0)),
                 pl.BlockSpec((TM, x_lat.shape[1]), lambda i: (jnp.maximum(i - ct, 0), 0))], [x_ctx, x_lat])

    def stream_tile(self, refs):
        if len(refs) == 1:
            return refs[0][...]
        return jnp.where(pl.program_id(0) < self.ctx_tiles, refs[0][...], refs[1][...])

    def pos_block(self, i):
        return jnp.where(i < self.ctx_tiles, 0, 1 + (i - self.ctx_tiles) % self.tiles_per_lat)


def _even_proj_kernel(*refs, rows):
    mod_ref, nw_ref, w_ref, o_ref = refs[-4:]
    x = rows.stream_tile(refs[:-4])
    h = _normmod(x, nw_ref[...], mod_ref[0:1, :], mod_ref[1:2, :]).astype(BF16)
    step = 256
    for c in range(EVEN_IN_WIDTH // step):
        o_ref[:, c * step:(c + 1) * step] = _dot(h, w_ref[:, c * step:(c + 1) * step])


def _even_proj(rows, x, mods, nw, w_in_bf16):
    x_specs, x_args = rows.stream_specs(x)
    return pl.pallas_call(
        functools.partial(_even_proj_kernel, rows=rows),
        out_shape=jax.ShapeDtypeStruct((rows.n, EVEN_IN_WIDTH), F32),
        grid=(rows.n_tiles,),
        in_specs=x_specs + [
            rows.mod_spec(),
            pl.BlockSpec((1, D_MODEL), lambda i: (0, 0)),
            pl.BlockSpec((D_MODEL, EVEN_IN_WIDTH), lambda i: (0, 0)),
        ],
        out_specs=pl.BlockSpec((TM, EVEN_IN_WIDTH), lambda i: (i, 0)),
        compiler_params=_params("arbitrary"),
        name="even_proj",
    )(*x_args, mods, nw.reshape(1, D_MODEL), w_in_bf16)


def _rope_tables(t_len, rot_dim, lane_lo, lane_hi, lead_rows):
    half = rot_dim // 2
    nf = half // 2
    lane = np.arange(LANE)
    d = (lane - lane_lo) % rot_dim
    active = (lane >= lane_lo) & (lane < lane_hi)
    use_col = d >= half
    fidx = d % nf
    first = (d % half) < nf
    pos = jnp.arange(t_len)
    row = (pos // GRID_W).astype(F32)
    col = (pos % GRID_W).astype(F32)
    inv = jnp.exp(-math.log(ROPE_THETA) * jnp.arange(nf, dtype=F32) / nf)
    p = jnp.where(jnp.asarray(use_col)[None, :], col[:, None], row[:, None])
    ang = p * inv[jnp.asarray(fidx)][None, :]
    act = jnp.asarray(active)[None, :]
    cos = jnp.where(act, jnp.cos(ang), 1.0)
    sin = jnp.where(act, jnp.sin(ang), 0.0)
    sin = jnp.where(jnp.asarray(first)[None, :], -sin, sin)
    if lead_rows:
        cos = jnp.concatenate([jnp.ones((lead_rows, LANE), F32), cos], axis=0)
        sin = jnp.concatenate([jnp.zeros((lead_rows, LANE), F32), sin], axis=0)
    return cos, sin


def _rope(x, cos, sin, nf):
    lane = lax.broadcasted_iota(I32, x.shape, 1)
    up = pltpu.roll(x, LANE - nf, axis=1)
    dn = pltpu.roll(x, nf, axis=1)
    partner = jnp.where((lane & nf) == 0, up, dn)
    return x * cos + partner * sin


def _swa_ctx_kernel(sink_ref, q_ref, k_ref, v_ref, o_ref):
    scale = SWA_HEAD_DIM ** -0.5
    k = k_ref[...]
    v = v_ref[...]
    for h in range(SWA_HEADS):
        kv = h // SWA_GROUP
        lo, klo = h * SWA_HEAD_DIM, kv * SWA_HEAD_DIM
        q = q_ref[:, lo:lo + SWA_HEAD_DIM].astype(BF16)
        kk = k[:, klo:klo + SWA_HEAD_DIM].astype(BF16)
        vv = v[:, klo:klo + SWA_HEAD_DIM].astype(BF16)
        s = _dot_nt(q, kk) * scale
        sk = sink_ref[h]
        m = jnp.maximum(jnp.max(s, axis=-1, keepdims=True), sk)
        p = jnp.exp(s - m)
        den = jnp.sum(p, axis=-1, keepdims=True) + jnp.exp(sk - m)
        o_ref[:, lo:lo + SWA_HEAD_DIM] = _dot(p.astype(BF16), vv) / den


def _swa_ctx(rows, proj, sink):
    t = rows.nc_t
    return pl.pallas_call(
        _swa_ctx_kernel,
        out_shape=jax.ShapeDtypeStruct((rows.nc, SWA_WIDTH), F32),
        grid=(rows.nc_b,),
        in_specs=[
            pl.BlockSpec(memory_space=pltpu.SMEM),
            pl.BlockSpec((t, SWA_WIDTH), lambda b: (b, COL_Q)),
            pl.BlockSpec((t, LANE), lambda b: (b, COL_K)),
            pl.BlockSpec((t, LANE), lambda b: (b, COL_V)),
        ],
        out_specs=pl.BlockSpec((t, SWA_WIDTH), lambda b: (b, 0)),
        compiler_params=_params("parallel"),
        name="swa_ctx",
    )(sink, proj, proj, proj)


def _lane_fold(x, op):
    out = x[:, :LANE]
    for i in range(1, x.shape[1] // LANE):
        out = op(out, x[:, i * LANE:(i + 1) * LANE])
    return out


def _swa_lat_kernel(sink_ref, q_ref, k_ref, v_ref, kc_ref, vc_ref, cos_ref, sin_ref, o_ref,
                    kl_scr, vl_scr, kc_scr, vc_scr, *, n_blocks):
    scale = SWA_HEAD_DIM ** -0.5
    nf = SWA_HEAD_DIM // 4
    n = pl.program_id(1)

    @pl.when(n == 0)
    def _():
        kr = _rope(k_ref[...], cos_ref[...], sin_ref[...], nf)
        for kv in range(SWA_KV_HEADS):
            cols = slice(kv * SWA_HEAD_DIM, (kv + 1) * SWA_HEAD_DIM)
            kl_scr[kv] = kr[:, cols].astype(BF16)
            vl_scr[kv] = v_ref[:, cols].astype(BF16)
            kc_scr[kv] = kc_ref[:, cols].astype(BF16)
            vc_scr[kv] = vc_ref[:, cols].astype(BF16)

    q0 = pl.multiple_of(n * SWA_BLOCK, SWA_BLOCK)
    cq = cos_ref[pl.ds(q0, SWA_BLOCK), :]
    sq = sin_ref[pl.ds(q0, SWA_BLOCK), :]
    qs = [_rope(q_ref[:, g * LANE:(g + 1) * LANE], cq, sq, nf) for g in range(SWA_WIDTH // LANE)]

    qi = lax.broadcasted_iota(I32, (SWA_BLOCK, SWA_BLOCK), 0)
    r = lax.broadcasted_iota(I32, (SWA_BLOCK, SWA_BLOCK), 1)
    band, mb = [], []
    for off in (-1, 0, 1):
        blk = n + off
        valid = (blk >= 0) & (blk < n_blocks)
        st = pl.multiple_of(jnp.clip(blk, 0, n_blocks - 1) * SWA_BLOCK, SWA_BLOCK)
        band.append(pl.ds(st, SWA_BLOCK))
        rel = qi - r - off * SWA_BLOCK
        mb.append(jnp.where((jnp.abs(rel) <= SWA_WINDOW) & valid, 1.0, 0.0))
    mask = jnp.concatenate(mb, axis=1) > 0.5

    for kv in range(SWA_KV_HEADS):
        kl = jnp.concatenate([kl_scr[kv, sl, :] for sl in band], axis=0)
        vl = jnp.concatenate([vl_scr[kv, sl, :] for sl in band], axis=0)
        kc = kc_scr[kv]
        vc = vc_scr[kv]
        for g in range(SWA_GROUP):
            h = kv * SWA_GROUP + g
            lo = h * SWA_HEAD_DIM
            q = qs[lo // LANE][:, lo % LANE:lo % LANE + SWA_HEAD_DIM].astype(BF16)
            s_ctx = _dot_nt(q, kc) * scale
            s_loc = jnp.where(mask, _dot_nt(q, kl) * scale, NEG_BIG)
            sk = sink_ref[h]
            m = jnp.max(jnp.maximum(_lane_fold(s_ctx, jnp.maximum), _lane_fold(s_loc, jnp.maximum)),
                        axis=-1, keepdims=True)
            m = jnp.maximum(m, sk)
            p_ctx = jnp.exp(s_ctx - m)
            p_loc = jnp.exp(s_loc - m)
            den = jnp.sum(_lane_fold(p_ctx, jnp.add) + _lane_fold(p_loc, jnp.add), axis=-1, keepdims=True)
            den = den + jnp.exp(sk - m)
            o = _dot(p_ctx.astype(BF16), vc) + _dot(p_loc.astype(BF16), vl)
            o_ref[:, lo:lo + SWA_HEAD_DIM] = o / den


def _swa_lat(rows, proj, sink, k_ctx, v_ctx, cos, sin):
    t = rows.nl_t
    n_blocks = t // SWA_BLOCK
    q_base = rows.nc // SWA_BLOCK
    kv_base = rows.nc // t
    s_ctx = k_ctx.shape[1]
    return pl.pallas_call(
        functools.partial(_swa_lat_kernel, n_blocks=n_blocks),
        out_shape=jax.ShapeDtypeStruct((rows.nl, SWA_WIDTH), F32),
        grid=(rows.nl_b, n_blocks),
        in_specs=[
            pl.BlockSpec(memory_space=pltpu.SMEM),
            pl.BlockSpec((SWA_BLOCK, SWA_WIDTH), lambda b, n: (q_base + b * n_blocks + n, COL_Q)),
            pl.BlockSpec((t, LANE), lambda b, n: (kv_base + b, COL_K)),
            pl.BlockSpec((t, LANE), lambda b, n: (kv_base + b, COL_V)),
            pl.BlockSpec((None, s_ctx, LANE), lambda b, n: (b, 0, 0)),
            pl.BlockSpec((None, s_ctx, LANE), lambda b, n: (b, 0, 0)),
            pl.BlockSpec((t, LANE), lambda b, n: (0, 0)),
            pl.BlockSpec((t, LANE), lambda b, n: (0, 0)),
        ],
        out_specs=pl.BlockSpec((SWA_BLOCK, SWA_WIDTH), lambda b, n: (b * n_blocks + n, 0)),
        scratch_shapes=[pltpu.VMEM((SWA_KV_HEADS, t, SWA_HEAD_DIM), BF16),
                        pltpu.VMEM((SWA_KV_HEADS, t, SWA_HEAD_DIM), BF16),
                        pltpu.VMEM((SWA_KV_HEADS, s_ctx, SWA_HEAD_DIM), BF16),
                        pltpu.VMEM((SWA_KV_HEADS, s_ctx, SWA_HEAD_DIM), BF16)],
        compiler_params=_params("parallel", "arbitrary"),
        name="swa_lat",
    )(sink, proj, proj, proj, k_ctx, v_ctx, cos, sin)


def _hgrn_consts():
    c = HGRN_CHUNK
    t = np.arange(c)[:, None]
    u = np.arange(c)[None, :]
    tri_f = (u <= t).astype(np.float32)
    masks = []
    for m in HGRN_LEVELS:
        right = ((t // m) % 2) == 1
        masks.append(right & ((u // m) == (t // m) - 1))
    m_f = np.stack(masks).astype(np.float32)
    sel = np.kron(np.eye(c), np.ones((1, HGRN_SUB))).astype(np.float32)
    return (jnp.asarray(tri_f, BF16), jnp.asarray(m_f, F32),
            jnp.asarray(tri_f[::-1, ::-1], BF16), jnp.asarray(m_f[:, ::-1, ::-1], F32),
            jnp.ones((HGRN_KEY_DIM, LANE), BF16), jnp.asarray(sel, BF16))


def _hgrn_chunk(qb, fr, v, lbm, oml, tri_ref, m_ref, ones_b, sel_b, st, forward):
    c = HGRN_CHUNK
    nsub = c // HGRN_SUB
    q = _silu(qb) * (HGRN_KEY_DIM ** -0.5)
    f = lbm + oml * jax.nn.sigmoid(fr)
    lf = jnp.log(f)
    k = 1.0 - f

    hi = lf.astype(BF16)
    r1 = lf - hi.astype(F32)
    mid = r1.astype(BF16)
    lo = (r1 - mid.astype(F32)).astype(BF16)
    tri = tri_ref[...]
    b = _dot(tri, hi) + (_dot(tri, mid) + _dot(tri, lo))
    total = jnp.sum(lf, axis=0, keepdims=True)

    b3 = b.reshape(nsub, HGRN_SUB, HGRN_KEY_DIM)
    q3 = q.reshape(nsub, HGRN_SUB, HGRN_KEY_DIM)
    k3 = k.reshape(nsub, HGRN_SUB, HGRN_KEY_DIM)
    v3 = v.reshape(nsub, HGRN_SUB, HGRN_VAL_DIM)
    s_io = lax.broadcasted_iota(I32, (1, HGRN_SUB, HGRN_KEY_DIM), 1)
    ps = []
    for i in range(HGRN_SUB):
        keep = (s_io <= i) if forward else (s_io >= i)
        dec = jnp.exp(jnp.where(keep, b3[:, i:i + 1, :] - b3, NEG_BIG))
        ps.append((q3[:, i:i + 1, :] * dec) * k3)
    p = jnp.stack(ps, axis=1).reshape(nsub * HGRN_SUB * HGRN_SUB, HGRN_KEY_DIM)
    att = _dot(p.astype(BF16), ones_b)
    av = att.reshape(nsub, HGRN_SUB, HGRN_SUB, HGRN_VAL_DIM) * v3[:, None, :, :]
    o = _dot(sel_b, av.reshape(nsub * HGRN_SUB * HGRN_SUB, HGRN_VAL_DIM).astype(BF16))

    a = jnp.zeros((c, c), F32)
    for li, m in enumerate(HGRN_LEVELS):
        pieces = []
        for pair in range(c // (2 * m)):
            r = 2 * pair * m + (m - 1 if forward else m)
            pieces.append(jnp.broadcast_to(b[r:r + 1, :], (2 * m, HGRN_KEY_DIM)))
        bnd = pieces[0] if len(pieces) == 1 else jnp.concatenate(pieces, axis=0)
        fac = jnp.exp(-jnp.abs(b - bnd))
        a = a + m_ref[li] * _dot_nt((q * fac).astype(BF16), (k * fac).astype(BF16))
    vb = v.astype(BF16)
    o = o + _dot(a.astype(BF16), vb)

    o = o + _dot_nt((q * jnp.exp(b)).astype(BF16), st.astype(BF16))
    g = jnp.exp(total)
    kc = (k * jnp.exp(total - b)).astype(BF16)
    st_new = st * g + _dot(v.T.astype(BF16), kc)
    return o, st_new


def _hgrn_kernel(*refs, n_chunks, has_s0, emit_state):
    (qb_ref, ff_ref, fb_ref, ib_ref, gb_ref, lbp_ref, gw_ref,
     ef_ref, mf_ref, eb_ref, mb_ref, ones_ref, sel_ref) = refs[:13]
    rest = list(refs[13:])
    s0_ref = rest.pop(0) if has_s0 else None
    r_ref = rest.pop(0)
    sout_ref = rest.pop(0) if emit_state else None
    of_scr, ob_scr, stf_scr, stb_scr = rest
    c = HGRN_CHUNK
    ones_b = ones_ref[...]
    sel_b = sel_ref[...]
    gw = gw_ref[...]

    for d, st_scr in enumerate((stf_scr, stb_scr)):
        if has_s0:
            st_scr[...] = s0_ref[d].T
        else:
            st_scr[...] = jnp.zeros((HGRN_VAL_DIM, HGRN_KEY_DIM), F32)

    def sweep(i, carry):
        slf = pl.ds(pl.multiple_of(i * c, c), c)
        slb = pl.ds(pl.multiple_of((n_chunks - 1 - i) * c, c), c)
        o_f, st_f = _hgrn_chunk(qb_ref[slf, :], ff_ref[slf, :], ib_ref[slf, :], lbp_ref[0:1, :], lbp_ref[1:2, :],
                                ef_ref, mf_ref, ones_b, sel_b, stf_scr[...], True)
        o_b, st_b = _hgrn_chunk(qb_ref[slb, :], fb_ref[slb, :], ib_ref[slb, :], lbp_ref[2:3, :], lbp_ref[3:4, :],
                                eb_ref, mb_ref, ones_b, sel_b, stb_scr[...], False)
        of_scr[slf, :] = o_f
        ob_scr[slb, :] = o_b
        stf_scr[...] = st_f
        stb_scr[...] = st_b
        return carry

    lax.fori_loop(0, n_chunks, sweep, 0, unroll=min(4, n_chunks))
    if emit_state:
        sout_ref[0] = stf_scr[...].T
        sout_ref[1] = stb_scr[...].T

    def readout(ci, carry):
        sl = pl.ds(pl.multiple_of(ci * c, c), c)
        tot = of_scr[sl, :] + ob_scr[sl, :]
        ms = jnp.mean(tot * tot, axis=-1, keepdims=True)
        r_ref[sl, :] = (tot * lax.rsqrt(ms + NORM_EPS) * gw) * _silu(gb_ref[sl, :])
        return carry

    lax.fori_loop(0, n_chunks, readout, 0, unroll=2)


def _hgrn(proj, lbp, gw, consts, n_b, t_len, row_block0, n_rows, s0=None, s0_layer=0, emit_state=False):
    n_chunks = t_len // HGRN_CHUNK
    e_f, m_f, e_b, m_b, ones_b, sel_b = consts

    def col(off):
        return pl.BlockSpec((t_len, LANE), lambda b, h: (row_block0 + b, COL_HGRN + off * HGRN_HEADS + h))

    def whole(a):
        nd = a.ndim
        return pl.BlockSpec(a.shape, lambda b, h: (0,) * nd)

    in_specs = [col(0), col(1), col(2), col(3), col(4),
                pl.BlockSpec((None, 4, LANE), lambda b, h: (h, 0, 0)),
                pl.BlockSpec((1, HGRN_VAL_DIM), lambda b, h: (0, 0)),
                whole(e_f), whole(m_f), whole(e_b), whole(m_b), whole(ones_b), whole(sel_b)]
    args = [proj, proj, proj, proj, proj, lbp, gw.reshape(1, HGRN_VAL_DIM), e_f, m_f, e_b, m_b, ones_b, sel_b]
    if s0 is not None:
        in_specs.append(pl.BlockSpec((None, None, 2, None, HGRN_KEY_DIM, HGRN_VAL_DIM),
                                     lambda b, h: (b, s0_layer, 0, h, 0, 0)))
        args.append(s0)
    out_shape = [jax.ShapeDtypeStruct((n_rows, HGRN_WIDTH), F32)]
    out_specs = [pl.BlockSpec((t_len, LANE), lambda b, h: (b, h))]
    if emit_state:
        out_shape.append(jax.ShapeDtypeStruct((n_b, 2, HGRN_HEADS, HGRN_KEY_DIM, HGRN_VAL_DIM), F32))
        out_specs.append(pl.BlockSpec((None, 2, None, HGRN_KEY_DIM, HGRN_VAL_DIM),
                                      lambda b, h: (b, 0, h, 0, 0)))
    return pl.pallas_call(
        functools.partial(_hgrn_kernel, n_chunks=n_chunks, has_s0=s0 is not None, emit_state=emit_state),
        out_shape=out_shape,
        grid=(n_b, HGRN_HEADS),
        in_specs=in_specs,
        out_specs=out_specs,
        scratch_shapes=[pltpu.VMEM((t_len, HGRN_VAL_DIM), F32), pltpu.VMEM((t_len, HGRN_VAL_DIM), F32),
                        pltpu.VMEM((HGRN_VAL_DIM, HGRN_KEY_DIM), F32), pltpu.VMEM((HGRN_VAL_DIM, HGRN_KEY_DIM), F32)],
        compiler_params=_params("parallel", "parallel"),
        name="hgrn_lat" if s0 is not None else "hgrn_ctx",
    )(*args)


def _outproj_kernel(*refs, n_parts, n_x, rows):
    pair_refs = refs[:2 * n_parts]
    w_refs = refs[2 * n_parts:3 * n_parts]
    x_refs = refs[3 * n_parts:3 * n_parts + n_x]
    mod_ref, o_ref = refs[3 * n_parts + n_x:]
    acc = None
    for p in range(n_parts):
        a = rows.stream_tile(pair_refs[2 * p:2 * p + 2]).astype(BF16)
        d = _dot(a, w_refs[p][...])
        acc = d if acc is None else acc + d
    o_ref[...] = rows.stream_tile(x_refs) + mod_ref[2:3, :] * acc


def _outproj(rows, pairs, weights, x, mods):
    in_specs, args = [], []
    for pair in pairs:
        specs, arrs = rows.stream_specs(pair)
        in_specs += specs
        args += arrs
    in_specs += [pl.BlockSpec(w.shape, lambda i: (0, 0)) for w in weights]
    x_specs, x_args = rows.stream_specs(x)
    return pl.pallas_call(
        functools.partial(_outproj_kernel, n_parts=len(pairs), n_x=len(x_args), rows=rows),
        out_shape=jax.ShapeDtypeStruct((rows.n, D_MODEL), F32),
        grid=(rows.n_tiles,),
        in_specs=in_specs + x_specs + [rows.mod_spec()],
        out_specs=pl.BlockSpec((TM, D_MODEL), lambda i: (i, 0)),
        compiler_params=_params("arbitrary"),
        name="outproj",
    )(*args, *weights, *x_args, mods)


def _mla_proj_kernel(x_ref, mod_ref, nw_ref, wd_ref, qnw_ref, kvnw_ref, wuq_ref, wuqp_ref, cos_ref, sin_ref,
                     q_ref, ckv_ref, krb_ref):
    nf = MLA_ROPE_DIM // 4
    h = _normmod(x_ref[...], nw_ref[...], mod_ref[0:1, :], mod_ref[1:2, :]).astype(BF16)
    t1 = _dot(h, wd_ref[...])
    qd = t1[:, :MLA_Q_RANK]
    kvd = t1[:, MLA_Q_RANK:MLA_Q_RANK + MLA_KV_RANK]
    cos = cos_ref[...]
    sin = sin_ref[...]
    qn = qd * lax.rsqrt(jnp.mean(qd * qd, axis=-1, keepdims=True) + NORM_EPS) * qnw_ref[...]
    ckv_ref[...] = kvd * lax.rsqrt(jnp.mean(kvd * kvd, axis=-1, keepdims=True) + NORM_EPS) * kvnw_ref[...]
    krb_ref[...] = _rope(t1[:, MLA_Q_RANK + MLA_KV_RANK:], cos, sin, nf)
    qb = qn.astype(BF16)
    scale = MLA_QK_DIM ** -0.5
    cos2 = jnp.concatenate([cos, cos], axis=1)
    sin2 = jnp.concatenate([sin, sin], axis=1)
    for hp in range(MLA_HEADS // 2):
        cols = slice(hp * 2 * LANE, (hp + 1) * 2 * LANE)
        qh = _dot(qb, wuq_ref[:, cols])
        qp = _dot(qb, wuqp_ref[:, cols])
        q_ref[:, cols] = ((qh * cos2 + qp * sin2) * scale).astype(BF16)


def _mla_proj(rows, x, mods, nw, wd, qnw, kvnw, wuq, wuq_partner, cos, sin):
    return pl.pallas_call(
        _mla_proj_kernel,
        out_shape=[jax.ShapeDtypeStruct((rows.n, MLA_HEADS * LANE), BF16),
                   jax.ShapeDtypeStruct((rows.n, MLA_KV_RANK), F32),
                   jax.ShapeDtypeStruct((rows.n, LANE), F32)],
        grid=(rows.n_tiles,),
        in_specs=[
            pl.BlockSpec((TM, D_MODEL), lambda i: (i, 0)),
            rows.mod_spec(),
            pl.BlockSpec((1, D_MODEL), lambda i: (0, 0)),
            pl.BlockSpec((D_MODEL, MLA_DOWN_WIDTH), lambda i: (0, 0)),
            pl.BlockSpec((1, MLA_Q_RANK), lambda i: (0, 0)),
            pl.BlockSpec((1, MLA_KV_RANK), lambda i: (0, 0)),
            pl.BlockSpec((MLA_Q_RANK, MLA_HEADS * LANE), lambda i: (0, 0)),
            pl.BlockSpec((MLA_Q_RANK, MLA_HEADS * LANE), lambda i: (0, 0)),
            pl.BlockSpec((TM, LANE), lambda i: (rows.pos_block(i), 0)),
            pl.BlockSpec((TM, LANE), lambda i: (rows.pos_block(i), 0)),
        ],
        out_specs=[pl.BlockSpec((TM, MLA_HEADS * LANE), lambda i: (i, 0)),
                   pl.BlockSpec((TM, MLA_KV_RANK), lambda i: (i, 0)),
                   pl.BlockSpec((TM, LANE), lambda i: (i, 0))],
        compiler_params=_params("parallel"),
        name="mla_proj",
    )(x, mods, nw.reshape(1, D_MODEL), wd, qnw.reshape(1, MLA_Q_RANK), kvnw.reshape(1, MLA_KV_RANK), wuq, wuq_partner, cos, sin)


MLA_TQ = 256
MLA_HPS = 4


def _mla_attn_kernel(q_ref, ckv_ref, krb_ref, wkv_ref, o_ref, k_scr, v_scr, *, t_len):
    tq = min(MLA_TQ, t_len)
    ckv = ckv_ref[...].astype(BF16)
    krb = krb_ref[...]
    for hh in range(MLA_HPS):
        kvh = _dot(ckv, wkv_ref[hh])
        k_scr[hh] = (kvh[:, :LANE] + krb).astype(BF16)
        v_scr[hh] = kvh[:, LANE:].astype(BF16)

    def body(ti, carry):
        sl = pl.ds(pl.multiple_of(ti * tq, tq), tq)
        for pair in range(MLA_HPS // 2):
            o = None
            for hh in (2 * pair, 2 * pair + 1):
                s = _dot_nt(q_ref[sl, hh * LANE:(hh + 1) * LANE], k_scr[hh])
                m = jnp.max(s, axis=-1, keepdims=True)
                p = jnp.exp(s - m)
                den = jnp.sum(p, axis=-1, keepdims=True)
                oh = _dot(p.astype(BF16), v_scr[hh]) / den
                o = oh if o is None else o + oh
            o_ref[sl, pair * LANE:(pair + 1) * LANE] = o
        return carry

    lax.fori_loop(0, t_len // tq, body, 0)


def _mla_attn(q, ckv_all, krb_all, wkv, n_b, t_len, q_row_block0, n_rows):
    s_len = ckv_all.shape[1]
    return pl.pallas_call(
        functools.partial(_mla_attn_kernel, t_len=t_len),
        out_shape=jax.ShapeDtypeStruct((n_rows, MLA_HEADS * MLA_V_DIM), F32),
        grid=(n_b, MLA_HEADS // MLA_HPS),
        in_specs=[
            pl.BlockSpec((t_len, MLA_HPS * LANE), lambda b, hp: (q_row_block0 + b, hp)),
            pl.BlockSpec((None, s_len, MLA_KV_RANK), lambda b, hp: (b, 0, 0)),
            pl.BlockSpec((None, s_len, LANE), lambda b, hp: (b, 0, 0)),
            pl.BlockSpec((MLA_HPS, MLA_KV_RANK, 2 * LANE), lambda b, hp: (hp, 0, 0)),
        ],
        out_specs=pl.BlockSpec((t_len, MLA_HPS // 2 * LANE), lambda b, hp: (b, hp)),
        scratch_shapes=[pltpu.VMEM((MLA_HPS, s_len, LANE), BF16), pltpu.VMEM((MLA_HPS, s_len, LANE), BF16)],
        compiler_params=_params("parallel", "arbitrary"),
        name="mla_attn",
    )(q, ckv_all, krb_all, wkv)


META_E1, META_E2, META_W1, META_W2, META_R1, META_R2 = range(6)


def _router_kernel(x_ref, mod_ref, nw_ref, wr_ref, br_ref, tri_ref, xn_ref, meta_ref, cnt_ref, base_scr):
    @pl.when(pl.program_id(0) == 0)
    def _():
        base_scr[...] = jnp.zeros(base_scr.shape, F32)

    xn = _normmod(x_ref[...], nw_ref[...], mod_ref[3:4, :], mod_ref[4:5, :])
    xn_ref[...] = xn
    logits = _dot_f32ish(xn, wr_ref[...]) + br_ref[...]
    lane = lax.broadcasted_iota(I32, logits.shape, 1).astype(F32)
    far = float(LANE)

    def first_argmax(vals, vmax):
        return jnp.min(jnp.where(vals == vmax, lane, far), axis=-1, keepdims=True)

    gl = jnp.where(lane < MOE_GROUPS, logits, NEG_BIG)
    gmax = jnp.max(gl, axis=-1, keepdims=True)
    g_w = 1.0 / jnp.sum(jnp.exp(gl - gmax), axis=-1, keepdims=True)
    g_idx = first_argmax(gl, gmax)
    e_lo = MOE_GROUPS + MOE_EPG * g_idx
    el = jnp.where((lane >= e_lo) & (lane < e_lo + MOE_EPG), logits, NEG_BIG)
    m1 = jnp.max(el, axis=-1, keepdims=True)
    i1 = first_argmax(el, m1)
    el2 = jnp.where(lane == i1, NEG_BIG, el)
    m2 = jnp.max(el2, axis=-1, keepdims=True)
    i2 = first_argmax(el2, m2)
    esum = jnp.sum(jnp.exp(el - m1), axis=-1, keepdims=True)
    p1 = 1.0 / esum
    p2 = jnp.exp(m2 - m1) / esum
    w1 = g_w * (p1 / (p1 + p2))
    w2 = g_w * (p2 / (p1 + p2))
    e1 = i1 - MOE_GROUPS
    e2 = i2 - MOE_GROUPS

    oh1 = lane == e1
    oh2 = lane == e2
    oh = jnp.where(oh1 | oh2, 1.0, 0.0)
    before = _dot(tri_ref[...], oh.astype(BF16)) + base_scr[0:1, :]
    r1 = jnp.sum(jnp.where(oh1, before, 0.0), axis=-1, keepdims=True)
    r2 = jnp.sum(jnp.where(oh2, before, 0.0), axis=-1, keepdims=True)
    base_scr[...] = base_scr[...] + jnp.sum(oh, axis=0, keepdims=True)
    cnt_ref[...] = base_scr[...]

    meta = jnp.zeros(logits.shape, F32)
    for slot, val in ((META_E1, e1), (META_E2, e2), (META_W1, w1), (META_W2, w2), (META_R1, r1), (META_R2, r2)):
        meta = jnp.where(lane == slot, val, meta)
    meta_ref[...] = meta


def _router(rows, x, mods, nw, wr, br, tri):
    return pl.pallas_call(
        _router_kernel,
        out_shape=[jax.ShapeDtypeStruct((rows.n, D_MODEL), F32),
                   jax.ShapeDtypeStruct((rows.n, LANE), F32),
                   jax.ShapeDtypeStruct((8, LANE), F32)],
        grid=(rows.n_tiles,),
        in_specs=[
            pl.BlockSpec((TM, D_MODEL), lambda i: (i, 0)),
            rows.mod_spec(),
            pl.BlockSpec((1, D_MODEL), lambda i: (0, 0)),
            pl.BlockSpec((D_MODEL, LANE), lambda i: (0, 0)),
            pl.BlockSpec((1, LANE), lambda i: (0, 0)),
            pl.BlockSpec((TM, TM), lambda i: (0, 0)),
        ],
        out_specs=[pl.BlockSpec((TM, D_MODEL), lambda i: (i, 0)),
                   pl.BlockSpec((TM, LANE), lambda i: (i, 0)),
                   pl.BlockSpec((8, LANE), lambda i: (0, 0))],
        scratch_shapes=[pltpu.VMEM((8, LANE), F32)],
        compiler_params=_params("arbitrary"),
        name="moe_router",
    )(x, mods, nw.reshape(1, D_MODEL), wr, br, tri)


def _row_copy(src, src_row, dst, dst_row, sem):
    return pltpu.make_async_copy(src.at[pl.ds(src_row, 1), :], dst.at[pl.ds(dst_row, 1), :], sem)


def _dispatch_kernel(fill_ref, pos_ref, xn_ref, xs_hbm, zero_scr, sem, fill_sem, *, n_tiles):
    @pl.when(pl.program_id(0) == 0)
    def _():
        zero_scr[...] = jnp.zeros(zero_scr.shape, F32)

        def fill_copy(t):
            return pltpu.make_async_copy(zero_scr, xs_hbm.at[pl.ds(pl.multiple_of(t * TME, TME), TME), :], fill_sem)

        def fill_start(t, carry):
            @pl.when(fill_ref[t] != 0)
            def _():
                fill_copy(t).start()
            return carry

        def fill_wait(t, carry):
            @pl.when(fill_ref[t] != 0)
            def _():
                fill_copy(t).wait()
            return carry

        lax.fori_loop(0, n_tiles, fill_start, 0)
        lax.fori_loop(0, n_tiles, fill_wait, 0)

    def start(r, carry):
        for k in range(2):
            _row_copy(xn_ref, r, xs_hbm, pos_ref[0, 2 * r + k], sem).start(priority=k)
        return carry

    lax.fori_loop(0, TM, start, 0, unroll=DMA_UNROLL)
    for _ in range(2):
        pltpu.make_async_copy(xn_ref, xs_hbm.at[pl.ds(0, TM), :], sem).wait()


def _dispatch(rows, tile_fill, pos, xn, n_tiles):
    return pl.pallas_call(
        functools.partial(_dispatch_kernel, n_tiles=n_tiles),
        out_shape=jax.ShapeDtypeStruct((n_tiles * TME, D_MODEL), F32),
        grid_spec=pltpu.PrefetchScalarGridSpec(
            num_scalar_prefetch=1,
            grid=(rows.n_tiles,),
            in_specs=[
                pl.BlockSpec((None, 1, 2 * TM), lambda i, fill: (i, 0, 0), memory_space=pltpu.SMEM),
                pl.BlockSpec((TM, D_MODEL), lambda i, fill: (i, 0)),
            ],
            out_specs=pl.BlockSpec(memory_space=pl.ANY),
            scratch_shapes=[pltpu.VMEM((TME, D_MODEL), F32), pltpu.SemaphoreType.DMA(()),
                            pltpu.SemaphoreType.DMA(())],
        ),
        compiler_params=_params("arbitrary"),
        name="moe_dispatch",
    )(tile_fill, pos, xn)


def _ffn_kernel(te_ref, nv_ref, x_ref, wg_ref, wu_ref, wd_ref, y_ref, wg_b, wu_b, wd_b):
    t = pl.program_id(0)
    valid = t < nv_ref[0]
    new_expert = (t == 0) | (te_ref[t] != te_ref[jnp.maximum(t - 1, 0)])

    @pl.when(valid & new_expert)
    def _():
        wg_b[...] = wg_ref[...].astype(BF16)
        wu_b[...] = wu_ref[...].astype(BF16)
        wd_b[...] = wd_ref[...].astype(BF16)

    @pl.when(valid)
    def _():
        x = x_ref[...].astype(BF16)
        a = _silu(_dot(x, wg_b[...])) * _dot(x, wu_b[...])
        y_ref[...] = _dot(a.astype(BF16), wd_b[...])

    @pl.when(jnp.logical_not(valid))
    def _():
        y_ref[...] = jnp.zeros(y_ref.shape, F32)


def _ffn(tile_expert, n_valid, xs, w_gate, w_up, w_down, layer, n_tiles):
    def xmap(t, te, nv):
        return (jnp.minimum(t, nv[0] - 1), 0)

    def wmap(t, te, nv):
        return (layer, te[t], 0, 0)

    return pl.pallas_call(
        _ffn_kernel,
        out_shape=jax.ShapeDtypeStruct((n_tiles * TME, D_MODEL), F32),
        grid_spec=pltpu.PrefetchScalarGridSpec(
            num_scalar_prefetch=2,
            grid=(n_tiles,),
            in_specs=[
                pl.BlockSpec((TME, D_MODEL), xmap),
                pl.BlockSpec((None, None, D_MODEL, MOE_HIDDEN), wmap),
                pl.BlockSpec((None, None, D_MODEL, MOE_HIDDEN), wmap),
                pl.BlockSpec((None, None, MOE_HIDDEN, D_MODEL), wmap),
            ],
            out_specs=pl.BlockSpec((TME, D_MODEL), lambda t, te, nv: (t, 0)),
            scratch_shapes=[pltpu.VMEM((D_MODEL, MOE_HIDDEN), BF16),
                            pltpu.VMEM((D_MODEL, MOE_HIDDEN), BF16),
                            pltpu.VMEM((MOE_HIDDEN, D_MODEL), BF16)],
        ),
        compiler_params=_params("arbitrary"),
        name="moe_ffn",
    )(tile_expert, n_valid, xs, w_gate, w_up, w_down)


def _combine_kernel(pos_ref, x_ref, meta_ref, mod_ref, fnw_ref, ys_hbm, *rest, final, ctx_tiles):
    if final:
        o_ctx_ref, o_lat_ref, buf0, buf1, sem = rest
    else:
        o_ref, buf0, buf1, sem = rest
    bufs = (buf0, buf1)

    def start(r, carry):
        for k in range(2):
            _row_copy(ys_hbm, pos_ref[0, 2 * r + k], bufs[k], r, sem).start(priority=k)
        return carry

    lax.fori_loop(0, TM, start, 0, unroll=DMA_UNROLL)
    for k in range(2):
        pltpu.make_async_copy(ys_hbm.at[pl.ds(0, TM), :], bufs[k], sem).wait()
    meta = meta_ref[...]
    y = meta[:, META_W1:META_W1 + 1] * buf0[...] + meta[:, META_W2:META_W2 + 1] * buf1[...]
    xo = x_ref[...] + mod_ref[5:6, :] * y
    if not final:
        o_ref[...] = xo
        return
    xo = xo * lax.rsqrt(jnp.mean(xo * xo, axis=-1, keepdims=True) + NORM_EPS) * fnw_ref[...]
    is_ctx = pl.program_id(0) < ctx_tiles

    @pl.when(is_ctx)
    def _():
        o_ctx_ref[...] = xo

    @pl.when(jnp.logical_not(is_ctx))
    def _():
        o_lat_ref[...] = xo


def _combine(rows, pos, x, meta, mods, fnw, ys, final):
    ct = rows.ctx_tiles
    if final:
        out_shape = [jax.ShapeDtypeStruct((rows.nc, D_MODEL), F32), jax.ShapeDtypeStruct((rows.nl, D_MODEL), F32)]
        out_specs = [pl.BlockSpec((TM, D_MODEL), lambda i: (jnp.minimum(i, ct - 1), 0)),
                     pl.BlockSpec((TM, D_MODEL), lambda i: (jnp.maximum(i - ct, 0), 0))]
    else:
        out_shape = jax.ShapeDtypeStruct((rows.n, D_MODEL), F32)
        out_specs = pl.BlockSpec((TM, D_MODEL), lambda i: (i, 0))
    return pl.pallas_call(
        functools.partial(_combine_kernel, final=final, ctx_tiles=ct),
        out_shape=out_shape,
        grid=(rows.n_tiles,),
        in_specs=[
            pl.BlockSpec((None, 1, 2 * TM), lambda i: (i, 0, 0), memory_space=pltpu.SMEM),
            pl.BlockSpec((TM, D_MODEL), lambda i: (i, 0)),
            pl.BlockSpec((TM, LANE), lambda i: (i, 0)),
            rows.mod_spec(),
            pl.BlockSpec((1, D_MODEL), lambda i: (0, 0)),
            pl.BlockSpec(memory_space=pl.ANY),
        ],
        out_specs=out_specs,
        scratch_shapes=[pltpu.VMEM((TM, D_MODEL), F32), pltpu.VMEM((TM, D_MODEL), F32),
                        pltpu.SemaphoreType.DMA(())],
        compiler_params=_params("arbitrary"),
        name="moe_combine",
    )(pos, x, meta, mods, fnw.reshape(1, D_MODEL), ys)


def _moe(rows, x, mods, nw, wr, br, tri, w_gate, w_up, w_down, layer, fnw, final):
    n_assign = 2 * rows.n
    n_tiles = n_assign // TME + MOE_EXPERTS
    xn, meta, cnt = _router(rows, x, mods, nw, wr, br, tri)

    counts = cnt[0, :MOE_EXPERTS].astype(I32)
    padded = ((counts + TME - 1) // TME) * TME
    ends = jnp.cumsum(padded)
    starts = ends - padded
    experts = jnp.arange(MOE_EXPERTS, dtype=I32)
    e = meta[:, META_E1:META_E2 + 1].astype(I32)
    rank = meta[:, META_R1:META_R2 + 1].astype(I32)
    start_of = jnp.sum(jnp.where(e[..., None] == experts, starts, 0), axis=-1)
    pos = (start_of + rank).reshape(rows.n_tiles, 1, 2 * TM)
    n_valid = ends[-1] // TME
    tile_first = jnp.arange(n_tiles, dtype=I32) * TME
    tile_start = jnp.minimum(tile_first, ends[-1] - TME)
    tile_expert = jnp.sum((ends[None, :] <= tile_start[:, None]).astype(I32), axis=1)
    tile_expert = jnp.minimum(tile_expert, MOE_EXPERTS - 1)
    tile_oh = tile_expert[:, None] == experts
    tile_rows = jnp.sum(jnp.where(tile_oh, counts + starts, 0), axis=1) - tile_start
    tile_fill = ((tile_first >= ends[-1]) | (tile_rows < TME)).astype(I32)

    xs = _dispatch(rows, tile_fill, pos, xn, n_tiles)
    ys = _ffn(tile_expert, n_valid.reshape(1).astype(I32), xs, w_gate, w_up, w_down, layer, n_tiles)
    return _combine(rows, pos, x, meta, mods, fnw, ys, final)


def _lower_bound_params(p):
    pr = jax.nn.softmax(p.astype(F32), axis=0)
    lb = jnp.cumsum(pr, axis=0) - pr[0:1]
    lb = jnp.clip(lb, 0.0, 1.0 - 1e-6)
    return jnp.maximum(lb, LOG_TINY), 1.0 - lb


def kernel(x_prompt, x_sample, c, cache_swa_k, cache_swa_v, state_hgrn, cache_mla_ckv, cache_mla_krope, c_ctx, mod_w, mod_b, norm1_w, norm2_w, final_norm_w, even_w_in, even_w_out, swa_sink, hgrn_lb_fwd, hgrn_lb_bwd, hgrn_gnorm_w, mla_w_dq, mla_qnorm_w, mla_w_uq, mla_w_dkv, mla_kvnorm_w, mla_w_ukv, mla_w_o, moe_router_group_w, moe_router_group_b, moe_router_expert_w, moe_router_expert_b, moe_w_gate, moe_w_up, moe_w_down):
    nc_b, nc_t, _ = x_prompt.shape
    nl_b, nl_t, _ = x_sample.shape
    rows = _Rows(nc_b, nc_t, nl_b, nl_t)
    past = cache_swa_k.shape[2]

    x = (x_prompt.reshape(rows.nc, D_MODEL), x_sample.reshape(rows.nl, D_MODEL))
    mod_rows = 16
    cvec = jnp.concatenate([c_ctx[None, :], c, jnp.zeros((mod_rows - 1 - nl_b, D_MODEL), F32)], axis=0)
    mods_all = _modulation(cvec, mod_w, mod_b).reshape(DEPTH, mod_rows, 6, D_MODEL)

    hconsts = _hgrn_consts()
    la_f, l1_f = _lower_bound_params(hgrn_lb_fwd)
    la_b, l1_b = _lower_bound_params(hgrn_lb_bwd)
    lbp_all = jnp.stack([la_f, l1_f, la_b, l1_b], axis=1).reshape(N_EVEN, 4, HGRN_HEADS, LANE).transpose(0, 2, 1, 3)
    swa_cos, swa_sin = _rope_tables(nl_t, SWA_HEAD_DIM, 0, LANE, 0)
    mla_cos, mla_sin = _rope_tables(nl_t, MLA_ROPE_DIM, MLA_NOPE_DIM, MLA_QK_DIM, TM)
    tri =jnp.asarray(np.tril(np.ones((TM, TM), np.float32), -1), BF16)

    new_k, new_v, new_s, new_ckv, new_kr = [], [], [], [], []
    for l in range(DEPTH):
        j = l // 2
        mods = mods_all[l]
        if l % 2 == 0:
            proj = _even_proj(rows, x, mods, norm1_w[l], even_w_in[j].astype(BF16))
            a_ctx = _swa_ctx(rows, proj, swa_sink[j])
            a_lat = _swa_lat(rows, proj, swa_sink[j],
                             cache_swa_k[:, j].reshape(nl_b, past, SWA_KV_WIDTH),
                             cache_swa_v[:, j].reshape(nl_b, past, SWA_KV_WIDTH), swa_cos, swa_sin)
            r_ctx, s_ctx = _hgrn(proj, lbp_all[j], hgrn_gnorm_w[j], hconsts, nc_b, nc_t, 0, rows.nc,
                                 emit_state=True)
            (r_lat,) = _hgrn(proj, lbp_all[j], hgrn_gnorm_w[j], hconsts, nl_b, nl_t, rows.nc // nl_t, rows.nl,
                             s0=state_hgrn, s0_layer=j)
            w_out = even_w_out[j].astype(BF16)
            x = _outproj(rows, [(a_ctx, a_lat), (r_ctx, r_lat)], [w_out[:SWA_WIDTH], w_out[SWA_WIDTH:]], x, mods)
            kv = proj[:rows.nc, SWA_WIDTH:SWA_WIDTH + 2 * SWA_KV_WIDTH]
            new_k.append(kv[:, :SWA_KV_WIDTH].reshape(nc_b, nc_t, SWA_KV_HEADS, SWA_HEAD_DIM))
            new_v.append(kv[:, SWA_KV_WIDTH:].reshape(nc_b, nc_t, SWA_KV_HEADS, SWA_HEAD_DIM))
            new_s.append(s_ctx)
        else:
            wd = jnp.zeros((D_MODEL, MLA_DOWN_WIDTH), F32)
            wd = wd.at[:, :MLA_Q_RANK].set(mla_w_dq[j])
            wd = wd.at[:, MLA_Q_RANK:MLA_Q_RANK + MLA_KV_RANK].set(mla_w_dkv[j][:, :MLA_KV_RANK])
            kr_lo = MLA_Q_RANK + MLA_KV_RANK + MLA_NOPE_DIM
            wd = wd.at[:, kr_lo:kr_lo + MLA_ROPE_DIM].set(mla_w_dkv[j][:, MLA_KV_RANK:])
            wuq = jnp.pad(mla_w_uq[j].reshape(MLA_Q_RANK, MLA_HEADS, MLA_QK_DIM),
                          ((0, 0), (0, 0), (0, LANE - MLA_QK_DIM))).reshape(MLA_Q_RANK, MLA_HEADS * LANE)
            wukv = mla_w_ukv[j].reshape(MLA_KV_RANK, MLA_HEADS, MLA_NOPE_DIM + MLA_V_DIM).transpose(1, 0, 2)
            wk = jnp.pad(wukv[..., :MLA_NOPE_DIM], ((0, 0), (0, 0), (0, LANE - MLA_NOPE_DIM)))
            wv_e = jnp.pad(wukv[..., MLA_NOPE_DIM:], ((0, 0), (0, 0), (0, LANE - MLA_V_DIM)))
            wv_o = jnp.pad(wukv[..., MLA_NOPE_DIM:], ((0, 0), (0, 0), (LANE - MLA_V_DIM, 0)))
            odd = (jnp.arange(MLA_HEADS) % 2 == 1)[:, None, None]
            wv = jnp.where(odd, wv_o, wv_e)
            wuq = wuq.astype(BF16)
            w3 = wuq.reshape(MLA_Q_RANK, MLA_HEADS, LANE)
            nf = MLA_ROPE_DIM // 4
            rot = w3[..., MLA_NOPE_DIM:MLA_QK_DIM].reshape(MLA_Q_RANK, MLA_HEADS, 2, 2, nf)[..., ::-1, :]
            wuq_partner = jnp.concatenate([w3[..., :MLA_NOPE_DIM], rot.reshape(MLA_Q_RANK, MLA_HEADS, MLA_ROPE_DIM),
                                           w3[..., MLA_QK_DIM:]], axis=-1).reshape(MLA_Q_RANK, MLA_HEADS * LANE)
            q, ckv, krb = _mla_proj(rows, x, mods, norm1_w[l], wd.astype(BF16), mla_qnorm_w[j], mla_kvnorm_w[j],
                                    wuq, wuq_partner, mla_cos, mla_sin)
            wkv = jnp.concatenate([wk, wv], axis=-1).astype(BF16)
            ckv_c = ckv[:rows.nc].reshape(nc_b, nc_t, MLA_KV_RANK)
            krb_c = krb[:rows.nc].reshape(nc_b, nc_t, LANE)
            o_ctx = _mla_attn(q, ckv_c, krb_c, wkv, nc_b, nc_t, 0, rows.nc)
            cache_kr = jnp.pad(cache_mla_krope[:, j], ((0, 0), (0, 0), (MLA_NOPE_DIM, LANE - MLA_QK_DIM)))
            ckv_l = jnp.concatenate([cache_mla_ckv[:, j], ckv[rows.nc:].reshape(nl_b, nl_t, MLA_KV_RANK)], axis=1)
            krb_l = jnp.concatenate([cache_kr, krb[rows.nc:].reshape(nl_b, nl_t, LANE)], axis=1)
            o_lat = _mla_attn(q, ckv_l, krb_l, wkv, nl_b, nl_t, rows.nc // nl_t, rows.nl)
            x = _outproj(rows, [(o_ctx, o_lat)], [mla_w_o[j].astype(BF16)], x, mods)
            new_ckv.append(ckv_c)
            new_kr.append(krb_c[..., MLA_NOPE_DIM:MLA_QK_DIM])

        wr = jnp.zeros((D_MODEL, LANE), F32)
        wr = wr.at[:, :MOE_GROUPS].set(moe_router_group_w[l])
        wr = wr.at[:, MOE_GROUPS:MOE_GROUPS + MOE_EXPERTS].set(
            moe_router_expert_w[l].transpose(1, 0, 2).reshape(D_MODEL, MOE_EXPERTS))
        br = jnp.zeros((1, LANE), F32)
        br = br.at[0, :MOE_GROUPS].set(moe_router_group_b[l])
        br = br.at[0, MOE_GROUPS:MOE_GROUPS + MOE_EXPERTS].set(moe_router_expert_b[l].reshape(MOE_EXPERTS))
        x = _moe(rows, x, mods, norm2_w[l], wr, br, tri, moe_w_gate, moe_w_up, moe_w_down, l,
                 final_norm_w, final=(l == DEPTH - 1))

    y_prompt = x[0].reshape(nc_b, nc_t, D_MODEL)
    y_sample = x[1].reshape(nl_b, nl_t, D_MODEL)
    return (y_prompt, y_sample, jnp.stack(new_k, axis=1), jnp.stack(new_v, axis=1), jnp.stack(new_s, axis=1),
            jnp.stack(new_ckv, axis=1), jnp.stack(new_kr, axis=1))
```

```python
import functools
import math

import numpy as np
import jax
import jax.numpy as jnp
from jax import lax
from jax.experimental import pallas as pl
from jax.experimental.pallas import tpu as pltpu

F32, BF16, I32 = jnp.float32, jnp.bfloat16, jnp.int32

D_MODEL = 1024
DEPTH = 4
GRID_W = 64
ROPE_THETA = 10000.0
NORM_EPS = 1e-6
NEG_BIG = -1e30
LOG_TINY = 1e-30
N_EVEN = (DEPTH + 1) // 2
N_ODD = DEPTH // 2

SWA_HEADS = 8
SWA_KV_HEADS = 2
SWA_GROUP = SWA_HEADS // SWA_KV_HEADS
SWA_HEAD_DIM = 64
SWA_WIDTH = SWA_HEADS * SWA_HEAD_DIM
SWA_KV_WIDTH = SWA_KV_HEADS * SWA_HEAD_DIM
SWA_WINDOW = 128
SWA_BLOCK = 128

HGRN_HEADS = 4
HGRN_KEY_DIM = 128
HGRN_VAL_DIM = 128
HGRN_WIDTH = HGRN_HEADS * HGRN_KEY_DIM
HGRN_CHUNK = 128
HGRN_SUB = 8
HGRN_LEVELS = (8, 16, 32, 64)
HGRN_GROUP = 4

EVEN_IN_WIDTH = SWA_WIDTH + 2 * SWA_KV_WIDTH + 5 * HGRN_WIDTH
LANE = 128
COL_Q, COL_K, COL_V = 0, SWA_WIDTH // LANE, (SWA_WIDTH + SWA_KV_WIDTH) // LANE
COL_HGRN = (SWA_WIDTH + 2 * SWA_KV_WIDTH) // LANE

MLA_HEADS = 16
MLA_Q_RANK = 384
MLA_KV_RANK = 256
MLA_NOPE_DIM = 64
MLA_ROPE_DIM = 32
MLA_V_DIM = 64
MLA_QK_DIM = MLA_NOPE_DIM + MLA_ROPE_DIM
MLA_DOWN_WIDTH = MLA_Q_RANK + MLA_KV_RANK + LANE

MOE_GROUPS = 4
MOE_EPG = 8
MOE_EXPERTS = MOE_GROUPS * MOE_EPG
MOE_HIDDEN = 256

TM = 512
TME = 512
DMA_UNROLL = 8
VMEM_LIMIT = 48 * 1024 * 1024


def _params(*sem):
    return pltpu.CompilerParams(dimension_semantics=sem, vmem_limit_bytes=VMEM_LIMIT)


def _dot(a, b):
    return jnp.dot(a, b, preferred_element_type=F32)


def _dot_nt(a, b):
    return lax.dot_general(a, b, (((1,), (1,)), ((), ())), preferred_element_type=F32)


def _split2(a):
    hi = a.astype(BF16)
    return hi, (a - hi.astype(F32)).astype(BF16)


def _dot_f32ish(a, b):
    ah, al = _split2(a)
    bh, bl = _split2(b)
    return _dot(ah, bh) + (_dot(ah, bl) + _dot(al, bh))


def _silu(x):
    return x * jax.nn.sigmoid(x)


def _normmod(x, nw, shift, scale):
    ms = jnp.mean(x * x, axis=-1, keepdims=True)
    return (x * lax.rsqrt(ms + NORM_EPS) * nw) * (1.0 + scale) + shift


def _mod_kernel(c_ref, w_ref, b_ref, o_ref):
    o_ref[...] = _dot_f32ish(_silu(c_ref[...]), w_ref[...]) + b_ref[...]


def _modulation(cvec, mod_w, mod_b):
    rows = cvec.shape[0]
    nb = 6 * D_MODEL // 1024
    return pl.pallas_call(
        _mod_kernel,
        out_shape=jax.ShapeDtypeStruct((DEPTH, rows, 6 * D_MODEL), F32),
        grid=(DEPTH, nb),
        in_specs=[
            pl.BlockSpec((rows, D_MODEL), lambda l, n: (0, 0)),
            pl.BlockSpec((None, D_MODEL, 1024), lambda l, n: (l, 0, n)),
            pl.BlockSpec((None, 1, 1024), lambda l, n: (l, 0, n)),
        ],
        out_specs=pl.BlockSpec((None, rows, 1024), lambda l, n: (l, 0, n)),
        compiler_params=_params("parallel", "parallel"),
        name="modulation",
    )(cvec, mod_w, mod_b.reshape(DEPTH, 1, 6 * D_MODEL))


class _Rows:
    def __init__(self, nc_b, nc_t, nl_b, nl_t):
        self.nc_b, self.nc_t, self.nl_b, self.nl_t = nc_b, nc_t, nl_b, nl_t
        self.nc = nc_b * nc_t
        self.nl = nl_b * nl_t
        self.n = self.nc + self.nl
        assert self.nc % TM == 0 and nl_t % TM == 0 and self.nc % nl_t == 0
        self.ctx_tiles = self.nc // TM
        self.tiles_per_lat = nl_t // TM
        self.n_tiles = self.n // TM

    def mod_row(self, i):
        return jnp.where(i < self.ctx_tiles, 0, 1 + (i - self.ctx_tiles) // self.tiles_per_lat)

    def mod_spec(self):
        return pl.BlockSpec((None, 6, D_MODEL), lambda i: (self.mod_row(i), 0, 0))

    def stream_specs(self, x):
        ct = self.ctx_tiles
        if not isinstance(x, tuple):
            return [pl.BlockSpec((TM, x.shape[1]), lambda i: (i, 0))], [x]
        x_ctx, x_lat = x
        return ([pl.BlockSpec((TM, x_ctx.shape[1]), lambda i: (jnp.minimum(i, ct - 1), 0)),
                 pl.BlockSpec((TM, x_lat.shape[1]), lambda i: (jnp.maximum(i - ct, 0), 0))], [x_ctx, x_lat])

    def stream_tile(self, refs):
        if len(refs) == 1:
            return refs[0][...]
        return jnp.where(pl.program_id(0) < self.ctx_tiles, refs[0][...], refs[1][...])

    def pos_block(self, i):
        return jnp.where(i < self.ctx_tiles, 0, 1 + (i - self.ctx_tiles) % self.tiles_per_lat)


def _even_proj_kernel(*refs, rows):
    mod_ref, nw_ref, w_ref, o_ref = refs[-4:]
    x = rows.stream_tile(refs[:-4])
    h = _normmod(x, nw_ref[...], mod_ref[0:1, :], mod_ref[1:2, :]).astype(BF16)
    step = 256
    for c in range(EVEN_IN_WIDTH // step):
        o_ref[:, c * step:(c + 1) * step] = _dot(h, w_ref[:, c * step:(c + 1) * step])


def _even_proj(rows, x, mods, nw, w_in_bf16):
    x_specs, x_args = rows.stream_specs(x)
    return pl.pallas_call(
        functools.partial(_even_proj_kernel, rows=rows),
        out_shape=jax.ShapeDtypeStruct((rows.n, EVEN_IN_WIDTH), F32),
        grid=(rows.n_tiles,),
        in_specs=x_specs + [
            rows.mod_spec(),
            pl.BlockSpec((1, D_MODEL), lambda i: (0, 0)),
            pl.BlockSpec((D_MODEL, EVEN_IN_WIDTH), lambda i: (0, 0)),
        ],
        out_specs=pl.BlockSpec((TM, EVEN_IN_WIDTH), lambda i: (i, 0)),
        compiler_params=_params("arbitrary"),
        name="even_proj",
    )(*x_args, mods, nw.reshape(1, D_MODEL), w_in_bf16)


def _rope_tables(t_len, rot_dim, lane_lo, lane_hi, lead_rows):
    half = rot_dim // 2
    nf = half // 2
    lane = np.arange(LANE)
    d = (lane - lane_lo) % rot_dim
    active = (lane >= lane_lo) & (lane < lane_hi)
    use_col = d >= half
    fidx = d % nf
    first = (d % half) < nf
    pos = jnp.arange(t_len)
    row = (pos // GRID_W).astype(F32)
    col = (pos % GRID_W).astype(F32)
    inv = jnp.exp(-math.log(ROPE_THETA) * jnp.arange(nf, dtype=F32) / nf)
    p = jnp.where(jnp.asarray(use_col)[None, :], col[:, None], row[:, None])
    ang = p * inv[jnp.asarray(fidx)][None, :]
    act = jnp.asarray(active)[None, :]
    cos = jnp.where(act, jnp.cos(ang), 1.0)
    sin = jnp.where(act, jnp.sin(ang), 0.0)
    sin = jnp.where(jnp.asarray(first)[None, :], -sin, sin)
    if lead_rows:
        cos = jnp.concatenate([jnp.ones((lead_rows, LANE), F32), cos], axis=0)
        sin = jnp.concatenate([jnp.zeros((lead_rows, LANE), F32), sin], axis=0)
    return cos, sin


def _rope(x, cos, sin, nf):
    lane = lax.broadcasted_iota(I32, x.shape, 1)
    up = pltpu.roll(x, LANE - nf, axis=1)
    dn = pltpu.roll(x, nf, axis=1)
    partner = jnp.where((lane & nf) == 0, up, dn)
    return x * cos + partner * sin


def _swa_ctx_kernel(sink_ref, q_ref, k_ref, v_ref, o_ref):
    scale = SWA_HEAD_DIM ** -0.5
    k = k_ref[...]
    v = v_ref[...]
    for h in range(SWA_HEADS):
        kv = h // SWA_GROUP
        lo, klo = h * SWA_HEAD_DIM, kv * SWA_HEAD_DIM
        q = q_ref[:, lo:lo + SWA_HEAD_DIM].astype(BF16)
        kk = k[:, klo:klo + SWA_HEAD_DIM].astype(BF16)
        vv = v[:, klo:klo + SWA_HEAD_DIM].astype(BF16)
        s = _dot_nt(q, kk) * scale
        sk = sink_ref[h]
        m = jnp.maximum(jnp.max(s, axis=-1, keepdims=True), sk)
        p = jnp.exp(s - m)
        den = jnp.sum(p, axis=-1, keepdims=True) + jnp.exp(sk - m)
        o_ref[:, lo:lo + SWA_HEAD_DIM] = _dot(p.astype(BF16), vv) / den


def _swa_ctx(rows, proj, sink):
    t = rows.nc_t
    return pl.pallas_call(
        _swa_ctx_kernel,
        out_shape=jax.ShapeDtypeStruct((rows.nc, SWA_WIDTH), F32),
        grid=(rows.nc_b,),
        in_specs=[
            pl.BlockSpec(memory_space=pltpu.SMEM),
            pl.BlockSpec((t, SWA_WIDTH), lambda b: (b, COL_Q)),
            pl.BlockSpec((t, LANE), lambda b: (b, COL_K)),
            pl.BlockSpec((t, LANE), lambda b: (b, COL_V)),
        ],
        out_specs=pl.BlockSpec((t, SWA_WIDTH), lambda b: (b, 0)),
        compiler_params=_params("parallel"),
        name="swa_ctx",
    )(sink, proj, proj, proj)


def _lane_fold(x, op):
    out = x[:, :LANE]
    for i in range(1, x.shape[1] // LANE):
        out = op(out, x[:, i * LANE:(i + 1) * LANE])
    return out


def _swa_lat_kernel(sink_ref, q_ref, k_ref, v_ref, kc_ref, vc_ref, cos_ref, sin_ref, o_ref,
                    kl_scr, vl_scr, kc_scr, vc_scr, *, n_blocks):
    scale = SWA_HEAD_DIM ** -0.5
    nf = SWA_HEAD_DIM // 4
    n = pl.program_id(1)

    @pl.when(n == 0)
    def _():
        kr = _rope(k_ref[...], cos_ref[...], sin_ref[...], nf)
        for kv in range(SWA_KV_HEADS):
            cols = slice(kv * SWA_HEAD_DIM, (kv + 1) * SWA_HEAD_DIM)
            kl_scr[kv] = kr[:, cols].astype(BF16)
            vl_scr[kv] = v_ref[:, cols].astype(BF16)
            kc_scr[kv] = kc_ref[:, cols].astype(BF16)
            vc_scr[kv] = vc_ref[:, cols].astype(BF16)

    q0 = pl.multiple_of(n * SWA_BLOCK, SWA_BLOCK)
    cq = cos_ref[pl.ds(q0, SWA_BLOCK), :]
    sq = sin_ref[pl.ds(q0, SWA_BLOCK), :]
    qs = [_rope(q_ref[:, g * LANE:(g + 1) * LANE], cq, sq, nf) for g in range(SWA_WIDTH // LANE)]

    qi = lax.broadcasted_iota(I32, (SWA_BLOCK, SWA_BLOCK), 0)
    r = lax.broadcasted_iota(I32, (SWA_BLOCK, SWA_BLOCK), 1)
    band, mb = [], []
    for off in (-1, 0, 1):
        blk = n + off
        valid = (blk >= 0) & (blk < n_blocks)
        st = pl.multiple_of(jnp.clip(blk, 0, n_blocks - 1) * SWA_BLOCK, SWA_BLOCK)
        band.append(pl.ds(st, SWA_BLOCK))
        rel = qi - r - off * SWA_BLOCK
        mb.append(jnp.where((jnp.abs(rel) <= SWA_WINDOW) & valid, 1.0, 0.0))
    mask = jnp.concatenate(mb, axis=1) > 0.5

    kl = [jnp.concatenate([kl_scr[kv, sl, :] for sl in band], axis=0) for kv in range(SWA_KV_HEADS)]
    vl = [jnp.concatenate([vl_scr[kv, sl, :] for sl in band], axis=0) for kv in range(SWA_KV_HEADS)]
    heads = range(SWA_HEADS)
    lanes = [slice((h * SWA_HEAD_DIM) % LANE, (h * SWA_HEAD_DIM) % LANE + SWA_HEAD_DIM) for h in heads]
    q = [qs[h * SWA_HEAD_DIM // LANE][:, lanes[h]].astype(BF16) for h in heads]
    s_ctx = [_dot_nt(q[h], kc_scr[h // SWA_GROUP]) * scale for h in heads]
    s_loc = [jnp.where(mask, _dot_nt(q[h], kl[h // SWA_GROUP]) * scale, NEG_BIG) for h in heads]
    m = [jnp.maximum(jnp.max(jnp.maximum(_lane_fold(s_ctx[h], jnp.maximum), _lane_fold(s_loc[h], jnp.maximum)),
                             axis=-1, keepdims=True), sink_ref[h]) for h in heads]
    p_ctx = [jnp.exp(s_ctx[h] - m[h]) for h in heads]
    p_loc = [jnp.exp(s_loc[h] - m[h]) for h in heads]
    den = [jnp.sum(_lane_fold(p_ctx[h], jnp.add) + _lane_fold(p_loc[h], jnp.add), axis=-1, keepdims=True)
           + jnp.exp(sink_ref[h] - m[h]) for h in heads]
    o = [_dot(p_ctx[h].astype(BF16), vc_scr[h // SWA_GROUP]) + _dot(p_loc[h].astype(BF16), vl[h // SWA_GROUP])
         for h in heads]
    for h in heads:
        o_ref[:, h * SWA_HEAD_DIM:(h + 1) * SWA_HEAD_DIM] = o[h] / den[h]


def _swa_lat(rows, proj, sink, k_ctx, v_ctx, cos, sin):
    t = rows.nl_t
    n_blocks = t // SWA_BLOCK
    q_base = rows.nc // SWA_BLOCK
    kv_base = rows.nc // t
    s_ctx = k_ctx.shape[1]
    return pl.pallas_call(
        functools.partial(_swa_lat_kernel, n_blocks=n_blocks),
        out_shape=jax.ShapeDtypeStruct((rows.nl, SWA_WIDTH), F32),
        grid=(rows.nl_b, n_blocks),
        in_specs=[
            pl.BlockSpec(memory_space=pltpu.SMEM),
            pl.BlockSpec((SWA_BLOCK, SWA_WIDTH), lambda b, n: (q_base + b * n_blocks + n, COL_Q)),
            pl.BlockSpec((t, LANE), lambda b, n: (kv_base + b, COL_K)),
            pl.BlockSpec((t, LANE), lambda b, n: (kv_base + b, COL_V)),
            pl.BlockSpec((None, s_ctx, LANE), lambda b, n: (b, 0, 0)),
            pl.BlockSpec((None, s_ctx, LANE), lambda b, n: (b, 0, 0)),
            pl.BlockSpec((t, LANE), lambda b, n: (0, 0)),
            pl.BlockSpec((t, LANE), lambda b, n: (0, 0)),
        ],
        out_specs=pl.BlockSpec((SWA_BLOCK, SWA_WIDTH), lambda b, n: (b * n_blocks + n, 0)),
        scratch_shapes=[pltpu.VMEM((SWA_KV_HEADS, t, SWA_HEAD_DIM), BF16),
                        pltpu.VMEM((SWA_KV_HEADS, t, SWA_HEAD_DIM), BF16),
                        pltpu.VMEM((SWA_KV_HEADS, s_ctx, SWA_HEAD_DIM), BF16),
                        pltpu.VMEM((SWA_KV_HEADS, s_ctx, SWA_HEAD_DIM), BF16)],
        compiler_params=_params("parallel", "arbitrary"),
        name="swa_lat",
    )(sink, proj, proj, proj, k_ctx, v_ctx, cos, sin)


def _hgrn_consts():
    c = HGRN_CHUNK
    t = np.arange(c)[:, None]
    u = np.arange(c)[None, :]
    tri_f = (u <= t).astype(np.float32)
    masks = []
    for m in HGRN_LEVELS:
        right = ((t // m) % 2) == 1
        masks.append(right & ((u // m) == (t // m) - 1))
    m_f = np.stack(masks).astype(np.float32)
    sel = np.kron(np.eye(c), np.ones((1, HGRN_SUB))).astype(np.float32)
    return (jnp.asarray(tri_f, BF16), jnp.asarray(m_f, F32),
            jnp.asarray(tri_f[::-1, ::-1], BF16), jnp.asarray(m_f[:, ::-1, ::-1], F32),
            jnp.ones((HGRN_KEY_DIM, LANE), BF16), jnp.asarray(sel, BF16))


def _hgrn_intra(items, ones_b, sel_b):
    c = HGRN_CHUNK
    nsub = c // HGRN_SUB
    n = range(len(items))
    fwd = [it[7] for it in items]
    v = [it[2] for it in items]
    q = [_silu(it[0]) * (HGRN_KEY_DIM ** -0.5) for it in items]
    f = [it[3] + it[4] * jax.nn.sigmoid(it[1]) for it in items]
    lf = [jnp.log(f[j]) for j in n]
    k = [1.0 - f[j] for j in n]

    hi = [lf[j].astype(BF16) for j in n]
    r1 = [lf[j] - hi[j].astype(F32) for j in n]
    mid = [r1[j].astype(BF16) for j in n]
    lo = [(r1[j] - mid[j].astype(F32)).astype(BF16) for j in n]
    tri = [items[j][5][...] for j in n]
    b = [_dot(tri[j], hi[j]) + (_dot(tri[j], mid[j]) + _dot(tri[j], lo[j])) for j in n]
    total = [jnp.sum(lf[j], axis=0, keepdims=True) for j in n]

    s_io = lax.broadcasted_iota(I32, (1, HGRN_SUB, HGRN_KEY_DIM), 1)
    p = []
    for j in n:
        b3 = b[j].reshape(nsub, HGRN_SUB, HGRN_KEY_DIM)
        q3 = q[j].reshape(nsub, HGRN_SUB, HGRN_KEY_DIM)
        k3 = k[j].reshape(nsub, HGRN_SUB, HGRN_KEY_DIM)
        ps = []
        for i in range(HGRN_SUB):
            keep = (s_io <= i) if fwd[j] else (s_io >= i)
            dec = jnp.exp(jnp.where(keep, b3[:, i:i + 1, :] - b3, NEG_BIG))
            ps.append((q3[:, i:i + 1, :] * dec) * k3)
        p.append(jnp.stack(ps, axis=1).reshape(nsub * HGRN_SUB * HGRN_SUB, HGRN_KEY_DIM).astype(BF16))
    att = [_dot(p[j], ones_b) for j in n]
    av = [(att[j].reshape(nsub, HGRN_SUB, HGRN_SUB, HGRN_VAL_DIM)
           * v[j].reshape(nsub, HGRN_SUB, HGRN_VAL_DIM)[:, None, :, :]
           ).reshape(nsub * HGRN_SUB * HGRN_SUB, HGRN_VAL_DIM).astype(BF16) for j in n]
    o = [_dot(sel_b, av[j]) for j in n]

    a = [jnp.zeros((c, c), F32) for _ in n]
    for li, m in enumerate(HGRN_LEVELS):
        fac = []
        for j in n:
            pieces = []
            for pair in range(c // (2 * m)):
                r = 2 * pair * m + (m - 1 if fwd[j] else m)
                pieces.append(jnp.broadcast_to(b[j][r:r + 1, :], (2 * m, HGRN_KEY_DIM)))
            bnd = pieces[0] if len(pieces) == 1 else jnp.concatenate(pieces, axis=0)
            fac.append(jnp.exp(-jnp.abs(b[j] - bnd)))
        prod = [_dot_nt((q[j] * fac[j]).astype(BF16), (k[j] * fac[j]).astype(BF16)) for j in n]
        a = [a[j] + items[j][6][li] * prod[j] for j in n]
    o = [o[j] + _dot(a[j].astype(BF16), v[j].astype(BF16)) for j in n]
    return o, q, k, b, total


def _hgrn_carry(o, q, k, v, b, total, st):
    o = o + _dot_nt((q * jnp.exp(b)).astype(BF16), st.astype(BF16))
    kc = (k * jnp.exp(total - b)).astype(BF16)
    return o, st * jnp.exp(total) + _dot(v.T.astype(BF16), kc)


def _hgrn_kernel(*refs, n_chunks, has_s0, emit_state):
    (qb_ref, ff_ref, fb_ref, ib_ref, gb_ref, lbp_ref, gw_ref,
     ef_ref, mf_ref, eb_ref, mb_ref, ones_ref, sel_ref) = refs[:13]
    rest = list(refs[13:])
    s0_ref = rest.pop(0) if has_s0 else None
    r_ref = rest.pop(0)
    sout_ref = rest.pop(0) if emit_state else None
    of_scr, ob_scr, stf_scr, stb_scr = rest
    c = HGRN_CHUNK
    ones_b = ones_ref[...]
    sel_b = sel_ref[...]
    gw = gw_ref[...]

    for d, st_scr in enumerate((stf_scr, stb_scr)):
        if has_s0:
            st_scr[...] = s0_ref[d].T
        else:
            st_scr[...] = jnp.zeros((HGRN_VAL_DIM, HGRN_KEY_DIM), F32)

    group = min(HGRN_GROUP, n_chunks)

    def sweep(i, carry):
        sls, items = [], []
        for u in range(group):
            sls.append(pl.ds(pl.multiple_of((i * group + u) * c, c), c))
            items.append((qb_ref[sls[-1], :], ff_ref[sls[-1], :], ib_ref[sls[-1], :], lbp_ref[0:1, :],
                          lbp_ref[1:2, :], ef_ref, mf_ref, True))
        for u in range(group):
            sls.append(pl.ds(pl.multiple_of((n_chunks - 1 - (i * group + u)) * c, c), c))
            items.append((qb_ref[sls[-1], :], fb_ref[sls[-1], :], ib_ref[sls[-1], :], lbp_ref[2:3, :],
                          lbp_ref[3:4, :], eb_ref, mb_ref, False))
        o, q, k, b, total = _hgrn_intra(items, ones_b, sel_b)
        for d, (st_scr, o_scr) in enumerate(((stf_scr, of_scr), (stb_scr, ob_scr))):
            st = st_scr[...]
            for u in range(group):
                j = d * group + u
                o_j, st = _hgrn_carry(o[j], q[j], k[j], items[j][2], b[j], total[j], st)
                o_scr[sls[j], :] = o_j
            st_scr[...] = st
        return carry

    lax.fori_loop(0, n_chunks // group, sweep, 0)
    if emit_state:
        sout_ref[0] = stf_scr[...].T
        sout_ref[1] = stb_scr[...].T

    def readout(ci, carry):
        sl = pl.ds(pl.multiple_of(ci * c, c), c)
        tot = of_scr[sl, :] + ob_scr[sl, :]
        ms = jnp.mean(tot * tot, axis=-1, keepdims=True)
        r_ref[sl, :] = (tot * lax.rsqrt(ms + NORM_EPS) * gw) * _silu(gb_ref[sl, :])
        return carry

    lax.fori_loop(0, n_chunks, readout, 0, unroll=2)


def _hgrn(proj, lbp, gw, consts, n_b, t_len, row_block0, n_rows, s0=None, s0_layer=0, emit_state=False):
    n_chunks = t_len // HGRN_CHUNK
    e_f, m_f, e_b, m_b, ones_b, sel_b = consts

    def col(off):
        return pl.BlockSpec((t_len, LANE), lambda b, h: (row_block0 + b, COL_HGRN + off * HGRN_HEADS + h))

    def whole(a):
        nd = a.ndim
        return pl.BlockSpec(a.shape, lambda b, h: (0,) * nd)

    in_specs = [col(0), col(1), col(2), col(3), col(4),
                pl.BlockSpec((None, 4, LANE), lambda b, h: (h, 0, 0)),
                pl.BlockSpec((1, HGRN_VAL_DIM), lambda b, h: (0, 0)),
                whole(e_f), whole(m_f), whole(e_b), whole(m_b), whole(ones_b), whole(sel_b)]
    args = [proj, proj, proj, proj, proj, lbp, gw.reshape(1, HGRN_VAL_DIM), e_f, m_f, e_b, m_b, ones_b, sel_b]
    if s0 is not None:
        in_specs.append(pl.BlockSpec((None, None, 2, None, HGRN_KEY_DIM, HGRN_VAL_DIM),
                                     lambda b, h: (b, s0_layer, 0, h, 0, 0)))
        args.append(s0)
    out_shape = [jax.ShapeDtypeStruct((n_rows, HGRN_WIDTH), F32)]
    out_specs = [pl.BlockSpec((t_len, LANE), lambda b, h: (b, h))]
    if emit_state:
        out_shape.append(jax.ShapeDtypeStruct((n_b, 2, HGRN_HEADS, HGRN_KEY_DIM, HGRN_VAL_DIM), F32))
        out_specs.append(pl.BlockSpec((None, 2, None, HGRN_KEY_DIM, HGRN_VAL_DIM),
                                      lambda b, h: (b, 0, h, 0, 0)))
    return pl.pallas_call(
        functools.partial(_hgrn_kernel, n_chunks=n_chunks, has_s0=s0 is not None, emit_state=emit_state),
        out_shape=out_shape,
        grid=(n_b, HGRN_HEADS),
        in_specs=in_specs,
        out_specs=out_specs,
        scratch_shapes=[pltpu.VMEM((t_len, HGRN_VAL_DIM), F32), pltpu.VMEM((t_len, HGRN_VAL_DIM), F32),
                        pltpu.VMEM((HGRN_VAL_DIM, HGRN_KEY_DIM), F32), pltpu.VMEM((HGRN_VAL_DIM, HGRN_KEY_DIM), F32)],
        compiler_params=_params("parallel", "parallel"),
        name="hgrn_lat" if s0 is not None else "hgrn_ctx",
    )(*args)


def _outproj_kernel(*refs, n_parts, n_x, rows):
    pair_refs = refs[:2 * n_parts]
    w_refs = refs[2 * n_parts:3 * n_parts]
    x_refs = refs[3 * n_parts:3 * n_parts + n_x]
    mod_ref, o_ref = refs[3 * n_parts + n_x:]
    acc = None
    for p in range(n_parts):
        a = rows.stream_tile(pair_refs[2 * p:2 * p + 2]).astype(BF16)
        d = _dot(a, w_refs[p][...])
        acc = d if acc is None else acc + d
    o_ref[...] = rows.stream_tile(x_refs) + mod_ref[2:3, :] * acc


def _outproj(rows, pairs, weights, x, mods):
    in_specs, args = [], []
    for pair in pairs:
        specs, arrs = rows.stream_specs(pair)
        in_specs += specs
        args += arrs
    in_specs += [pl.BlockSpec(w.shape, lambda i: (0, 0)) for w in weights]
    x_specs, x_args = rows.stream_specs(x)
    return pl.pallas_call(
        functools.partial(_outproj_kernel, n_parts=len(pairs), n_x=len(x_args), rows=rows),
        out_shape=jax.ShapeDtypeStruct((rows.n, D_MODEL), F32),
        grid=(rows.n_tiles,),
        in_specs=in_specs + x_specs + [rows.mod_spec()],
        out_specs=pl.BlockSpec((TM, D_MODEL), lambda i: (i, 0)),
        compiler_params=_params("arbitrary"),
        name="outproj",
    )(*args, *weights, *x_args, mods)


def _mla_proj_kernel(x_ref, mod_ref, nw_ref, wd_ref, qnw_ref, kvnw_ref, wuq_ref, wuqp_ref, cos_ref, sin_ref,
                     q_ref, ckv_ref, krb_ref):
    nf = MLA_ROPE_DIM // 4
    h = _normmod(x_ref[...], nw_ref[...], mod_ref[0:1, :], mod_ref[1:2, :]).astype(BF16)
    t1 = _dot(h, wd_ref[...])
    qd = t1[:, :MLA_Q_RANK]
    kvd = t1[:, MLA_Q_RANK:MLA_Q_RANK + MLA_KV_RANK]
    cos = cos_ref[...]
    sin = sin_ref[...]
    qn = qd * lax.rsqrt(jnp.mean(qd * qd, axis=-1, keepdims=True) + NORM_EPS) * qnw_ref[...]
    ckv_ref[...] = kvd * lax.rsqrt(jnp.mean(kvd * kvd, axis=-1, keepdims=True) + NORM_EPS) * kvnw_ref[...]
    krb_ref[...] = _rope(t1[:, MLA_Q_RANK + MLA_KV_RANK:], cos, sin, nf)
    qb = qn.astype(BF16)
    scale = MLA_QK_DIM ** -0.5
    cos2 = jnp.concatenate([cos, cos], axis=1)
    sin2 = jnp.concatenate([sin, sin], axis=1)
    for hp in range(MLA_HEADS // 2):
        cols = slice(hp * 2 * LANE, (hp + 1) * 2 * LANE)
        qh = _dot(qb, wuq_ref[:, cols])
        qp = _dot(qb, wuqp_ref[:, cols])
        q_ref[:, cols] = ((qh * cos2 + qp * sin2) * scale).astype(BF16)


def _mla_proj(rows, x, mods, nw, wd, qnw, kvnw, wuq, wuq_partner, cos, sin):
    return pl.pallas_call(
        _mla_proj_kernel,
        out_shape=[jax.ShapeDtypeStruct((rows.n, MLA_HEADS * LANE), BF16),
                   jax.ShapeDtypeStruct((rows.n, MLA_KV_RANK), F32),
                   jax.ShapeDtypeStruct((rows.n, LANE), F32)],
        grid=(rows.n_tiles,),
        in_specs=[
            pl.BlockSpec((TM, D_MODEL), lambda i: (i, 0)),
            rows.mod_spec(),
            pl.BlockSpec((1, D_MODEL), lambda i: (0, 0)),
            pl.BlockSpec((D_MODEL, MLA_DOWN_WIDTH), lambda i: (0, 0)),
            pl.BlockSpec((1, MLA_Q_RANK), lambda i: (0, 0)),
            pl.BlockSpec((1, MLA_KV_RANK), lambda i: (0, 0)),
            pl.BlockSpec((MLA_Q_RANK, MLA_HEADS * LANE), lambda i: (0, 0)),
            pl.BlockSpec((MLA_Q_RANK, MLA_HEADS * LANE), lambda i: (0, 0)),
            pl.BlockSpec((TM, LANE), lambda i: (rows.pos_block(i), 0)),
            pl.BlockSpec((TM, LANE), lambda i: (rows.pos_block(i), 0)),
        ],
        out_specs=[pl.BlockSpec((TM, MLA_HEADS * LANE), lambda i: (i, 0)),
                   pl.BlockSpec((TM, MLA_KV_RANK), lambda i: (i, 0)),
                   pl.BlockSpec((TM, LANE), lambda i: (i, 0))],
        compiler_params=_params("parallel"),
        name="mla_proj",
    )(x, mods, nw.reshape(1, D_MODEL), wd, qnw.reshape(1, MLA_Q_RANK), kvnw.reshape(1, MLA_KV_RANK), wuq, wuq_partner, cos, sin)


MLA_TQ = 256
MLA_HPS = 4


def _mla_attn_kernel(q_ref, ckv_ref, krb_ref, wkv_ref, o_ref, k_scr, v_scr, *, t_len):
    tq = min(MLA_TQ, t_len)
    ckv = ckv_ref[...].astype(BF16)
    krb = krb_ref[...]
    for hh in range(MLA_HPS):
        kvh = _dot(ckv, wkv_ref[hh])
        k_scr[hh] = (kvh[:, :LANE] + krb).astype(BF16)
        v_scr[hh] = kvh[:, LANE:].astype(BF16)

    def body(ti, carry):
        sl = pl.ds(pl.multiple_of(ti * tq, tq), tq)
        heads = range(MLA_HPS)
        s = [_dot_nt(q_ref[sl, hh * LANE:(hh + 1) * LANE], k_scr[hh]) for hh in heads]
        m = [jnp.max(s[hh], axis=-1, keepdims=True) for hh in heads]
        p = [jnp.exp(s[hh] - m[hh]) for hh in heads]
        den = [jnp.sum(p[hh], axis=-1, keepdims=True) for hh in heads]
        o = [_dot(p[hh].astype(BF16), v_scr[hh]) / den[hh] for hh in heads]
        for pair in range(MLA_HPS // 2):
            o_ref[sl, pair * LANE:(pair + 1) * LANE] = o[2 * pair] + o[2 * pair + 1]
        return carry

    lax.fori_loop(0, t_len // tq, body, 0)


def _mla_attn(q, ckv_all, krb_all, wkv, n_b, t_len, q_row_block0, n_rows):
    s_len = ckv_all.shape[1]
    return pl.pallas_call(
        functools.partial(_mla_attn_kernel, t_len=t_len),
        out_shape=jax.ShapeDtypeStruct((n_rows, MLA_HEADS * MLA_V_DIM), F32),
        grid=(n_b, MLA_HEADS // MLA_HPS),
        in_specs=[
            pl.BlockSpec((t_len, MLA_HPS * LANE), lambda b, hp: (q_row_block0 + b, hp)),
            pl.BlockSpec((None, s_len, MLA_KV_RANK), lambda b, hp: (b, 0, 0)),
            pl.BlockSpec((None, s_len, LANE), lambda b, hp: (b, 0, 0)),
            pl.BlockSpec((MLA_HPS, MLA_KV_RANK, 2 * LANE), lambda b, hp: (hp, 0, 0)),
        ],
        out_specs=pl.BlockSpec((t_len, MLA_HPS // 2 * LANE), lambda b, hp: (b, hp)),
        scratch_shapes=[pltpu.VMEM((MLA_HPS, s_len, LANE), BF16), pltpu.VMEM((MLA_HPS, s_len, LANE), BF16)],
        compiler_params=_params("parallel", "arbitrary"),
        name="mla_attn",
    )(q, ckv_all, krb_all, wkv)


META_E1, META_E2, META_W1, META_W2, META_R1, META_R2 = range(6)


def _router_kernel(x_ref, mod_ref, nw_ref, wr_ref, br_ref, tri_ref, xn_ref, meta_ref, cnt_ref, base_scr):
    @pl.when(pl.program_id(0) == 0)
    def _():
        base_scr[...] = jnp.zeros(base_scr.shape, F32)

    xn = _normmod(x_ref[...], nw_ref[...], mod_ref[3:4, :], mod_ref[4:5, :])
    xn_ref[...] = xn
    logits = _dot_f32ish(xn, wr_ref[...]) + br_ref[...]
    lane = lax.broadcasted_iota(I32, logits.shape, 1).astype(F32)
    far = float(LANE)

    def first_argmax(vals, vmax):
        return jnp.min(jnp.where(vals == vmax, lane, far), axis=-1, keepdims=True)

    gl = jnp.where(lane < MOE_GROUPS, logits, NEG_BIG)
    gmax = jnp.max(gl, axis=-1, keepdims=True)
    g_w = 1.0 / jnp.sum(jnp.exp(gl - gmax), axis=-1, keepdims=True)
    g_idx = first_argmax(gl, gmax)
    e_lo = MOE_GROUPS + MOE_EPG * g_idx
    el = jnp.where((lane >= e_lo) & (lane < e_lo + MOE_EPG), logits, NEG_BIG)
    m1 = jnp.max(el, axis=-1, keepdims=True)
    i1 = first_argmax(el, m1)
    el2 = jnp.where(lane == i1, NEG_BIG, el)
    m2 = jnp.max(el2, axis=-1, keepdims=True)
    i2 = first_argmax(el2, m2)
    esum = jnp.sum(jnp.exp(el - m1), axis=-1, keepdims=True)
    p1 = 1.0 / esum
    p2 = jnp.exp(m2 - m1) / esum
    w1 = g_w * (p1 / (p1 + p2))
    w2 = g_w * (p2 / (p1 + p2))
    e1 = i1 - MOE_GROUPS
    e2 = i2 - MOE_GROUPS

    oh1 = lane == e1
    oh2 = lane == e2
    oh = jnp.where(oh1 | oh2, 1.0, 0.0)
    before = _dot(tri_ref[...], oh.astype(BF16)) + base_scr[0:1, :]
    r1 = jnp.sum(jnp.where(oh1, before, 0.0), axis=-1, keepdims=True)
    r2 = jnp.sum(jnp.where(oh2, before, 0.0), axis=-1, keepdims=True)
    base_scr[...] = base_scr[...] + jnp.sum(oh, axis=0, keepdims=True)
    cnt_ref[...] = base_scr[...]

    meta = jnp.zeros(logits.shape, F32)
    for slot, val in ((META_E1, e1), (META_E2, e2), (META_W1, w1), (META_W2, w2), (META_R1, r1), (META_R2, r2)):
        meta = jnp.where(lane == slot, val, meta)
    meta_ref[...] = meta


def _router(rows, x, mods, nw, wr, br, tri):
    return pl.pallas_call(
        _router_kernel,
        out_shape=[jax.ShapeDtypeStruct((rows.n, D_MODEL), F32),
                   jax.ShapeDtypeStruct((rows.n, LANE), F32),
                   jax.ShapeDtypeStruct((8, LANE), F32)],
        grid=(rows.n_tiles,),
        in_specs=[
            pl.BlockSpec((TM, D_MODEL), lambda i: (i, 0)),
            rows.mod_spec(),
            pl.BlockSpec((1, D_MODEL), lambda i: (0, 0)),
            pl.BlockSpec((D_MODEL, LANE), lambda i: (0, 0)),
            pl.BlockSpec((1, LANE), lambda i: (0, 0)),
            pl.BlockSpec((TM, TM), lambda i: (0, 0)),
        ],
        out_specs=[pl.BlockSpec((TM, D_MODEL), lambda i: (i, 0)),
                   pl.BlockSpec((TM, LANE), lambda i: (i, 0)),
                   pl.BlockSpec((8, LANE), lambda i: (0, 0))],
        scratch_shapes=[pltpu.VMEM((8, LANE), F32)],
        compiler_params=_params("arbitrary"),
        name="moe_router",
    )(x, mods, nw.reshape(1, D_MODEL), wr, br, tri)


def _row_copy(src, src_row, dst, dst_row, sem):
    return pltpu.make_async_copy(src.at[pl.ds(src_row, 1), :], dst.at[pl.ds(dst_row, 1), :], sem)


def _dispatch_kernel(fill_ref, pos_ref, xn_ref, xs_hbm, zero_scr, sem, fill_sem, *, n_tiles):
    @pl.when(pl.program_id(0) == 0)
    def _():
        zero_scr[...] = jnp.zeros(zero_scr.shape, F32)

        def fill_copy(t):
            return pltpu.make_async_copy(zero_scr, xs_hbm.at[pl.ds(pl.multiple_of(t * TME, TME), TME), :], fill_sem)

        def fill_start(t, carry):
            @pl.when(fill_ref[t] != 0)
            def _():
                fill_copy(t).start()
            return carry

        def fill_wait(t, carry):
            @pl.when(fill_ref[t] != 0)
            def _():
                fill_copy(t).wait()
            return carry

        lax.fori_loop(0, n_tiles, fill_start, 0)
        lax.fori_loop(0, n_tiles, fill_wait, 0)

    def start(r, carry):
        for k in range(2):
            _row_copy(xn_ref, r, xs_hbm, pos_ref[0, 2 * r + k], sem).start(priority=k)
        return carry

    lax.fori_loop(0, TM, start, 0, unroll=DMA_UNROLL)
    for _ in range(2):
        pltpu.make_async_copy(xn_ref, xs_hbm.at[pl.ds(0, TM), :], sem).wait()


def _dispatch(rows, tile_fill, pos, xn, n_tiles):
    return pl.pallas_call(
        functools.partial(_dispatch_kernel, n_tiles=n_tiles),
        out_shape=jax.ShapeDtypeStruct((n_tiles * TME, D_MODEL), F32),
        grid_spec=pltpu.PrefetchScalarGridSpec(
            num_scalar_prefetch=1,
            grid=(rows.n_tiles,),
            in_specs=[
                pl.BlockSpec((None, 1, 2 * TM), lambda i, fill: (i, 0, 0), memory_space=pltpu.SMEM),
                pl.BlockSpec((TM, D_MODEL), lambda i, fill: (i, 0)),
            ],
            out_specs=pl.BlockSpec(memory_space=pl.ANY),
            scratch_shapes=[pltpu.VMEM((TME, D_MODEL), F32), pltpu.SemaphoreType.DMA(()),
                            pltpu.SemaphoreType.DMA(())],
        ),
        compiler_params=_params("arbitrary"),
        name="moe_dispatch",
    )(tile_fill, pos, xn)


def _ffn_kernel(te_ref, nv_ref, x_ref, wg_ref, wu_ref, wd_ref, y_ref, wg_b, wu_b, wd_b):
    t = pl.program_id(0)
    valid = t < nv_ref[0]
    new_expert = (t == 0) | (te_ref[t] != te_ref[jnp.maximum(t - 1, 0)])

    @pl.when(valid & new_expert)
    def _():
        wg_b[...] = wg_ref[...].astype(BF16)
        wu_b[...] = wu_ref[...].astype(BF16)
        wd_b[...] = wd_ref[...].astype(BF16)

    @pl.when(valid)
    def _():
        x = x_ref[...].astype(BF16)
        a = _silu(_dot(x, wg_b[...])) * _dot(x, wu_b[...])
        y_ref[...] = _dot(a.astype(BF16), wd_b[...])

    @pl.when(jnp.logical_not(valid))
    def _():
        y_ref[...] = jnp.zeros(y_ref.shape, F32)


def _ffn(tile_expert, n_valid, xs, w_gate, w_up, w_down, layer, n_tiles):
    def xmap(t, te, nv):
        return (jnp.minimum(t, nv[0] - 1), 0)

    def wmap(t, te, nv):
        return (layer, te[t], 0, 0)

    return pl.pallas_call(
        _ffn_kernel,
        out_shape=jax.ShapeDtypeStruct((n_tiles * TME, D_MODEL), F32),
        grid_spec=pltpu.PrefetchScalarGridSpec(
            num_scalar_prefetch=2,
            grid=(n_tiles,),
            in_specs=[
                pl.BlockSpec((TME, D_MODEL), xmap),
                pl.BlockSpec((None, None, D_MODEL, MOE_HIDDEN), wmap),
                pl.BlockSpec((None, None, D_MODEL, MOE_HIDDEN), wmap),
                pl.BlockSpec((None, None, MOE_HIDDEN, D_MODEL), wmap),
            ],
            out_specs=pl.BlockSpec((TME, D_MODEL), lambda t, te, nv: (t, 0)),
            scratch_shapes=[pltpu.VMEM((D_MODEL, MOE_HIDDEN), BF16),
                            pltpu.VMEM((D_MODEL, MOE_HIDDEN), BF16),
                            pltpu.VMEM((MOE_HIDDEN, D_MODEL), BF16)],
        ),
        compiler_params=_params("arbitrary"),
        name="moe_ffn",
    )(tile_expert, n_valid, xs, w_gate, w_up, w_down)


def _combine_kernel(pos_ref, x_ref, meta_ref, mod_ref, fnw_ref, ys_hbm, *rest, final, ctx_tiles):
    if final:
        o_ctx_ref, o_lat_ref, buf0, buf1, sem = rest
    else:
        o_ref, buf0, buf1, sem = rest
    bufs = (buf0, buf1)

    def start(r, carry):
        for k in range(2):
            _row_copy(ys_hbm, pos_ref[0, 2 * r + k], bufs[k], r, sem).start(priority=k)
        return carry

    lax.fori_loop(0, TM, start, 0, unroll=DMA_UNROLL)
    for k in range(2):
        pltpu.make_async_copy(ys_hbm.at[pl.ds(0, TM), :], bufs[k], sem).wait()
    meta = meta_ref[...]
    y = meta[:, META_W1:META_W1 + 1] * buf0[...] + meta[:, META_W2:META_W2 + 1] * buf1[...]
    xo = x_ref[...] + mod_ref[5:6, :] * y
    if not final:
        o_ref[...] = xo
        return
    xo = xo * lax.rsqrt(jnp.mean(xo * xo, axis=-1, keepdims=True) + NORM_EPS) * fnw_ref[...]
    is_ctx = pl.program_id(0) < ctx_tiles

    @pl.when(is_ctx)
    def _():
        o_ctx_ref[...] = xo

    @pl.when(jnp.logical_not(is_ctx))
    def _():
        o_lat_ref[...] = xo


def _combine(rows, pos, x, meta, mods, fnw, ys, final):
    ct = rows.ctx_tiles
    if final:
        out_shape = [jax.ShapeDtypeStruct((rows.nc, D_MODEL), F32), jax.ShapeDtypeStruct((rows.nl, D_MODEL), F32)]
        out_specs = [pl.BlockSpec((TM, D_MODEL), lambda i: (jnp.minimum(i, ct - 1), 0)),
                     pl.BlockSpec((TM, D_MODEL), lambda i: (jnp.maximum(i - ct, 0), 0))]
    else:
        out_shape = jax.ShapeDtypeStruct((rows.n, D_MODEL), F32)
        out_specs = pl.BlockSpec((TM, D_MODEL), lambda i: (i, 0))
    return pl.pallas_call(
        functools.partial(_combine_kernel, final=final, ctx_tiles=ct),
        out_shape=out_shape,
        grid=(rows.n_tiles,),
        in_specs=[
            pl.BlockSpec((None, 1, 2 * TM), lambda i: (i, 0, 0), memory_space=pltpu.SMEM),
            pl.BlockSpec((TM, D_MODEL), lambda i: (i, 0)),
            pl.BlockSpec((TM, LANE), lambda i: (i, 0)),
            rows.mod_spec(),
            pl.BlockSpec((1, D_MODEL), lambda i: (0, 0)),
            pl.BlockSpec(memory_space=pl.ANY),
        ],
        out_specs=out_specs,
        scratch_shapes=[pltpu.VMEM((TM, D_MODEL), F32), pltpu.VMEM((TM, D_MODEL), F32),
                        pltpu.SemaphoreType.DMA(())],
        compiler_params=_params("arbitrary"),
        name="moe_combine",
    )(pos, x, meta, mods, fnw.reshape(1, D_MODEL), ys)


def _moe(rows, x, mods, nw, wr, br, tri, w_gate, w_up, w_down, layer, fnw, final):
    n_assign = 2 * rows.n
    n_tiles = n_assign // TME + MOE_EXPERTS
    xn, meta, cnt = _router(rows, x, mods, nw, wr, br, tri)

    counts = cnt[0, :MOE_EXPERTS].astype(I32)
    padded = ((counts + TME - 1) // TME) * TME
    ends = jnp.cumsum(padded)
    starts = ends - padded
    experts = jnp.arange(MOE_EXPERTS, dtype=I32)
    e = meta[:, META_E1:META_E2 + 1].astype(I32)
    rank = meta[:, META_R1:META_R2 + 1].astype(I32)
    start_of = jnp.sum(jnp.where(e[..., None] == experts, starts, 0), axis=-1)
    pos = (start_of + rank).reshape(rows.n_tiles, 1, 2 * TM)
    n_valid = ends[-1] // TME
    tile_first = jnp.arange(n_tiles, dtype=I32) * TME
    tile_start = jnp.minimum(tile_first, ends[-1] - TME)
    tile_expert = jnp.sum((ends[None, :] <= tile_start[:, None]).astype(I32), axis=1)
    tile_expert = jnp.minimum(tile_expert, MOE_EXPERTS - 1)
    tile_oh = tile_expert[:, None] == experts
    tile_rows = jnp.sum(jnp.where(tile_oh, counts + starts, 0), axis=1) - tile_start
    tile_fill = ((tile_first >= ends[-1]) | (tile_rows < TME)).astype(I32)

    xs = _dispatch(rows, tile_fill, pos, xn, n_tiles)
    ys = _ffn(tile_expert, n_valid.reshape(1).astype(I32), xs, w_gate, w_up, w_down, layer, n_tiles)
    return _combine(rows, pos, x, meta, mods, fnw, ys, final)


def _lower_bound_params(p):
    pr = jax.nn.softmax(p.astype(F32), axis=0)
    lb = jnp.cumsum(pr, axis=0) - pr[0:1]
    lb = jnp.clip(lb, 0.0, 1.0 - 1e-6)
    return jnp.maximum(lb, LOG_TINY), 1.0 - lb


def kernel(x_prompt, x_sample, c, cache_swa_k, cache_swa_v, state_hgrn, cache_mla_ckv, cache_mla_krope, c_ctx, mod_w, mod_b, norm1_w, norm2_w, final_norm_w, even_w_in, even_w_out, swa_sink, hgrn_lb_fwd, hgrn_lb_bwd, hgrn_gnorm_w, mla_w_dq, mla_qnorm_w, mla_w_uq, mla_w_dkv, mla_kvnorm_w, mla_w_ukv, mla_w_o, moe_router_group_w, moe_router_group_b, moe_router_expert_w, moe_router_expert_b, moe_w_gate, moe_w_up, moe_w_down):
    nc_b, nc_t, _ = x_prompt.shape
    nl_b, nl_t, _ = x_sample.shape
    rows = _Rows(nc_b, nc_t, nl_b, nl_t)
    past = cache_swa_k.shape[2]

    x = (x_prompt.reshape(rows.nc, D_MODEL), x_sample.reshape(rows.nl, D_MODEL))
    mod_rows = 16
    cvec = jnp.concatenate([c_ctx[None, :], c, jnp.zeros((mod_rows - 1 - nl_b, D_MODEL), F32)], axis=0)
    mods_all = _modulation(cvec, mod_w, mod_b).reshape(DEPTH, mod_rows, 6, D_MODEL)

    hconsts = _hgrn_consts()
    la_f, l1_f = _lower_bound_params(hgrn_lb_fwd)
    la_b, l1_b = _lower_bound_params(hgrn_lb_bwd)
    lbp_all = jnp.stack([la_f, l1_f, la_b, l1_b], axis=1).reshape(N_EVEN, 4, HGRN_HEADS, LANE).transpose(0, 2, 1, 3)
    swa_cos, swa_sin = _rope_tables(nl_t, SWA_HEAD_DIM, 0, LANE, 0)
    mla_cos, mla_sin = _rope_tables(nl_t, MLA_ROPE_DIM, MLA_NOPE_DIM, MLA_QK_DIM, TM)
    tri =jnp.asarray(np.tril(np.ones((TM, TM), np.float32), -1), BF16)

    new_k, new_v, new_s, new_ckv, new_kr = [], [], [], [], []
    for l in range(DEPTH):
        j = l // 2
        mods = mods_all[l]
        if l % 2 == 0:
            proj = _even_proj(rows, x, mods, norm1_w[l], even_w_in[j].astype(BF16))
            a_ctx = _swa_ctx(rows, proj, swa_sink[j])
            a_lat = _swa_lat(rows, proj, swa_sink[j],
                             cache_swa_k[:, j].reshape(nl_b, past, SWA_KV_WIDTH),
                             cache_swa_v[:, j].reshape(nl_b, past, SWA_KV_WIDTH), swa_cos, swa_sin)
            r_ctx, s_ctx = _hgrn(proj, lbp_all[j], hgrn_gnorm_w[j], hconsts, nc_b, nc_t, 0, rows.nc,
                                 emit_state=True)
            (r_lat,) = _hgrn(proj, lbp_all[j], hgrn_gnorm_w[j], hconsts, nl_b, nl_t, rows.nc // nl_t, rows.nl,
                             s0=state_hgrn, s0_layer=j)
            w_out = even_w_out[j].astype(BF16)
            x = _outproj(rows, [(a_ctx, a_lat), (r_ctx, r_lat)], [w_out[:SWA_WIDTH], w_out[SWA_WIDTH:]], x, mods)
            kv = proj[:rows.nc, SWA_WIDTH:SWA_WIDTH + 2 * SWA_KV_WIDTH]
            new_k.append(kv[:, :SWA_KV_WIDTH].reshape(nc_b, nc_t, SWA_KV_HEADS, SWA_HEAD_DIM))
            new_v.append(kv[:, SWA_KV_WIDTH:].reshape(nc_b, nc_t, SWA_KV_HEADS, SWA_HEAD_DIM))
            new_s.append(s_ctx)
        else:
            wd = jnp.zeros((D_MODEL, MLA_DOWN_WIDTH), F32)
            wd = wd.at[:, :MLA_Q_RANK].set(mla_w_dq[j])
            wd = wd.at[:, MLA_Q_RANK:MLA_Q_RANK + MLA_KV_RANK].set(mla_w_dkv[j][:, :MLA_KV_RANK])
            kr_lo = MLA_Q_RANK + MLA_KV_RANK + MLA_NOPE_DIM
            wd = wd.at[:, kr_lo:kr_lo + MLA_ROPE_DIM].set(mla_w_dkv[j][:, MLA_KV_RANK:])
            wuq = jnp.pad(mla_w_uq[j].reshape(MLA_Q_RANK, MLA_HEADS, MLA_QK_DIM),
                          ((0, 0), (0, 0), (0, LANE - MLA_QK_DIM))).reshape(MLA_Q_RANK, MLA_HEADS * LANE)
            wukv = mla_w_ukv[j].reshape(MLA_KV_RANK, MLA_HEADS, MLA_NOPE_DIM + MLA_V_DIM).transpose(1, 0, 2)
            wk = jnp.pad(wukv[..., :MLA_NOPE_DIM], ((0, 0), (0, 0), (0, LANE - MLA_NOPE_DIM)))
            wv_e = jnp.pad(wukv[..., MLA_NOPE_DIM:], ((0, 0), (0, 0), (0, LANE - MLA_V_DIM)))
            wv_o = jnp.pad(wukv[..., MLA_NOPE_DIM:], ((0, 0), (0, 0), (LANE - MLA_V_DIM, 0)))
            odd = (jnp.arange(MLA_HEADS) % 2 == 1)[:, None, None]
            wv = jnp.where(odd, wv_o, wv_e)
            wuq = wuq.astype(BF16)
            w3 = wuq.reshape(MLA_Q_RANK, MLA_HEADS, LANE)
            nf = MLA_ROPE_DIM // 4
            rot = w3[..., MLA_NOPE_DIM:MLA_QK_DIM].reshape(MLA_Q_RANK, MLA_HEADS, 2, 2, nf)[..., ::-1, :]
            wuq_partner = jnp.concatenate([w3[..., :MLA_NOPE_DIM], rot.reshape(MLA_Q_RANK, MLA_HEADS, MLA_ROPE_DIM),
                                           w3[..., MLA_QK_DIM:]], axis=-1).reshape(MLA_Q_RANK, MLA_HEADS * LANE)
            q, ckv, krb = _mla_proj(rows, x, mods, norm1_w[l], wd.astype(BF16), mla_qnorm_w[j], mla_kvnorm_w[j],
                                    wuq, wuq_partner, mla_cos, mla_sin)
            wkv = jnp.concatenate([wk, wv], axis=-1).astype(BF16)
            ckv_c = ckv[:rows.nc].reshape(nc_b, nc_t, MLA_KV_RANK)
            krb_c = krb[:rows.nc].reshape(nc_b, nc_t, LANE)
            o_ctx = _mla_attn(q, ckv_c, krb_c, wkv, nc_b, nc_t, 0, rows.nc)
            cache_kr = jnp.pad(cache_mla_krope[:, j], ((0, 0), (0, 0), (MLA_NOPE_DIM, LANE - MLA_QK_DIM)))
            ckv_l = jnp.concatenate([cache_mla_ckv[:, j], ckv[rows.nc:].reshape(nl_b, nl_t, MLA_KV_RANK)], axis=1)
            krb_l = jnp.concatenate([cache_kr, krb[rows.nc:].reshape(nl_b, nl_t, LANE)], axis=1)
            o_lat = _mla_attn(q, ckv_l, krb_l, wkv, nl_b, nl_t, rows.nc // nl_t, rows.nl)
            x = _outproj(rows, [(o_ctx, o_lat)], [mla_w_o[j].astype(BF16)], x, mods)
            new_ckv.append(ckv_c)
            new_kr.append(krb_c[..., MLA_NOPE_DIM:MLA_QK_DIM])

        wr = jnp.zeros((D_MODEL, LANE), F32)
        wr = wr.at[:, :MOE_GROUPS].set(moe_router_group_w[l])
        wr = wr.at[:, MOE_GROUPS:MOE_GROUPS + MOE_EXPERTS].set(
            moe_router_expert_w[l].transpose(1, 0, 2).reshape(D_MODEL, MOE_EXPERTS))
        br = jnp.zeros((1, LANE), F32)
        br = br.at[0, :MOE_GROUPS].set(moe_router_group_b[l])
        br = br.at[0, MOE_GROUPS:MOE_GROUPS + MOE_EXPERTS].set(moe_router_expert_b[l].reshape(MOE_EXPERTS))
        x = _moe(rows, x, mods, norm2_w[l], wr, br, tri, moe_w_gate, moe_w_up, moe_w_down, l,
                 final_norm_w, final=(l == DEPTH - 1))

    y_prompt = x[0].reshape(nc_b, nc_t, D_MODEL)
    y_sample = x[1].reshape(nl_b, nl_t, D_MODEL)
    return (y_prompt, y_sample, jnp.stack(new_k, axis=1), jnp.stack(new_v, axis=1), jnp.stack(new_s, axis=1),
            jnp.stack(new_ckv, axis=1), jnp.stack(new_kr, axis=1))
```

```python
import functools
import math

import numpy as np
import jax
import jax.numpy as jnp
from jax import lax
from jax.experimental import pallas as pl
from jax.experimental.pallas import tpu as pltpu

F32, BF16, I32 = jnp.float32, jnp.bfloat16, jnp.int32

D_MODEL = 1024
DEPTH = 4
GRID_W = 64
ROPE_THETA = 10000.0
NORM_EPS = 1e-6
NEG_BIG = -1e30
LOG_TINY = 1e-30
N_EVEN = (DEPTH + 1) // 2
N_ODD = DEPTH // 2

SWA_HEADS = 8
SWA_KV_HEADS = 2
SWA_GROUP = SWA_HEADS // SWA_KV_HEADS
SWA_HEAD_DIM = 64
SWA_WIDTH = SWA_HEADS * SWA_HEAD_DIM
SWA_KV_WIDTH = SWA_KV_HEADS * SWA_HEAD_DIM
SWA_WINDOW = 128
SWA_BLOCK = 128

HGRN_HEADS = 4
HGRN_KEY_DIM = 128
HGRN_VAL_DIM = 128
HGRN_WIDTH = HGRN_HEADS * HGRN_KEY_DIM
HGRN_CHUNK = 128
HGRN_SUB = 8
HGRN_LEVELS = (8, 16, 32, 64)
HGRN_GROUP = 4

EVEN_IN_WIDTH = SWA_WIDTH + 2 * SWA_KV_WIDTH + 5 * HGRN_WIDTH
LANE = 128
COL_Q, COL_K, COL_V = 0, SWA_WIDTH // LANE, (SWA_WIDTH + SWA_KV_WIDTH) // LANE
COL_HGRN = (SWA_WIDTH + 2 * SWA_KV_WIDTH) // LANE

MLA_HEADS = 16
MLA_Q_RANK = 384
MLA_KV_RANK = 256
MLA_NOPE_DIM = 64
MLA_ROPE_DIM = 32
MLA_V_DIM = 64
MLA_QK_DIM = MLA_NOPE_DIM + MLA_ROPE_DIM
MLA_DOWN_WIDTH = MLA_Q_RANK + MLA_KV_RANK + LANE

MOE_GROUPS = 4
MOE_EPG = 8
MOE_EXPERTS = MOE_GROUPS * MOE_EPG
MOE_HIDDEN = 256

TM = 512
TME = 512
DMA_UNROLL = 8
VMEM_LIMIT = 48 * 1024 * 1024


def _params(*sem):
    return pltpu.CompilerParams(dimension_semantics=sem, vmem_limit_bytes=VMEM_LIMIT)


def _dot(a, b):
    return jnp.dot(a, b, preferred_element_type=F32)


def _dot_nt(a, b):
    return lax.dot_general(a, b, (((1,), (1,)), ((), ())), preferred_element_type=F32)


def _split2(a):
    hi = a.astype(BF16)
    return hi, (a - hi.astype(F32)).astype(BF16)


def _dot_f32ish(a, b):
    ah, al = _split2(a)
    bh, bl = _split2(b)
    return _dot(ah, bh) + (_dot(ah, bl) + _dot(al, bh))


def _silu(x):
    return x * jax.nn.sigmoid(x)


def _normmod(x, nw, shift, scale):
    ms = jnp.mean(x * x, axis=-1, keepdims=True)
    return (x * lax.rsqrt(ms + NORM_EPS) * nw) * (1.0 + scale) + shift


def _mod_kernel(c_ref, w_ref, b_ref, o_ref):
    o_ref[...] = _dot_f32ish(_silu(c_ref[...]), w_ref[...]) + b_ref[...]


def _modulation(cvec, mod_w, mod_b):
    rows = cvec.shape[0]
    nb = 6 * D_MODEL // 1024
    return pl.pallas_call(
        _mod_kernel,
        out_shape=jax.ShapeDtypeStruct((DEPTH, rows, 6 * D_MODEL), F32),
        grid=(DEPTH, nb),
        in_specs=[
            pl.BlockSpec((rows, D_MODEL), lambda l, n: (0, 0)),
            pl.BlockSpec((None, D_MODEL, 1024), lambda l, n: (l, 0, n)),
            pl.BlockSpec((None, 1, 1024), lambda l, n: (l, 0, n)),
        ],
        out_specs=pl.BlockSpec((None, rows, 1024), lambda l, n: (l, 0, n)),
        compiler_params=_params("parallel", "parallel"),
        name="modulation",
    )(cvec, mod_w, mod_b.reshape(DEPTH, 1, 6 * D_MODEL))


class _Rows:
    def __init__(self, nc_b, nc_t, nl_b, nl_t):
        self.nc_b, self.nc_t, self.nl_b, self.nl_t = nc_b, nc_t, nl_b, nl_t
        self.nc = nc_b * nc_t
        self.nl = nl_b * nl_t
        self.n = self.nc + self.nl
        assert self.nc % TM == 0 and nl_t % TM == 0 and self.nc % nl_t == 0
        self.ctx_tiles = self.nc // TM
        self.tiles_per_lat = nl_t // TM
        self.n_tiles = self.n // TM

    def mod_row(self, i):
        return jnp.where(i < self.ctx_tiles, 0, 1 + (i - self.ctx_tiles) // self.tiles_per_lat)

    def mod_spec(self):
        return pl.BlockSpec((None, 6, D_MODEL), lambda i: (self.mod_row(i), 0, 0))

    def stream_specs(self, x):
        ct = self.ctx_tiles
        if not isinstance(x, tuple):
            return [pl.BlockSpec((TM, x.shape[1]), lambda i: (i, 0))], [x]
        x_ctx, x_lat = x
        return ([pl.BlockSpec((TM, x_ctx.shape[1]), lambda i: (jnp.minimum(i, ct - 1), 0)),
                 pl.BlockSpec((TM, x_lat.shape[1]), lambda i: (jnp.maximum(i - ct, 0), 0))], [x_ctx, x_lat])

    def stream_tile(self, refs):
        if len(refs) == 1:
            return refs[0][...]
        return jnp.where(pl.program_id(0) < self.ctx_tiles, refs[0][...], refs[1][...])

    def pos_block(self, i):
        return jnp.where(i < self.ctx_tiles, 0, 1 + (i - self.ctx_tiles) % self.tiles_per_lat)


def _even_proj_kernel(*refs, rows):
    mod_ref, nw_ref, w_ref, o_ref = refs[-4:]
    x = rows.stream_tile(refs[:-4])
    h = _normmod(x, nw_ref[...], mod_ref[0:1, :], mod_ref[1:2, :]).astype(BF16)
    step = 256
    for c in range(EVEN_IN_WIDTH // step):
        o_ref[:, c * step:(c + 1) * step] = _dot(h, w_ref[:, c * step:(c + 1) * step])


def _even_proj(rows, x, mods, nw, w_in_bf16):
    x_specs, x_args = rows.stream_specs(x)
    return pl.pallas_call(
        functools.partial(_even_proj_kernel, rows=rows),
        out_shape=jax.ShapeDtypeStruct((rows.n, EVEN_IN_WIDTH), F32),
        grid=(rows.n_tiles,),
        in_specs=x_specs + [
            rows.mod_spec(),
            pl.BlockSpec((1, D_MODEL), lambda i: (0, 0)),
            pl.BlockSpec((D_MODEL, EVEN_IN_WIDTH), lambda i: (0, 0)),
        ],
        out_specs=pl.BlockSpec((TM, EVEN_IN_WIDTH), lambda i: (i, 0)),
        compiler_params=_params("arbitrary"),
        name="even_proj",
    )(*x_args, mods, nw.reshape(1, D_MODEL), w_in_bf16)


def _rope_tables(t_len, rot_dim, lane_lo, lane_hi, lead_rows):
    half = rot_dim // 2
    nf = half // 2
    lane = np.arange(LANE)
    d = (lane - lane_lo) % rot_dim
    active = (lane >= lane_lo) & (lane < lane_hi)
    use_col = d >= half
    fidx = d % nf
    first = (d % half) < nf
    pos = jnp.arange(t_len)
    row = (pos // GRID_W).astype(F32)
    col = (pos % GRID_W).astype(F32)
    inv = jnp.exp(-math.log(ROPE_THETA) * jnp.arange(nf, dtype=F32) / nf)
    p = jnp.where(jnp.asarray(use_col)[None, :], col[:, None], row[:, None])
    ang = p * inv[jnp.asarray(fidx)][None, :]
    act = jnp.asarray(active)[None, :]
    cos = jnp.where(act, jnp.cos(ang), 1.0)
    sin = jnp.where(act, jnp.sin(ang), 0.0)
    sin = jnp.where(jnp.asarray(first)[None, :], -sin, sin)
    if lead_rows:
        cos = jnp.concatenate([jnp.ones((lead_rows, LANE), F32), cos], axis=0)
        sin = jnp.concatenate([jnp.zeros((lead_rows, LANE), F32), sin], axis=0)
    return cos, sin


def _rope(x, cos, sin, nf):
    lane = lax.broadcasted_iota(I32, x.shape, 1)
    up = pltpu.roll(x, LANE - nf, axis=1)
    dn = pltpu.roll(x, nf, axis=1)
    partner = jnp.where((lane & nf) == 0, up, dn)
    return x * cos + partner * sin


def _swa_ctx_kernel(sink_ref, q_ref, k_ref, v_ref, o_ref):
    scale = SWA_HEAD_DIM ** -0.5
    heads = range(SWA_HEADS)
    kk = [k_ref[:, kv * SWA_HEAD_DIM:(kv + 1) * SWA_HEAD_DIM].astype(BF16) for kv in range(SWA_KV_HEADS)]
    vv = [v_ref[:, kv * SWA_HEAD_DIM:(kv + 1) * SWA_HEAD_DIM].astype(BF16) for kv in range(SWA_KV_HEADS)]
    q = [q_ref[:, h * SWA_HEAD_DIM:(h + 1) * SWA_HEAD_DIM].astype(BF16) for h in heads]
    s = [_dot_nt(q[h], kk[h // SWA_GROUP]) * scale for h in heads]
    m = [jnp.maximum(jnp.max(s[h], axis=-1, keepdims=True), sink_ref[h]) for h in heads]
    p = [jnp.exp(s[h] - m[h]) for h in heads]
    den = [jnp.sum(p[h], axis=-1, keepdims=True) + jnp.exp(sink_ref[h] - m[h]) for h in heads]
    o = [_dot(p[h].astype(BF16), vv[h // SWA_GROUP]) / den[h] for h in heads]
    for h in heads:
        o_ref[:, h * SWA_HEAD_DIM:(h + 1) * SWA_HEAD_DIM] = o[h]


def _swa_ctx(rows, proj, sink):
    t = rows.nc_t
    return pl.pallas_call(
        _swa_ctx_kernel,
        out_shape=jax.ShapeDtypeStruct((rows.nc, SWA_WIDTH), F32),
        grid=(rows.nc_b,),
        in_specs=[
            pl.BlockSpec(memory_space=pltpu.SMEM),
            pl.BlockSpec((t, SWA_WIDTH), lambda b: (b, COL_Q)),
            pl.BlockSpec((t, LANE), lambda b: (b, COL_K)),
            pl.BlockSpec((t, LANE), lambda b: (b, COL_V)),
        ],
        out_specs=pl.BlockSpec((t, SWA_WIDTH), lambda b: (b, 0)),
        compiler_params=_params("parallel"),
        name="swa_ctx",
    )(sink, proj, proj, proj)


def _lane_fold(x, op):
    out = x[:, :LANE]
    for i in range(1, x.shape[1] // LANE):
        out = op(out, x[:, i * LANE:(i + 1) * LANE])
    return out


def _swa_lat_kernel(sink_ref, q_ref, k_ref, v_ref, kc_ref, vc_ref, cos_ref, sin_ref, o_ref,
                    kl_scr, vl_scr, kc_scr, vc_scr, *, n_blocks):
    scale = SWA_HEAD_DIM ** -0.5
    nf = SWA_HEAD_DIM // 4
    n = pl.program_id(1)

    @pl.when(n == 0)
    def _():
        kr = _rope(k_ref[...], cos_ref[...], sin_ref[...], nf)
        for kv in range(SWA_KV_HEADS):
            cols = slice(kv * SWA_HEAD_DIM, (kv + 1) * SWA_HEAD_DIM)
            kl_scr[kv] = kr[:, cols].astype(BF16)
            vl_scr[kv] = v_ref[:, cols].astype(BF16)
            kc_scr[kv] = kc_ref[:, cols].astype(BF16)
            vc_scr[kv] = vc_ref[:, cols].astype(BF16)

    q0 = pl.multiple_of(n * SWA_BLOCK, SWA_BLOCK)
    cq = cos_ref[pl.ds(q0, SWA_BLOCK), :]
    sq = sin_ref[pl.ds(q0, SWA_BLOCK), :]
    qs = [_rope(q_ref[:, g * LANE:(g + 1) * LANE], cq, sq, nf) for g in range(SWA_WIDTH // LANE)]

    qi = lax.broadcasted_iota(I32, (SWA_BLOCK, SWA_BLOCK), 0)
    r = lax.broadcasted_iota(I32, (SWA_BLOCK, SWA_BLOCK), 1)
    band, mb = [], []
    for off in (-1, 0, 1):
        blk = n + off
        valid = (blk >= 0) & (blk < n_blocks)
        st = pl.multiple_of(jnp.clip(blk, 0, n_blocks - 1) * SWA_BLOCK, SWA_BLOCK)
        band.append(pl.ds(st, SWA_BLOCK))
        rel = qi - r - off * SWA_BLOCK
        mb.append(jnp.where((jnp.abs(rel) <= SWA_WINDOW) & valid, 1.0, 0.0))
    mask = jnp.concatenate(mb, axis=1) > 0.5

    kl = [jnp.concatenate([kl_scr[kv, sl, :] for sl in band], axis=0) for kv in range(SWA_KV_HEADS)]
    vl = [jnp.concatenate([vl_scr[kv, sl, :] for sl in band], axis=0) for kv in range(SWA_KV_HEADS)]
    heads = range(SWA_HEADS)
    lanes = [slice((h * SWA_HEAD_DIM) % LANE, (h * SWA_HEAD_DIM) % LANE + SWA_HEAD_DIM) for h in heads]
    q = [qs[h * SWA_HEAD_DIM // LANE][:, lanes[h]].astype(BF16) for h in heads]
    s_ctx = [_dot_nt(q[h], kc_scr[h // SWA_GROUP]) * scale for h in heads]
    s_loc = [jnp.where(mask, _dot_nt(q[h], kl[h // SWA_GROUP]) * scale, NEG_BIG) for h in heads]
    m = [jnp.maximum(jnp.max(jnp.maximum(_lane_fold(s_ctx[h], jnp.maximum), _lane_fold(s_loc[h], jnp.maximum)),
                             axis=-1, keepdims=True), sink_ref[h]) for h in heads]
    p_ctx = [jnp.exp(s_ctx[h] - m[h]) for h in heads]
    p_loc = [jnp.exp(s_loc[h] - m[h]) for h in heads]
    den = [jnp.sum(_lane_fold(p_ctx[h], jnp.add) + _lane_fold(p_loc[h], jnp.add), axis=-1, keepdims=True)
           + jnp.exp(sink_ref[h] - m[h]) for h in heads]
    o = [_dot(p_ctx[h].astype(BF16), vc_scr[h // SWA_GROUP]) + _dot(p_loc[h].astype(BF16), vl[h // SWA_GROUP])
         for h in heads]
    for h in heads:
        o_ref[:, h * SWA_HEAD_DIM:(h + 1) * SWA_HEAD_DIM] = o[h] / den[h]


def _swa_lat(rows, proj, sink, k_ctx, v_ctx, cos, sin):
    t = rows.nl_t
    n_blocks = t // SWA_BLOCK
    q_base = rows.nc // SWA_BLOCK
    kv_base = rows.nc // t
    s_ctx = k_ctx.shape[1]
    return pl.pallas_call(
        functools.partial(_swa_lat_kernel, n_blocks=n_blocks),
        out_shape=jax.ShapeDtypeStruct((rows.nl, SWA_WIDTH), F32),
        grid=(rows.nl_b, n_blocks),
        in_specs=[
            pl.BlockSpec(memory_space=pltpu.SMEM),
            pl.BlockSpec((SWA_BLOCK, SWA_WIDTH), lambda b, n: (q_base + b * n_blocks + n, COL_Q)),
            pl.BlockSpec((t, LANE), lambda b, n: (kv_base + b, COL_K)),
            pl.BlockSpec((t, LANE), lambda b, n: (kv_base + b, COL_V)),
            pl.BlockSpec((None, s_ctx, LANE), lambda b, n: (b, 0, 0)),
            pl.BlockSpec((None, s_ctx, LANE), lambda b, n: (b, 0, 0)),
            pl.BlockSpec((t, LANE), lambda b, n: (0, 0)),
            pl.BlockSpec((t, LANE), lambda b, n: (0, 0)),
        ],
        out_specs=pl.BlockSpec((SWA_BLOCK, SWA_WIDTH), lambda b, n: (b * n_blocks + n, 0)),
        scratch_shapes=[pltpu.VMEM((SWA_KV_HEADS, t, SWA_HEAD_DIM), BF16),
                        pltpu.VMEM((SWA_KV_HEADS, t, SWA_HEAD_DIM), BF16),
                        pltpu.VMEM((SWA_KV_HEADS, s_ctx, SWA_HEAD_DIM), BF16),
                        pltpu.VMEM((SWA_KV_HEADS, s_ctx, SWA_HEAD_DIM), BF16)],
        compiler_params=_params("parallel", "arbitrary"),
        name="swa_lat",
    )(sink, proj, proj, proj, k_ctx, v_ctx, cos, sin)


def _hgrn_consts():
    c = HGRN_CHUNK
    t = np.arange(c)[:, None]
    u = np.arange(c)[None, :]
    tri_f = (u <= t).astype(np.float32)
    masks = []
    for m in HGRN_LEVELS:
        right = ((t // m) % 2) == 1
        masks.append(right & ((u // m) == (t // m) - 1))
    m_f = np.stack(masks).astype(np.float32)
    sel = np.kron(np.eye(c), np.ones((1, HGRN_SUB))).astype(np.float32)
    return (jnp.asarray(tri_f, BF16), jnp.asarray(m_f, F32),
            jnp.asarray(tri_f[::-1, ::-1], BF16), jnp.asarray(m_f[:, ::-1, ::-1], F32),
            jnp.ones((HGRN_KEY_DIM, LANE), BF16), jnp.asarray(sel, BF16))


def _hgrn_intra(items, ones_b, sel_b):
    c = HGRN_CHUNK
    nsub = c // HGRN_SUB
    n = range(len(items))
    fwd = [it[7] for it in items]
    v = [it[2] for it in items]
    q = [_silu(it[0]) * (HGRN_KEY_DIM ** -0.5) for it in items]
    f = [it[3] + it[4] * jax.nn.sigmoid(it[1]) for it in items]
    lf = [jnp.log(f[j]) for j in n]
    k = [1.0 - f[j] for j in n]

    hi = [lf[j].astype(BF16) for j in n]
    r1 = [lf[j] - hi[j].astype(F32) for j in n]
    mid = [r1[j].astype(BF16) for j in n]
    lo = [(r1[j] - mid[j].astype(F32)).astype(BF16) for j in n]
    tri = [items[j][5][...] for j in n]
    b = [_dot(tri[j], hi[j]) + (_dot(tri[j], mid[j]) + _dot(tri[j], lo[j])) for j in n]
    total = [jnp.sum(lf[j], axis=0, keepdims=True) for j in n]

    s_io = lax.broadcasted_iota(I32, (1, HGRN_SUB, HGRN_KEY_DIM), 1)
    p = []
    for j in n:
        b3 = b[j].reshape(nsub, HGRN_SUB, HGRN_KEY_DIM)
        q3 = q[j].reshape(nsub, HGRN_SUB, HGRN_KEY_DIM)
        k3 = k[j].reshape(nsub, HGRN_SUB, HGRN_KEY_DIM)
        ps = []
        for i in range(HGRN_SUB):
            keep = (s_io <= i) if fwd[j] else (s_io >= i)
            dec = jnp.exp(jnp.where(keep, b3[:, i:i + 1, :] - b3, NEG_BIG))
            ps.append((q3[:, i:i + 1, :] * dec) * k3)
        p.append(jnp.stack(ps, axis=1).reshape(nsub * HGRN_SUB * HGRN_SUB, HGRN_KEY_DIM).astype(BF16))
    att = [_dot(p[j], ones_b) for j in n]
    av = [(att[j].reshape(nsub, HGRN_SUB, HGRN_SUB, HGRN_VAL_DIM)
           * v[j].reshape(nsub, HGRN_SUB, HGRN_VAL_DIM)[:, None, :, :]
           ).reshape(nsub * HGRN_SUB * HGRN_SUB, HGRN_VAL_DIM).astype(BF16) for j in n]
    o = [_dot(sel_b, av[j]) for j in n]

    a = [jnp.zeros((c, c), F32) for _ in n]
    for li, m in enumerate(HGRN_LEVELS):
        fac = []
        for j in n:
            pieces = []
            for pair in range(c // (2 * m)):
                r = 2 * pair * m + (m - 1 if fwd[j] else m)
                pieces.append(jnp.broadcast_to(b[j][r:r + 1, :], (2 * m, HGRN_KEY_DIM)))
            bnd = pieces[0] if len(pieces) == 1 else jnp.concatenate(pieces, axis=0)
            fac.append(jnp.exp(-jnp.abs(b[j] - bnd)))
        prod = [_dot_nt((q[j] * fac[j]).astype(BF16), (k[j] * fac[j]).astype(BF16)) for j in n]
        a = [a[j] + items[j][6][li] * prod[j] for j in n]
    o = [o[j] + _dot(a[j].astype(BF16), v[j].astype(BF16)) for j in n]
    return o, q, k, b, total


def _hgrn_carry(o, q, k, v, b, total, st):
    o = o + _dot_nt((q * jnp.exp(b)).astype(BF16), st.astype(BF16))
    kc = (k * jnp.exp(total - b)).astype(BF16)
    return o, st * jnp.exp(total) + _dot(v.T.astype(BF16), kc)


def _hgrn_kernel(*refs, n_chunks, has_s0, emit_state):
    (qb_ref, ff_ref, fb_ref, ib_ref, gb_ref, lbp_ref, gw_ref,
     ef_ref, mf_ref, eb_ref, mb_ref, ones_ref, sel_ref) = refs[:13]
    rest = list(refs[13:])
    s0_ref = rest.pop(0) if has_s0 else None
    r_ref = rest.pop(0)
    sout_ref = rest.pop(0) if emit_state else None
    of_scr, ob_scr, stf_scr, stb_scr = rest
    c = HGRN_CHUNK
    ones_b = ones_ref[...]
    sel_b = sel_ref[...]
    gw = gw_ref[...]

    for d, st_scr in enumerate((stf_scr, stb_scr)):
        if has_s0:
            st_scr[...] = s0_ref[d].T
        else:
            st_scr[...] = jnp.zeros((HGRN_VAL_DIM, HGRN_KEY_DIM), F32)

    group = min(HGRN_GROUP, n_chunks)

    def sweep(i, carry):
        sls, items = [], []
        for u in range(group):
            sls.append(pl.ds(pl.multiple_of((i * group + u) * c, c), c))
            items.append((qb_ref[sls[-1], :], ff_ref[sls[-1], :], ib_ref[sls[-1], :], lbp_ref[0:1, :],
                          lbp_ref[1:2, :], ef_ref, mf_ref, True))
        for u in range(group):
            sls.append(pl.ds(pl.multiple_of((n_chunks - 1 - (i * group + u)) * c, c), c))
            items.append((qb_ref[sls[-1], :], fb_ref[sls[-1], :], ib_ref[sls[-1], :], lbp_ref[2:3, :],
                          lbp_ref[3:4, :], eb_ref, mb_ref, False))
        o, q, k, b, total = _hgrn_intra(items, ones_b, sel_b)
        for d, (st_scr, o_scr) in enumerate(((stf_scr, of_scr), (stb_scr, ob_scr))):
            st = st_scr[...]
            for u in range(group):
                j = d * group + u
                o_j, st = _hgrn_carry(o[j], q[j], k[j], items[j][2], b[j], total[j], st)
                o_scr[sls[j], :] = o_j
            st_scr[...] = st
        return carry

    lax.fori_loop(0, n_chunks // group, sweep, 0)
    if emit_state:
        sout_ref[0] = stf_scr[...].T
        sout_ref[1] = stb_scr[...].T

    def readout(ci, carry):
        sl = pl.ds(pl.multiple_of(ci * c, c), c)
        tot = of_scr[sl, :] + ob_scr[sl, :]
        ms = jnp.mean(tot * tot, axis=-1, keepdims=True)
        r_ref[sl, :] = (tot * lax.rsqrt(ms + NORM_EPS) * gw) * _silu(gb_ref[sl, :])
        return carry

    lax.fori_loop(0, n_chunks, readout, 0, unroll=2)


def _hgrn(proj, lbp, gw, consts, n_b, t_len, row_block0, n_rows, s0=None, s0_layer=0, emit_state=False):
    n_chunks = t_len // HGRN_CHUNK
    e_f, m_f, e_b, m_b, ones_b, sel_b = consts

    def col(off):
        return pl.BlockSpec((t_len, LANE), lambda b, h: (row_block0 + b, COL_HGRN + off * HGRN_HEADS + h))

    def whole(a):
        nd = a.ndim
        return pl.BlockSpec(a.shape, lambda b, h: (0,) * nd)

    in_specs = [col(0), col(1), col(2), col(3), col(4),
                pl.BlockSpec((None, 4, LANE), lambda b, h: (h, 0, 0)),
                pl.BlockSpec((1, HGRN_VAL_DIM), lambda b, h: (0, 0)),
                whole(e_f), whole(m_f), whole(e_b), whole(m_b), whole(ones_b), whole(sel_b)]
    args = [proj, proj, proj, proj, proj, lbp, gw.reshape(1, HGRN_VAL_DIM), e_f, m_f, e_b, m_b, ones_b, sel_b]
    if s0 is not None:
        in_specs.append(pl.BlockSpec((None, None, 2, None, HGRN_KEY_DIM, HGRN_VAL_DIM),
                                     lambda b, h: (b, s0_layer, 0, h, 0, 0)))
        args.append(s0)
    out_shape = [jax.ShapeDtypeStruct((n_rows, HGRN_WIDTH), F32)]
    out_specs = [pl.BlockSpec((t_len, LANE), lambda b, h: (b, h))]
    if emit_state:
        out_shape.append(jax.ShapeDtypeStruct((n_b, 2, HGRN_HEADS, HGRN_KEY_DIM, HGRN_VAL_DIM), F32))
        out_specs.append(pl.BlockSpec((None, 2, None, HGRN_KEY_DIM, HGRN_VAL_DIM),
                                      lambda b, h: (b, 0, h, 0, 0)))
    return pl.pallas_call(
        functools.partial(_hgrn_kernel, n_chunks=n_chunks, has_s0=s0 is not None, emit_state=emit_state),
        out_shape=out_shape,
        grid=(n_b, HGRN_HEADS),
        in_specs=in_specs,
        out_specs=out_specs,
        scratch_shapes=[pltpu.VMEM((t_len, HGRN_VAL_DIM), F32), pltpu.VMEM((t_len, HGRN_VAL_DIM), F32),
                        pltpu.VMEM((HGRN_VAL_DIM, HGRN_KEY_DIM), F32), pltpu.VMEM((HGRN_VAL_DIM, HGRN_KEY_DIM), F32)],
        compiler_params=_params("parallel", "parallel"),
        name="hgrn_lat" if s0 is not None else "hgrn_ctx",
    )(*args)


def _outproj_kernel(*refs, n_parts, n_x, rows):
    pair_refs = refs[:2 * n_parts]
    w_refs = refs[2 * n_parts:3 * n_parts]
    x_refs = refs[3 * n_parts:3 * n_parts + n_x]
    mod_ref, nw_ref, wr_ref, br_ref, tri_ref, o_ref, xn_ref, meta_ref, cnt_ref, base_scr = refs[3 * n_parts + n_x:]
    acc = None
    for p in range(n_parts):
        a = rows.stream_tile(pair_refs[2 * p:2 * p + 2]).astype(BF16)
        d = _dot(a, w_refs[p][...])
        acc = d if acc is None else acc + d
    x_new = rows.stream_tile(x_refs) + mod_ref[2:3, :] * acc
    o_ref[...] = x_new
    _route_tile(x_new, mod_ref, nw_ref, wr_ref, br_ref, tri_ref, xn_ref, meta_ref, cnt_ref, base_scr)


def _outproj_route(rows, pairs, weights, x, mods, nw2, wr, br, tri):
    in_specs, args = [], []
    for pair in pairs:
        specs, arrs = rows.stream_specs(pair)
        in_specs += specs
        args += arrs
    in_specs += [pl.BlockSpec(w.shape, lambda i: (0, 0)) for w in weights]
    x_specs, x_args = rows.stream_specs(x)
    in_specs += x_specs + [
        rows.mod_spec(),
        pl.BlockSpec((1, D_MODEL), lambda i: (0, 0)),
        pl.BlockSpec((D_MODEL, LANE), lambda i: (0, 0)),
        pl.BlockSpec((1, LANE), lambda i: (0, 0)),
        pl.BlockSpec((TM, TM), lambda i: (0, 0)),
    ]
    return pl.pallas_call(
        functools.partial(_outproj_kernel, n_parts=len(pairs), n_x=len(x_args), rows=rows),
        out_shape=[jax.ShapeDtypeStruct((rows.n, D_MODEL), F32),
                   jax.ShapeDtypeStruct((rows.n, D_MODEL), F32),
                   jax.ShapeDtypeStruct((rows.n, LANE), F32),
                   jax.ShapeDtypeStruct((8, LANE), F32)],
        grid=(rows.n_tiles,),
        in_specs=in_specs,
        out_specs=[pl.BlockSpec((TM, D_MODEL), lambda i: (i, 0)),
                   pl.BlockSpec((TM, D_MODEL), lambda i: (i, 0)),
                   pl.BlockSpec((TM, LANE), lambda i: (i, 0)),
                   pl.BlockSpec((8, LANE), lambda i: (0, 0))],
        scratch_shapes=[pltpu.VMEM((8, LANE), F32)],
        compiler_params=_params("arbitrary"),
        name="outproj_route",
    )(*args, *weights, *x_args, mods, nw2.reshape(1, D_MODEL), wr, br, tri)


def _mla_proj_kernel(x_ref, mod_ref, nw_ref, wd_ref, qnw_ref, kvnw_ref, wuq_ref, wuqp_ref, cos_ref, sin_ref,
                     q_ref, ckv_ref, krb_ref):
    nf = MLA_ROPE_DIM // 4
    h = _normmod(x_ref[...], nw_ref[...], mod_ref[0:1, :], mod_ref[1:2, :]).astype(BF16)
    t1 = _dot(h, wd_ref[...])
    qd = t1[:, :MLA_Q_RANK]
    kvd = t1[:, MLA_Q_RANK:MLA_Q_RANK + MLA_KV_RANK]
    cos = cos_ref[...]
    sin = sin_ref[...]
    qn = qd * lax.rsqrt(jnp.mean(qd * qd, axis=-1, keepdims=True) + NORM_EPS) * qnw_ref[...]
    ckv_ref[...] = kvd * lax.rsqrt(jnp.mean(kvd * kvd, axis=-1, keepdims=True) + NORM_EPS) * kvnw_ref[...]
    krb_ref[...] = _rope(t1[:, MLA_Q_RANK + MLA_KV_RANK:], cos, sin, nf)
    qb = qn.astype(BF16)
    scale = MLA_QK_DIM ** -0.5
    cos2 = jnp.concatenate([cos, cos], axis=1)
    sin2 = jnp.concatenate([sin, sin], axis=1)
    for hp in range(MLA_HEADS // 2):
        cols = slice(hp * 2 * LANE, (hp + 1) * 2 * LANE)
        qh = _dot(qb, wuq_ref[:, cols])
        qp = _dot(qb, wuqp_ref[:, cols])
        q_ref[:, cols] = ((qh * cos2 + qp * sin2) * scale).astype(BF16)


def _mla_proj(rows, x, mods, nw, wd, qnw, kvnw, wuq, wuq_partner, cos, sin):
    return pl.pallas_call(
        _mla_proj_kernel,
        out_shape=[jax.ShapeDtypeStruct((rows.n, MLA_HEADS * LANE), BF16),
                   jax.ShapeDtypeStruct((rows.n, MLA_KV_RANK), F32),
                   jax.ShapeDtypeStruct((rows.n, LANE), F32)],
        grid=(rows.n_tiles,),
        in_specs=[
            pl.BlockSpec((TM, D_MODEL), lambda i: (i, 0)),
            rows.mod_spec(),
            pl.BlockSpec((1, D_MODEL), lambda i: (0, 0)),
            pl.BlockSpec((D_MODEL, MLA_DOWN_WIDTH), lambda i: (0, 0)),
            pl.BlockSpec((1, MLA_Q_RANK), lambda i: (0, 0)),
            pl.BlockSpec((1, MLA_KV_RANK), lambda i: (0, 0)),
            pl.BlockSpec((MLA_Q_RANK, MLA_HEADS * LANE), lambda i: (0, 0)),
            pl.BlockSpec((MLA_Q_RANK, MLA_HEADS * LANE), lambda i: (0, 0)),
            pl.BlockSpec((TM, LANE), lambda i: (rows.pos_block(i), 0)),
            pl.BlockSpec((TM, LANE), lambda i: (rows.pos_block(i), 0)),
        ],
        out_specs=[pl.BlockSpec((TM, MLA_HEADS * LANE), lambda i: (i, 0)),
                   pl.BlockSpec((TM, MLA_KV_RANK), lambda i: (i, 0)),
                   pl.BlockSpec((TM, LANE), lambda i: (i, 0))],
        compiler_params=_params("parallel"),
        name="mla_proj",
    )(x, mods, nw.reshape(1, D_MODEL), wd, qnw.reshape(1, MLA_Q_RANK), kvnw.reshape(1, MLA_KV_RANK), wuq, wuq_partner, cos, sin)


MLA_TQ = 256
MLA_HPS = 4


def _mla_attn_kernel(q_ref, ckv_ref, krb_ref, wkv_ref, o_ref, k_scr, v_scr, *, t_len, hps):
    tq = min(MLA_TQ, t_len)
    ckv = ckv_ref[...].astype(BF16)
    krb = krb_ref[...]
    for hh in range(hps):
        kvh = _dot(ckv, wkv_ref[hh])
        k_scr[hh] = (kvh[:, :LANE] + krb).astype(BF16)
        v_scr[hh] = kvh[:, LANE:].astype(BF16)

    def body(ti, carry):
        sl = pl.ds(pl.multiple_of(ti * tq, tq), tq)
        heads = range(hps)
        s = [_dot_nt(q_ref[sl, hh * LANE:(hh + 1) * LANE], k_scr[hh]) for hh in heads]
        m = [jnp.max(s[hh], axis=-1, keepdims=True) for hh in heads]
        p = [jnp.exp(s[hh] - m[hh]) for hh in heads]
        den = [jnp.sum(p[hh], axis=-1, keepdims=True) for hh in heads]
        o = [_dot(p[hh].astype(BF16), v_scr[hh]) / den[hh] for hh in heads]
        for pair in range(hps // 2):
            o_ref[sl, pair * LANE:(pair + 1) * LANE] = o[2 * pair] + o[2 * pair + 1]
        return carry

    lax.fori_loop(0, t_len // tq, body, 0)


def _mla_attn(q, ckv_all, krb_all, wkv, n_b, t_len, q_row_block0, n_rows):
    s_len = ckv_all.shape[1]
    hps = MLA_HEADS if t_len <= MLA_TQ else MLA_HPS
    return pl.pallas_call(
        functools.partial(_mla_attn_kernel, t_len=t_len, hps=hps),
        out_shape=jax.ShapeDtypeStruct((n_rows, MLA_HEADS * MLA_V_DIM), F32),
        grid=(n_b, MLA_HEADS // hps),
        in_specs=[
            pl.BlockSpec((t_len, hps * LANE), lambda b, hp: (q_row_block0 + b, hp)),
            pl.BlockSpec((None, s_len, MLA_KV_RANK), lambda b, hp: (b, 0, 0)),
            pl.BlockSpec((None, s_len, LANE), lambda b, hp: (b, 0, 0)),
            pl.BlockSpec((hps, MLA_KV_RANK, 2 * LANE), lambda b, hp: (hp, 0, 0)),
        ],
        out_specs=pl.BlockSpec((t_len, hps // 2 * LANE), lambda b, hp: (b, hp)),
        scratch_shapes=[pltpu.VMEM((hps, s_len, LANE), BF16), pltpu.VMEM((hps, s_len, LANE), BF16)],
        compiler_params=_params("parallel", "arbitrary"),
        name="mla_attn",
    )(q, ckv_all, krb_all, wkv)


META_E1, META_E2, META_W1, META_W2, META_R1, META_R2 = range(6)


def _route_tile(x, mod_ref, nw_ref, wr_ref, br_ref, tri_ref, xn_ref, meta_ref, cnt_ref, base_scr):
    @pl.when(pl.program_id(0) == 0)
    def _():
        base_scr[...] = jnp.zeros(base_scr.shape, F32)

    xn = _normmod(x, nw_ref[...], mod_ref[3:4, :], mod_ref[4:5, :])
    xn_ref[...] = xn
    logits = _dot_f32ish(xn, wr_ref[...]) + br_ref[...]
    lane = lax.broadcasted_iota(I32, logits.shape, 1).astype(F32)
    far = float(LANE)

    def first_argmax(vals, vmax):
        return jnp.min(jnp.where(vals == vmax, lane, far), axis=-1, keepdims=True)

    gl = jnp.where(lane < MOE_GROUPS, logits, NEG_BIG)
    gmax = jnp.max(gl, axis=-1, keepdims=True)
    g_w = 1.0 / jnp.sum(jnp.exp(gl - gmax), axis=-1, keepdims=True)
    g_idx = first_argmax(gl, gmax)
    e_lo = MOE_GROUPS + MOE_EPG * g_idx
    el = jnp.where((lane >= e_lo) & (lane < e_lo + MOE_EPG), logits, NEG_BIG)
    m1 = jnp.max(el, axis=-1, keepdims=True)
    i1 = first_argmax(el, m1)
    el2 = jnp.where(lane == i1, NEG_BIG, el)
    m2 = jnp.max(el2, axis=-1, keepdims=True)
    i2 = first_argmax(el2, m2)
    esum = jnp.sum(jnp.exp(el - m1), axis=-1, keepdims=True)
    p1 = 1.0 / esum
    p2 = jnp.exp(m2 - m1) / esum
    w1 = g_w * (p1 / (p1 + p2))
    w2 = g_w * (p2 / (p1 + p2))
    e1 = i1 - MOE_GROUPS
    e2 = i2 - MOE_GROUPS

    oh1 = lane == e1
    oh2 = lane == e2
    oh = jnp.where(oh1 | oh2, 1.0, 0.0)
    before = _dot(tri_ref[...], oh.astype(BF16)) + base_scr[0:1, :]
    r1 = jnp.sum(jnp.where(oh1, before, 0.0), axis=-1, keepdims=True)
    r2 = jnp.sum(jnp.where(oh2, before, 0.0), axis=-1, keepdims=True)
    base_scr[...] = base_scr[...] + jnp.sum(oh, axis=0, keepdims=True)
    cnt_ref[...] = base_scr[...]

    meta = jnp.zeros(logits.shape, F32)
    for slot, val in ((META_E1, e1), (META_E2, e2), (META_W1, w1), (META_W2, w2), (META_R1, r1), (META_R2, r2)):
        meta = jnp.where(lane == slot, val, meta)
    meta_ref[...] = meta


def _row_copy(src, src_row, dst, dst_row, sem):
    return pltpu.make_async_copy(src.at[pl.ds(src_row, 1), :], dst.at[pl.ds(dst_row, 1), :], sem)


def _dispatch_kernel(fill_ref, pos_ref, xn_ref, xs_hbm, zero_scr, sem, fill_sem, *, n_tiles):
    @pl.when(pl.program_id(0) == 0)
    def _():
        zero_scr[...] = jnp.zeros(zero_scr.shape, F32)

        def fill_copy(t):
            return pltpu.make_async_copy(zero_scr, xs_hbm.at[pl.ds(pl.multiple_of(t * TME, TME), TME), :], fill_sem)

        def fill_start(t, carry):
            @pl.when(fill_ref[t] != 0)
            def _():
                fill_copy(t).start()
            return carry

        def fill_wait(t, carry):
            @pl.when(fill_ref[t] != 0)
            def _():
                fill_copy(t).wait()
            return carry

        lax.fori_loop(0, n_tiles, fill_start, 0)
        lax.fori_loop(0, n_tiles, fill_wait, 0)

    def start(r, carry):
        for k in range(2):
            _row_copy(xn_ref, r, xs_hbm, pos_ref[0, 2 * r + k], sem).start(priority=k)
        return carry

    lax.fori_loop(0, TM, start, 0, unroll=DMA_UNROLL)
    for _ in range(2):
        pltpu.make_async_copy(xn_ref, xs_hbm.at[pl.ds(0, TM), :], sem).wait()


def _dispatch(rows, tile_fill, pos, xn, n_tiles):
    return pl.pallas_call(
        functools.partial(_dispatch_kernel, n_tiles=n_tiles),
        out_shape=jax.ShapeDtypeStruct((n_tiles * TME, D_MODEL), F32),
        grid_spec=pltpu.PrefetchScalarGridSpec(
            num_scalar_prefetch=1,
            grid=(rows.n_tiles,),
            in_specs=[
                pl.BlockSpec((None, 1, 2 * TM), lambda i, fill: (i, 0, 0), memory_space=pltpu.SMEM),
                pl.BlockSpec((TM, D_MODEL), lambda i, fill: (i, 0)),
            ],
            out_specs=pl.BlockSpec(memory_space=pl.ANY),
            scratch_shapes=[pltpu.VMEM((TME, D_MODEL), F32), pltpu.SemaphoreType.DMA(()),
                            pltpu.SemaphoreType.DMA(())],
        ),
        compiler_params=_params("arbitrary"),
        name="moe_dispatch",
    )(tile_fill, pos, xn)


def _ffn_kernel(te_ref, nv_ref, x_ref, wg_ref, wu_ref, wd_ref, y_ref, wg_b, wu_b, wd_b):
    t = pl.program_id(0)
    valid = t < nv_ref[0]
    new_expert = (t == 0) | (te_ref[t] != te_ref[jnp.maximum(t - 1, 0)])

    @pl.when(valid & new_expert)
    def _():
        wg_b[...] = wg_ref[...].astype(BF16)
        wu_b[...] = wu_ref[...].astype(BF16)
        wd_b[...] = wd_ref[...].astype(BF16)

    @pl.when(valid)
    def _():
        x = x_ref[...].astype(BF16)
        a = _silu(_dot(x, wg_b[...])) * _dot(x, wu_b[...])
        y_ref[...] = _dot(a.astype(BF16), wd_b[...])

    @pl.when(jnp.logical_not(valid))
    def _():
        y_ref[...] = jnp.zeros(y_ref.shape, F32)


def _ffn(tile_expert, n_valid, xs, w_gate, w_up, w_down, layer, n_tiles):
    def xmap(t, te, nv):
        return (jnp.minimum(t, nv[0] - 1), 0)

    def wmap(t, te, nv):
        return (layer, te[t], 0, 0)

    return pl.pallas_call(
        _ffn_kernel,
        out_shape=jax.ShapeDtypeStruct((n_tiles * TME, D_MODEL), F32),
        grid_spec=pltpu.PrefetchScalarGridSpec(
            num_scalar_prefetch=2,
            grid=(n_tiles,),
            in_specs=[
                pl.BlockSpec((TME, D_MODEL), xmap),
                pl.BlockSpec((None, None, D_MODEL, MOE_HIDDEN), wmap),
                pl.BlockSpec((None, None, D_MODEL, MOE_HIDDEN), wmap),
                pl.BlockSpec((None, None, MOE_HIDDEN, D_MODEL), wmap),
            ],
            out_specs=pl.BlockSpec((TME, D_MODEL), lambda t, te, nv: (t, 0)),
            scratch_shapes=[pltpu.VMEM((D_MODEL, MOE_HIDDEN), BF16),
                            pltpu.VMEM((D_MODEL, MOE_HIDDEN), BF16),
                            pltpu.VMEM((MOE_HIDDEN, D_MODEL), BF16)],
        ),
        compiler_params=_params("arbitrary"),
        name="moe_ffn",
    )(tile_expert, n_valid, xs, w_gate, w_up, w_down)


def _combine_kernel(pos_ref, x_ref, meta_ref, mod_ref, fnw_ref, ys_hbm, *rest, final, ctx_tiles):
    if final:
        o_ctx_ref, o_lat_ref, buf0, buf1, sem = rest
    else:
        o_ref, buf0, buf1, sem = rest
    bufs = (buf0, buf1)

    def start(r, carry):
        for k in range(2):
            _row_copy(ys_hbm, pos_ref[0, 2 * r + k], bufs[k], r, sem).start(priority=k)
        return carry

    lax.fori_loop(0, TM, start, 0, unroll=DMA_UNROLL)
    for k in range(2):
        pltpu.make_async_copy(ys_hbm.at[pl.ds(0, TM), :], bufs[k], sem).wait()
    meta = meta_ref[...]
    y = meta[:, META_W1:META_W1 + 1] * buf0[...] + meta[:, META_W2:META_W2 + 1] * buf1[...]
    xo = x_ref[...] + mod_ref[5:6, :] * y
    if not final:
        o_ref[...] = xo
        return
    xo = xo * lax.rsqrt(jnp.mean(xo * xo, axis=-1, keepdims=True) + NORM_EPS) * fnw_ref[...]
    is_ctx = pl.program_id(0) < ctx_tiles

    @pl.when(is_ctx)
    def _():
        o_ctx_ref[...] = xo

    @pl.when(jnp.logical_not(is_ctx))
    def _():
        o_lat_ref[...] = xo


def _combine(rows, pos, x, meta, mods, fnw, ys, final):
    ct = rows.ctx_tiles
    if final:
        out_shape = [jax.ShapeDtypeStruct((rows.nc, D_MODEL), F32), jax.ShapeDtypeStruct((rows.nl, D_MODEL), F32)]
        out_specs = [pl.BlockSpec((TM, D_MODEL), lambda i: (jnp.minimum(i, ct - 1), 0)),
                     pl.BlockSpec((TM, D_MODEL), lambda i: (jnp.maximum(i - ct, 0), 0))]
    else:
        out_shape = jax.ShapeDtypeStruct((rows.n, D_MODEL), F32)
        out_specs = pl.BlockSpec((TM, D_MODEL), lambda i: (i, 0))
    return pl.pallas_call(
        functools.partial(_combine_kernel, final=final, ctx_tiles=ct),
        out_shape=out_shape,
        grid=(rows.n_tiles,),
        in_specs=[
            pl.BlockSpec((None, 1, 2 * TM), lambda i: (i, 0, 0), memory_space=pltpu.SMEM),
            pl.BlockSpec((TM, D_MODEL), lambda i: (i, 0)),
            pl.BlockSpec((TM, LANE), lambda i: (i, 0)),
            rows.mod_spec(),
            pl.BlockSpec((1, D_MODEL), lambda i: (0, 0)),
            pl.BlockSpec(memory_space=pl.ANY),
        ],
        out_specs=out_specs,
        scratch_shapes=[pltpu.VMEM((TM, D_MODEL), F32), pltpu.VMEM((TM, D_MODEL), F32),
                        pltpu.SemaphoreType.DMA(())],
        compiler_params=_params("arbitrary"),
        name="moe_combine",
    )(pos, x, meta, mods, fnw.reshape(1, D_MODEL), ys)


def _moe(rows, x, xn, meta, cnt, mods, w_gate, w_up, w_down, layer, fnw, final):
    n_assign = 2 * rows.n
    n_tiles = n_assign // TME + MOE_EXPERTS

    counts = cnt[0, :MOE_EXPERTS].astype(I32)
    padded = ((counts + TME - 1) // TME) * TME
    ends = jnp.cumsum(padded)
    starts = ends - padded
    experts = jnp.arange(MOE_EXPERTS, dtype=I32)
    e = meta[:, META_E1:META_E2 + 1].astype(I32)
    rank = meta[:, META_R1:META_R2 + 1].astype(I32)
    start_of = jnp.sum(jnp.where(e[..., None] == experts, starts, 0), axis=-1)
    pos = (start_of + rank).reshape(rows.n_tiles, 1, 2 * TM)
    n_valid = ends[-1] // TME
    tile_first = jnp.arange(n_tiles, dtype=I32) * TME
    tile_start = jnp.minimum(tile_first, ends[-1] - TME)
    tile_expert = jnp.sum((ends[None, :] <= tile_start[:, None]).astype(I32), axis=1)
    tile_expert = jnp.minimum(tile_expert, MOE_EXPERTS - 1)
    tile_oh = tile_expert[:, None] == experts
    tile_rows = jnp.sum(jnp.where(tile_oh, counts + starts, 0), axis=1) - tile_start
    tile_fill = ((tile_first >= ends[-1]) | (tile_rows < TME)).astype(I32)

    xs = _dispatch(rows, tile_fill, pos, xn, n_tiles)
    ys = _ffn(tile_expert, n_valid.reshape(1).astype(I32), xs, w_gate, w_up, w_down, layer, n_tiles)
    return _combine(rows, pos, x, meta, mods, fnw, ys, final)


def _lower_bound_params(p):
    pr = jax.nn.softmax(p.astype(F32), axis=0)
    lb = jnp.cumsum(pr, axis=0) - pr[0:1]
    lb = jnp.clip(lb, 0.0, 1.0 - 1e-6)
    return jnp.maximum(lb, LOG_TINY), 1.0 - lb


def kernel(x_prompt, x_sample, c, cache_swa_k, cache_swa_v, state_hgrn, cache_mla_ckv, cache_mla_krope, c_ctx, mod_w, mod_b, norm1_w, norm2_w, final_norm_w, even_w_in, even_w_out, swa_sink, hgrn_lb_fwd, hgrn_lb_bwd, hgrn_gnorm_w, mla_w_dq, mla_qnorm_w, mla_w_uq, mla_w_dkv, mla_kvnorm_w, mla_w_ukv, mla_w_o, moe_router_group_w, moe_router_group_b, moe_router_expert_w, moe_router_expert_b, moe_w_gate, moe_w_up, moe_w_down):
    nc_b, nc_t, _ = x_prompt.shape
    nl_b, nl_t, _ = x_sample.shape
    rows = _Rows(nc_b, nc_t, nl_b, nl_t)
    past = cache_swa_k.shape[2]

    x = (x_prompt.reshape(rows.nc, D_MODEL), x_sample.reshape(rows.nl, D_MODEL))
    mod_rows = 16
    cvec = jnp.concatenate([c_ctx[None, :], c, jnp.zeros((mod_rows - 1 - nl_b, D_MODEL), F32)], axis=0)
    mods_all = _modulation(cvec, mod_w, mod_b).reshape(DEPTH, mod_rows, 6, D_MODEL)

    hconsts = _hgrn_consts()
    la_f, l1_f = _lower_bound_params(hgrn_lb_fwd)
    la_b, l1_b = _lower_bound_params(hgrn_lb_bwd)
    lbp_all = jnp.stack([la_f, l1_f, la_b, l1_b], axis=1).reshape(N_EVEN, 4, HGRN_HEADS, LANE).transpose(0, 2, 1, 3)
    swa_cos, swa_sin = _rope_tables(nl_t, SWA_HEAD_DIM, 0, LANE, 0)
    mla_cos, mla_sin = _rope_tables(nl_t, MLA_ROPE_DIM, MLA_NOPE_DIM, MLA_QK_DIM, TM)
    tri =jnp.asarray(np.tril(np.ones((TM, TM), np.float32), -1), BF16)

    new_k, new_v, new_s, new_ckv, new_kr = [], [], [], [], []
    for l in range(DEPTH):
        j = l // 2
        mods = mods_all[l]
        wr = jnp.zeros((D_MODEL, LANE), F32)
        wr = wr.at[:, :MOE_GROUPS].set(moe_router_group_w[l])
        wr = wr.at[:, MOE_GROUPS:MOE_GROUPS + MOE_EXPERTS].set(
            moe_router_expert_w[l].transpose(1, 0, 2).reshape(D_MODEL, MOE_EXPERTS))
        br = jnp.zeros((1, LANE), F32)
        br = br.at[0, :MOE_GROUPS].set(moe_router_group_b[l])
        br = br.at[0, MOE_GROUPS:MOE_GROUPS + MOE_EXPERTS].set(moe_router_expert_b[l].reshape(MOE_EXPERTS))
        if l % 2 == 0:
            proj = _even_proj(rows, x, mods, norm1_w[l], even_w_in[j].astype(BF16))
            a_ctx = _swa_ctx(rows, proj, swa_sink[j])
            a_lat = _swa_lat(rows, proj, swa_sink[j],
                             cache_swa_k[:, j].reshape(nl_b, past, SWA_KV_WIDTH),
                             cache_swa_v[:, j].reshape(nl_b, past, SWA_KV_WIDTH), swa_cos, swa_sin)
            r_ctx, s_ctx = _hgrn(proj, lbp_all[j], hgrn_gnorm_w[j], hconsts, nc_b, nc_t, 0, rows.nc,
                                 emit_state=True)
            (r_lat,) = _hgrn(proj, lbp_all[j], hgrn_gnorm_w[j], hconsts, nl_b, nl_t, rows.nc // nl_t, rows.nl,
                             s0=state_hgrn, s0_layer=j)
            w_out = even_w_out[j].astype(BF16)
            x, xn, meta, cnt = _outproj_route(rows, [(a_ctx, a_lat), (r_ctx, r_lat)],
                                              [w_out[:SWA_WIDTH], w_out[SWA_WIDTH:]], x, mods, norm2_w[l], wr, br, tri)
            kv = proj[:rows.nc, SWA_WIDTH:SWA_WIDTH + 2 * SWA_KV_WIDTH]
            new_k.append(kv[:, :SWA_KV_WIDTH].reshape(nc_b, nc_t, SWA_KV_HEADS, SWA_HEAD_DIM))
            new_v.append(kv[:, SWA_KV_WIDTH:].reshape(nc_b, nc_t, SWA_KV_HEADS, SWA_HEAD_DIM))
            new_s.append(s_ctx)
        else:
            wd = jnp.zeros((D_MODEL, MLA_DOWN_WIDTH), F32)
            wd = wd.at[:, :MLA_Q_RANK].set(mla_w_dq[j])
            wd = wd.at[:, MLA_Q_RANK:MLA_Q_RANK + MLA_KV_RANK].set(mla_w_dkv[j][:, :MLA_KV_RANK])
            kr_lo = MLA_Q_RANK + MLA_KV_RANK + MLA_NOPE_DIM
            wd = wd.at[:, kr_lo:kr_lo + MLA_ROPE_DIM].set(mla_w_dkv[j][:, MLA_KV_RANK:])
            wuq = jnp.pad(mla_w_uq[j].reshape(MLA_Q_RANK, MLA_HEADS, MLA_QK_DIM),
                          ((0, 0), (0, 0), (0, LANE - MLA_QK_DIM))).reshape(MLA_Q_RANK, MLA_HEADS * LANE)
            wukv = mla_w_ukv[j].reshape(MLA_KV_RANK, MLA_HEADS, MLA_NOPE_DIM + MLA_V_DIM).transpose(1, 0, 2)
            wk = jnp.pad(wukv[..., :MLA_NOPE_DIM], ((0, 0), (0, 0), (0, LANE - MLA_NOPE_DIM)))
            wv_e = jnp.pad(wukv[..., MLA_NOPE_DIM:], ((0, 0), (0, 0), (0, LANE - MLA_V_DIM)))
            wv_o = jnp.pad(wukv[..., MLA_NOPE_DIM:], ((0, 0), (0, 0), (LANE - MLA_V_DIM, 0)))
            odd = (jnp.arange(MLA_HEADS) % 2 == 1)[:, None, None]
            wv = jnp.where(odd, wv_o, wv_e)
            wuq = wuq.astype(BF16)
            w3 = wuq.reshape(MLA_Q_RANK, MLA_HEADS, LANE)
            nf = MLA_ROPE_DIM // 4
            rot = w3[..., MLA_NOPE_DIM:MLA_QK_DIM].reshape(MLA_Q_RANK, MLA_HEADS, 2, 2, nf)[..., ::-1, :]
            wuq_partner = jnp.concatenate([w3[..., :MLA_NOPE_DIM], rot.reshape(MLA_Q_RANK, MLA_HEADS, MLA_ROPE_DIM),
                                           w3[..., MLA_QK_DIM:]], axis=-1).reshape(MLA_Q_RANK, MLA_HEADS * LANE)
            q, ckv, krb = _mla_proj(rows, x, mods, norm1_w[l], wd.astype(BF16), mla_qnorm_w[j], mla_kvnorm_w[j],
                                    wuq, wuq_partner, mla_cos, mla_sin)
            wkv = jnp.concatenate([wk, wv], axis=-1).astype(BF16)
            ckv_c = ckv[:rows.nc].reshape(nc_b, nc_t, MLA_KV_RANK)
            krb_c = krb[:rows.nc].reshape(nc_b, nc_t, LANE)
            o_ctx = _mla_attn(q, ckv_c, krb_c, wkv, nc_b, nc_t, 0, rows.nc)
            cache_kr = jnp.pad(cache_mla_krope[:, j], ((0, 0), (0, 0), (MLA_NOPE_DIM, LANE - MLA_QK_DIM)))
            ckv_l = jnp.concatenate([cache_mla_ckv[:, j], ckv[rows.nc:].reshape(nl_b, nl_t, MLA_KV_RANK)], axis=1)
            krb_l = jnp.concatenate([cache_kr, krb[rows.nc:].reshape(nl_b, nl_t, LANE)], axis=1)
            o_lat = _mla_attn(q, ckv_l, krb_l, wkv, nl_b, nl_t, rows.nc // nl_t, rows.nl)
            x, xn, meta, cnt = _outproj_route(rows, [(o_ctx, o_lat)], [mla_w_o[j].astype(BF16)], x, mods,
                                              norm2_w[l], wr, br, tri)
            new_ckv.append(ckv_c)
            new_kr.append(krb_c[..., MLA_NOPE_DIM:MLA_QK_DIM])

        x = _moe(rows, x, xn, meta, cnt, mods, moe_w_gate, moe_w_up, moe_w_down, l,
                 final_norm_w, final=(l == DEPTH - 1))

    y_prompt = x[0].reshape(nc_b, nc_t, D_MODEL)
    y_sample = x[1].reshape(nl_b, nl_t, D_MODEL)
    return (y_prompt, y_sample, jnp.stack(new_k, axis=1), jnp.stack(new_v, axis=1), jnp.stack(new_s, axis=1),
            jnp.stack(new_ckv, axis=1), jnp.stack(new_kr, axis=1))
```

```python
import functools
import math

import numpy as np
import jax
import jax.numpy as jnp
from jax import lax
from jax.experimental import pallas as pl
from jax.experimental.pallas import tpu as pltpu

F32, BF16, I32 = jnp.float32, jnp.bfloat16, jnp.int32

D_MODEL = 1024
DEPTH = 4
GRID_W = 64
ROPE_THETA = 10000.0
NORM_EPS = 1e-6
NEG_BIG = -1e30
LOG_TINY = 1e-30
N_EVEN = (DEPTH + 1) // 2
N_ODD = DEPTH // 2

SWA_HEADS = 8
SWA_KV_HEADS = 2
SWA_GROUP = SWA_HEADS // SWA_KV_HEADS
SWA_HEAD_DIM = 64
SWA_WIDTH = SWA_HEADS * SWA_HEAD_DIM
SWA_KV_WIDTH = SWA_KV_HEADS * SWA_HEAD_DIM
SWA_WINDOW = 128
SWA_BLOCK = 128

HGRN_HEADS = 4
HGRN_KEY_DIM = 128
HGRN_VAL_DIM = 128
HGRN_WIDTH = HGRN_HEADS * HGRN_KEY_DIM
HGRN_CHUNK = 128
HGRN_LEVELS = (1, 2, 4, 8, 16, 32, 64)
HGRN_GROUP = 4

EVEN_IN_WIDTH = SWA_WIDTH + 2 * SWA_KV_WIDTH + 5 * HGRN_WIDTH
LANE = 128
COL_Q, COL_K, COL_V = 0, SWA_WIDTH // LANE, (SWA_WIDTH + SWA_KV_WIDTH) // LANE
COL_HGRN = (SWA_WIDTH + 2 * SWA_KV_WIDTH) // LANE

MLA_HEADS = 16
MLA_Q_RANK = 384
MLA_KV_RANK = 256
MLA_NOPE_DIM = 64
MLA_ROPE_DIM = 32
MLA_V_DIM = 64
MLA_QK_DIM = MLA_NOPE_DIM + MLA_ROPE_DIM
MLA_DOWN_WIDTH = MLA_Q_RANK + MLA_KV_RANK + LANE

MOE_GROUPS = 4
MOE_EPG = 8
MOE_EXPERTS = MOE_GROUPS * MOE_EPG
MOE_HIDDEN = 256

TM = 512
TME = 512
DMA_UNROLL = 8
SUBLANES = 8
VMEM_LIMIT = 48 * 1024 * 1024


def _params(*sem):
    return pltpu.CompilerParams(dimension_semantics=sem, vmem_limit_bytes=VMEM_LIMIT)


def _dot(a, b):
    return jnp.dot(a, b, preferred_element_type=F32)


def _dot_nt(a, b):
    return lax.dot_general(a, b, (((1,), (1,)), ((), ())), preferred_element_type=F32)


def _split2(a):
    hi = a.astype(BF16)
    return hi, (a - hi.astype(F32)).astype(BF16)


def _dot_f32ish(a, b):
    ah, al = _split2(a)
    bh, bl = _split2(b)
    return _dot(ah, bh) + (_dot(ah, bl) + _dot(al, bh))


def _silu(x):
    return x * jax.nn.sigmoid(x)


def _normmod(x, nw, shift, scale):
    ms = jnp.mean(x * x, axis=-1, keepdims=True)
    return (x * lax.rsqrt(ms + NORM_EPS) * nw) * (1.0 + scale) + shift


def _mod_kernel(c_ref, w_ref, b_ref, o_ref):
    o_ref[...] = _dot_f32ish(_silu(c_ref[...]), w_ref[...]) + b_ref[...]


def _modulation(cvec, mod_w, mod_b):
    rows = cvec.shape[0]
    nb = 6 * D_MODEL // 1024
    return pl.pallas_call(
        _mod_kernel,
        out_shape=jax.ShapeDtypeStruct((DEPTH, rows, 6 * D_MODEL), F32),
        grid=(DEPTH, nb),
        in_specs=[
            pl.BlockSpec((rows, D_MODEL), lambda l, n: (0, 0)),
            pl.BlockSpec((None, D_MODEL, 1024), lambda l, n: (l, 0, n)),
            pl.BlockSpec((None, 1, 1024), lambda l, n: (l, 0, n)),
        ],
        out_specs=pl.BlockSpec((None, rows, 1024), lambda l, n: (l, 0, n)),
        compiler_params=_params("parallel", "parallel"),
        name="modulation",
    )(cvec, mod_w, mod_b.reshape(DEPTH, 1, 6 * D_MODEL))


class _Rows:
    def __init__(self, nc_b, nc_t, nl_b, nl_t):
        self.nc_b, self.nc_t, self.nl_b, self.nl_t = nc_b, nc_t, nl_b, nl_t
        self.nc = nc_b * nc_t
        self.nl = nl_b * nl_t
        self.n = self.nc + self.nl
        assert self.nc % TM == 0 and nl_t % TM == 0 and self.nc % nl_t == 0
        self.ctx_tiles = self.nc // TM
        self.tiles_per_lat = nl_t // TM
        self.n_tiles = self.n // TM

    def mod_row(self, i):
        return jnp.where(i < self.ctx_tiles, 0, 1 + (i - self.ctx_tiles) // self.tiles_per_lat)

    def mod_spec(self):
        return pl.BlockSpec((None, 6, D_MODEL), lambda i: (self.mod_row(i), 0, 0))

    def stream_specs(self, x):
        ct = self.ctx_tiles
        if not isinstance(x, tuple):
            return [pl.BlockSpec((TM, x.shape[1]), lambda i: (i, 0))], [x]
        x_ctx, x_lat = x
        return ([pl.BlockSpec((TM, x_ctx.shape[1]), lambda i: (jnp.minimum(i, ct - 1), 0)),
                 pl.BlockSpec((TM, x_lat.shape[1]), lambda i: (jnp.maximum(i - ct, 0), 0))], [x_ctx, x_lat])

    def stream_tile(self, refs):
        if len(refs) == 1:
            return refs[0][...]
        return jnp.where(pl.program_id(0) < self.ctx_tiles, refs[0][...], refs[1][...])

    def pos_block(self, i):
        return jnp.where(i < self.ctx_tiles, 0, 1 + (i - self.ctx_tiles) % self.tiles_per_lat)


def _even_proj_kernel(*refs, rows):
    mod_ref, nw_ref, w_ref, o_ref = refs[-4:]
    x = rows.stream_tile(refs[:-4])
    h = _normmod(x, nw_ref[...], mod_ref[0:1, :], mod_ref[1:2, :]).astype(BF16)
    step = 256
    for c in range(EVEN_IN_WIDTH // step):
        o_ref[:, c * step:(c + 1) * step] = _dot(h, w_ref[:, c * step:(c + 1) * step])


def _even_proj(rows, x, mods, nw, w_in_bf16):
    x_specs, x_args = rows.stream_specs(x)
    return pl.pallas_call(
        functools.partial(_even_proj_kernel, rows=rows),
        out_shape=jax.ShapeDtypeStruct((rows.n, EVEN_IN_WIDTH), F32),
        grid=(rows.n_tiles,),
        in_specs=x_specs + [
            rows.mod_spec(),
            pl.BlockSpec((1, D_MODEL), lambda i: (0, 0)),
            pl.BlockSpec((D_MODEL, EVEN_IN_WIDTH), lambda i: (0, 0)),
        ],
        out_specs=pl.BlockSpec((TM, EVEN_IN_WIDTH), lambda i: (i, 0)),
        compiler_params=_params("arbitrary"),
        name="even_proj",
    )(*x_args, mods, nw.reshape(1, D_MODEL), w_in_bf16)


def _rope_tables(t_len, rot_dim, lane_lo, lane_hi, lead_rows):
    half = rot_dim // 2
    nf = half // 2
    lane = np.arange(LANE)
    d = (lane - lane_lo) % rot_dim
    active = (lane >= lane_lo) & (lane < lane_hi)
    use_col = d >= half
    fidx = d % nf
    first = (d % half) < nf
    pos = jnp.arange(t_len)
    row = (pos // GRID_W).astype(F32)
    col = (pos % GRID_W).astype(F32)
    inv = jnp.exp(-math.log(ROPE_THETA) * jnp.arange(nf, dtype=F32) / nf)
    p = jnp.where(jnp.asarray(use_col)[None, :], col[:, None], row[:, None])
    ang = p * inv[jnp.asarray(fidx)][None, :]
    act = jnp.asarray(active)[None, :]
    cos = jnp.where(act, jnp.cos(ang), 1.0)
    sin = jnp.where(act, jnp.sin(ang), 0.0)
    sin = jnp.where(jnp.asarray(first)[None, :], -sin, sin)
    if lead_rows:
        cos = jnp.concatenate([jnp.ones((lead_rows, LANE), F32), cos], axis=0)
        sin = jnp.concatenate([jnp.zeros((lead_rows, LANE), F32), sin], axis=0)
    return cos, sin


def _rope(x, cos, sin, nf):
    lane = lax.broadcasted_iota(I32, x.shape, 1)
    up = pltpu.roll(x, LANE - nf, axis=1)
    dn = pltpu.roll(x, nf, axis=1)
    partner = jnp.where((lane & nf) == 0, up, dn)
    return x * cos + partner * sin


def _swa_ctx_kernel(sink_ref, q_ref, k_ref, v_ref, o_ref):
    scale = SWA_HEAD_DIM ** -0.5
    heads = range(SWA_HEADS)
    kk = [k_ref[:, kv * SWA_HEAD_DIM:(kv + 1) * SWA_HEAD_DIM].astype(BF16) for kv in range(SWA_KV_HEADS)]
    vv = [v_ref[:, kv * SWA_HEAD_DIM:(kv + 1) * SWA_HEAD_DIM].astype(BF16) for kv in range(SWA_KV_HEADS)]
    q = [q_ref[:, h * SWA_HEAD_DIM:(h + 1) * SWA_HEAD_DIM].astype(BF16) for h in heads]
    s = [_dot_nt(q[h], kk[h // SWA_GROUP]) * scale for h in heads]
    m = [jnp.maximum(jnp.max(s[h], axis=-1, keepdims=True), sink_ref[h]) for h in heads]
    p = [jnp.exp(s[h] - m[h]) for h in heads]
    den = [jnp.sum(p[h], axis=-1, keepdims=True) + jnp.exp(sink_ref[h] - m[h]) for h in heads]
    o = [_dot(p[h].astype(BF16), vv[h // SWA_GROUP]) / den[h] for h in heads]
    for h in heads:
        o_ref[:, h * SWA_HEAD_DIM:(h + 1) * SWA_HEAD_DIM] = o[h]


def _swa_ctx(rows, proj, sink):
    t = rows.nc_t
    return pl.pallas_call(
        _swa_ctx_kernel,
        out_shape=jax.ShapeDtypeStruct((rows.nc, SWA_WIDTH), F32),
        grid=(rows.nc_b,),
        in_specs=[
            pl.BlockSpec(memory_space=pltpu.SMEM),
            pl.BlockSpec((t, SWA_WIDTH), lambda b: (b, COL_Q)),
            pl.BlockSpec((t, LANE), lambda b: (b, COL_K)),
            pl.BlockSpec((t, LANE), lambda b: (b, COL_V)),
        ],
        out_specs=pl.BlockSpec((t, SWA_WIDTH), lambda b: (b, 0)),
        compiler_params=_params("parallel"),
        name="swa_ctx",
    )(sink, proj, proj, proj)


def _lane_fold(x, op):
    out = x[:, :LANE]
    for i in range(1, x.shape[1] // LANE):
        out = op(out, x[:, i * LANE:(i + 1) * LANE])
    return out


def _swa_lat_kernel(sink_ref, q_ref, k_ref, v_ref, kc_ref, vc_ref, cos_ref, sin_ref, o_ref,
                    kl_scr, vl_scr, kc_scr, vc_scr, *, n_blocks):
    scale = SWA_HEAD_DIM ** -0.5
    nf = SWA_HEAD_DIM // 4
    n = pl.program_id(1)

    @pl.when(n == 0)
    def _():
        kr = _rope(k_ref[...], cos_ref[...], sin_ref[...], nf)
        for kv in range(SWA_KV_HEADS):
            cols = slice(kv * SWA_HEAD_DIM, (kv + 1) * SWA_HEAD_DIM)
            kl_scr[kv] = kr[:, cols].astype(BF16)
            vl_scr[kv] = v_ref[:, cols].astype(BF16)
            kc_scr[kv] = kc_ref[:, cols].astype(BF16)
            vc_scr[kv] = vc_ref[:, cols].astype(BF16)

    q0 = pl.multiple_of(n * SWA_BLOCK, SWA_BLOCK)
    cq = cos_ref[pl.ds(q0, SWA_BLOCK), :]
    sq = sin_ref[pl.ds(q0, SWA_BLOCK), :]
    qs = [_rope(q_ref[:, g * LANE:(g + 1) * LANE], cq, sq, nf) for g in range(SWA_WIDTH // LANE)]

    qi = lax.broadcasted_iota(I32, (SWA_BLOCK, SWA_BLOCK), 0)
    r = lax.broadcasted_iota(I32, (SWA_BLOCK, SWA_BLOCK), 1)
    band, mb = [], []
    for off in (-1, 0, 1):
        blk = n + off
        valid = (blk >= 0) & (blk < n_blocks)
        st = pl.multiple_of(jnp.clip(blk, 0, n_blocks - 1) * SWA_BLOCK, SWA_BLOCK)
        band.append(pl.ds(st, SWA_BLOCK))
        rel = qi - r - off * SWA_BLOCK
        mb.append(jnp.where((jnp.abs(rel) <= SWA_WINDOW) & valid, 1.0, 0.0))
    mask = jnp.concatenate(mb, axis=1) > 0.5

    kl = [jnp.concatenate([kl_scr[kv, sl, :] for sl in band], axis=0) for kv in range(SWA_KV_HEADS)]
    vl = [jnp.concatenate([vl_scr[kv, sl, :] for sl in band], axis=0) for kv in range(SWA_KV_HEADS)]
    heads = range(SWA_HEADS)
    lanes = [slice((h * SWA_HEAD_DIM) % LANE, (h * SWA_HEAD_DIM) % LANE + SWA_HEAD_DIM) for h in heads]
    q = [qs[h * SWA_HEAD_DIM // LANE][:, lanes[h]].astype(BF16) for h in heads]
    s_ctx = [_dot_nt(q[h], kc_scr[h // SWA_GROUP]) * scale for h in heads]
    s_loc = [jnp.where(mask, _dot_nt(q[h], kl[h // SWA_GROUP]) * scale, NEG_BIG) for h in heads]
    m = [jnp.maximum(jnp.max(jnp.maximum(_lane_fold(s_ctx[h], jnp.maximum), _lane_fold(s_loc[h], jnp.maximum)),
                             axis=-1, keepdims=True), sink_ref[h]) for h in heads]
    p_ctx = [jnp.exp(s_ctx[h] - m[h]) for h in heads]
    p_loc = [jnp.exp(s_loc[h] - m[h]) for h in heads]
    den = [jnp.sum(_lane_fold(p_ctx[h], jnp.add) + _lane_fold(p_loc[h], jnp.add), axis=-1, keepdims=True)
           + jnp.exp(sink_ref[h] - m[h]) for h in heads]
    o = [_dot(p_ctx[h].astype(BF16), vc_scr[h // SWA_GROUP]) + _dot(p_loc[h].astype(BF16), vl[h // SWA_GROUP])
         for h in heads]
    for h in heads:
        o_ref[:, h * SWA_HEAD_DIM:(h + 1) * SWA_HEAD_DIM] = o[h] / den[h]


def _swa_lat(rows, proj, sink, k_ctx, v_ctx, cos, sin):
    t = rows.nl_t
    n_blocks = t // SWA_BLOCK
    q_base = rows.nc // SWA_BLOCK
    kv_base = rows.nc // t
    s_ctx = k_ctx.shape[1]
    return pl.pallas_call(
        functools.partial(_swa_lat_kernel, n_blocks=n_blocks),
        out_shape=jax.ShapeDtypeStruct((rows.nl, SWA_WIDTH), F32),
        grid=(rows.nl_b, n_blocks),
        in_specs=[
            pl.BlockSpec(memory_space=pltpu.SMEM),
            pl.BlockSpec((SWA_BLOCK, SWA_WIDTH), lambda b, n: (q_base + b * n_blocks + n, COL_Q)),
            pl.BlockSpec((t, LANE), lambda b, n: (kv_base + b, COL_K)),
            pl.BlockSpec((t, LANE), lambda b, n: (kv_base + b, COL_V)),
            pl.BlockSpec((None, s_ctx, LANE), lambda b, n: (b, 0, 0)),
            pl.BlockSpec((None, s_ctx, LANE), lambda b, n: (b, 0, 0)),
            pl.BlockSpec((t, LANE), lambda b, n: (0, 0)),
            pl.BlockSpec((t, LANE), lambda b, n: (0, 0)),
        ],
        out_specs=pl.BlockSpec((SWA_BLOCK, SWA_WIDTH), lambda b, n: (b * n_blocks + n, 0)),
        scratch_shapes=[pltpu.VMEM((SWA_KV_HEADS, t, SWA_HEAD_DIM), BF16),
                        pltpu.VMEM((SWA_KV_HEADS, t, SWA_HEAD_DIM), BF16),
                        pltpu.VMEM((SWA_KV_HEADS, s_ctx, SWA_HEAD_DIM), BF16),
                        pltpu.VMEM((SWA_KV_HEADS, s_ctx, SWA_HEAD_DIM), BF16)],
        compiler_params=_params("parallel", "arbitrary"),
        name="swa_lat",
    )(sink, proj, proj, proj, k_ctx, v_ctx, cos, sin)


def _hgrn_consts():
    c = HGRN_CHUNK
    t = np.arange(c)[:, None]
    u = np.arange(c)[None, :]
    tri_f = (u <= t).astype(np.float32)
    masks = [t == u]
    for m in HGRN_LEVELS:
        right = ((t // m) % 2) == 1
        masks.append(right & ((u // m) == (t // m) - 1))
    m_f = np.stack(masks).astype(np.float32)
    return (jnp.asarray(tri_f, BF16), jnp.asarray(m_f, F32),
            jnp.asarray(tri_f[::-1, ::-1], BF16), jnp.asarray(m_f[:, ::-1, ::-1], F32))


def _hgrn_intra(items):
    c = HGRN_CHUNK
    nsub = c // SUBLANES
    n = range(len(items))
    fwd = [it[7] for it in items]
    v = [it[2] for it in items]
    q = [_silu(it[0]) * (HGRN_KEY_DIM ** -0.5) for it in items]
    f = [it[3] + it[4] * jax.nn.sigmoid(it[1]) for it in items]
    lf = [jnp.log(f[j]) for j in n]
    k = [1.0 - f[j] for j in n]

    hi = [lf[j].astype(BF16) for j in n]
    r1 = [lf[j] - hi[j].astype(F32) for j in n]
    mid = [r1[j].astype(BF16) for j in n]
    lo = [(r1[j] - mid[j].astype(F32)).astype(BF16) for j in n]
    tri = [items[j][5][...] for j in n]
    b = [_dot(tri[j], hi[j]) + (_dot(tri[j], mid[j]) + _dot(tri[j], lo[j])) for j in n]
    total = [jnp.sum(lf[j], axis=0, keepdims=True) for j in n]

    sub_io = lax.broadcasted_iota(I32, (1, SUBLANES, HGRN_KEY_DIM), 1)
    a = [items[j][6][0] * _dot_nt(q[j].astype(BF16), k[j].astype(BF16)) for j in n]
    for li, m in enumerate(HGRN_LEVELS, start=1):
        fac = []
        for j in n:
            first = m - 1 if fwd[j] else m
            if 2 * m <= SUBLANES:
                b3 = b[j].reshape(nsub, SUBLANES, HGRN_KEY_DIM)
                bnd = None
                for pair in reversed(range(SUBLANES // (2 * m))):
                    r = 2 * pair * m + first
                    piece = jnp.broadcast_to(b3[:, r:r + 1, :], b3.shape)
                    bnd = piece if bnd is None else jnp.where(sub_io < 2 * m * (pair + 1), piece, bnd)
                bnd = bnd.reshape(c, HGRN_KEY_DIM)
            else:
                pieces = [jnp.broadcast_to(b[j][2 * pair * m + first:2 * pair * m + first + 1, :],
                                           (2 * m, HGRN_KEY_DIM)) for pair in range(c // (2 * m))]
                bnd = pieces[0] if len(pieces) == 1 else jnp.concatenate(pieces, axis=0)
            fac.append(jnp.exp(-jnp.abs(b[j] - bnd)))
        prod = [_dot_nt((q[j] * fac[j]).astype(BF16), (k[j] * fac[j]).astype(BF16)) for j in n]
        a = [a[j] + items[j][6][li] * prod[j] for j in n]
    o = [_dot(a[j].astype(BF16), v[j].astype(BF16)) for j in n]
    return o, q, k, b, total


def _hgrn_carry(o, q, k, v, b, total, st):
    o = o + _dot_nt((q * jnp.exp(b)).astype(BF16), st.astype(BF16))
    kc = (k * jnp.exp(total - b)).astype(BF16)
    return o, st * jnp.exp(total) + _dot(v.T.astype(BF16), kc)


def _hgrn_kernel(*refs, n_chunks, has_s0, emit_state):
    (qb_ref, ff_ref, fb_ref, ib_ref, gb_ref, lbp_ref, gw_ref,
     ef_ref, mf_ref, eb_ref, mb_ref) = refs[:11]
    rest = list(refs[11:])
    s0_ref = rest.pop(0) if has_s0 else None
    r_ref = rest.pop(0)
    sout_ref = rest.pop(0) if emit_state else None
    of_scr, ob_scr, stf_scr, stb_scr = rest
    c = HGRN_CHUNK
    gw = gw_ref[...]

    for d, st_scr in enumerate((stf_scr, stb_scr)):
        if has_s0:
            st_scr[...] = s0_ref[d].T
        else:
            st_scr[...] = jnp.zeros((HGRN_VAL_DIM, HGRN_KEY_DIM), F32)

    group = min(HGRN_GROUP, n_chunks)

    def sweep(i, carry):
        sls, items = [], []
        for u in range(group):
            sls.append(pl.ds(pl.multiple_of((i * group + u) * c, c), c))
            items.append((qb_ref[sls[-1], :], ff_ref[sls[-1], :], ib_ref[sls[-1], :], lbp_ref[0:1, :],
                          lbp_ref[1:2, :], ef_ref, mf_ref, True))
        for u in range(group):
            sls.append(pl.ds(pl.multiple_of((n_chunks - 1 - (i * group + u)) * c, c), c))
            items.append((qb_ref[sls[-1], :], fb_ref[sls[-1], :], ib_ref[sls[-1], :], lbp_ref[2:3, :],
                          lbp_ref[3:4, :], eb_ref, mb_ref, False))
        o, q, k, b, total = _hgrn_intra(items)
        for d, (st_scr, o_scr) in enumerate(((stf_scr, of_scr), (stb_scr, ob_scr))):
            st = st_scr[...]
            for u in range(group):
                j = d * group + u
                o_j, st = _hgrn_carry(o[j], q[j], k[j], items[j][2], b[j], total[j], st)
                o_scr[sls[j], :] = o_j
            st_scr[...] = st
        return carry

    lax.fori_loop(0, n_chunks // group, sweep, 0)
    if emit_state:
        sout_ref[0] = stf_scr[...].T
        sout_ref[1] = stb_scr[...].T

    def readout(ci, carry):
        sl = pl.ds(pl.multiple_of(ci * c, c), c)
        tot = of_scr[sl, :] + ob_scr[sl, :]
        ms = jnp.mean(tot * tot, axis=-1, keepdims=True)
        r_ref[sl, :] = (tot * lax.rsqrt(ms + NORM_EPS) * gw) * _silu(gb_ref[sl, :])
        return carry

    lax.fori_loop(0, n_chunks, readout, 0, unroll=2)


def _hgrn(proj, lbp, gw, consts, n_b, t_len, row_block0, n_rows, s0=None, s0_layer=0, emit_state=False):
    n_chunks = t_len // HGRN_CHUNK
    e_f, m_f, e_b, m_b = consts

    def col(off):
        return pl.BlockSpec((t_len, LANE), lambda b, h: (row_block0 + b, COL_HGRN + off * HGRN_HEADS + h))

    def whole(a):
        nd = a.ndim
        return pl.BlockSpec(a.shape, lambda b, h: (0,) * nd)

    in_specs = [col(0), col(1), col(2), col(3), col(4),
                pl.BlockSpec((None, 4, LANE), lambda b, h: (h, 0, 0)),
                pl.BlockSpec((1, HGRN_VAL_DIM), lambda b, h: (0, 0)),
                whole(e_f), whole(m_f), whole(e_b), whole(m_b)]
    args = [proj, proj, proj, proj, proj, lbp, gw.reshape(1, HGRN_VAL_DIM), e_f, m_f, e_b, m_b]
    if s0 is not None:
        in_specs.append(pl.BlockSpec((None, None, 2, None, HGRN_KEY_DIM, HGRN_VAL_DIM),
                                     lambda b, h: (b, s0_layer, 0, h, 0, 0)))
        args.append(s0)
    out_shape = [jax.ShapeDtypeStruct((n_rows, HGRN_WIDTH), F32)]
    out_specs = [pl.BlockSpec((t_len, LANE), lambda b, h: (b, h))]
    if emit_state:
        out_shape.append(jax.ShapeDtypeStruct((n_b, 2, HGRN_HEADS, HGRN_KEY_DIM, HGRN_VAL_DIM), F32))
        out_specs.append(pl.BlockSpec((None, 2, None, HGRN_KEY_DIM, HGRN_VAL_DIM),
                                      lambda b, h: (b, 0, h, 0, 0)))
    return pl.pallas_call(
        functools.partial(_hgrn_kernel, n_chunks=n_chunks, has_s0=s0 is not None, emit_state=emit_state),
        out_shape=out_shape,
        grid=(n_b, HGRN_HEADS),
        in_specs=in_specs,
        out_specs=out_specs,
        scratch_shapes=[pltpu.VMEM((t_len, HGRN_VAL_DIM), F32), pltpu.VMEM((t_len, HGRN_VAL_DIM), F32),
                        pltpu.VMEM((HGRN_VAL_DIM, HGRN_KEY_DIM), F32), pltpu.VMEM((HGRN_VAL_DIM, HGRN_KEY_DIM), F32)],
        compiler_params=_params("parallel", "parallel"),
        name="hgrn_lat" if s0 is not None else "hgrn_ctx",
    )(*args)


def _outproj_kernel(*refs, n_parts, n_x, rows):
    pair_refs = refs[:2 * n_parts]
    w_refs = refs[2 * n_parts:3 * n_parts]
    x_refs = refs[3 * n_parts:3 * n_parts + n_x]
    mod_ref, nw_ref, wr_ref, br_ref, tri_ref, o_ref, xn_ref, meta_ref, cnt_ref, base_scr = refs[3 * n_parts + n_x:]
    acc = None
    for p in range(n_parts):
        a = rows.stream_tile(pair_refs[2 * p:2 * p + 2]).astype(BF16)
        d = _dot(a, w_refs[p][...])
        acc = d if acc is None else acc + d
    x_new = rows.stream_tile(x_refs) + mod_ref[2:3, :] * acc
    o_ref[...] = x_new
    _route_tile(x_new, mod_ref, nw_ref, wr_ref, br_ref, tri_ref, xn_ref, meta_ref, cnt_ref, base_scr)


def _outproj_route(rows, pairs, weights, x, mods, nw2, wr, br, tri):
    in_specs, args = [], []
    for pair in pairs:
        specs, arrs = rows.stream_specs(pair)
        in_specs += specs
        args += arrs
    in_specs += [pl.BlockSpec(w.shape, lambda i: (0, 0)) for w in weights]
    x_specs, x_args = rows.stream_specs(x)
    in_specs += x_specs + [
        rows.mod_spec(),
        pl.BlockSpec((1, D_MODEL), lambda i: (0, 0)),
        pl.BlockSpec((D_MODEL, LANE), lambda i: (0, 0)),
        pl.BlockSpec((1, LANE), lambda i: (0, 0)),
        pl.BlockSpec((TM, TM), lambda i: (0, 0)),
    ]
    return pl.pallas_call(
        functools.partial(_outproj_kernel, n_parts=len(pairs), n_x=len(x_args), rows=rows),
        out_shape=[jax.ShapeDtypeStruct((rows.n, D_MODEL), F32),
                   jax.ShapeDtypeStruct((rows.n, D_MODEL), F32),
                   jax.ShapeDtypeStruct((rows.n, LANE), F32),
                   jax.ShapeDtypeStruct((8, LANE), F32)],
        grid=(rows.n_tiles,),
        in_specs=in_specs,
        out_specs=[pl.BlockSpec((TM, D_MODEL), lambda i: (i, 0)),
                   pl.BlockSpec((TM, D_MODEL), lambda i: (i, 0)),
                   pl.BlockSpec((TM, LANE), lambda i: (i, 0)),
                   pl.BlockSpec((8, LANE), lambda i: (0, 0))],
        scratch_shapes=[pltpu.VMEM((8, LANE), F32)],
        compiler_params=_params("arbitrary"),
        name="outproj_route",
    )(*args, *weights, *x_args, mods, nw2.reshape(1, D_MODEL), wr, br, tri)


def _mla_proj_kernel(x_ref, mod_ref, nw_ref, wd_ref, qnw_ref, kvnw_ref, wuq_ref, wuqp_ref, cos_ref, sin_ref,
                     q_ref, ckv_ref, krb_ref):
    nf = MLA_ROPE_DIM // 4
    h = _normmod(x_ref[...], nw_ref[...], mod_ref[0:1, :], mod_ref[1:2, :]).astype(BF16)
    t1 = _dot(h, wd_ref[...])
    qd = t1[:, :MLA_Q_RANK]
    kvd = t1[:, MLA_Q_RANK:MLA_Q_RANK + MLA_KV_RANK]
    cos = cos_ref[...]
    sin = sin_ref[...]
    qn = qd * lax.rsqrt(jnp.mean(qd * qd, axis=-1, keepdims=True) + NORM_EPS) * qnw_ref[...]
    ckv_ref[...] = kvd * lax.rsqrt(jnp.mean(kvd * kvd, axis=-1, keepdims=True) + NORM_EPS) * kvnw_ref[...]
    krb_ref[...] = _rope(t1[:, MLA_Q_RANK + MLA_KV_RANK:], cos, sin, nf)
    qb = qn.astype(BF16)
    scale = MLA_QK_DIM ** -0.5
    cos2 = jnp.concatenate([cos, cos], axis=1)
    sin2 = jnp.concatenate([sin, sin], axis=1)
    for hp in range(MLA_HEADS // 2):
        cols = slice(hp * 2 * LANE, (hp + 1) * 2 * LANE)
        qh = _dot(qb, wuq_ref[:, cols])
        qp = _dot(qb, wuqp_ref[:, cols])
        q_ref[:, cols] = ((qh * cos2 + qp * sin2) * scale).astype(BF16)


def _mla_proj(rows, x, mods, nw, wd, qnw, kvnw, wuq, wuq_partner, cos, sin):
    return pl.pallas_call(
        _mla_proj_kernel,
        out_shape=[jax.ShapeDtypeStruct((rows.n, MLA_HEADS * LANE), BF16),
                   jax.ShapeDtypeStruct((rows.n, MLA_KV_RANK), F32),
                   jax.ShapeDtypeStruct((rows.n, LANE), F32)],
        grid=(rows.n_tiles,),
        in_specs=[
            pl.BlockSpec((TM, D_MODEL), lambda i: (i, 0)),
            rows.mod_spec(),
            pl.BlockSpec((1, D_MODEL), lambda i: (0, 0)),
            pl.BlockSpec((D_MODEL, MLA_DOWN_WIDTH), lambda i: (0, 0)),
            pl.BlockSpec((1, MLA_Q_RANK), lambda i: (0, 0)),
            pl.BlockSpec((1, MLA_KV_RANK), lambda i: (0, 0)),
            pl.BlockSpec((MLA_Q_RANK, MLA_HEADS * LANE), lambda i: (0, 0)),
            pl.BlockSpec((MLA_Q_RANK, MLA_HEADS * LANE), lambda i: (0, 0)),
            pl.BlockSpec((TM, LANE), lambda i: (rows.pos_block(i), 0)),
            pl.BlockSpec((TM, LANE), lambda i: (rows.pos_block(i), 0)),
        ],
        out_specs=[pl.BlockSpec((TM, MLA_HEADS * LANE), lambda i: (i, 0)),
                   pl.BlockSpec((TM, MLA_KV_RANK), lambda i: (i, 0)),
                   pl.BlockSpec((TM, LANE), lambda i: (i, 0))],
        compiler_params=_params("parallel"),
        name="mla_proj",
    )(x, mods, nw.reshape(1, D_MODEL), wd, qnw.reshape(1, MLA_Q_RANK), kvnw.reshape(1, MLA_KV_RANK), wuq, wuq_partner, cos, sin)


MLA_TQ = 256
MLA_HPS = 4


def _mla_attn_kernel(q_ref, ckv_ref, krb_ref, wkv_ref, o_ref, k_scr, v_scr, *, t_len, hps):
    tq = min(MLA_TQ, t_len)
    ckv = ckv_ref[...].astype(BF16)
    krb = krb_ref[...]
    for hh in range(hps):
        kvh = _dot(ckv, wkv_ref[hh])
        k_scr[hh] = (kvh[:, :LANE] + krb).astype(BF16)
        v_scr[hh] = kvh[:, LANE:].astype(BF16)

    def body(ti, carry):
        sl = pl.ds(pl.multiple_of(ti * tq, tq), tq)
        heads = range(hps)
        s = [_dot_nt(q_ref[sl, hh * LANE:(hh + 1) * LANE], k_scr[hh]) for hh in heads]
        m = [jnp.max(s[hh], axis=-1, keepdims=True) for hh in heads]
        p = [jnp.exp(s[hh] - m[hh]) for hh in heads]
        den = [jnp.sum(p[hh], axis=-1, keepdims=True) for hh in heads]
        o = [_dot(p[hh].astype(BF16), v_scr[hh]) / den[hh] for hh in heads]
        for pair in range(hps // 2):
            o_ref[sl, pair * LANE:(pair + 1) * LANE] = o[2 * pair] + o[2 * pair + 1]
        return carry

    lax.fori_loop(0, t_len // tq, body, 0)


def _mla_attn(q, ckv_all, krb_all, wkv, n_b, t_len, q_row_block0, n_rows):
    s_len = ckv_all.shape[1]
    hps = MLA_HEADS if t_len <= MLA_TQ else MLA_HPS
    return pl.pallas_call(
        functools.partial(_mla_attn_kernel, t_len=t_len, hps=hps),
        out_shape=jax.ShapeDtypeStruct((n_rows, MLA_HEADS * MLA_V_DIM), F32),
        grid=(n_b, MLA_HEADS // hps),
        in_specs=[
            pl.BlockSpec((t_len, hps * LANE), lambda b, hp: (q_row_block0 + b, hp)),
            pl.BlockSpec((None, s_len, MLA_KV_RANK), lambda b, hp: (b, 0, 0)),
            pl.BlockSpec((None, s_len, LANE), lambda b, hp: (b, 0, 0)),
            pl.BlockSpec((hps, MLA_KV_RANK, 2 * LANE), lambda b, hp: (hp, 0, 0)),
        ],
        out_specs=pl.BlockSpec((t_len, hps // 2 * LANE), lambda b, hp: (b, hp)),
        scratch_shapes=[pltpu.VMEM((hps, s_len, LANE), BF16), pltpu.VMEM((hps, s_len, LANE), BF16)],
        compiler_params=_params("parallel", "arbitrary"),
        name="mla_attn",
    )(q, ckv_all, krb_all, wkv)


META_E1, META_E2, META_W1, META_W2, META_R1, META_R2 = range(6)


def _route_tile(x, mod_ref, nw_ref, wr_ref, br_ref, tri_ref, xn_ref, meta_ref, cnt_ref, base_scr):
    @pl.when(pl.program_id(0) == 0)
    def _():
        base_scr[...] = jnp.zeros(base_scr.shape, F32)

    xn = _normmod(x, nw_ref[...], mod_ref[3:4, :], mod_ref[4:5, :])
    xn_ref[...] = xn
    logits = _dot_f32ish(xn, wr_ref[...]) + br_ref[...]
    lane = lax.broadcasted_iota(I32, logits.shape, 1).astype(F32)
    far = float(LANE)

    def first_argmax(vals, vmax):
        return jnp.min(jnp.where(vals == vmax, lane, far), axis=-1, keepdims=True)

    gl = jnp.where(lane < MOE_GROUPS, logits, NEG_BIG)
    gmax = jnp.max(gl, axis=-1, keepdims=True)
    g_w = 1.0 / jnp.sum(jnp.exp(gl - gmax), axis=-1, keepdims=True)
    g_idx = first_argmax(gl, gmax)
    e_lo = MOE_GROUPS + MOE_EPG * g_idx
    el = jnp.where((lane >= e_lo) & (lane < e_lo + MOE_EPG), logits, NEG_BIG)
    m1 = jnp.max(el, axis=-1, keepdims=True)
    i1 = first_argmax(el, m1)
    el2 = jnp.where(lane == i1, NEG_BIG, el)
    m2 = jnp.max(el2, axis=-1, keepdims=True)
    i2 = first_argmax(el2, m2)
    esum = jnp.sum(jnp.exp(el - m1), axis=-1, keepdims=True)
    p1 = 1.0 / esum
    p2 = jnp.exp(m2 - m1) / esum
    w1 = g_w * (p1 / (p1 + p2))
    w2 = g_w * (p2 / (p1 + p2))
    e1 = i1 - MOE_GROUPS
    e2 = i2 - MOE_GROUPS

    oh1 = lane == e1
    oh2 = lane == e2
    oh = jnp.where(oh1 | oh2, 1.0, 0.0)
    before = _dot(tri_ref[...], oh.astype(BF16)) + base_scr[0:1, :]
    r1 = jnp.sum(jnp.where(oh1, before, 0.0), axis=-1, keepdims=True)
    r2 = jnp.sum(jnp.where(oh2, before, 0.0), axis=-1, keepdims=True)
    base_scr[...] = base_scr[...] + jnp.sum(oh, axis=0, keepdims=True)
    cnt_ref[...] = base_scr[...]

    meta = jnp.zeros(logits.shape, F32)
    for slot, val in ((META_E1, e1), (META_E2, e2), (META_W1, w1), (META_W2, w2), (META_R1, r1), (META_R2, r2)):
        meta = jnp.where(lane == slot, val, meta)
    meta_ref[...] = meta


def _row_copy(src, src_row, dst, dst_row, sem):
    return pltpu.make_async_copy(src.at[pl.ds(src_row, 1), :], dst.at[pl.ds(dst_row, 1), :], sem)


def _dispatch_kernel(fill_ref, pos_ref, xn_ref, xs_hbm, zero_scr, sem, fill_sem, *, n_tiles):
    @pl.when(pl.program_id(0) == 0)
    def _():
        zero_scr[...] = jnp.zeros(zero_scr.shape, F32)

        def fill_copy(t):
            return pltpu.make_async_copy(zero_scr, xs_hbm.at[pl.ds(pl.multiple_of(t * TME, TME), TME), :], fill_sem)

        def fill_start(t, carry):
            @pl.when(fill_ref[t] != 0)
            def _():
                fill_copy(t).start()
            return carry

        def fill_wait(t, carry):
            @pl.when(fill_ref[t] != 0)
            def _():
                fill_copy(t).wait()
            return carry

        lax.fori_loop(0, n_tiles, fill_start, 0)
        lax.fori_loop(0, n_tiles, fill_wait, 0)

    def start(r, carry):
        for k in range(2):
            _row_copy(xn_ref, r, xs_hbm, pos_ref[0, 2 * r + k], sem).start(priority=k)
        return carry

    lax.fori_loop(0, TM, start, 0, unroll=DMA_UNROLL)
    for _ in range(2):
        pltpu.make_async_copy(xn_ref, xs_hbm.at[pl.ds(0, TM), :], sem).wait()


def _dispatch(rows, tile_fill, pos, xn, n_tiles):
    return pl.pallas_call(
        functools.partial(_dispatch_kernel, n_tiles=n_tiles),
        out_shape=jax.ShapeDtypeStruct((n_tiles * TME, D_MODEL), F32),
        grid_spec=pltpu.PrefetchScalarGridSpec(
            num_scalar_prefetch=1,
            grid=(rows.n_tiles,),
            in_specs=[
                pl.BlockSpec((None, 1, 2 * TM), lambda i, fill: (i, 0, 0), memory_space=pltpu.SMEM),
                pl.BlockSpec((TM, D_MODEL), lambda i, fill: (i, 0)),
            ],
            out_specs=pl.BlockSpec(memory_space=pl.ANY),
            scratch_shapes=[pltpu.VMEM((TME, D_MODEL), F32), pltpu.SemaphoreType.DMA(()),
                            pltpu.SemaphoreType.DMA(())],
        ),
        compiler_params=_params("arbitrary"),
        name="moe_dispatch",
    )(tile_fill, pos, xn)


def _ffn_kernel(te_ref, nv_ref, x_ref, wg_ref, wu_ref, wd_ref, y_ref, wg_b, wu_b, wd_b):
    t = pl.program_id(0)
    valid = t < nv_ref[0]
    new_expert = (t == 0) | (te_ref[t] != te_ref[jnp.maximum(t - 1, 0)])

    @pl.when(valid & new_expert)
    def _():
        wg_b[...] = wg_ref[...].astype(BF16)
        wu_b[...] = wu_ref[...].astype(BF16)
        wd_b[...] = wd_ref[...].astype(BF16)

    @pl.when(valid)
    def _():
        x = x_ref[...].astype(BF16)
        a = _silu(_dot(x, wg_b[...])) * _dot(x, wu_b[...])
        y_ref[...] = _dot(a.astype(BF16), wd_b[...])

    @pl.when(jnp.logical_not(valid))
    def _():
        y_ref[...] = jnp.zeros(y_ref.shape, F32)


def _ffn(tile_expert, n_valid, xs, w_gate, w_up, w_down, layer, n_tiles):
    def xmap(t, te, nv):
        return (jnp.minimum(t, nv[0] - 1), 0)

    def wmap(t, te, nv):
        return (layer, te[t], 0, 0)

    return pl.pallas_call(
        _ffn_kernel,
        out_shape=jax.ShapeDtypeStruct((n_tiles * TME, D_MODEL), F32),
        grid_spec=pltpu.PrefetchScalarGridSpec(
            num_scalar_prefetch=2,
            grid=(n_tiles,),
            in_specs=[
                pl.BlockSpec((TME, D_MODEL), xmap),
                pl.BlockSpec((None, None, D_MODEL, MOE_HIDDEN), wmap),
                pl.BlockSpec((None, None, D_MODEL, MOE_HIDDEN), wmap),
                pl.BlockSpec((None, None, MOE_HIDDEN, D_MODEL), wmap),
            ],
            out_specs=pl.BlockSpec((TME, D_MODEL), lambda t, te, nv: (t, 0)),
            scratch_shapes=[pltpu.VMEM((D_MODEL, MOE_HIDDEN), BF16),
                            pltpu.VMEM((D_MODEL, MOE_HIDDEN), BF16),
                            pltpu.VMEM((MOE_HIDDEN, D_MODEL), BF16)],
        ),
        compiler_params=_params("arbitrary"),
        name="moe_ffn",
    )(tile_expert, n_valid, xs, w_gate, w_up, w_down)


def _combine_kernel(pos_ref, x_ref, meta_ref, mod_ref, fnw_ref, ys_hbm, *rest, final, ctx_tiles):
    if final:
        o_ctx_ref, o_lat_ref, buf0, buf1, sem = rest
    else:
        o_ref, buf0, buf1, sem = rest
    bufs = (buf0, buf1)

    def start(r, carry):
        for k in range(2):
            _row_copy(ys_hbm, pos_ref[0, 2 * r + k], bufs[k], r, sem).start(priority=k)
        return carry

    lax.fori_loop(0, TM, start, 0, unroll=DMA_UNROLL)
    for k in range(2):
        pltpu.make_async_copy(ys_hbm.at[pl.ds(0, TM), :], bufs[k], sem).wait()
    meta = meta_ref[...]
    y = meta[:, META_W1:META_W1 + 1] * buf0[...] + meta[:, META_W2:META_W2 + 1] * buf1[...]
    xo = x_ref[...] + mod_ref[5:6, :] * y
    if not final:
        o_ref[...] = xo
        return
    xo = xo * lax.rsqrt(jnp.mean(xo * xo, axis=-1, keepdims=True) + NORM_EPS) * fnw_ref[...]
    is_ctx = pl.program_id(0) < ctx_tiles

    @pl.when(is_ctx)
    def _():
        o_ctx_ref[...] = xo

    @pl.when(jnp.logical_not(is_ctx))
    def _():
        o_lat_ref[...] = xo


def _combine(rows, pos, x, meta, mods, fnw, ys, final):
    ct = rows.ctx_tiles
    if final:
        out_shape = [jax.ShapeDtypeStruct((rows.nc, D_MODEL), F32), jax.ShapeDtypeStruct((rows.nl, D_MODEL), F32)]
        out_specs = [pl.BlockSpec((TM, D_MODEL), lambda i: (jnp.minimum(i, ct - 1), 0)),
                     pl.BlockSpec((TM, D_MODEL), lambda i: (jnp.maximum(i - ct, 0), 0))]
    else:
        out_shape = jax.ShapeDtypeStruct((rows.n, D_MODEL), F32)
        out_specs = pl.BlockSpec((TM, D_MODEL), lambda i: (i, 0))
    return pl.pallas_call(
        functools.partial(_combine_kernel, final=final, ctx_tiles=ct),
        out_shape=out_shape,
        grid=(rows.n_tiles,),
        in_specs=[
            pl.BlockSpec((None, 1, 2 * TM), lambda i: (i, 0, 0), memory_space=pltpu.SMEM),
            pl.BlockSpec((TM, D_MODEL), lambda i: (i, 0)),
            pl.BlockSpec((TM, LANE), lambda i: (i, 0)),
            rows.mod_spec(),
            pl.BlockSpec((1, D_MODEL), lambda i: (0, 0)),
            pl.BlockSpec(memory_space=pl.ANY),
        ],
        out_specs=out_specs,
        scratch_shapes=[pltpu.VMEM((TM, D_MODEL), F32), pltpu.VMEM((TM, D_MODEL), F32),
                        pltpu.SemaphoreType.DMA(())],
        compiler_params=_params("arbitrary"),
        name="moe_combine",
    )(pos, x, meta, mods, fnw.reshape(1, D_MODEL), ys)


def _moe(rows, x, xn, meta, cnt, mods, w_gate, w_up, w_down, layer, fnw, final):
    n_assign = 2 * rows.n
    n_tiles = n_assign // TME + MOE_EXPERTS

    counts = cnt[0, :MOE_EXPERTS].astype(I32)
    padded = ((counts + TME - 1) // TME) * TME
    ends = jnp.cumsum(padded)
    starts = ends - padded
    experts = jnp.arange(MOE_EXPERTS, dtype=I32)
    e = meta[:, META_E1:META_E2 + 1].astype(I32)
    rank = meta[:, META_R1:META_R2 + 1].astype(I32)
    start_of = jnp.sum(jnp.where(e[..., None] == experts, starts, 0), axis=-1)
    pos = (start_of + rank).reshape(rows.n_tiles, 1, 2 * TM)
    n_valid = ends[-1] // TME
    tile_first = jnp.arange(n_tiles, dtype=I32) * TME
    tile_start = jnp.minimum(tile_first, ends[-1] - TME)
    tile_expert = jnp.sum((ends[None, :] <= tile_start[:, None]).astype(I32), axis=1)
    tile_expert = jnp.minimum(tile_expert, MOE_EXPERTS - 1)
    tile_oh = tile_expert[:, None] == experts
    tile_rows = jnp.sum(jnp.where(tile_oh, counts + starts, 0), axis=1) - tile_start
    tile_fill = ((tile_first >= ends[-1]) | (tile_rows < TME)).astype(I32)

    xs = _dispatch(rows, tile_fill, pos, xn, n_tiles)
    ys = _ffn(tile_expert, n_valid.reshape(1).astype(I32), xs, w_gate, w_up, w_down, layer, n_tiles)
    return _combine(rows, pos, x, meta, mods, fnw, ys, final)


def _lower_bound_params(p):
    pr = jax.nn.softmax(p.astype(F32), axis=0)
    lb = jnp.cumsum(pr, axis=0) - pr[0:1]
    lb = jnp.clip(lb, 0.0, 1.0 - 1e-6)
    return jnp.maximum(lb, LOG_TINY), 1.0 - lb


def kernel(x_prompt, x_sample, c, cache_swa_k, cache_swa_v, state_hgrn, cache_mla_ckv, cache_mla_krope, c_ctx, mod_w, mod_b, norm1_w, norm2_w, final_norm_w, even_w_in, even_w_out, swa_sink, hgrn_lb_fwd, hgrn_lb_bwd, hgrn_gnorm_w, mla_w_dq, mla_qnorm_w, mla_w_uq, mla_w_dkv, mla_kvnorm_w, mla_w_ukv, mla_w_o, moe_router_group_w, moe_router_group_b, moe_router_expert_w, moe_router_expert_b, moe_w_gate, moe_w_up, moe_w_down):
    nc_b, nc_t, _ = x_prompt.shape
    nl_b, nl_t, _ = x_sample.shape
    rows = _Rows(nc_b, nc_t, nl_b, nl_t)
    past = cache_swa_k.shape[2]

    x = (x_prompt.reshape(rows.nc, D_MODEL), x_sample.reshape(rows.nl, D_MODEL))
    mod_rows = 16
    cvec = jnp.concatenate([c_ctx[None, :], c, jnp.zeros((mod_rows - 1 - nl_b, D_MODEL), F32)], axis=0)
    mods_all = _modulation(cvec, mod_w, mod_b).reshape(DEPTH, mod_rows, 6, D_MODEL)

    hconsts = _hgrn_consts()
    la_f, l1_f = _lower_bound_params(hgrn_lb_fwd)
    la_b, l1_b = _lower_bound_params(hgrn_lb_bwd)
    lbp_all = jnp.stack([la_f, l1_f, la_b, l1_b], axis=1).reshape(N_EVEN, 4, HGRN_HEADS, LANE).transpose(0, 2, 1, 3)
    swa_cos, swa_sin = _rope_tables(nl_t, SWA_HEAD_DIM, 0, LANE, 0)
    mla_cos, mla_sin = _rope_tables(nl_t, MLA_ROPE_DIM, MLA_NOPE_DIM, MLA_QK_DIM, TM)
    tri =jnp.asarray(np.tril(np.ones((TM, TM), np.float32), -1), BF16)

    new_k, new_v, new_s, new_ckv, new_kr = [], [], [], [], []
    for l in range(DEPTH):
        j = l // 2
        mods = mods_all[l]
        wr = jnp.zeros((D_MODEL, LANE), F32)
        wr = wr.at[:, :MOE_GROUPS].set(moe_router_group_w[l])
        wr = wr.at[:, MOE_GROUPS:MOE_GROUPS + MOE_EXPERTS].set(
            moe_router_expert_w[l].transpose(1, 0, 2).reshape(D_MODEL, MOE_EXPERTS))
        br = jnp.zeros((1, LANE), F32)
        br = br.at[0, :MOE_GROUPS].set(moe_router_group_b[l])
        br = br.at[0, MOE_GROUPS:MOE_GROUPS + MOE_EXPERTS].set(moe_router_expert_b[l].reshape(MOE_EXPERTS))
        if l % 2 == 0:
            proj = _even_proj(rows, x, mods, norm1_w[l], even_w_in[j].astype(BF16))
            a_ctx = _swa_ctx(rows, proj, swa_sink[j])
            a_lat = _swa_lat(rows, proj, swa_sink[j],
                             cache_swa_k[:, j].reshape(nl_b, past, SWA_KV_WIDTH),
                             cache_swa_v[:, j].reshape(nl_b, past, SWA_KV_WIDTH), swa_cos, swa_sin)
            r_ctx, s_ctx = _hgrn(proj, lbp_all[j], hgrn_gnorm_w[j], hconsts, nc_b, nc_t, 0, rows.nc,
                                 emit_state=True)
            (r_lat,) = _hgrn(proj, lbp_all[j], hgrn_gnorm_w[j], hconsts, nl_b, nl_t, rows.nc // nl_t, rows.nl,
                             s0=state_hgrn, s0_layer=j)
            w_out = even_w_out[j].astype(BF16)
            x, xn, meta, cnt = _outproj_route(rows, [(a_ctx, a_lat), (r_ctx, r_lat)],
                                              [w_out[:SWA_WIDTH], w_out[SWA_WIDTH:]], x, mods, norm2_w[l], wr, br, tri)
            kv = proj[:rows.nc, SWA_WIDTH:SWA_WIDTH + 2 * SWA_KV_WIDTH]
            new_k.append(kv[:, :SWA_KV_WIDTH].reshape(nc_b, nc_t, SWA_KV_HEADS, SWA_HEAD_DIM))
            new_v.append(kv[:, SWA_KV_WIDTH:].reshape(nc_b, nc_t, SWA_KV_HEADS, SWA_HEAD_DIM))
            new_s.append(s_ctx)
        else:
            wd = jnp.zeros((D_MODEL, MLA_DOWN_WIDTH), F32)
            wd = wd.at[:, :MLA_Q_RANK].set(mla_w_dq[j])
            wd = wd.at[:, MLA_Q_RANK:MLA_Q_RANK + MLA_KV_RANK].set(mla_w_dkv[j][:, :MLA_KV_RANK])
            kr_lo = MLA_Q_RANK + MLA_KV_RANK + MLA_NOPE_DIM
            wd = wd.at[:, kr_lo:kr_lo + MLA_ROPE_DIM].set(mla_w_dkv[j][:, MLA_KV_RANK:])
            wuq = jnp.pad(mla_w_uq[j].reshape(MLA_Q_RANK, MLA_HEADS, MLA_QK_DIM),
                          ((0, 0), (0, 0), (0, LANE - MLA_QK_DIM))).reshape(MLA_Q_RANK, MLA_HEADS * LANE)
            wukv = mla_w_ukv[j].reshape(MLA_KV_RANK, MLA_HEADS, MLA_NOPE_DIM + MLA_V_DIM).transpose(1, 0, 2)
            wk = jnp.pad(wukv[..., :MLA_NOPE_DIM], ((0, 0), (0, 0), (0, LANE - MLA_NOPE_DIM)))
            wv_e = jnp.pad(wukv[..., MLA_NOPE_DIM:], ((0, 0), (0, 0), (0, LANE - MLA_V_DIM)))
            wv_o = jnp.pad(wukv[..., MLA_NOPE_DIM:], ((0, 0), (0, 0), (LANE - MLA_V_DIM, 0)))
            odd = (jnp.arange(MLA_HEADS) % 2 == 1)[:, None, None]
            wv = jnp.where(odd, wv_o, wv_e)
            wuq = wuq.astype(BF16)
            w3 = wuq.reshape(MLA_Q_RANK, MLA_HEADS, LANE)
            nf = MLA_ROPE_DIM // 4
            rot = w3[..., MLA_NOPE_DIM:MLA_QK_DIM].reshape(MLA_Q_RANK, MLA_HEADS, 2, 2, nf)[..., ::-1, :]
            wuq_partner = jnp.concatenate([w3[..., :MLA_NOPE_DIM], rot.reshape(MLA_Q_RANK, MLA_HEADS, MLA_ROPE_DIM),
                                           w3[..., MLA_QK_DIM:]], axis=-1).reshape(MLA_Q_RANK, MLA_HEADS * LANE)
            q, ckv, krb = _mla_proj(rows, x, mods, norm1_w[l], wd.astype(BF16), mla_qnorm_w[j], mla_kvnorm_w[j],
                                    wuq, wuq_partner, mla_cos, mla_sin)
            wkv = jnp.concatenate([wk, wv], axis=-1).astype(BF16)
            ckv_c = ckv[:rows.nc].reshape(nc_b, nc_t, MLA_KV_RANK)
            krb_c = krb[:rows.nc].reshape(nc_b, nc_t, LANE)
            o_ctx = _mla_attn(q, ckv_c, krb_c, wkv, nc_b, nc_t, 0, rows.nc)
            cache_kr = jnp.pad(cache_mla_krope[:, j], ((0, 0), (0, 0), (MLA_NOPE_DIM, LANE - MLA_QK_DIM)))
            ckv_l = jnp.concatenate([cache_mla_ckv[:, j], ckv[rows.nc:].reshape(nl_b, nl_t, MLA_KV_RANK)], axis=1)
            krb_l = jnp.concatenate([cache_kr, krb[rows.nc:].reshape(nl_b, nl_t, LANE)], axis=1)
            o_lat = _mla_attn(q, ckv_l, krb_l, wkv, nl_b, nl_t, rows.nc // nl_t, rows.nl)
            x, xn, meta, cnt = _outproj_route(rows, [(o_ctx, o_lat)], [mla_w_o[j].astype(BF16)], x, mods,
                                              norm2_w[l], wr, br, tri)
            new_ckv.append(ckv_c)
            new_kr.append(krb_c[..., MLA_NOPE_DIM:MLA_QK_DIM])

        x = _moe(rows, x, xn, meta, cnt, mods, moe_w_gate, moe_w_up, moe_w_down, l,
                 final_norm_w, final=(l == DEPTH - 1))

    y_prompt = x[0].reshape(nc_b, nc_t, D_MODEL)
    y_sample = x[1].reshape(nl_b, nl_t, D_MODEL)
    return (y_prompt, y_sample, jnp.stack(new_k, axis=1), jnp.stack(new_v, axis=1), jnp.stack(new_s, axis=1),
            jnp.stack(new_ckv, axis=1), jnp.stack(new_kr, axis=1))
```

```python
import functools
import math

import numpy as np
import jax
import jax.numpy as jnp
from jax import lax
from jax.experimental import pallas as pl
from jax.experimental.pallas import tpu as pltpu

F32, BF16, I32 = jnp.float32, jnp.bfloat16, jnp.int32

D_MODEL = 1024
DEPTH = 4
GRID_W = 64
ROPE_THETA = 10000.0
NORM_EPS = 1e-6
NEG_BIG = -1e30
LOG_TINY = 1e-30
N_EVEN = (DEPTH + 1) // 2
N_ODD = DEPTH // 2

SWA_HEADS = 8
SWA_KV_HEADS = 2
SWA_GROUP = SWA_HEADS // SWA_KV_HEADS
SWA_HEAD_DIM = 64
SWA_WIDTH = SWA_HEADS * SWA_HEAD_DIM
SWA_KV_WIDTH = SWA_KV_HEADS * SWA_HEAD_DIM
SWA_WINDOW = 128
SWA_BLOCK = 128

HGRN_HEADS = 4
HGRN_KEY_DIM = 128
HGRN_VAL_DIM = 128
HGRN_WIDTH = HGRN_HEADS * HGRN_KEY_DIM
HGRN_CHUNK = 128
HGRN_LEVELS = (1, 2, 4, 8, 16, 32, 64)
HGRN_GROUP = 4

EVEN_IN_WIDTH = SWA_WIDTH + 2 * SWA_KV_WIDTH + 5 * HGRN_WIDTH
LANE = 128
COL_Q, COL_K, COL_V = 0, SWA_WIDTH // LANE, (SWA_WIDTH + SWA_KV_WIDTH) // LANE
COL_HGRN = (SWA_WIDTH + 2 * SWA_KV_WIDTH) // LANE

MLA_HEADS = 16
MLA_Q_RANK = 384
MLA_KV_RANK = 256
MLA_NOPE_DIM = 64
MLA_ROPE_DIM = 32
MLA_V_DIM = 64
MLA_QK_DIM = MLA_NOPE_DIM + MLA_ROPE_DIM
MLA_DOWN_WIDTH = MLA_Q_RANK + MLA_KV_RANK + LANE

MOE_GROUPS = 4
MOE_EPG = 8
MOE_EXPERTS = MOE_GROUPS * MOE_EPG
MOE_HIDDEN = 256

TM = 512
TME = 512
DMA_UNROLL = 8
SUBLANES = 8
VMEM_LIMIT = 48 * 1024 * 1024


def _params(*sem):
    return pltpu.CompilerParams(dimension_semantics=sem, vmem_limit_bytes=VMEM_LIMIT)


def _dot(a, b):
    return jnp.dot(a, b, preferred_element_type=F32)


def _dot_nt(a, b):
    return lax.dot_general(a, b, (((1,), (1,)), ((), ())), preferred_element_type=F32)


def _split2(a):
    hi = a.astype(BF16)
    return hi, (a - hi.astype(F32)).astype(BF16)


def _dot_f32ish(a, b):
    ah, al = _split2(a)
    bh, bl = _split2(b)
    return _dot(ah, bh) + (_dot(ah, bl) + _dot(al, bh))


def _silu(x):
    return x * jax.nn.sigmoid(x)


def _normmod(x, nw, shift, scale):
    ms = jnp.mean(x * x, axis=-1, keepdims=True)
    return (x * lax.rsqrt(ms + NORM_EPS) * nw) * (1.0 + scale) + shift


def _mod_kernel(c_ref, w_ref, b_ref, o_ref):
    o_ref[...] = _dot_f32ish(_silu(c_ref[...]), w_ref[...]) + b_ref[...]


def _modulation(cvec, mod_w, mod_b):
    rows = cvec.shape[0]
    nb = 6 * D_MODEL // 1024
    return pl.pallas_call(
        _mod_kernel,
        out_shape=jax.ShapeDtypeStruct((DEPTH, rows, 6 * D_MODEL), F32),
        grid=(DEPTH, nb),
        in_specs=[
            pl.BlockSpec((rows, D_MODEL), lambda l, n: (0, 0)),
            pl.BlockSpec((None, D_MODEL, 1024), lambda l, n: (l, 0, n)),
            pl.BlockSpec((None, 1, 1024), lambda l, n: (l, 0, n)),
        ],
        out_specs=pl.BlockSpec((None, rows, 1024), lambda l, n: (l, 0, n)),
        compiler_params=_params("parallel", "parallel"),
        name="modulation",
    )(cvec, mod_w, mod_b.reshape(DEPTH, 1, 6 * D_MODEL))


class _Rows:
    def __init__(self, nc_b, nc_t, nl_b, nl_t):
        self.nc_b, self.nc_t, self.nl_b, self.nl_t = nc_b, nc_t, nl_b, nl_t
        self.nc = nc_b * nc_t
        self.nl = nl_b * nl_t
        self.n = self.nc + self.nl
        assert self.nc % TM == 0 and nl_t % TM == 0 and self.nc % nl_t == 0
        self.ctx_tiles = self.nc // TM
        self.tiles_per_lat = nl_t // TM
        self.n_tiles = self.n // TM

    def mod_row(self, i):
        return jnp.where(i < self.ctx_tiles, 0, 1 + (i - self.ctx_tiles) // self.tiles_per_lat)

    def mod_spec(self):
        return pl.BlockSpec((None, 6, D_MODEL), lambda i: (self.mod_row(i), 0, 0))

    def stream_specs(self, x):
        ct = self.ctx_tiles
        if not isinstance(x, tuple):
            return [pl.BlockSpec((TM, x.shape[1]), lambda i: (i, 0))], [x]
        x_ctx, x_lat = x
        return ([pl.BlockSpec((TM, x_ctx.shape[1]), lambda i: (jnp.minimum(i, ct - 1), 0)),
                 pl.BlockSpec((TM, x_lat.shape[1]), lambda i: (jnp.maximum(i - ct, 0), 0))], [x_ctx, x_lat])

    def stream_tile(self, refs):
        if len(refs) == 1:
            return refs[0][...]
        return jnp.where(pl.program_id(0) < self.ctx_tiles, refs[0][...], refs[1][...])

    def pos_block(self, i):
        return jnp.where(i < self.ctx_tiles, 0, 1 + (i - self.ctx_tiles) % self.tiles_per_lat)


def _even_proj_kernel(*refs, rows):
    mod_ref, nw_ref, w_ref, o_ref = refs[-4:]
    x = rows.stream_tile(refs[:-4])
    h = _normmod(x, nw_ref[...], mod_ref[0:1, :], mod_ref[1:2, :]).astype(BF16)
    step = 256
    for c in range(EVEN_IN_WIDTH // step):
        o_ref[:, c * step:(c + 1) * step] = _dot(h, w_ref[:, c * step:(c + 1) * step])


def _even_proj(rows, x, mods, nw, w_in_bf16):
    x_specs, x_args = rows.stream_specs(x)
    return pl.pallas_call(
        functools.partial(_even_proj_kernel, rows=rows),
        out_shape=jax.ShapeDtypeStruct((rows.n, EVEN_IN_WIDTH), F32),
        grid=(rows.n_tiles,),
        in_specs=x_specs + [
            rows.mod_spec(),
            pl.BlockSpec((1, D_MODEL), lambda i: (0, 0)),
            pl.BlockSpec((D_MODEL, EVEN_IN_WIDTH), lambda i: (0, 0)),
        ],
        out_specs=pl.BlockSpec((TM, EVEN_IN_WIDTH), lambda i: (i, 0)),
        compiler_params=_params("arbitrary"),
        name="even_proj",
    )(*x_args, mods, nw.reshape(1, D_MODEL), w_in_bf16)


def _rope_tables(t_len, rot_dim, lane_lo, lane_hi, lead_rows):
    half = rot_dim // 2
    nf = half // 2
    lane = np.arange(LANE)
    d = (lane - lane_lo) % rot_dim
    active = (lane >= lane_lo) & (lane < lane_hi)
    use_col = d >= half
    fidx = d % nf
    first = (d % half) < nf
    pos = jnp.arange(t_len)
    row = (pos // GRID_W).astype(F32)
    col = (pos % GRID_W).astype(F32)
    inv = jnp.exp(-math.log(ROPE_THETA) * jnp.arange(nf, dtype=F32) / nf)
    p = jnp.where(jnp.asarray(use_col)[None, :], col[:, None], row[:, None])
    ang = p * inv[jnp.asarray(fidx)][None, :]
    act = jnp.asarray(active)[None, :]
    cos = jnp.where(act, jnp.cos(ang), 1.0)
    sin = jnp.where(act, jnp.sin(ang), 0.0)
    sin = jnp.where(jnp.asarray(first)[None, :], -sin, sin)
    if lead_rows:
        cos = jnp.concatenate([jnp.ones((lead_rows, LANE), F32), cos], axis=0)
        sin = jnp.concatenate([jnp.zeros((lead_rows, LANE), F32), sin], axis=0)
    return cos, sin


def _rope(x, cos, sin, nf):
    lane = lax.broadcasted_iota(I32, x.shape, 1)
    up = pltpu.roll(x, LANE - nf, axis=1)
    dn = pltpu.roll(x, nf, axis=1)
    partner = jnp.where((lane & nf) == 0, up, dn)
    return x * cos + partner * sin


def _swa_ctx_kernel(sink_ref, q_ref, k_ref, v_ref, o_ref):
    scale = SWA_HEAD_DIM ** -0.5
    heads = range(SWA_HEADS)
    kk = [k_ref[:, kv * SWA_HEAD_DIM:(kv + 1) * SWA_HEAD_DIM].astype(BF16) for kv in range(SWA_KV_HEADS)]
    vv = [v_ref[:, kv * SWA_HEAD_DIM:(kv + 1) * SWA_HEAD_DIM].astype(BF16) for kv in range(SWA_KV_HEADS)]
    q = [q_ref[:, h * SWA_HEAD_DIM:(h + 1) * SWA_HEAD_DIM].astype(BF16) for h in heads]
    s = [_dot_nt(q[h], kk[h // SWA_GROUP]) * scale for h in heads]
    m = [jnp.maximum(jnp.max(s[h], axis=-1, keepdims=True), sink_ref[h]) for h in heads]
    p = [jnp.exp(s[h] - m[h]) for h in heads]
    den = [jnp.sum(p[h], axis=-1, keepdims=True) + jnp.exp(sink_ref[h] - m[h]) for h in heads]
    o = [_dot(p[h].astype(BF16), vv[h // SWA_GROUP]) / den[h] for h in heads]
    for h in heads:
        o_ref[:, h * SWA_HEAD_DIM:(h + 1) * SWA_HEAD_DIM] = o[h]


def _swa_ctx(rows, proj, sink):
    t = rows.nc_t
    return pl.pallas_call(
        _swa_ctx_kernel,
        out_shape=jax.ShapeDtypeStruct((rows.nc, SWA_WIDTH), F32),
        grid=(rows.nc_b,),
        in_specs=[
            pl.BlockSpec(memory_space=pltpu.SMEM),
            pl.BlockSpec((t, SWA_WIDTH), lambda b: (b, COL_Q)),
            pl.BlockSpec((t, LANE), lambda b: (b, COL_K)),
            pl.BlockSpec((t, LANE), lambda b: (b, COL_V)),
        ],
        out_specs=pl.BlockSpec((t, SWA_WIDTH), lambda b: (b, 0)),
        compiler_params=_params("parallel"),
        name="swa_ctx",
    )(sink, proj, proj, proj)


def _lane_fold(x, op):
    out = x[:, :LANE]
    for i in range(1, x.shape[1] // LANE):
        out = op(out, x[:, i * LANE:(i + 1) * LANE])
    return out


def _swa_lat_kernel(sink_ref, q_ref, k_ref, v_ref, kc_ref, vc_ref, cos_ref, sin_ref, o_ref,
                    kl_scr, vl_scr, kc_scr, vc_scr, *, n_blocks):
    scale = SWA_HEAD_DIM ** -0.5
    nf = SWA_HEAD_DIM // 4
    n = pl.program_id(1)

    @pl.when(n == 0)
    def _():
        kr = _rope(k_ref[...], cos_ref[...], sin_ref[...], nf)
        for kv in range(SWA_KV_HEADS):
            cols = slice(kv * SWA_HEAD_DIM, (kv + 1) * SWA_HEAD_DIM)
            kl_scr[kv] = kr[:, cols].astype(BF16)
            vl_scr[kv] = v_ref[:, cols].astype(BF16)
            kc_scr[kv] = kc_ref[:, cols].astype(BF16)
            vc_scr[kv] = vc_ref[:, cols].astype(BF16)

    q0 = pl.multiple_of(n * SWA_BLOCK, SWA_BLOCK)
    cq = cos_ref[pl.ds(q0, SWA_BLOCK), :]
    sq = sin_ref[pl.ds(q0, SWA_BLOCK), :]
    qs = [_rope(q_ref[:, g * LANE:(g + 1) * LANE], cq, sq, nf) for g in range(SWA_WIDTH // LANE)]

    qi = lax.broadcasted_iota(I32, (SWA_BLOCK, SWA_BLOCK), 0)
    r = lax.broadcasted_iota(I32, (SWA_BLOCK, SWA_BLOCK), 1)
    band, mb = [], []
    for off in (-1, 0, 1):
        blk = n + off
        valid = (blk >= 0) & (blk < n_blocks)
        st = pl.multiple_of(jnp.clip(blk, 0, n_blocks - 1) * SWA_BLOCK, SWA_BLOCK)
        band.append(pl.ds(st, SWA_BLOCK))
        rel = qi - r - off * SWA_BLOCK
        mb.append(jnp.where((jnp.abs(rel) <= SWA_WINDOW) & valid, 1.0, 0.0))
    mask = jnp.concatenate(mb, axis=1) > 0.5

    kl = [jnp.concatenate([kl_scr[kv, sl, :] for sl in band], axis=0) for kv in range(SWA_KV_HEADS)]
    vl = [jnp.concatenate([vl_scr[kv, sl, :] for sl in band], axis=0) for kv in range(SWA_KV_HEADS)]
    heads = range(SWA_HEADS)
    lanes = [slice((h * SWA_HEAD_DIM) % LANE, (h * SWA_HEAD_DIM) % LANE + SWA_HEAD_DIM) for h in heads]
    q = [qs[h * SWA_HEAD_DIM // LANE][:, lanes[h]].astype(BF16) for h in heads]
    s_ctx = [_dot_nt(q[h], kc_scr[h // SWA_GROUP]) * scale for h in heads]
    s_loc = [jnp.where(mask, _dot_nt(q[h], kl[h // SWA_GROUP]) * scale, NEG_BIG) for h in heads]
    m = [jnp.maximum(jnp.max(jnp.maximum(_lane_fold(s_ctx[h], jnp.maximum), _lane_fold(s_loc[h], jnp.maximum)),
                             axis=-1, keepdims=True), sink_ref[h]) for h in heads]
    p_ctx = [jnp.exp(s_ctx[h] - m[h]) for h in heads]
    p_loc = [jnp.exp(s_loc[h] - m[h]) for h in heads]
    den = [jnp.sum(_lane_fold(p_ctx[h], jnp.add) + _lane_fold(p_loc[h], jnp.add), axis=-1, keepdims=True)
           + jnp.exp(sink_ref[h] - m[h]) for h in heads]
    o = [_dot(p_ctx[h].astype(BF16), vc_scr[h // SWA_GROUP]) + _dot(p_loc[h].astype(BF16), vl[h // SWA_GROUP])
         for h in heads]
    for h in heads:
        o_ref[:, h * SWA_HEAD_DIM:(h + 1) * SWA_HEAD_DIM] = o[h] / den[h]


def _swa_lat(rows, proj, sink, k_ctx, v_ctx, cos, sin):
    t = rows.nl_t
    n_blocks = t // SWA_BLOCK
    q_base = rows.nc // SWA_BLOCK
    kv_base = rows.nc // t
    s_ctx = k_ctx.shape[1]
    return pl.pallas_call(
        functools.partial(_swa_lat_kernel, n_blocks=n_blocks),
        out_shape=jax.ShapeDtypeStruct((rows.nl, SWA_WIDTH), F32),
        grid=(rows.nl_b, n_blocks),
        in_specs=[
            pl.BlockSpec(memory_space=pltpu.SMEM),
            pl.BlockSpec((SWA_BLOCK, SWA_WIDTH), lambda b, n: (q_base + b * n_blocks + n, COL_Q)),
            pl.BlockSpec((t, LANE), lambda b, n: (kv_base + b, COL_K)),
            pl.BlockSpec((t, LANE), lambda b, n: (kv_base + b, COL_V)),
            pl.BlockSpec((None, s_ctx, LANE), lambda b, n: (b, 0, 0)),
            pl.BlockSpec((None, s_ctx, LANE), lambda b, n: (b, 0, 0)),
            pl.BlockSpec((t, LANE), lambda b, n: (0, 0)),
            pl.BlockSpec((t, LANE), lambda b, n: (0, 0)),
        ],
        out_specs=pl.BlockSpec((SWA_BLOCK, SWA_WIDTH), lambda b, n: (b * n_blocks + n, 0)),
        scratch_shapes=[pltpu.VMEM((SWA_KV_HEADS, t, SWA_HEAD_DIM), BF16),
                        pltpu.VMEM((SWA_KV_HEADS, t, SWA_HEAD_DIM), BF16),
                        pltpu.VMEM((SWA_KV_HEADS, s_ctx, SWA_HEAD_DIM), BF16),
                        pltpu.VMEM((SWA_KV_HEADS, s_ctx, SWA_HEAD_DIM), BF16)],
        compiler_params=_params("parallel", "arbitrary"),
        name="swa_lat",
    )(sink, proj, proj, proj, k_ctx, v_ctx, cos, sin)


def _hgrn_consts():
    c = HGRN_CHUNK
    t = np.arange(c)[:, None]
    u = np.arange(c)[None, :]
    tri_f = (u <= t).astype(np.float32)
    masks = [t == u]
    for m in HGRN_LEVELS:
        right = ((t // m) % 2) == 1
        masks.append(right & ((u // m) == (t // m) - 1))
    m_f = np.stack(masks).astype(np.float32)
    return (jnp.asarray(tri_f, BF16), jnp.asarray(m_f, F32),
            jnp.asarray(tri_f[::-1, ::-1], BF16), jnp.asarray(m_f[:, ::-1, ::-1], F32))


def _hgrn_intra(items):
    c = HGRN_CHUNK
    nsub = c // SUBLANES
    n = range(len(items))
    fwd = [it[7] for it in items]
    v = [it[2] for it in items]
    q = [_silu(it[0]) * (HGRN_KEY_DIM ** -0.5) for it in items]
    f = [it[3] + it[4] * jax.nn.sigmoid(it[1]) for it in items]
    lf = [jnp.log(f[j]) for j in n]
    k = [1.0 - f[j] for j in n]

    hi = [lf[j].astype(BF16) for j in n]
    r1 = [lf[j] - hi[j].astype(F32) for j in n]
    mid = [r1[j].astype(BF16) for j in n]
    lo = [(r1[j] - mid[j].astype(F32)).astype(BF16) for j in n]
    tri = [items[j][5][...] for j in n]
    b = [_dot(tri[j], hi[j]) + (_dot(tri[j], mid[j]) + _dot(tri[j], lo[j])) for j in n]
    total = [jnp.sum(lf[j], axis=0, keepdims=True) for j in n]

    sub_io = lax.broadcasted_iota(I32, (1, SUBLANES, HGRN_KEY_DIM), 1)
    row_io = lax.broadcasted_iota(I32, (c, HGRN_KEY_DIM), 0)
    a = [items[j][6][0] * _dot_nt(q[j].astype(BF16), k[j].astype(BF16)) for j in n]
    for li, m in enumerate(HGRN_LEVELS, start=1):
        second = (row_io & m) != 0
        qk = [jnp.where(second if fwd[j] else jnp.logical_not(second), q[j], k[j]) for j in n]
        fac = []
        for j in n:
            first = m - 1 if fwd[j] else m
            if 2 * m <= SUBLANES:
                b3 = b[j].reshape(nsub, SUBLANES, HGRN_KEY_DIM)
                bnd = None
                for pair in reversed(range(SUBLANES // (2 * m))):
                    r = 2 * pair * m + first
                    piece = jnp.broadcast_to(b3[:, r:r + 1, :], b3.shape)
                    bnd = piece if bnd is None else jnp.where(sub_io < 2 * m * (pair + 1), piece, bnd)
                bnd = bnd.reshape(c, HGRN_KEY_DIM)
            else:
                pieces = [jnp.broadcast_to(b[j][2 * pair * m + first:2 * pair * m + first + 1, :],
                                           (2 * m, HGRN_KEY_DIM)) for pair in range(c // (2 * m))]
                bnd = pieces[0] if len(pieces) == 1 else jnp.concatenate(pieces, axis=0)
            fac.append(jnp.exp(-jnp.abs(b[j] - bnd)))
        x = [(qk[j] * fac[j]).astype(BF16) for j in n]
        prod = [_dot_nt(x[j], x[j]) for j in n]
        a = [jnp.where(items[j][6][li] > 0.5, prod[j], a[j]) for j in n]
    o = [_dot(a[j].astype(BF16), v[j].astype(BF16)) for j in n]
    return o, q, k, b, total


def _hgrn_carry(o, q, k, v, b, total, st):
    o = o + _dot_nt((q * jnp.exp(b)).astype(BF16), st.astype(BF16))
    kc = (k * jnp.exp(total - b)).astype(BF16)
    return o, st * jnp.exp(total) + _dot(v.T.astype(BF16), kc)


def _hgrn_kernel(*refs, n_chunks, has_s0, emit_state):
    (qb_ref, ff_ref, fb_ref, ib_ref, gb_ref, lbp_ref, gw_ref,
     ef_ref, mf_ref, eb_ref, mb_ref) = refs[:11]
    rest = list(refs[11:])
    s0_ref = rest.pop(0) if has_s0 else None
    r_ref = rest.pop(0)
    sout_ref = rest.pop(0) if emit_state else None
    of_scr, ob_scr, stf_scr, stb_scr = rest
    c = HGRN_CHUNK
    gw = gw_ref[...]

    for d, st_scr in enumerate((stf_scr, stb_scr)):
        if has_s0:
            st_scr[...] = s0_ref[d].T
        else:
            st_scr[...] = jnp.zeros((HGRN_VAL_DIM, HGRN_KEY_DIM), F32)

    group = min(HGRN_GROUP, n_chunks)

    def sweep(i, carry):
        sls, items = [], []
        for u in range(group):
            sls.append(pl.ds(pl.multiple_of((i * group + u) * c, c), c))
            items.append((qb_ref[sls[-1], :], ff_ref[sls[-1], :], ib_ref[sls[-1], :], lbp_ref[0:1, :],
                          lbp_ref[1:2, :], ef_ref, mf_ref, True))
        for u in range(group):
            sls.append(pl.ds(pl.multiple_of((n_chunks - 1 - (i * group + u)) * c, c), c))
            items.append((qb_ref[sls[-1], :], fb_ref[sls[-1], :], ib_ref[sls[-1], :], lbp_ref[2:3, :],
                          lbp_ref[3:4, :], eb_ref, mb_ref, False))
        o, q, k, b, total = _hgrn_intra(items)
        for d, (st_scr, o_scr) in enumerate(((stf_scr, of_scr), (stb_scr, ob_scr))):
            st = st_scr[...]
            for u in range(group):
                j = d * group + u
                o_j, st = _hgrn_carry(o[j], q[j], k[j], items[j][2], b[j], total[j], st)
                o_scr[sls[j], :] = o_j
            st_scr[...] = st
        return carry

    lax.fori_loop(0, n_chunks // group, sweep, 0)
    if emit_state:
        sout_ref[0] = stf_scr[...].T
        sout_ref[1] = stb_scr[...].T

    def readout(ci, carry):
        sl = pl.ds(pl.multiple_of(ci * c, c), c)
        tot = of_scr[sl, :] + ob_scr[sl, :]
        ms = jnp.mean(tot * tot, axis=-1, keepdims=True)
        r_ref[sl, :] = (tot * lax.rsqrt(ms + NORM_EPS) * gw) * _silu(gb_ref[sl, :])
        return carry

    lax.fori_loop(0, n_chunks, readout, 0, unroll=2)


def _hgrn(proj, lbp, gw, consts, n_b, t_len, row_block0, n_rows, s0=None, s0_layer=0, emit_state=False):
    n_chunks = t_len // HGRN_CHUNK
    e_f, m_f, e_b, m_b = consts

    def col(off):
        return pl.BlockSpec((t_len, LANE), lambda b, h: (row_block0 + b, COL_HGRN + off * HGRN_HEADS + h))

    def whole(a):
        nd = a.ndim
        return pl.BlockSpec(a.shape, lambda b, h: (0,) * nd)

    in_specs = [col(0), col(1), col(2), col(3), col(4),
                pl.BlockSpec((None, 4, LANE), lambda b, h: (h, 0, 0)),
                pl.BlockSpec((1, HGRN_VAL_DIM), lambda b, h: (0, 0)),
                whole(e_f), whole(m_f), whole(e_b), whole(m_b)]
    args = [proj, proj, proj, proj, proj, lbp, gw.reshape(1, HGRN_VAL_DIM), e_f, m_f, e_b, m_b]
    if s0 is not None:
        in_specs.append(pl.BlockSpec((None, None, 2, None, HGRN_KEY_DIM, HGRN_VAL_DIM),
                                     lambda b, h: (b, s0_layer, 0, h, 0, 0)))
        args.append(s0)
    out_shape = [jax.ShapeDtypeStruct((n_rows, HGRN_WIDTH), F32)]
    out_specs = [pl.BlockSpec((t_len, LANE), lambda b, h: (b, h))]
    if emit_state:
        out_shape.append(jax.ShapeDtypeStruct((n_b, 2, HGRN_HEADS, HGRN_KEY_DIM, HGRN_VAL_DIM), F32))
        out_specs.append(pl.BlockSpec((None, 2, None, HGRN_KEY_DIM, HGRN_VAL_DIM),
                                      lambda b, h: (b, 0, h, 0, 0)))
    return pl.pallas_call(
        functools.partial(_hgrn_kernel, n_chunks=n_chunks, has_s0=s0 is not None, emit_state=emit_state),
        out_shape=out_shape,
        grid=(n_b, HGRN_HEADS),
        in_specs=in_specs,
        out_specs=out_specs,
        scratch_shapes=[pltpu.VMEM((t_len, HGRN_VAL_DIM), F32), pltpu.VMEM((t_len, HGRN_VAL_DIM), F32),
                        pltpu.VMEM((HGRN_VAL_DIM, HGRN_KEY_DIM), F32), pltpu.VMEM((HGRN_VAL_DIM, HGRN_KEY_DIM), F32)],
        compiler_params=_params("parallel", "parallel"),
        name="hgrn_lat" if s0 is not None else "hgrn_ctx",
    )(*args)


def _outproj_kernel(*refs, n_parts, n_x, rows):
    pair_refs = refs[:2 * n_parts]
    w_refs = refs[2 * n_parts:3 * n_parts]
    x_refs = refs[3 * n_parts:3 * n_parts + n_x]
    mod_ref, nw_ref, wr_ref, br_ref, tri_ref, o_ref, xn_ref, meta_ref, cnt_ref, base_scr = refs[3 * n_parts + n_x:]
    acc = None
    for p in range(n_parts):
        a = rows.stream_tile(pair_refs[2 * p:2 * p + 2]).astype(BF16)
        d = _dot(a, w_refs[p][...])
        acc = d if acc is None else acc + d
    x_new = rows.stream_tile(x_refs) + mod_ref[2:3, :] * acc
    o_ref[...] = x_new
    _route_tile(x_new, mod_ref, nw_ref, wr_ref, br_ref, tri_ref, xn_ref, meta_ref, cnt_ref, base_scr)


def _outproj_route(rows, pairs, weights, x, mods, nw2, wr, br, tri):
    in_specs, args = [], []
    for pair in pairs:
        specs, arrs = rows.stream_specs(pair)
        in_specs += specs
        args += arrs
    in_specs += [pl.BlockSpec(w.shape, lambda i: (0, 0)) for w in weights]
    x_specs, x_args = rows.stream_specs(x)
    in_specs += x_specs + [
        rows.mod_spec(),
        pl.BlockSpec((1, D_MODEL), lambda i: (0, 0)),
        pl.BlockSpec((D_MODEL, LANE), lambda i: (0, 0)),
        pl.BlockSpec((1, LANE), lambda i: (0, 0)),
        pl.BlockSpec((TM, TM), lambda i: (0, 0)),
    ]
    return pl.pallas_call(
        functools.partial(_outproj_kernel, n_parts=len(pairs), n_x=len(x_args), rows=rows),
        out_shape=[jax.ShapeDtypeStruct((rows.n, D_MODEL), F32),
                   jax.ShapeDtypeStruct((rows.n, D_MODEL), F32),
                   jax.ShapeDtypeStruct((rows.n, LANE), F32),
                   jax.ShapeDtypeStruct((8, LANE), F32)],
        grid=(rows.n_tiles,),
        in_specs=in_specs,
        out_specs=[pl.BlockSpec((TM, D_MODEL), lambda i: (i, 0)),
                   pl.BlockSpec((TM, D_MODEL), lambda i: (i, 0)),
                   pl.BlockSpec((TM, LANE), lambda i: (i, 0)),
                   pl.BlockSpec((8, LANE), lambda i: (0, 0))],
        scratch_shapes=[pltpu.VMEM((8, LANE), F32)],
        compiler_params=_params("arbitrary"),
        name="outproj_route",
    )(*args, *weights, *x_args, mods, nw2.reshape(1, D_MODEL), wr, br, tri)


def _mla_proj_kernel(x_ref, mod_ref, nw_ref, wd_ref, qnw_ref, kvnw_ref, wuq_ref, wuqp_ref, cos_ref, sin_ref,
                     q_ref, ckv_ref, krb_ref):
    nf = MLA_ROPE_DIM // 4
    h = _normmod(x_ref[...], nw_ref[...], mod_ref[0:1, :], mod_ref[1:2, :]).astype(BF16)
    t1 = _dot(h, wd_ref[...])
    qd = t1[:, :MLA_Q_RANK]
    kvd = t1[:, MLA_Q_RANK:MLA_Q_RANK + MLA_KV_RANK]
    cos = cos_ref[...]
    sin = sin_ref[...]
    qn = qd * lax.rsqrt(jnp.mean(qd * qd, axis=-1, keepdims=True) + NORM_EPS) * qnw_ref[...]
    ckv_ref[...] = kvd * lax.rsqrt(jnp.mean(kvd * kvd, axis=-1, keepdims=True) + NORM_EPS) * kvnw_ref[...]
    krb_ref[...] = _rope(t1[:, MLA_Q_RANK + MLA_KV_RANK:], cos, sin, nf)
    qb = qn.astype(BF16)
    scale = MLA_QK_DIM ** -0.5
    cos2 = jnp.concatenate([cos, cos], axis=1)
    sin2 = jnp.concatenate([sin, sin], axis=1)
    for hp in range(MLA_HEADS // 2):
        cols = slice(hp * 2 * LANE, (hp + 1) * 2 * LANE)
        qh = _dot(qb, wuq_ref[:, cols])
        qp = _dot(qb, wuqp_ref[:, cols])
        q_ref[:, cols] = ((qh * cos2 + qp * sin2) * scale).astype(BF16)


def _mla_proj(rows, x, mods, nw, wd, qnw, kvnw, wuq, wuq_partner, cos, sin):
    return pl.pallas_call(
        _mla_proj_kernel,
        out_shape=[jax.ShapeDtypeStruct((rows.n, MLA_HEADS * LANE), BF16),
                   jax.ShapeDtypeStruct((rows.n, MLA_KV_RANK), F32),
                   jax.ShapeDtypeStruct((rows.n, LANE), F32)],
        grid=(rows.n_tiles,),
        in_specs=[
            pl.BlockSpec((TM, D_MODEL), lambda i: (i, 0)),
            rows.mod_spec(),
            pl.BlockSpec((1, D_MODEL), lambda i: (0, 0)),
            pl.BlockSpec((D_MODEL, MLA_DOWN_WIDTH), lambda i: (0, 0)),
            pl.BlockSpec((1, MLA_Q_RANK), lambda i: (0, 0)),
            pl.BlockSpec((1, MLA_KV_RANK), lambda i: (0, 0)),
            pl.BlockSpec((MLA_Q_RANK, MLA_HEADS * LANE), lambda i: (0, 0)),
            pl.BlockSpec((MLA_Q_RANK, MLA_HEADS * LANE), lambda i: (0, 0)),
            pl.BlockSpec((TM, LANE), lambda i: (rows.pos_block(i), 0)),
            pl.BlockSpec((TM, LANE), lambda i: (rows.pos_block(i), 0)),
        ],
        out_specs=[pl.BlockSpec((TM, MLA_HEADS * LANE), lambda i: (i, 0)),
                   pl.BlockSpec((TM, MLA_KV_RANK), lambda i: (i, 0)),
                   pl.BlockSpec((TM, LANE), lambda i: (i, 0))],
        compiler_params=_params("parallel"),
        name="mla_proj",
    )(x, mods, nw.reshape(1, D_MODEL), wd, qnw.reshape(1, MLA_Q_RANK), kvnw.reshape(1, MLA_KV_RANK), wuq, wuq_partner, cos, sin)


MLA_TQ = 256
MLA_HPS = 4


def _mla_attn_kernel(q_ref, ckv_ref, krb_ref, wkv_ref, o_ref, k_scr, v_scr, *, t_len, hps):
    tq = min(MLA_TQ, t_len)
    ckv = ckv_ref[...].astype(BF16)
    krb = krb_ref[...]
    for hh in range(hps):
        kvh = _dot(ckv, wkv_ref[hh])
        k_scr[hh] = (kvh[:, :LANE] + krb).astype(BF16)
        v_scr[hh] = kvh[:, LANE:].astype(BF16)

    def body(ti, carry):
        sl = pl.ds(pl.multiple_of(ti * tq, tq), tq)
        heads = range(hps)
        s = [_dot_nt(q_ref[sl, hh * LANE:(hh + 1) * LANE], k_scr[hh]) for hh in heads]
        m = [jnp.max(s[hh], axis=-1, keepdims=True) for hh in heads]
        p = [jnp.exp(s[hh] - m[hh]) for hh in heads]
        den = [jnp.sum(p[hh], axis=-1, keepdims=True) for hh in heads]
        o = [_dot(p[hh].astype(BF16), v_scr[hh]) / den[hh] for hh in heads]
        for pair in range(hps // 2):
            o_ref[sl, pair * LANE:(pair + 1) * LANE] = o[2 * pair] + o[2 * pair + 1]
        return carry

    lax.fori_loop(0, t_len // tq, body, 0)


def _mla_attn(q, ckv_all, krb_all, wkv, n_b, t_len, q_row_block0, n_rows):
    s_len = ckv_all.shape[1]
    hps = MLA_HEADS if t_len <= MLA_TQ else MLA_HPS
    return pl.pallas_call(
        functools.partial(_mla_attn_kernel, t_len=t_len, hps=hps),
        out_shape=jax.ShapeDtypeStruct((n_rows, MLA_HEADS * MLA_V_DIM), F32),
        grid=(n_b, MLA_HEADS // hps),
        in_specs=[
            pl.BlockSpec((t_len, hps * LANE), lambda b, hp: (q_row_block0 + b, hp)),
            pl.BlockSpec((None, s_len, MLA_KV_RANK), lambda b, hp: (b, 0, 0)),
            pl.BlockSpec((None, s_len, LANE), lambda b, hp: (b, 0, 0)),
            pl.BlockSpec((hps, MLA_KV_RANK, 2 * LANE), lambda b, hp: (hp, 0, 0)),
        ],
        out_specs=pl.BlockSpec((t_len, hps // 2 * LANE), lambda b, hp: (b, hp)),
        scratch_shapes=[pltpu.VMEM((hps, s_len, LANE), BF16), pltpu.VMEM((hps, s_len, LANE), BF16)],
        compiler_params=_params("parallel", "arbitrary"),
        name="mla_attn",
    )(q, ckv_all, krb_all, wkv)


META_E1, META_E2, META_W1, META_W2, META_R1, META_R2 = range(6)


def _route_tile(x, mod_ref, nw_ref, wr_ref, br_ref, tri_ref, xn_ref, meta_ref, cnt_ref, base_scr):
    @pl.when(pl.program_id(0) == 0)
    def _():
        base_scr[...] = jnp.zeros(base_scr.shape, F32)

    xn = _normmod(x, nw_ref[...], mod_ref[3:4, :], mod_ref[4:5, :])
    xn_ref[...] = xn
    logits = _dot_f32ish(xn, wr_ref[...]) + br_ref[...]
    lane = lax.broadcasted_iota(I32, logits.shape, 1).astype(F32)
    far = float(LANE)

    def first_argmax(vals, vmax):
        return jnp.min(jnp.where(vals == vmax, lane, far), axis=-1, keepdims=True)

    gl = jnp.where(lane < MOE_GROUPS, logits, NEG_BIG)
    gmax = jnp.max(gl, axis=-1, keepdims=True)
    g_w = 1.0 / jnp.sum(jnp.exp(gl - gmax), axis=-1, keepdims=True)
    g_idx = first_argmax(gl, gmax)
    e_lo = MOE_GROUPS + MOE_EPG * g_idx
    el = jnp.where((lane >= e_lo) & (lane < e_lo + MOE_EPG), logits, NEG_BIG)
    m1 = jnp.max(el, axis=-1, keepdims=True)
    i1 = first_argmax(el, m1)
    el2 = jnp.where(lane == i1, NEG_BIG, el)
    m2 = jnp.max(el2, axis=-1, keepdims=True)
    i2 = first_argmax(el2, m2)
    esum = jnp.sum(jnp.exp(el - m1), axis=-1, keepdims=True)
    p1 = 1.0 / esum
    p2 = jnp.exp(m2 - m1) / esum
    w1 = g_w * (p1 / (p1 + p2))
    w2 = g_w * (p2 / (p1 + p2))
    e1 = i1 - MOE_GROUPS
    e2 = i2 - MOE_GROUPS

    oh1 = lane == e1
    oh2 = lane == e2
    oh = jnp.where(oh1 | oh2, 1.0, 0.0)
    before = _dot(tri_ref[...], oh.astype(BF16)) + base_scr[0:1, :]
    r1 = jnp.sum(jnp.where(oh1, before, 0.0), axis=-1, keepdims=True)
    r2 = jnp.sum(jnp.where(oh2, before, 0.0), axis=-1, keepdims=True)
    base_scr[...] = base_scr[...] + jnp.sum(oh, axis=0, keepdims=True)
    cnt_ref[...] = base_scr[...]

    meta = jnp.zeros(logits.shape, F32)
    for slot, val in ((META_E1, e1), (META_E2, e2), (META_W1, w1), (META_W2, w2), (META_R1, r1), (META_R2, r2)):
        meta = jnp.where(lane == slot, val, meta)
    meta_ref[...] = meta


def _row_copy(src, src_row, dst, dst_row, sem):
    return pltpu.make_async_copy(src.at[pl.ds(src_row, 1), :], dst.at[pl.ds(dst_row, 1), :], sem)


def _dispatch_kernel(fill_ref, pos_ref, xn_ref, xs_hbm, zero_scr, sem, fill_sem, *, n_tiles):
    @pl.when(pl.program_id(0) == 0)
    def _():
        zero_scr[...] = jnp.zeros(zero_scr.shape, F32)

        def fill_copy(t):
            return pltpu.make_async_copy(zero_scr, xs_hbm.at[pl.ds(pl.multiple_of(t * TME, TME), TME), :], fill_sem)

        def fill_start(t, carry):
            @pl.when(fill_ref[t] != 0)
            def _():
                fill_copy(t).start()
            return carry

        def fill_wait(t, carry):
            @pl.when(fill_ref[t] != 0)
            def _():
                fill_copy(t).wait()
            return carry

        lax.fori_loop(0, n_tiles, fill_start, 0)
        lax.fori_loop(0, n_tiles, fill_wait, 0)

    def start(r, carry):
        for k in range(2):
            _row_copy(xn_ref, r, xs_hbm, pos_ref[0, 2 * r + k], sem).start(priority=k)
        return carry

    lax.fori_loop(0, TM, start, 0, unroll=DMA_UNROLL)
    for _ in range(2):
        pltpu.make_async_copy(xn_ref, xs_hbm.at[pl.ds(0, TM), :], sem).wait()


def _dispatch(rows, tile_fill, pos, xn, n_tiles):
    return pl.pallas_call(
        functools.partial(_dispatch_kernel, n_tiles=n_tiles),
        out_shape=jax.ShapeDtypeStruct((n_tiles * TME, D_MODEL), F32),
        grid_spec=pltpu.PrefetchScalarGridSpec(
            num_scalar_prefetch=1,
            grid=(rows.n_tiles,),
            in_specs=[
                pl.BlockSpec((None, 1, 2 * TM), lambda i, fill: (i, 0, 0), memory_space=pltpu.SMEM),
                pl.BlockSpec((TM, D_MODEL), lambda i, fill: (i, 0)),
            ],
            out_specs=pl.BlockSpec(memory_space=pl.ANY),
            scratch_shapes=[pltpu.VMEM((TME, D_MODEL), F32), pltpu.SemaphoreType.DMA(()),
                            pltpu.SemaphoreType.DMA(())],
        ),
        compiler_params=_params("arbitrary"),
        name="moe_dispatch",
    )(tile_fill, pos, xn)


def _ffn_kernel(te_ref, nv_ref, x_ref, wg_ref, wu_ref, wd_ref, y_ref, wg_b, wu_b, wd_b):
    t = pl.program_id(0)
    valid = t < nv_ref[0]
    new_expert = (t == 0) | (te_ref[t] != te_ref[jnp.maximum(t - 1, 0)])

    @pl.when(valid & new_expert)
    def _():
        wg_b[...] = wg_ref[...].astype(BF16)
        wu_b[...] = wu_ref[...].astype(BF16)
        wd_b[...] = wd_ref[...].astype(BF16)

    @pl.when(valid)
    def _():
        x = x_ref[...].astype(BF16)
        a = _silu(_dot(x, wg_b[...])) * _dot(x, wu_b[...])
        y_ref[...] = _dot(a.astype(BF16), wd_b[...])

    @pl.when(jnp.logical_not(valid))
    def _():
        y_ref[...] = jnp.zeros(y_ref.shape, F32)


def _ffn(tile_expert, n_valid, xs, w_gate, w_up, w_down, layer, n_tiles):
    def xmap(t, te, nv):
        return (jnp.minimum(t, nv[0] - 1), 0)

    def wmap(t, te, nv):
        return (layer, te[t], 0, 0)

    return pl.pallas_call(
        _ffn_kernel,
        out_shape=jax.ShapeDtypeStruct((n_tiles * TME, D_MODEL), F32),
        grid_spec=pltpu.PrefetchScalarGridSpec(
            num_scalar_prefetch=2,
            grid=(n_tiles,),
            in_specs=[
                pl.BlockSpec((TME, D_MODEL), xmap),
                pl.BlockSpec((None, None, D_MODEL, MOE_HIDDEN), wmap),
                pl.BlockSpec((None, None, D_MODEL, MOE_HIDDEN), wmap),
                pl.BlockSpec((None, None, MOE_HIDDEN, D_MODEL), wmap),
            ],
            out_specs=pl.BlockSpec((TME, D_MODEL), lambda t, te, nv: (t, 0)),
            scratch_shapes=[pltpu.VMEM((D_MODEL, MOE_HIDDEN), BF16),
                            pltpu.VMEM((D_MODEL, MOE_HIDDEN), BF16),
                            pltpu.VMEM((MOE_HIDDEN, D_MODEL), BF16)],
        ),
        compiler_params=_params("arbitrary"),
        name="moe_ffn",
    )(tile_expert, n_valid, xs, w_gate, w_up, w_down)


def _combine_kernel(pos_ref, x_ref, meta_ref, mod_ref, fnw_ref, ys_hbm, *rest, final, ctx_tiles):
    if final:
        o_ctx_ref, o_lat_ref, buf0, buf1, sem = rest
    else:
        o_ref, buf0, buf1, sem = rest
    bufs = (buf0, buf1)

    def start(r, carry):
        for k in range(2):
            _row_copy(ys_hbm, pos_ref[0, 2 * r + k], bufs[k], r, sem).start(priority=k)
        return carry

    lax.fori_loop(0, TM, start, 0, unroll=DMA_UNROLL)
    for k in range(2):
        pltpu.make_async_copy(ys_hbm.at[pl.ds(0, TM), :], bufs[k], sem).wait()
    meta = meta_ref[...]
    y = meta[:, META_W1:META_W1 + 1] * buf0[...] + meta[:, META_W2:META_W2 + 1] * buf1[...]
    xo = x_ref[...] + mod_ref[5:6, :] * y
    if not final:
        o_ref[...] = xo
        return
    xo = xo * lax.rsqrt(jnp.mean(xo * xo, axis=-1, keepdims=True) + NORM_EPS) * fnw_ref[...]
    is_ctx = pl.program_id(0) < ctx_tiles

    @pl.when(is_ctx)
    def _():
        o_ctx_ref[...] = xo

    @pl.when(jnp.logical_not(is_ctx))
    def _():
        o_lat_ref[...] = xo


def _combine(rows, pos, x, meta, mods, fnw, ys, final):
    ct = rows.ctx_tiles
    if final:
        out_shape = [jax.ShapeDtypeStruct((rows.nc, D_MODEL), F32), jax.ShapeDtypeStruct((rows.nl, D_MODEL), F32)]
        out_specs = [pl.BlockSpec((TM, D_MODEL), lambda i: (jnp.minimum(i, ct - 1), 0)),
                     pl.BlockSpec((TM, D_MODEL), lambda i: (jnp.maximum(i - ct, 0), 0))]
    else:
        out_shape = jax.ShapeDtypeStruct((rows.n, D_MODEL), F32)
        out_specs = pl.BlockSpec((TM, D_MODEL), lambda i: (i, 0))
    return pl.pallas_call(
        functools.partial(_combine_kernel, final=final, ctx_tiles=ct),
        out_shape=out_shape,
        grid=(rows.n_tiles,),
        in_specs=[
            pl.BlockSpec((None, 1, 2 * TM), lambda i: (i, 0, 0), memory_space=pltpu.SMEM),
            pl.BlockSpec((TM, D_MODEL), lambda i: (i, 0)),
            pl.BlockSpec((TM, LANE), lambda i: (i, 0)),
            rows.mod_spec(),
            pl.BlockSpec((1, D_MODEL), lambda i: (0, 0)),
            pl.BlockSpec(memory_space=pl.ANY),
        ],
        out_specs=out_specs,
        scratch_shapes=[pltpu.VMEM((TM, D_MODEL), F32), pltpu.VMEM((TM, D_MODEL), F32),
                        pltpu.SemaphoreType.DMA(())],
        compiler_params=_params("arbitrary"),
        name="moe_combine",
    )(pos, x, meta, mods, fnw.reshape(1, D_MODEL), ys)


def _moe(rows, x, xn, meta, cnt, mods, w_gate, w_up, w_down, layer, fnw, final):
    n_assign = 2 * rows.n
    n_tiles = n_assign // TME + MOE_EXPERTS

    counts = cnt[0, :MOE_EXPERTS].astype(I32)
    padded = ((counts + TME - 1) // TME) * TME
    ends = jnp.cumsum(padded)
    starts = ends - padded
    experts = jnp.arange(MOE_EXPERTS, dtype=I32)
    e = meta[:, META_E1:META_E2 + 1].astype(I32)
    rank = meta[:, META_R1:META_R2 + 1].astype(I32)
    start_of = jnp.sum(jnp.where(e[..., None] == experts, starts, 0), axis=-1)
    pos = (start_of + rank).reshape(rows.n_tiles, 1, 2 * TM)
    n_valid = ends[-1] // TME
    tile_first = jnp.arange(n_tiles, dtype=I32) * TME
    tile_start = jnp.minimum(tile_first, ends[-1] - TME)
    tile_expert = jnp.sum((ends[None, :] <= tile_start[:, None]).astype(I32), axis=1)
    tile_expert = jnp.minimum(tile_expert, MOE_EXPERTS - 1)
    tile_oh = tile_expert[:, None] == experts
    tile_rows = jnp.sum(jnp.where(tile_oh, counts + starts, 0), axis=1) - tile_start
    tile_fill = ((tile_first >= ends[-1]) | (tile_rows < TME)).astype(I32)

    xs = _dispatch(rows, tile_fill, pos, xn, n_tiles)
    ys = _ffn(tile_expert, n_valid.reshape(1).astype(I32), xs, w_gate, w_up, w_down, layer, n_tiles)
    return _combine(rows, pos, x, meta, mods, fnw, ys, final)


def _lower_bound_params(p):
    pr = jax.nn.softmax(p.astype(F32), axis=0)
    lb = jnp.cumsum(pr, axis=0) - pr[0:1]
    lb = jnp.clip(lb, 0.0, 1.0 - 1e-6)
    return jnp.maximum(lb, LOG_TINY), 1.0 - lb


def kernel(x_prompt, x_sample, c, cache_swa_k, cache_swa_v, state_hgrn, cache_mla_ckv, cache_mla_krope, c_ctx, mod_w, mod_b, norm1_w, norm2_w, final_norm_w, even_w_in, even_w_out, swa_sink, hgrn_lb_fwd, hgrn_lb_bwd, hgrn_gnorm_w, mla_w_dq, mla_qnorm_w, mla_w_uq, mla_w_dkv, mla_kvnorm_w, mla_w_ukv, mla_w_o, moe_router_group_w, moe_router_group_b, moe_router_expert_w, moe_router_expert_b, moe_w_gate, moe_w_up, moe_w_down):
    nc_b, nc_t, _ = x_prompt.shape
    nl_b, nl_t, _ = x_sample.shape
    rows = _Rows(nc_b, nc_t, nl_b, nl_t)
    past = cache_swa_k.shape[2]

    x = (x_prompt.reshape(rows.nc, D_MODEL), x_sample.reshape(rows.nl, D_MODEL))
    mod_rows = 16
    cvec = jnp.concatenate([c_ctx[None, :], c, jnp.zeros((mod_rows - 1 - nl_b, D_MODEL), F32)], axis=0)
    mods_all = _modulation(cvec, mod_w, mod_b).reshape(DEPTH, mod_rows, 6, D_MODEL)

    hconsts = _hgrn_consts()
    la_f, l1_f = _lower_bound_params(hgrn_lb_fwd)
    la_b, l1_b = _lower_bound_params(hgrn_lb_bwd)
    lbp_all = jnp.stack([la_f, l1_f, la_b, l1_b], axis=1).reshape(N_EVEN, 4, HGRN_HEADS, LANE).transpose(0, 2, 1, 3)
    swa_cos, swa_sin = _rope_tables(nl_t, SWA_HEAD_DIM, 0, LANE, 0)
    mla_cos, mla_sin = _rope_tables(nl_t, MLA_ROPE_DIM, MLA_NOPE_DIM, MLA_QK_DIM, TM)
    tri =jnp.asarray(np.tril(np.ones((TM, TM), np.float32), -1), BF16)

    new_k, new_v, new_s, new_ckv, new_kr = [], [], [], [], []
    for l in range(DEPTH):
        j = l // 2
        mods = mods_all[l]
        wr = jnp.zeros((D_MODEL, LANE), F32)
        wr = wr.at[:, :MOE_GROUPS].set(moe_router_group_w[l])
        wr = wr.at[:, MOE_GROUPS:MOE_GROUPS + MOE_EXPERTS].set(
            moe_router_expert_w[l].transpose(1, 0, 2).reshape(D_MODEL, MOE_EXPERTS))
        br = jnp.zeros((1, LANE), F32)
        br = br.at[0, :MOE_GROUPS].set(moe_router_group_b[l])
        br = br.at[0, MOE_GROUPS:MOE_GROUPS + MOE_EXPERTS].set(moe_router_expert_b[l].reshape(MOE_EXPERTS))
        if l % 2 == 0:
            proj = _even_proj(rows, x, mods, norm1_w[l], even_w_in[j].astype(BF16))
            a_ctx = _swa_ctx(rows, proj, swa_sink[j])
            a_lat = _swa_lat(rows, proj, swa_sink[j],
                             cache_swa_k[:, j].reshape(nl_b, past, SWA_KV_WIDTH),
                             cache_swa_v[:, j].reshape(nl_b, past, SWA_KV_WIDTH), swa_cos, swa_sin)
            r_ctx, s_ctx = _hgrn(proj, lbp_all[j], hgrn_gnorm_w[j], hconsts, nc_b, nc_t, 0, rows.nc,
                                 emit_state=True)
            (r_lat,) = _hgrn(proj, lbp_all[j], hgrn_gnorm_w[j], hconsts, nl_b, nl_t, rows.nc // nl_t, rows.nl,
                             s0=state_hgrn, s0_layer=j)
            w_out = even_w_out[j].astype(BF16)
            x, xn, meta, cnt = _outproj_route(rows, [(a_ctx, a_lat), (r_ctx, r_lat)],
                                              [w_out[:SWA_WIDTH], w_out[SWA_WIDTH:]], x, mods, norm2_w[l], wr, br, tri)
            kv = proj[:rows.nc, SWA_WIDTH:SWA_WIDTH + 2 * SWA_KV_WIDTH]
            new_k.append(kv[:, :SWA_KV_WIDTH].reshape(nc_b, nc_t, SWA_KV_HEADS, SWA_HEAD_DIM))
            new_v.append(kv[:, SWA_KV_WIDTH:].reshape(nc_b, nc_t, SWA_KV_HEADS, SWA_HEAD_DIM))
            new_s.append(s_ctx)
        else:
            wd = jnp.zeros((D_MODEL, MLA_DOWN_WIDTH), F32)
            wd = wd.at[:, :MLA_Q_RANK].set(mla_w_dq[j])
            wd = wd.at[:, MLA_Q_RANK:MLA_Q_RANK + MLA_KV_RANK].set(mla_w_dkv[j][:, :MLA_KV_RANK])
            kr_lo = MLA_Q_RANK + MLA_KV_RANK + MLA_NOPE_DIM
            wd = wd.at[:, kr_lo:kr_lo + MLA_ROPE_DIM].set(mla_w_dkv[j][:, MLA_KV_RANK:])
            wuq = jnp.pad(mla_w_uq[j].reshape(MLA_Q_RANK, MLA_HEADS, MLA_QK_DIM),
                          ((0, 0), (0, 0), (0, LANE - MLA_QK_DIM))).reshape(MLA_Q_RANK, MLA_HEADS * LANE)
            wukv = mla_w_ukv[j].reshape(MLA_KV_RANK, MLA_HEADS, MLA_NOPE_DIM + MLA_V_DIM).transpose(1, 0, 2)
            wk = jnp.pad(wukv[..., :MLA_NOPE_DIM], ((0, 0), (0, 0), (0, LANE - MLA_NOPE_DIM)))
            wv_e = jnp.pad(wukv[..., MLA_NOPE_DIM:], ((0, 0), (0, 0), (0, LANE - MLA_V_DIM)))
            wv_o = jnp.pad(wukv[..., MLA_NOPE_DIM:], ((0, 0), (0, 0), (LANE - MLA_V_DIM, 0)))
            odd = (jnp.arange(MLA_HEADS) % 2 == 1)[:, None, None]
            wv = jnp.where(odd, wv_o, wv_e)
            wuq = wuq.astype(BF16)
            w3 = wuq.reshape(MLA_Q_RANK, MLA_HEADS, LANE)
            nf = MLA_ROPE_DIM // 4
            rot = w3[..., MLA_NOPE_DIM:MLA_QK_DIM].reshape(MLA_Q_RANK, MLA_HEADS, 2, 2, nf)[..., ::-1, :]
            wuq_partner = jnp.concatenate([w3[..., :MLA_NOPE_DIM], rot.reshape(MLA_Q_RANK, MLA_HEADS, MLA_ROPE_DIM),
                                           w3[..., MLA_QK_DIM:]], axis=-1).reshape(MLA_Q_RANK, MLA_HEADS * LANE)
            q, ckv, krb = _mla_proj(rows, x, mods, norm1_w[l], wd.astype(BF16), mla_qnorm_w[j], mla_kvnorm_w[j],
                                    wuq, wuq_partner, mla_cos, mla_sin)
            wkv = jnp.concatenate([wk, wv], axis=-1).astype(BF16)
            ckv_c = ckv[:rows.nc].reshape(nc_b, nc_t, MLA_KV_RANK)
            krb_c = krb[:rows.nc].reshape(nc_b, nc_t, LANE)
            o_ctx = _mla_attn(q, ckv_c, krb_c, wkv, nc_b, nc_t, 0, rows.nc)
            cache_kr = jnp.pad(cache_mla_krope[:, j], ((0, 0), (0, 0), (MLA_NOPE_DIM, LANE - MLA_QK_DIM)))
            ckv_l = jnp.concatenate([cache_mla_ckv[:, j], ckv[rows.nc:].reshape(nl_b, nl_t, MLA_KV_RANK)], axis=1)
            krb_l = jnp.concatenate([cache_kr, krb[rows.nc:].reshape(nl_b, nl_t, LANE)], axis=1)
            o_lat = _mla_attn(q, ckv_l, krb_l, wkv, nl_b, nl_t, rows.nc // nl_t, rows.nl)
            x, xn, meta, cnt = _outproj_route(rows, [(o_ctx, o_lat)], [mla_w_o[j].astype(BF16)], x, mods,
                                              norm2_w[l], wr, br, tri)
            new_ckv.append(ckv_c)
            new_kr.append(krb_c[..., MLA_NOPE_DIM:MLA_QK_DIM])

        x = _moe(rows, x, xn, meta, cnt, mods, moe_w_gate, moe_w_up, moe_w_down, l,
                 final_norm_w, final=(l == DEPTH - 1))

    y_prompt = x[0].reshape(nc_b, nc_t, D_MODEL)
    y_sample = x[1].reshape(nl_b, nl_t, D_MODEL)
    return (y_prompt, y_sample, jnp.stack(new_k, axis=1), jnp.stack(new_v, axis=1), jnp.stack(new_s, axis=1),
            jnp.stack(new_ckv, axis=1), jnp.stack(new_kr, axis=1))
```

```python
import functools
import math

import numpy as np
import jax
import jax.numpy as jnp
from jax import lax
from jax.experimental import pallas as pl
from jax.experimental.pallas import tpu as pltpu

F32, BF16, I32 = jnp.float32, jnp.bfloat16, jnp.int32

D_MODEL = 1024
DEPTH = 4
GRID_W = 64
ROPE_THETA = 10000.0
NORM_EPS = 1e-6
NEG_BIG = -1e30
LOG_TINY = 1e-30
N_EVEN = (DEPTH + 1) // 2
N_ODD = DEPTH // 2

SWA_HEADS = 8
SWA_KV_HEADS = 2
SWA_GROUP = SWA_HEADS // SWA_KV_HEADS
SWA_HEAD_DIM = 64
SWA_WIDTH = SWA_HEADS * SWA_HEAD_DIM
SWA_KV_WIDTH = SWA_KV_HEADS * SWA_HEAD_DIM
SWA_WINDOW = 128
SWA_BLOCK = 128

HGRN_HEADS = 4
HGRN_KEY_DIM = 128
HGRN_VAL_DIM = 128
HGRN_WIDTH = HGRN_HEADS * HGRN_KEY_DIM
HGRN_CHUNK = 128
HGRN_LEVELS = (1, 2, 4, 8, 16, 32, 64)
HGRN_GROUP = 4

EVEN_IN_WIDTH = SWA_WIDTH + 2 * SWA_KV_WIDTH + 5 * HGRN_WIDTH
LANE = 128
COL_Q, COL_K, COL_V = 0, SWA_WIDTH // LANE, (SWA_WIDTH + SWA_KV_WIDTH) // LANE
COL_HGRN = (SWA_WIDTH + 2 * SWA_KV_WIDTH) // LANE

MLA_HEADS = 16
MLA_Q_RANK = 384
MLA_KV_RANK = 256
MLA_NOPE_DIM = 64
MLA_ROPE_DIM = 32
MLA_V_DIM = 64
MLA_QK_DIM = MLA_NOPE_DIM + MLA_ROPE_DIM
MLA_DOWN_WIDTH = MLA_Q_RANK + MLA_KV_RANK + LANE

MOE_GROUPS = 4
MOE_EPG = 8
MOE_EXPERTS = MOE_GROUPS * MOE_EPG
MOE_HIDDEN = 256

TM = 512
TME = 512
DMA_UNROLL = 8
SUBLANES = 8
VMEM_LIMIT = 48 * 1024 * 1024


def _params(*sem):
    return pltpu.CompilerParams(dimension_semantics=sem, vmem_limit_bytes=VMEM_LIMIT)


def _dot(a, b):
    return jnp.dot(a, b, preferred_element_type=F32)


def _dot_nt(a, b):
    return lax.dot_general(a, b, (((1,), (1,)), ((), ())), preferred_element_type=F32)


def _split2(a):
    hi = a.astype(BF16)
    return hi, (a - hi.astype(F32)).astype(BF16)


def _dot_f32ish(a, b):
    ah, al = _split2(a)
    bh, bl = _split2(b)
    return _dot(ah, bh) + (_dot(ah, bl) + _dot(al, bh))


def _silu(x):
    return x * jax.nn.sigmoid(x)


def _normmod(x, nw, shift, scale):
    ms = jnp.mean(x * x, axis=-1, keepdims=True)
    return (x * lax.rsqrt(ms + NORM_EPS) * nw) * (1.0 + scale) + shift


def _mod_kernel(c_ref, w_ref, b_ref, o_ref):
    o_ref[...] = _dot_f32ish(_silu(c_ref[...]), w_ref[...]) + b_ref[...]


def _modulation(cvec, mod_w, mod_b):
    rows = cvec.shape[0]
    nb = 6 * D_MODEL // 1024
    return pl.pallas_call(
        _mod_kernel,
        out_shape=jax.ShapeDtypeStruct((DEPTH, rows, 6 * D_MODEL), F32),
        grid=(DEPTH, nb),
        in_specs=[
            pl.BlockSpec((rows, D_MODEL), lambda l, n: (0, 0)),
            pl.BlockSpec((None, D_MODEL, 1024), lambda l, n: (l, 0, n)),
            pl.BlockSpec((None, 1, 1024), lambda l, n: (l, 0, n)),
        ],
        out_specs=pl.BlockSpec((None, rows, 1024), lambda l, n: (l, 0, n)),
        compiler_params=_params("parallel", "parallel"),
        name="modulation",
    )(cvec, mod_w, mod_b.reshape(DEPTH, 1, 6 * D_MODEL))


class _Rows:
    def __init__(self, nc_b, nc_t, nl_b, nl_t):
        self.nc_b, self.nc_t, self.nl_b, self.nl_t = nc_b, nc_t, nl_b, nl_t
        self.nc = nc_b * nc_t
        self.nl = nl_b * nl_t
        self.n = self.nc + self.nl
        assert self.nc % TM == 0 and nl_t % TM == 0 and self.nc % nl_t == 0
        self.ctx_tiles = self.nc // TM
        self.tiles_per_lat = nl_t // TM
        self.n_tiles = self.n // TM

    def mod_row(self, i):
        return jnp.where(i < self.ctx_tiles, 0, 1 + (i - self.ctx_tiles) // self.tiles_per_lat)

    def mod_spec(self):
        return pl.BlockSpec((None, 6, D_MODEL), lambda i: (self.mod_row(i), 0, 0))

    def stream_specs(self, x):
        ct = self.ctx_tiles
        if not isinstance(x, tuple):
            return [pl.BlockSpec((TM, x.shape[1]), lambda i: (i, 0))], [x]
        x_ctx, x_lat = x
        return ([pl.BlockSpec((TM, x_ctx.shape[1]), lambda i: (jnp.minimum(i, ct - 1), 0)),
                 pl.BlockSpec((TM, x_lat.shape[1]), lambda i: (jnp.maximum(i - ct, 0), 0))], [x_ctx, x_lat])

    def stream_tile(self, refs):
        if len(refs) == 1:
            return refs[0][...]
        return jnp.where(pl.program_id(0) < self.ctx_tiles, refs[0][...], refs[1][...])

    def pos_block(self, i):
        return jnp.where(i < self.ctx_tiles, 0, 1 + (i - self.ctx_tiles) % self.tiles_per_lat)


def _even_proj_kernel(*refs, rows):
    mod_ref, nw_ref, w_ref, o_ref = refs[-4:]
    x = rows.stream_tile(refs[:-4])
    h = _normmod(x, nw_ref[...], mod_ref[0:1, :], mod_ref[1:2, :]).astype(BF16)
    step = 256
    for c in range(EVEN_IN_WIDTH // step):
        o_ref[:, c * step:(c + 1) * step] = _dot(h, w_ref[:, c * step:(c + 1) * step])


def _even_proj(rows, x, mods, nw, w_in_bf16):
    x_specs, x_args = rows.stream_specs(x)
    return pl.pallas_call(
        functools.partial(_even_proj_kernel, rows=rows),
        out_shape=jax.ShapeDtypeStruct((rows.n, EVEN_IN_WIDTH), F32),
        grid=(rows.n_tiles,),
        in_specs=x_specs + [
            rows.mod_spec(),
            pl.BlockSpec((1, D_MODEL), lambda i: (0, 0)),
            pl.BlockSpec((D_MODEL, EVEN_IN_WIDTH), lambda i: (0, 0)),
        ],
        out_specs=pl.BlockSpec((TM, EVEN_IN_WIDTH), lambda i: (i, 0)),
        compiler_params=_params("arbitrary"),
        name="even_proj",
    )(*x_args, mods, nw.reshape(1, D_MODEL), w_in_bf16)


def _rope_tables(t_len, rot_dim, lane_lo, lane_hi, lead_rows):
    half = rot_dim // 2
    nf = half // 2
    lane = np.arange(LANE)
    d = (lane - lane_lo) % rot_dim
    active = (lane >= lane_lo) & (lane < lane_hi)
    use_col = d >= half
    fidx = d % nf
    first = (d % half) < nf
    pos = jnp.arange(t_len)
    row = (pos // GRID_W).astype(F32)
    col = (pos % GRID_W).astype(F32)
    inv = jnp.exp(-math.log(ROPE_THETA) * jnp.arange(nf, dtype=F32) / nf)
    p = jnp.where(jnp.asarray(use_col)[None, :], col[:, None], row[:, None])
    ang = p * inv[jnp.asarray(fidx)][None, :]
    act = jnp.asarray(active)[None, :]
    cos = jnp.where(act, jnp.cos(ang), 1.0)
    sin = jnp.where(act, jnp.sin(ang), 0.0)
    sin = jnp.where(jnp.asarray(first)[None, :], -sin, sin)
    if lead_rows:
        cos = jnp.concatenate([jnp.ones((lead_rows, LANE), F32), cos], axis=0)
        sin = jnp.concatenate([jnp.zeros((lead_rows, LANE), F32), sin], axis=0)
    return cos, sin


def _rope(x, cos, sin, nf):
    lane = lax.broadcasted_iota(I32, x.shape, 1)
    up = pltpu.roll(x, LANE - nf, axis=1)
    dn = pltpu.roll(x, nf, axis=1)
    partner = jnp.where((lane & nf) == 0, up, dn)
    return x * cos + partner * sin


def _swa_ctx_kernel(sink_ref, q_ref, k_ref, v_ref, o_ref):
    scale = SWA_HEAD_DIM ** -0.5
    heads = range(SWA_HEADS)
    kk = [k_ref[:, kv * SWA_HEAD_DIM:(kv + 1) * SWA_HEAD_DIM].astype(BF16) for kv in range(SWA_KV_HEADS)]
    vv = [v_ref[:, kv * SWA_HEAD_DIM:(kv + 1) * SWA_HEAD_DIM].astype(BF16) for kv in range(SWA_KV_HEADS)]
    q = [q_ref[:, h * SWA_HEAD_DIM:(h + 1) * SWA_HEAD_DIM].astype(BF16) for h in heads]
    s = [_dot_nt(q[h], kk[h // SWA_GROUP]) * scale for h in heads]
    m = [jnp.maximum(jnp.max(s[h], axis=-1, keepdims=True), sink_ref[h]) for h in heads]
    p = [jnp.exp(s[h] - m[h]) for h in heads]
    den = [jnp.sum(p[h], axis=-1, keepdims=True) + jnp.exp(sink_ref[h] - m[h]) for h in heads]
    o = [_dot(p[h].astype(BF16), vv[h // SWA_GROUP]) / den[h] for h in heads]
    for h in heads:
        o_ref[:, h * SWA_HEAD_DIM:(h + 1) * SWA_HEAD_DIM] = o[h]


def _swa_ctx(rows, proj, sink):
    t = rows.nc_t
    return pl.pallas_call(
        _swa_ctx_kernel,
        out_shape=jax.ShapeDtypeStruct((rows.nc, SWA_WIDTH), F32),
        grid=(rows.nc_b,),
        in_specs=[
            pl.BlockSpec(memory_space=pltpu.SMEM),
            pl.BlockSpec((t, SWA_WIDTH), lambda b: (b, COL_Q)),
            pl.BlockSpec((t, LANE), lambda b: (b, COL_K)),
            pl.BlockSpec((t, LANE), lambda b: (b, COL_V)),
        ],
        out_specs=pl.BlockSpec((t, SWA_WIDTH), lambda b: (b, 0)),
        compiler_params=_params("parallel"),
        name="swa_ctx",
    )(sink, proj, proj, proj)


def _lane_fold(x, op):
    out = x[:, :LANE]
    for i in range(1, x.shape[1] // LANE):
        out = op(out, x[:, i * LANE:(i + 1) * LANE])
    return out


def _swa_lat_kernel(sink_ref, q_ref, k_ref, v_ref, kc_ref, vc_ref, cos_ref, sin_ref, o_ref,
                    kl_scr, vl_scr, kc_scr, vc_scr, *, n_blocks):
    scale = SWA_HEAD_DIM ** -0.5
    nf = SWA_HEAD_DIM // 4
    n = pl.program_id(1)

    @pl.when(n == 0)
    def _():
        kr = _rope(k_ref[...], cos_ref[...], sin_ref[...], nf)
        for kv in range(SWA_KV_HEADS):
            cols = slice(kv * SWA_HEAD_DIM, (kv + 1) * SWA_HEAD_DIM)
            kl_scr[kv] = kr[:, cols].astype(BF16)
            vl_scr[kv] = v_ref[:, cols].astype(BF16)
            kc_scr[kv] = kc_ref[:, cols].astype(BF16)
            vc_scr[kv] = vc_ref[:, cols].astype(BF16)

    q0 = pl.multiple_of(n * SWA_BLOCK, SWA_BLOCK)
    cq = cos_ref[pl.ds(q0, SWA_BLOCK), :]
    sq = sin_ref[pl.ds(q0, SWA_BLOCK), :]
    qs = [_rope(q_ref[:, g * LANE:(g + 1) * LANE], cq, sq, nf) for g in range(SWA_WIDTH // LANE)]

    qi = lax.broadcasted_iota(I32, (SWA_BLOCK, SWA_BLOCK), 0)
    r = lax.broadcasted_iota(I32, (SWA_BLOCK, SWA_BLOCK), 1)
    band, mb = [], []
    for off in (-1, 0, 1):
        blk = n + off
        valid = (blk >= 0) & (blk < n_blocks)
        st = pl.multiple_of(jnp.clip(blk, 0, n_blocks - 1) * SWA_BLOCK, SWA_BLOCK)
        band.append(pl.ds(st, SWA_BLOCK))
        rel = qi - r - off * SWA_BLOCK
        mb.append(jnp.where((jnp.abs(rel) <= SWA_WINDOW) & valid, 1.0, 0.0))
    mask = jnp.concatenate(mb, axis=1) > 0.5

    kl = [jnp.concatenate([kl_scr[kv, sl, :] for sl in band], axis=0) for kv in range(SWA_KV_HEADS)]
    vl = [jnp.concatenate([vl_scr[kv, sl, :] for sl in band], axis=0) for kv in range(SWA_KV_HEADS)]
    heads = range(SWA_HEADS)
    lanes = [slice((h * SWA_HEAD_DIM) % LANE, (h * SWA_HEAD_DIM) % LANE + SWA_HEAD_DIM) for h in heads]
    q = [qs[h * SWA_HEAD_DIM // LANE][:, lanes[h]].astype(BF16) for h in heads]
    s_ctx = [_dot_nt(q[h], kc_scr[h // SWA_GROUP]) * scale for h in heads]
    s_loc = [jnp.where(mask, _dot_nt(q[h], kl[h // SWA_GROUP]) * scale, NEG_BIG) for h in heads]
    m = [jnp.maximum(jnp.max(jnp.maximum(_lane_fold(s_ctx[h], jnp.maximum), _lane_fold(s_loc[h], jnp.maximum)),
                             axis=-1, keepdims=True), sink_ref[h]) for h in heads]
    p_ctx = [jnp.exp(s_ctx[h] - m[h]) for h in heads]
    p_loc = [jnp.exp(s_loc[h] - m[h]) for h in heads]
    den = [jnp.sum(_lane_fold(p_ctx[h], jnp.add) + _lane_fold(p_loc[h], jnp.add), axis=-1, keepdims=True)
           + jnp.exp(sink_ref[h] - m[h]) for h in heads]
    o = [_dot(p_ctx[h].astype(BF16), vc_scr[h // SWA_GROUP]) + _dot(p_loc[h].astype(BF16), vl[h // SWA_GROUP])
         for h in heads]
    for h in heads:
        o_ref[:, h * SWA_HEAD_DIM:(h + 1) * SWA_HEAD_DIM] = o[h] / den[h]


def _swa_lat(rows, proj, sink, k_ctx, v_ctx, cos, sin):
    t = rows.nl_t
    n_blocks = t // SWA_BLOCK
    q_base = rows.nc // SWA_BLOCK
    kv_base = rows.nc // t
    s_ctx = k_ctx.shape[1]
    return pl.pallas_call(
        functools.partial(_swa_lat_kernel, n_blocks=n_blocks),
        out_shape=jax.ShapeDtypeStruct((rows.nl, SWA_WIDTH), F32),
        grid=(rows.nl_b, n_blocks),
        in_specs=[
            pl.BlockSpec(memory_space=pltpu.SMEM),
            pl.BlockSpec((SWA_BLOCK, SWA_WIDTH), lambda b, n: (q_base + b * n_blocks + n, COL_Q)),
            pl.BlockSpec((t, LANE), lambda b, n: (kv_base + b, COL_K)),
            pl.BlockSpec((t, LANE), lambda b, n: (kv_base + b, COL_V)),
            pl.BlockSpec((None, s_ctx, LANE), lambda b, n: (b, 0, 0)),
            pl.BlockSpec((None, s_ctx, LANE), lambda b, n: (b, 0, 0)),
            pl.BlockSpec((t, LANE), lambda b, n: (0, 0)),
            pl.BlockSpec((t, LANE), lambda b, n: (0, 0)),
        ],
        out_specs=pl.BlockSpec((SWA_BLOCK, SWA_WIDTH), lambda b, n: (b * n_blocks + n, 0)),
        scratch_shapes=[pltpu.VMEM((SWA_KV_HEADS, t, SWA_HEAD_DIM), BF16),
                        pltpu.VMEM((SWA_KV_HEADS, t, SWA_HEAD_DIM), BF16),
                        pltpu.VMEM((SWA_KV_HEADS, s_ctx, SWA_HEAD_DIM), BF16),
                        pltpu.VMEM((SWA_KV_HEADS, s_ctx, SWA_HEAD_DIM), BF16)],
        compiler_params=_params("parallel", "arbitrary"),
        name="swa_lat",
    )(sink, proj, proj, proj, k_ctx, v_ctx, cos, sin)


def _hgrn_consts():
    c = HGRN_CHUNK
    t = np.arange(c)[:, None]
    u = np.arange(c)[None, :]
    tri_f = (u <= t).astype(np.float32)
    masks = [t == u]
    for m in HGRN_LEVELS:
        right = ((t // m) % 2) == 1
        masks.append(right & ((u // m) == (t // m) - 1))
    m_f = np.stack(masks).astype(np.float32)
    return (jnp.asarray(tri_f, BF16), jnp.asarray(m_f, F32),
            jnp.asarray(tri_f[::-1, ::-1], BF16), jnp.asarray(m_f[:, ::-1, ::-1], F32))


def _hgrn_intra(items):
    c = HGRN_CHUNK
    nsub = c // SUBLANES
    n = range(len(items))
    fwd = [it[7] for it in items]
    v = [it[2] for it in items]
    q = [_silu(it[0]) * (HGRN_KEY_DIM ** -0.5) for it in items]
    f = [it[3] + it[4] * jax.nn.sigmoid(it[1]) for it in items]
    lf = [jnp.log(f[j]) for j in n]
    k = [1.0 - f[j] for j in n]

    hi = [lf[j].astype(BF16) for j in n]
    r1 = [lf[j] - hi[j].astype(F32) for j in n]
    mid = [r1[j].astype(BF16) for j in n]
    lo = [(r1[j] - mid[j].astype(F32)).astype(BF16) for j in n]
    tri = [items[j][5][...] for j in n]
    b = [_dot(tri[j], hi[j]) + (_dot(tri[j], mid[j]) + _dot(tri[j], lo[j])) for j in n]
    total = [jnp.sum(lf[j], axis=0, keepdims=True) for j in n]

    sub_io = lax.broadcasted_iota(I32, (1, SUBLANES, HGRN_KEY_DIM), 1)
    row_io = lax.broadcasted_iota(I32, (c, HGRN_KEY_DIM), 0)
    a = [items[j][6][0] * _dot_nt(q[j].astype(BF16), k[j].astype(BF16)) for j in n]
    for li, m in enumerate(HGRN_LEVELS, start=1):
        second = (row_io & m) != 0
        qk = [jnp.where(second if fwd[j] else jnp.logical_not(second), q[j], k[j]) for j in n]
        fac = []
        for j in n:
            first = m - 1 if fwd[j] else m
            if 2 * m <= SUBLANES:
                b3 = b[j].reshape(nsub, SUBLANES, HGRN_KEY_DIM)
                bnd = None
                for pair in reversed(range(SUBLANES // (2 * m))):
                    r = 2 * pair * m + first
                    piece = jnp.broadcast_to(b3[:, r:r + 1, :], b3.shape)
                    bnd = piece if bnd is None else jnp.where(sub_io < 2 * m * (pair + 1), piece, bnd)
                bnd = bnd.reshape(c, HGRN_KEY_DIM)
            else:
                pieces = [jnp.broadcast_to(b[j][2 * pair * m + first:2 * pair * m + first + 1, :],
                                           (2 * m, HGRN_KEY_DIM)) for pair in range(c // (2 * m))]
                bnd = pieces[0] if len(pieces) == 1 else jnp.concatenate(pieces, axis=0)
            fac.append(jnp.exp(-jnp.abs(b[j] - bnd)))
        x = [(qk[j] * fac[j]).astype(BF16) for j in n]
        prod = [_dot_nt(x[j], x[j]) for j in n]
        a = [jnp.where(items[j][6][li] > 0.5, prod[j], a[j]) for j in n]
    o = [_dot(a[j].astype(BF16), v[j].astype(BF16)) for j in n]
    return o, q, k, b, total


def _hgrn_carry(o, q, k, v, b, total, st):
    o = o + _dot_nt((q * jnp.exp(b)).astype(BF16), st.astype(BF16))
    kc = (k * jnp.exp(total - b)).astype(BF16)
    return o, st * jnp.exp(total) + _dot(v.T.astype(BF16), kc)


def _hgrn_kernel(*refs, n_chunks, has_s0, emit_state):
    (qb_ref, ff_ref, fb_ref, ib_ref, gb_ref, lbp_ref, gw_ref,
     ef_ref, mf_ref, eb_ref, mb_ref) = refs[:11]
    rest = list(refs[11:])
    s0_ref = rest.pop(0) if has_s0 else None
    r_ref = rest.pop(0)
    sout_ref = rest.pop(0) if emit_state else None
    of_scr, ob_scr, stf_scr, stb_scr = rest
    c = HGRN_CHUNK
    gw = gw_ref[...]

    for d, st_scr in enumerate((stf_scr, stb_scr)):
        if has_s0:
            st_scr[...] = s0_ref[d].T
        else:
            st_scr[...] = jnp.zeros((HGRN_VAL_DIM, HGRN_KEY_DIM), F32)

    group = min(HGRN_GROUP, n_chunks)

    def sweep(i, carry):
        sls, items = [], []
        for u in range(group):
            sls.append(pl.ds(pl.multiple_of((i * group + u) * c, c), c))
            items.append((qb_ref[sls[-1], :], ff_ref[sls[-1], :], ib_ref[sls[-1], :], lbp_ref[0:1, :],
                          lbp_ref[1:2, :], ef_ref, mf_ref, True))
        for u in range(group):
            sls.append(pl.ds(pl.multiple_of((n_chunks - 1 - (i * group + u)) * c, c), c))
            items.append((qb_ref[sls[-1], :], fb_ref[sls[-1], :], ib_ref[sls[-1], :], lbp_ref[2:3, :],
                          lbp_ref[3:4, :], eb_ref, mb_ref, False))
        o, q, k, b, total = _hgrn_intra(items)
        for d, (st_scr, o_scr) in enumerate(((stf_scr, of_scr), (stb_scr, ob_scr))):
            st = st_scr[...]
            for u in range(group):
                j = d * group + u
                o_j, st = _hgrn_carry(o[j], q[j], k[j], items[j][2], b[j], total[j], st)
                o_scr[sls[j], :] = o_j
            st_scr[...] = st
        return carry

    lax.fori_loop(0, n_chunks // group, sweep, 0)
    if emit_state:
        sout_ref[0] = stf_scr[...].T
        sout_ref[1] = stb_scr[...].T

    def readout(ci, carry):
        sl = pl.ds(pl.multiple_of(ci * c, c), c)
        tot = of_scr[sl, :] + ob_scr[sl, :]
        ms = jnp.mean(tot * tot, axis=-1, keepdims=True)
        r_ref[sl, :] = (tot * lax.rsqrt(ms + NORM_EPS) * gw) * _silu(gb_ref[sl, :])
        return carry

    lax.fori_loop(0, n_chunks, readout, 0, unroll=2)


def _hgrn(proj, lbp, gw, consts, n_b, t_len, row_block0, n_rows, s0=None, s0_layer=0, emit_state=False):
    n_chunks = t_len // HGRN_CHUNK
    e_f, m_f, e_b, m_b = consts

    def col(off):
        return pl.BlockSpec((t_len, LANE), lambda b, h: (row_block0 + b, COL_HGRN + off * HGRN_HEADS + h))

    def whole(a):
        nd = a.ndim
        return pl.BlockSpec(a.shape, lambda b, h: (0,) * nd)

    in_specs = [col(0), col(1), col(2), col(3), col(4),
                pl.BlockSpec((None, 4, LANE), lambda b, h: (h, 0, 0)),
                pl.BlockSpec((1, HGRN_VAL_DIM), lambda b, h: (0, 0)),
                whole(e_f), whole(m_f), whole(e_b), whole(m_b)]
    args = [proj, proj, proj, proj, proj, lbp, gw.reshape(1, HGRN_VAL_DIM), e_f, m_f, e_b, m_b]
    if s0 is not None:
        in_specs.append(pl.BlockSpec((None, None, 2, None, HGRN_KEY_DIM, HGRN_VAL_DIM),
                                     lambda b, h: (b, s0_layer, 0, h, 0, 0)))
        args.append(s0)
    out_shape = [jax.ShapeDtypeStruct((n_rows, HGRN_WIDTH), F32)]
    out_specs = [pl.BlockSpec((t_len, LANE), lambda b, h: (b, h))]
    if emit_state:
        out_shape.append(jax.ShapeDtypeStruct((n_b, 2, HGRN_HEADS, HGRN_KEY_DIM, HGRN_VAL_DIM), F32))
        out_specs.append(pl.BlockSpec((None, 2, None, HGRN_KEY_DIM, HGRN_VAL_DIM),
                                      lambda b, h: (b, 0, h, 0, 0)))
    return pl.pallas_call(
        functools.partial(_hgrn_kernel, n_chunks=n_chunks, has_s0=s0 is not None, emit_state=emit_state),
        out_shape=out_shape,
        grid=(n_b, HGRN_HEADS),
        in_specs=in_specs,
        out_specs=out_specs,
        scratch_shapes=[pltpu.VMEM((t_len, HGRN_VAL_DIM), F32), pltpu.VMEM((t_len, HGRN_VAL_DIM), F32),
                        pltpu.VMEM((HGRN_VAL_DIM, HGRN_KEY_DIM), F32), pltpu.VMEM((HGRN_VAL_DIM, HGRN_KEY_DIM), F32)],
        compiler_params=_params("parallel", "parallel"),
        name="hgrn_lat" if s0 is not None else "hgrn_ctx",
    )(*args)


def _outproj_kernel(*refs, n_parts, n_x, rows):
    pair_refs = refs[:2 * n_parts]
    w_refs = refs[2 * n_parts:3 * n_parts]
    x_refs = refs[3 * n_parts:3 * n_parts + n_x]
    mod_ref, nw_ref, wr_ref, br_ref, tri_ref, o_ref, xn_ref, meta_ref, cnt_ref, base_scr = refs[3 * n_parts + n_x:]
    acc = None
    for p in range(n_parts):
        a = rows.stream_tile(pair_refs[2 * p:2 * p + 2]).astype(BF16)
        d = _dot(a, w_refs[p][...])
        acc = d if acc is None else acc + d
    x_new = rows.stream_tile(x_refs) + mod_ref[2:3, :] * acc
    o_ref[...] = x_new
    _route_tile(x_new, mod_ref, nw_ref, wr_ref, br_ref, tri_ref, xn_ref, meta_ref, cnt_ref, base_scr)


def _outproj_route(rows, pairs, weights, x, mods, nw2, wr, br, tri):
    in_specs, args = [], []
    for pair in pairs:
        specs, arrs = rows.stream_specs(pair)
        in_specs += specs
        args += arrs
    in_specs += [pl.BlockSpec(w.shape, lambda i: (0, 0)) for w in weights]
    x_specs, x_args = rows.stream_specs(x)
    in_specs += x_specs + [
        rows.mod_spec(),
        pl.BlockSpec((1, D_MODEL), lambda i: (0, 0)),
        pl.BlockSpec((D_MODEL, LANE), lambda i: (0, 0)),
        pl.BlockSpec((1, LANE), lambda i: (0, 0)),
        pl.BlockSpec((TM, TM), lambda i: (0, 0)),
    ]
    return pl.pallas_call(
        functools.partial(_outproj_kernel, n_parts=len(pairs), n_x=len(x_args), rows=rows),
        out_shape=[jax.ShapeDtypeStruct((rows.n, D_MODEL), F32),
                   jax.ShapeDtypeStruct((rows.n, D_MODEL), F32),
                   jax.ShapeDtypeStruct((rows.n, LANE), F32),
                   jax.ShapeDtypeStruct((8, LANE), F32)],
        grid=(rows.n_tiles,),
        in_specs=in_specs,
        out_specs=[pl.BlockSpec((TM, D_MODEL), lambda i: (i, 0)),
                   pl.BlockSpec((TM, D_MODEL), lambda i: (i, 0)),
                   pl.BlockSpec((TM, LANE), lambda i: (i, 0)),
                   pl.BlockSpec((8, LANE), lambda i: (0, 0))],
        scratch_shapes=[pltpu.VMEM((8, LANE), F32)],
        compiler_params=_params("arbitrary"),
        name="outproj_route",
    )(*args, *weights, *x_args, mods, nw2.reshape(1, D_MODEL), wr, br, tri)


def _mla_proj_kernel(x_ref, mod_ref, nw_ref, wd_ref, qnw_ref, kvnw_ref, wuq_ref, wuqp_ref, cos_ref, sin_ref,
                     q_ref, ckv_ref, krb_ref):
    nf = MLA_ROPE_DIM // 4
    h = _normmod(x_ref[...], nw_ref[...], mod_ref[0:1, :], mod_ref[1:2, :]).astype(BF16)
    t1 = _dot(h, wd_ref[...])
    qd = t1[:, :MLA_Q_RANK]
    kvd = t1[:, MLA_Q_RANK:MLA_Q_RANK + MLA_KV_RANK]
    cos = cos_ref[...]
    sin = sin_ref[...]
    qn = qd * lax.rsqrt(jnp.mean(qd * qd, axis=-1, keepdims=True) + NORM_EPS) * qnw_ref[...]
    ckv_ref[...] = kvd * lax.rsqrt(jnp.mean(kvd * kvd, axis=-1, keepdims=True) + NORM_EPS) * kvnw_ref[...]
    krb_ref[...] = _rope(t1[:, MLA_Q_RANK + MLA_KV_RANK:], cos, sin, nf)
    qb = qn.astype(BF16)
    scale = MLA_QK_DIM ** -0.5
    cos2 = jnp.concatenate([cos, cos], axis=1)
    sin2 = jnp.concatenate([sin, sin], axis=1)
    for hp in range(MLA_HEADS // 2):
        cols = slice(hp * 2 * LANE, (hp + 1) * 2 * LANE)
        qh = _dot(qb, wuq_ref[:, cols])
        qp = _dot(qb, wuqp_ref[:, cols])
        q_ref[:, cols] = ((qh * cos2 + qp * sin2) * scale).astype(BF16)


def _mla_proj(rows, x, mods, nw, wd, qnw, kvnw, wuq, wuq_partner, cos, sin):
    return pl.pallas_call(
        _mla_proj_kernel,
        out_shape=[jax.ShapeDtypeStruct((rows.n, MLA_HEADS * LANE), BF16),
                   jax.ShapeDtypeStruct((rows.n, MLA_KV_RANK), F32),
                   jax.ShapeDtypeStruct((rows.n, LANE), F32)],
        grid=(rows.n_tiles,),
        in_specs=[
            pl.BlockSpec((TM, D_MODEL), lambda i: (i, 0)),
            rows.mod_spec(),
            pl.BlockSpec((1, D_MODEL), lambda i: (0, 0)),
            pl.BlockSpec((D_MODEL, MLA_DOWN_WIDTH), lambda i: (0, 0)),
            pl.BlockSpec((1, MLA_Q_RANK), lambda i: (0, 0)),
            pl.BlockSpec((1, MLA_KV_RANK), lambda i: (0, 0)),
            pl.BlockSpec((MLA_Q_RANK, MLA_HEADS * LANE), lambda i: (0, 0)),
            pl.BlockSpec((MLA_Q_RANK, MLA_HEADS * LANE), lambda i: (0, 0)),
            pl.BlockSpec((TM, LANE), lambda i: (rows.pos_block(i), 0)),
            pl.BlockSpec((TM, LANE), lambda i: (rows.pos_block(i), 0)),
        ],
        out_specs=[pl.BlockSpec((TM, MLA_HEADS * LANE), lambda i: (i, 0)),
                   pl.BlockSpec((TM, MLA_KV_RANK), lambda i: (i, 0)),
                   pl.BlockSpec((TM, LANE), lambda i: (i, 0))],
        compiler_params=_params("parallel"),
        name="mla_proj",
    )(x, mods, nw.reshape(1, D_MODEL), wd, qnw.reshape(1, MLA_Q_RANK), kvnw.reshape(1, MLA_KV_RANK), wuq, wuq_partner, cos, sin)


MLA_TQ = 256
MLA_HPS = 4


def _mla_attn_kernel(q_ref, ckv_ref, krb_ref, wkv_ref, o_ref, k_scr, v_scr, *, t_len, hps):
    tq = min(MLA_TQ, t_len)
    ckv = ckv_ref[...].astype(BF16)
    krb = krb_ref[...]
    for hh in range(hps):
        kvh = _dot(ckv, wkv_ref[hh])
        k_scr[hh] = (kvh[:, :LANE] + krb).astype(BF16)
        v_scr[hh] = kvh[:, LANE:].astype(BF16)

    def body(ti, carry):
        sl = pl.ds(pl.multiple_of(ti * tq, tq), tq)
        heads = range(hps)
        s = [_dot_nt(q_ref[sl, hh * LANE:(hh + 1) * LANE], k_scr[hh]) for hh in heads]
        m = [jnp.max(s[hh], axis=-1, keepdims=True) for hh in heads]
        p = [jnp.exp(s[hh] - m[hh]) for hh in heads]
        den = [jnp.sum(p[hh], axis=-1, keepdims=True) for hh in heads]
        o = [_dot(p[hh].astype(BF16), v_scr[hh]) / den[hh] for hh in heads]
        for pair in range(hps // 2):
            o_ref[sl, pair * LANE:(pair + 1) * LANE] = o[2 * pair] + o[2 * pair + 1]
        return carry

    lax.fori_loop(0, t_len // tq, body, 0)


def _mla_attn(q, ckv_all, krb_all, wkv, n_b, t_len, q_row_block0, n_rows):
    s_len = ckv_all.shape[1]
    hps = MLA_HEADS if t_len <= MLA_TQ else MLA_HPS
    return pl.pallas_call(
        functools.partial(_mla_attn_kernel, t_len=t_len, hps=hps),
        out_shape=jax.ShapeDtypeStruct((n_rows, MLA_HEADS * MLA_V_DIM), F32),
        grid=(n_b, MLA_HEADS // hps),
        in_specs=[
            pl.BlockSpec((t_len, hps * LANE), lambda b, hp: (q_row_block0 + b, hp)),
            pl.BlockSpec((None, s_len, MLA_KV_RANK), lambda b, hp: (b, 0, 0)),
            pl.BlockSpec((None, s_len, LANE), lambda b, hp: (b, 0, 0)),
            pl.BlockSpec((hps, MLA_KV_RANK, 2 * LANE), lambda b, hp: (hp, 0, 0)),
        ],
        out_specs=pl.BlockSpec((t_len, hps // 2 * LANE), lambda b, hp: (b, hp)),
        scratch_shapes=[pltpu.VMEM((hps, s_len, LANE), BF16), pltpu.VMEM((hps, s_len, LANE), BF16)],
        compiler_params=_params("parallel", "arbitrary"),
        name="mla_attn",
    )(q, ckv_all, krb_all, wkv)


META_E1, META_E2, META_W1, META_W2, META_R1, META_R2 = range(6)


def _route_tile(x, mod_ref, nw_ref, wr_ref, br_ref, tri_ref, xn_ref, meta_ref, cnt_ref, base_scr):
    @pl.when(pl.program_id(0) == 0)
    def _():
        base_scr[...] = jnp.zeros(base_scr.shape, F32)

    xn = _normmod(x, nw_ref[...], mod_ref[3:4, :], mod_ref[4:5, :])
    xn_ref[...] = xn
    logits = _dot_f32ish(xn, wr_ref[...]) + br_ref[...]
    lane = lax.broadcasted_iota(I32, logits.shape, 1).astype(F32)
    far = float(LANE)

    def first_argmax(vals, vmax):
        return jnp.min(jnp.where(vals == vmax, lane, far), axis=-1, keepdims=True)

    gl = jnp.where(lane < MOE_GROUPS, logits, NEG_BIG)
    gmax = jnp.max(gl, axis=-1, keepdims=True)
    g_w = 1.0 / jnp.sum(jnp.exp(gl - gmax), axis=-1, keepdims=True)
    g_idx = first_argmax(gl, gmax)
    e_lo = MOE_GROUPS + MOE_EPG * g_idx
    el = jnp.where((lane >= e_lo) & (lane < e_lo + MOE_EPG), logits, NEG_BIG)
    m1 = jnp.max(el, axis=-1, keepdims=True)
    i1 = first_argmax(el, m1)
    el2 = jnp.where(lane == i1, NEG_BIG, el)
    m2 = jnp.max(el2, axis=-1, keepdims=True)
    i2 = first_argmax(el2, m2)
    esum = jnp.sum(jnp.exp(el - m1), axis=-1, keepdims=True)
    p1 = 1.0 / esum
    p2 = jnp.exp(m2 - m1) / esum
    w1 = g_w * (p1 / (p1 + p2))
    w2 = g_w * (p2 / (p1 + p2))
    e1 = i1 - MOE_GROUPS
    e2 = i2 - MOE_GROUPS

    oh1 = lane == e1
    oh2 = lane == e2
    oh = jnp.where(oh1 | oh2, 1.0, 0.0)
    before = _dot(tri_ref[...], oh.astype(BF16)) + base_scr[0:1, :]
    r1 = jnp.sum(jnp.where(oh1, before, 0.0), axis=-1, keepdims=True)
    r2 = jnp.sum(jnp.where(oh2, before, 0.0), axis=-1, keepdims=True)
    base_scr[...] = base_scr[...] + jnp.sum(oh, axis=0, keepdims=True)
    cnt_ref[...] = base_scr[...]

    meta = jnp.zeros(logits.shape, F32)
    for slot, val in ((META_E1, e1), (META_E2, e2), (META_W1, w1), (META_W2, w2), (META_R1, r1), (META_R2, r2)):
        meta = jnp.where(lane == slot, val, meta)
    meta_ref[...] = meta


def _row_copy(src, src_row, dst, dst_row, sem):
    return pltpu.make_async_copy(src.at[pl.ds(src_row, 1), :], dst.at[pl.ds(dst_row, 1), :], sem)


def _dispatch_kernel(fill_ref, pos_ref, xn_ref, xs_hbm, zero_scr, sem, fill_sem, *, n_tiles):
    @pl.when(pl.program_id(0) == 0)
    def _():
        zero_scr[...] = jnp.zeros(zero_scr.shape, F32)

        def fill_copy(t):
            return pltpu.make_async_copy(zero_scr, xs_hbm.at[pl.ds(pl.multiple_of(t * TME, TME), TME), :], fill_sem)

        def fill_start(t, carry):
            @pl.when(fill_ref[t] != 0)
            def _():
                fill_copy(t).start()
            return carry

        def fill_wait(t, carry):
            @pl.when(fill_ref[t] != 0)
            def _():
                fill_copy(t).wait()
            return carry

        lax.fori_loop(0, n_tiles, fill_start, 0)
        lax.fori_loop(0, n_tiles, fill_wait, 0)

    def start(r, carry):
        for k in range(2):
            _row_copy(xn_ref, r, xs_hbm, pos_ref[0, 2 * r + k], sem).start(priority=k)
        return carry

    lax.fori_loop(0, TM, start, 0, unroll=DMA_UNROLL)
    for _ in range(2):
        pltpu.make_async_copy(xn_ref, xs_hbm.at[pl.ds(0, TM), :], sem).wait()


def _dispatch(rows, tile_fill, pos, xn, n_tiles):
    return pl.pallas_call(
        functools.partial(_dispatch_kernel, n_tiles=n_tiles),
        out_shape=jax.ShapeDtypeStruct((n_tiles * TME, D_MODEL), F32),
        grid_spec=pltpu.PrefetchScalarGridSpec(
            num_scalar_prefetch=1,
            grid=(rows.n_tiles,),
            in_specs=[
                pl.BlockSpec((None, 1, 2 * TM), lambda i, fill: (i, 0, 0), memory_space=pltpu.SMEM),
                pl.BlockSpec((TM, D_MODEL), lambda i, fill: (i, 0)),
            ],
            out_specs=pl.BlockSpec(memory_space=pl.ANY),
            scratch_shapes=[pltpu.VMEM((TME, D_MODEL), F32), pltpu.SemaphoreType.DMA(()),
                            pltpu.SemaphoreType.DMA(())],
        ),
        compiler_params=_params("arbitrary"),
        name="moe_dispatch",
    )(tile_fill, pos, xn)


def _ffn_kernel(te_ref, nv_ref, x_ref, wg_ref, wu_ref, wd_ref, y_ref, wg_b, wu_b, wd_b):
    t = pl.program_id(0)
    valid = t < nv_ref[0]
    new_expert = (t == 0) | (te_ref[t] != te_ref[jnp.maximum(t - 1, 0)])

    @pl.when(valid & new_expert)
    def _():
        wg_b[...] = wg_ref[...].astype(BF16)
        wu_b[...] = wu_ref[...].astype(BF16)
        wd_b[...] = wd_ref[...].astype(BF16)

    @pl.when(valid)
    def _():
        x = x_ref[...].astype(BF16)
        a = _silu(_dot(x, wg_b[...])) * _dot(x, wu_b[...])
        y_ref[...] = _dot(a.astype(BF16), wd_b[...])

    @pl.when(jnp.logical_not(valid))
    def _():
        y_ref[...] = jnp.zeros(y_ref.shape, F32)


def _ffn(tile_expert, n_valid, xs, w_gate, w_up, w_down, layer, n_tiles):
    def xmap(t, te, nv):
        return (jnp.minimum(t, nv[0] - 1), 0)

    def wmap(t, te, nv):
        return (layer, te[t], 0, 0)

    return pl.pallas_call(
        _ffn_kernel,
        out_shape=jax.ShapeDtypeStruct((n_tiles * TME, D_MODEL), F32),
        grid_spec=pltpu.PrefetchScalarGridSpec(
            num_scalar_prefetch=2,
            grid=(n_tiles,),
            in_specs=[
                pl.BlockSpec((TME, D_MODEL), xmap),
                pl.BlockSpec((None, None, D_MODEL, MOE_HIDDEN), wmap),
                pl.BlockSpec((None, None, D_MODEL, MOE_HIDDEN), wmap),
                pl.BlockSpec((None, None, MOE_HIDDEN, D_MODEL), wmap),
            ],
            out_specs=pl.BlockSpec((TME, D_MODEL), lambda t, te, nv: (t, 0)),
            scratch_shapes=[pltpu.VMEM((D_MODEL, MOE_HIDDEN), BF16),
                            pltpu.VMEM((D_MODEL, MOE_HIDDEN), BF16),
                            pltpu.VMEM((MOE_HIDDEN, D_MODEL), BF16)],
        ),
        compiler_params=_params("arbitrary"),
        name="moe_ffn",
    )(tile_expert, n_valid, xs, w_gate, w_up, w_down)


def _combine_kernel(pos_ref, pos_next_ref, x_ref, meta_ref, mod_ref, fnw_ref, ys_hbm, *rest, final, ctx_tiles,
                    n_tiles):
    if final:
        o_ctx_ref, o_lat_ref, bufs, sems = rest
    else:
        o_ref, bufs, sems = rest
    i = pl.program_id(0)
    slot = lax.rem(i, 2)

    def gather(p_ref, s):
        def start(r, carry):
            for k in range(2):
                _row_copy(ys_hbm, p_ref[0, 2 * r + k], bufs.at[s, k], r, sems.at[s]).start(priority=k)
            return carry

        lax.fori_loop(0, TM, start, 0, unroll=DMA_UNROLL)

    @pl.when(i == 0)
    def _():
        gather(pos_ref, 0)

    for s in range(2):
        @pl.when((i + 1 < n_tiles) & (slot == 1 - s))
        def _(s=s):
            gather(pos_next_ref, s)

    for k in range(2):
        pltpu.make_async_copy(ys_hbm.at[pl.ds(0, TM), :], bufs.at[slot, k], sems.at[slot]).wait()
    meta = meta_ref[...]
    y = meta[:, META_W1:META_W1 + 1] * bufs[slot, 0] + meta[:, META_W2:META_W2 + 1] * bufs[slot, 1]
    xo = x_ref[...] + mod_ref[5:6, :] * y
    if not final:
        o_ref[...] = xo
        return
    xo = xo * lax.rsqrt(jnp.mean(xo * xo, axis=-1, keepdims=True) + NORM_EPS) * fnw_ref[...]
    is_ctx = pl.program_id(0) < ctx_tiles

    @pl.when(is_ctx)
    def _():
        o_ctx_ref[...] = xo

    @pl.when(jnp.logical_not(is_ctx))
    def _():
        o_lat_ref[...] = xo


def _combine(rows, pos, x, meta, mods, fnw, ys, final):
    ct = rows.ctx_tiles
    if final:
        out_shape = [jax.ShapeDtypeStruct((rows.nc, D_MODEL), F32), jax.ShapeDtypeStruct((rows.nl, D_MODEL), F32)]
        out_specs = [pl.BlockSpec((TM, D_MODEL), lambda i: (jnp.minimum(i, ct - 1), 0)),
                     pl.BlockSpec((TM, D_MODEL), lambda i: (jnp.maximum(i - ct, 0), 0))]
    else:
        out_shape = jax.ShapeDtypeStruct((rows.n, D_MODEL), F32)
        out_specs = pl.BlockSpec((TM, D_MODEL), lambda i: (i, 0))
    return pl.pallas_call(
        functools.partial(_combine_kernel, final=final, ctx_tiles=ct, n_tiles=rows.n_tiles),
        out_shape=out_shape,
        grid=(rows.n_tiles,),
        in_specs=[
            pl.BlockSpec((None, 1, 2 * TM), lambda i: (i, 0, 0), memory_space=pltpu.SMEM),
            pl.BlockSpec((None, 1, 2 * TM), lambda i: (jnp.minimum(i + 1, rows.n_tiles - 1), 0, 0),
                         memory_space=pltpu.SMEM),
            pl.BlockSpec((TM, D_MODEL), lambda i: (i, 0)),
            pl.BlockSpec((TM, LANE), lambda i: (i, 0)),
            rows.mod_spec(),
            pl.BlockSpec((1, D_MODEL), lambda i: (0, 0)),
            pl.BlockSpec(memory_space=pl.ANY),
        ],
        out_specs=out_specs,
        scratch_shapes=[pltpu.VMEM((2, 2, TM, D_MODEL), F32), pltpu.SemaphoreType.DMA((2,))],
        compiler_params=_params("arbitrary"),
        name="moe_combine",
    )(pos, pos, x, meta, mods, fnw.reshape(1, D_MODEL), ys)


def _moe(rows, x, xn, meta, cnt, mods, w_gate, w_up, w_down, layer, fnw, final):
    n_assign = 2 * rows.n
    n_tiles = n_assign // TME + MOE_EXPERTS

    counts = cnt[0, :MOE_EXPERTS].astype(I32)
    padded = ((counts + TME - 1) // TME) * TME
    ends = jnp.cumsum(padded)
    starts = ends - padded
    experts = jnp.arange(MOE_EXPERTS, dtype=I32)
    e = meta[:, META_E1:META_E2 + 1].astype(I32)
    rank = meta[:, META_R1:META_R2 + 1].astype(I32)
    start_of = jnp.sum(jnp.where(e[..., None] == experts, starts, 0), axis=-1)
    pos = (start_of + rank).reshape(rows.n_tiles, 1, 2 * TM)
    n_valid = ends[-1] // TME
    tile_first = jnp.arange(n_tiles, dtype=I32) * TME
    tile_start = jnp.minimum(tile_first, ends[-1] - TME)
    tile_expert = jnp.sum((ends[None, :] <= tile_start[:, None]).astype(I32), axis=1)
    tile_expert = jnp.minimum(tile_expert, MOE_EXPERTS - 1)
    tile_oh = tile_expert[:, None] == experts
    tile_rows = jnp.sum(jnp.where(tile_oh, counts + starts, 0), axis=1) - tile_start
    tile_fill = ((tile_first >= ends[-1]) | (tile_rows < TME)).astype(I32)

    xs = _dispatch(rows, tile_fill, pos, xn, n_tiles)
    ys = _ffn(tile_expert, n_valid.reshape(1).astype(I32), xs, w_gate, w_up, w_down, layer, n_tiles)
    return _combine(rows, pos, x, meta, mods, fnw, ys, final)


def _lower_bound_params(p):
    pr = jax.nn.softmax(p.astype(F32), axis=0)
    lb = jnp.cumsum(pr, axis=0) - pr[0:1]
    lb = jnp.clip(lb, 0.0, 1.0 - 1e-6)
    return jnp.maximum(lb, LOG_TINY), 1.0 - lb


def kernel(x_prompt, x_sample, c, cache_swa_k, cache_swa_v, state_hgrn, cache_mla_ckv, cache_mla_krope, c_ctx, mod_w, mod_b, norm1_w, norm2_w, final_norm_w, even_w_in, even_w_out, swa_sink, hgrn_lb_fwd, hgrn_lb_bwd, hgrn_gnorm_w, mla_w_dq, mla_qnorm_w, mla_w_uq, mla_w_dkv, mla_kvnorm_w, mla_w_ukv, mla_w_o, moe_router_group_w, moe_router_group_b, moe_router_expert_w, moe_router_expert_b, moe_w_gate, moe_w_up, moe_w_down):
    nc_b, nc_t, _ = x_prompt.shape
    nl_b, nl_t, _ = x_sample.shape
    rows = _Rows(nc_b, nc_t, nl_b, nl_t)
    past = cache_swa_k.shape[2]

    x = (x_prompt.reshape(rows.nc, D_MODEL), x_sample.reshape(rows.nl, D_MODEL))
    mod_rows = 16
    cvec = jnp.concatenate([c_ctx[None, :], c, jnp.zeros((mod_rows - 1 - nl_b, D_MODEL), F32)], axis=0)
    mods_all = _modulation(cvec, mod_w, mod_b).reshape(DEPTH, mod_rows, 6, D_MODEL)

    hconsts = _hgrn_consts()
    la_f, l1_f = _lower_bound_params(hgrn_lb_fwd)
    la_b, l1_b = _lower_bound_params(hgrn_lb_bwd)
    lbp_all = jnp.stack([la_f, l1_f, la_b, l1_b], axis=1).reshape(N_EVEN, 4, HGRN_HEADS, LANE).transpose(0, 2, 1, 3)
    swa_cos, swa_sin = _rope_tables(nl_t, SWA_HEAD_DIM, 0, LANE, 0)
    mla_cos, mla_sin = _rope_tables(nl_t, MLA_ROPE_DIM, MLA_NOPE_DIM, MLA_QK_DIM, TM)
    tri =jnp.asarray(np.tril(np.ones((TM, TM), np.float32), -1), BF16)

    new_k, new_v, new_s, new_ckv, new_kr = [], [], [], [], []
    for l in range(DEPTH):
        j = l // 2
        mods = mods_all[l]
        wr = jnp.zeros((D_MODEL, LANE), F32)
        wr = wr.at[:, :MOE_GROUPS].set(moe_router_group_w[l])
        wr = wr.at[:, MOE_GROUPS:MOE_GROUPS + MOE_EXPERTS].set(
            moe_router_expert_w[l].transpose(1, 0, 2).reshape(D_MODEL, MOE_EXPERTS))
        br = jnp.zeros((1, LANE), F32)
        br = br.at[0, :MOE_GROUPS].set(moe_router_group_b[l])
        br = br.at[0, MOE_GROUPS:MOE_GROUPS + MOE_EXPERTS].set(moe_router_expert_b[l].reshape(MOE_EXPERTS))
        if l % 2 == 0:
            proj = _even_proj(rows, x, mods, norm1_w[l], even_w_in[j].astype(BF16))
            a_ctx = _swa_ctx(rows, proj, swa_sink[j])
            a_lat = _swa_lat(rows, proj, swa_sink[j],
                             cache_swa_k[:, j].reshape(nl_b, past, SWA_KV_WIDTH),
                             cache_swa_v[:, j].reshape(nl_b, past, SWA_KV_WIDTH), swa_cos, swa_sin)
            r_ctx, s_ctx = _hgrn(proj, lbp_all[j], hgrn_gnorm_w[j], hconsts, nc_b, nc_t, 0, rows.nc,
                                 emit_state=True)
            (r_lat,) = _hgrn(proj, lbp_all[j], hgrn_gnorm_w[j], hconsts, nl_b, nl_t, rows.nc // nl_t, rows.nl,
                             s0=state_hgrn, s0_layer=j)
            w_out = even_w_out[j].astype(BF16)
            x, xn, meta, cnt = _outproj_route(rows, [(a_ctx, a_lat), (r_ctx, r_lat)],
                                              [w_out[:SWA_WIDTH], w_out[SWA_WIDTH:]], x, mods, norm2_w[l], wr, br, tri)
            kv = proj[:rows.nc, SWA_WIDTH:SWA_WIDTH + 2 * SWA_KV_WIDTH]
            new_k.append(kv[:, :SWA_KV_WIDTH].reshape(nc_b, nc_t, SWA_KV_HEADS, SWA_HEAD_DIM))
            new_v.append(kv[:, SWA_KV_WIDTH:].reshape(nc_b, nc_t, SWA_KV_HEADS, SWA_HEAD_DIM))
            new_s.append(s_ctx)
        else:
            wd = jnp.zeros((D_MODEL, MLA_DOWN_WIDTH), F32)
            wd = wd.at[:, :MLA_Q_RANK].set(mla_w_dq[j])
            wd = wd.at[:, MLA_Q_RANK:MLA_Q_RANK + MLA_KV_RANK].set(mla_w_dkv[j][:, :MLA_KV_RANK])
            kr_lo = MLA_Q_RANK + MLA_KV_RANK + MLA_NOPE_DIM
            wd = wd.at[:, kr_lo:kr_lo + MLA_ROPE_DIM].set(mla_w_dkv[j][:, MLA_KV_RANK:])
            wuq = jnp.pad(mla_w_uq[j].reshape(MLA_Q_RANK, MLA_HEADS, MLA_QK_DIM),
                          ((0, 0), (0, 0), (0, LANE - MLA_QK_DIM))).reshape(MLA_Q_RANK, MLA_HEADS * LANE)
            wukv = mla_w_ukv[j].reshape(MLA_KV_RANK, MLA_HEADS, MLA_NOPE_DIM + MLA_V_DIM).transpose(1, 0, 2)
            wk = jnp.pad(wukv[..., :MLA_NOPE_DIM], ((0, 0), (0, 0), (0, LANE - MLA_NOPE_DIM)))
            wv_e = jnp.pad(wukv[..., MLA_NOPE_DIM:], ((0, 0), (0, 0), (0, LANE - MLA_V_DIM)))
            wv_o = jnp.pad(wukv[..., MLA_NOPE_DIM:], ((0, 0), (0, 0), (LANE - MLA_V_DIM, 0)))
            odd = (jnp.arange(MLA_HEADS) % 2 == 1)[:, None, None]
            wv = jnp.where(odd, wv_o, wv_e)
            wuq = wuq.astype(BF16)
            w3 = wuq.reshape(MLA_Q_RANK, MLA_HEADS, LANE)
            nf = MLA_ROPE_DIM // 4
            rot = w3[..., MLA_NOPE_DIM:MLA_QK_DIM].reshape(MLA_Q_RANK, MLA_HEADS, 2, 2, nf)[..., ::-1, :]
            wuq_partner = jnp.concatenate([w3[..., :MLA_NOPE_DIM], rot.reshape(MLA_Q_RANK, MLA_HEADS, MLA_ROPE_DIM),
                                           w3[..., MLA_QK_DIM:]], axis=-1).reshape(MLA_Q_RANK, MLA_HEADS * LANE)
            q, ckv, krb = _mla_proj(rows, x, mods, norm1_w[l], wd.astype(BF16), mla_qnorm_w[j], mla_kvnorm_w[j],
                                    wuq, wuq_partner, mla_cos, mla_sin)
            wkv = jnp.concatenate([wk, wv], axis=-1).astype(BF16)
            ckv_c = ckv[:rows.nc].reshape(nc_b, nc_t, MLA_KV_RANK)
            krb_c = krb[:rows.nc].reshape(nc_b, nc_t, LANE)
            o_ctx = _mla_attn(q, ckv_c, krb_c, wkv, nc_b, nc_t, 0, rows.nc)
            cache_kr = jnp.pad(cache_mla_krope[:, j], ((0, 0), (0, 0), (MLA_NOPE_DIM, LANE - MLA_QK_DIM)))
            ckv_l = jnp.concatenate([cache_mla_ckv[:, j], ckv[rows.nc:].reshape(nl_b, nl_t, MLA_KV_RANK)], axis=1)
            krb_l = jnp.concatenate([cache_kr, krb[rows.nc:].reshape(nl_b, nl_t, LANE)], axis=1)
            o_lat = _mla_attn(q, ckv_l, krb_l, wkv, nl_b, nl_t, rows.nc // nl_t, rows.nl)
            x, xn, meta, cnt = _outproj_route(rows, [(o_ctx, o_lat)], [mla_w_o[j].astype(BF16)], x, mods,
                                              norm2_w[l], wr, br, tri)
            new_ckv.append(ckv_c)
            new_kr.append(krb_c[..., MLA_NOPE_DIM:MLA_QK_DIM])

        x = _moe(rows, x, xn, meta, cnt, mods, moe_w_gate, moe_w_up, moe_w_down, l,
                 final_norm_w, final=(l == DEPTH - 1))

    y_prompt = x[0].reshape(nc_b, nc_t, D_MODEL)
    y_sample = x[1].reshape(nl_b, nl_t, D_MODEL)
    return (y_prompt, y_sample, jnp.stack(new_k, axis=1), jnp.stack(new_v, axis=1), jnp.stack(new_s, axis=1),
            jnp.stack(new_ckv, axis=1), jnp.stack(new_kr, axis=1))
```

```python
import functools
import math

import numpy as np
import jax
import jax.numpy as jnp
from jax import lax
from jax.experimental import pallas as pl
from jax.experimental.pallas import tpu as pltpu

F32, BF16, I32 = jnp.float32, jnp.bfloat16, jnp.int32

D_MODEL = 1024
DEPTH = 4
GRID_W = 64
ROPE_THETA = 10000.0
NORM_EPS = 1e-6
NEG_BIG = -1e30
LOG_TINY = 1e-30
N_EVEN = (DEPTH + 1) // 2
N_ODD = DEPTH // 2

SWA_HEADS = 8
SWA_KV_HEADS = 2
SWA_GROUP = SWA_HEADS // SWA_KV_HEADS
SWA_HEAD_DIM = 64
SWA_WIDTH = SWA_HEADS * SWA_HEAD_DIM
SWA_KV_WIDTH = SWA_KV_HEADS * SWA_HEAD_DIM
SWA_WINDOW = 128
SWA_BLOCK = 128

HGRN_HEADS = 4
HGRN_KEY_DIM = 128
HGRN_VAL_DIM = 128
HGRN_WIDTH = HGRN_HEADS * HGRN_KEY_DIM
HGRN_CHUNK = 128
HGRN_LEVELS = (1, 2, 4, 8, 16, 32, 64)
HGRN_GROUP = 4

EVEN_IN_WIDTH = SWA_WIDTH + 2 * SWA_KV_WIDTH + 5 * HGRN_WIDTH
LANE = 128
COL_Q, COL_K, COL_V = 0, SWA_WIDTH // LANE, (SWA_WIDTH + SWA_KV_WIDTH) // LANE
COL_HGRN = (SWA_WIDTH + 2 * SWA_KV_WIDTH) // LANE

MLA_HEADS = 16
MLA_Q_RANK = 384
MLA_KV_RANK = 256
MLA_NOPE_DIM = 64
MLA_ROPE_DIM = 32
MLA_V_DIM = 64
MLA_QK_DIM = MLA_NOPE_DIM + MLA_ROPE_DIM
MLA_DOWN_WIDTH = MLA_Q_RANK + MLA_KV_RANK + LANE

MOE_GROUPS = 4
MOE_EPG = 8
MOE_EXPERTS = MOE_GROUPS * MOE_EPG
MOE_HIDDEN = 256

TM = 512
TME = 512
DMA_UNROLL = 8
SUBLANES = 8
VMEM_LIMIT = 48 * 1024 * 1024


def _params(*sem):
    return pltpu.CompilerParams(dimension_semantics=sem, vmem_limit_bytes=VMEM_LIMIT)


def _dot(a, b):
    return jnp.dot(a, b, preferred_element_type=F32)


def _dot_nt(a, b):
    return lax.dot_general(a, b, (((1,), (1,)), ((), ())), preferred_element_type=F32)


def _split2(a):
    hi = a.astype(BF16)
    return hi, (a - hi.astype(F32)).astype(BF16)


def _dot_f32ish(a, b):
    ah, al = _split2(a)
    bh, bl = _split2(b)
    return _dot(ah, bh) + (_dot(ah, bl) + _dot(al, bh))


def _silu(x):
    return x * jax.nn.sigmoid(x)


def _normmod(x, nw, shift, scale):
    ms = jnp.mean(x * x, axis=-1, keepdims=True)
    return (x * lax.rsqrt(ms + NORM_EPS) * nw) * (1.0 + scale) + shift


def _mod_kernel(c_ref, w_ref, b_ref, o_ref):
    o_ref[...] = _dot_f32ish(_silu(c_ref[...]), w_ref[...]) + b_ref[...]


def _modulation(cvec, mod_w, mod_b):
    rows = cvec.shape[0]
    nb = 6 * D_MODEL // 1024
    return pl.pallas_call(
        _mod_kernel,
        out_shape=jax.ShapeDtypeStruct((DEPTH, rows, 6 * D_MODEL), F32),
        grid=(DEPTH, nb),
        in_specs=[
            pl.BlockSpec((rows, D_MODEL), lambda l, n: (0, 0)),
            pl.BlockSpec((None, D_MODEL, 1024), lambda l, n: (l, 0, n)),
            pl.BlockSpec((None, 1, 1024), lambda l, n: (l, 0, n)),
        ],
        out_specs=pl.BlockSpec((None, rows, 1024), lambda l, n: (l, 0, n)),
        compiler_params=_params("parallel", "parallel"),
        name="modulation",
    )(cvec, mod_w, mod_b.reshape(DEPTH, 1, 6 * D_MODEL))


class _Rows:
    def __init__(self, nc_b, nc_t, nl_b, nl_t):
        self.nc_b, self.nc_t, self.nl_b, self.nl_t = nc_b, nc_t, nl_b, nl_t
        self.nc = nc_b * nc_t
        self.nl = nl_b * nl_t
        self.n = self.nc + self.nl
        assert self.nc % TM == 0 and nl_t % TM == 0 and self.nc % nl_t == 0
        self.ctx_tiles = self.nc // TM
        self.tiles_per_lat = nl_t // TM
        self.n_tiles = self.n // TM

    def mod_row(self, i):
        return jnp.where(i < self.ctx_tiles, 0, 1 + (i - self.ctx_tiles) // self.tiles_per_lat)

    def mod_spec(self):
        return pl.BlockSpec((None, 6, D_MODEL), lambda i: (self.mod_row(i), 0, 0))

    def stream_specs(self, x):
        ct = self.ctx_tiles
        if not isinstance(x, tuple):
            return [pl.BlockSpec((TM, x.shape[1]), lambda i: (i, 0))], [x]
        x_ctx, x_lat = x
        return ([pl.BlockSpec((TM, x_ctx.shape[1]), lambda i: (jnp.minimum(i, ct - 1), 0)),
                 pl.BlockSpec((TM, x_lat.shape[1]), lambda i: (jnp.maximum(i - ct, 0), 0))], [x_ctx, x_lat])

    def stream_tile(self, refs):
        if len(refs) == 1:
            return refs[0][...]
        return jnp.where(pl.program_id(0) < self.ctx_tiles, refs[0][...], refs[1][...])

    def pos_block(self, i):
        return jnp.where(i < self.ctx_tiles, 0, 1 + (i - self.ctx_tiles) % self.tiles_per_lat)


def _even_proj_kernel(*refs, rows):
    mod_ref, nw_ref, w_ref, o_ref = refs[-4:]
    x = rows.stream_tile(refs[:-4])
    h = _normmod(x, nw_ref[...], mod_ref[0:1, :], mod_ref[1:2, :]).astype(BF16)
    step = 256
    for c in range(EVEN_IN_WIDTH // step):
        o_ref[:, c * step:(c + 1) * step] = _dot(h, w_ref[:, c * step:(c + 1) * step])


def _even_proj(rows, x, mods, nw, w_in_bf16):
    x_specs, x_args = rows.stream_specs(x)
    return pl.pallas_call(
        functools.partial(_even_proj_kernel, rows=rows),
        out_shape=jax.ShapeDtypeStruct((rows.n, EVEN_IN_WIDTH), F32),
        grid=(rows.n_tiles,),
        in_specs=x_specs + [
            rows.mod_spec(),
            pl.BlockSpec((1, D_MODEL), lambda i: (0, 0)),
            pl.BlockSpec((D_MODEL, EVEN_IN_WIDTH), lambda i: (0, 0)),
        ],
        out_specs=pl.BlockSpec((TM, EVEN_IN_WIDTH), lambda i: (i, 0)),
        compiler_params=_params("arbitrary"),
        name="even_proj",
    )(*x_args, mods, nw.reshape(1, D_MODEL), w_in_bf16)


def _rope_tables(t_len, rot_dim, lane_lo, lane_hi, lead_rows):
    half = rot_dim // 2
    nf = half // 2
    lane = np.arange(LANE)
    d = (lane - lane_lo) % rot_dim
    active = (lane >= lane_lo) & (lane < lane_hi)
    use_col = d >= half
    fidx = d % nf
    first = (d % half) < nf
    pos = jnp.arange(t_len)
    row = (pos // GRID_W).astype(F32)
    col = (pos % GRID_W).astype(F32)
    inv = jnp.exp(-math.log(ROPE_THETA) * jnp.arange(nf, dtype=F32) / nf)
    p = jnp.where(jnp.asarray(use_col)[None, :], col[:, None], row[:, None])
    ang = p * inv[jnp.asarray(fidx)][None, :]
    act = jnp.asarray(active)[None, :]
    cos = jnp.where(act, jnp.cos(ang), 1.0)
    sin = jnp.where(act, jnp.sin(ang), 0.0)
    sin = jnp.where(jnp.asarray(first)[None, :], -sin, sin)
    if lead_rows:
        cos = jnp.concatenate([jnp.ones((lead_rows, LANE), F32), cos], axis=0)
        sin = jnp.concatenate([jnp.zeros((lead_rows, LANE), F32), sin], axis=0)
    return cos, sin


def _rope(x, cos, sin, nf):
    lane = lax.broadcasted_iota(I32, x.shape, 1)
    up = pltpu.roll(x, LANE - nf, axis=1)
    dn = pltpu.roll(x, nf, axis=1)
    partner = jnp.where((lane & nf) == 0, up, dn)
    return x * cos + partner * sin


def _swa_ctx_kernel(sink_ref, q_ref, k_ref, v_ref, o_ref):
    scale = SWA_HEAD_DIM ** -0.5
    heads = range(SWA_HEADS)
    kk = [k_ref[:, kv * SWA_HEAD_DIM:(kv + 1) * SWA_HEAD_DIM].astype(BF16) for kv in range(SWA_KV_HEADS)]
    vv = [v_ref[:, kv * SWA_HEAD_DIM:(kv + 1) * SWA_HEAD_DIM].astype(BF16) for kv in range(SWA_KV_HEADS)]
    q = [q_ref[:, h * SWA_HEAD_DIM:(h + 1) * SWA_HEAD_DIM].astype(BF16) for h in heads]
    s = [_dot_nt(q[h], kk[h // SWA_GROUP]) * scale for h in heads]
    m = [jnp.maximum(jnp.max(s[h], axis=-1, keepdims=True), sink_ref[h]) for h in heads]
    p = [jnp.exp(s[h] - m[h]) for h in heads]
    den = [jnp.sum(p[h], axis=-1, keepdims=True) + jnp.exp(sink_ref[h] - m[h]) for h in heads]
    o = [_dot(p[h].astype(BF16), vv[h // SWA_GROUP]) / den[h] for h in heads]
    for h in heads:
        o_ref[:, h * SWA_HEAD_DIM:(h + 1) * SWA_HEAD_DIM] = o[h]


def _swa_ctx(rows, proj, sink):
    t = rows.nc_t
    return pl.pallas_call(
        _swa_ctx_kernel,
        out_shape=jax.ShapeDtypeStruct((rows.nc, SWA_WIDTH), F32),
        grid=(rows.nc_b,),
        in_specs=[
            pl.BlockSpec(memory_space=pltpu.SMEM),
            pl.BlockSpec((t, SWA_WIDTH), lambda b: (b, COL_Q)),
            pl.BlockSpec((t, LANE), lambda b: (b, COL_K)),
            pl.BlockSpec((t, LANE), lambda b: (b, COL_V)),
        ],
        out_specs=pl.BlockSpec((t, SWA_WIDTH), lambda b: (b, 0)),
        compiler_params=_params("parallel"),
        name="swa_ctx",
    )(sink, proj, proj, proj)


def _lane_fold(x, op):
    out = x[:, :LANE]
    for i in range(1, x.shape[1] // LANE):
        out = op(out, x[:, i * LANE:(i + 1) * LANE])
    return out


def _swa_lat_kernel(sink_ref, q_ref, k_ref, v_ref, kc_ref, vc_ref, cos_ref, sin_ref, o_ref,
                    kl_scr, vl_scr, kc_scr, vc_scr, *, n_blocks):
    scale = SWA_HEAD_DIM ** -0.5
    nf = SWA_HEAD_DIM // 4
    n = pl.program_id(1)

    @pl.when(n == 0)
    def _():
        kr = _rope(k_ref[...], cos_ref[...], sin_ref[...], nf)
        for kv in range(SWA_KV_HEADS):
            cols = slice(kv * SWA_HEAD_DIM, (kv + 1) * SWA_HEAD_DIM)
            kl_scr[kv] = kr[:, cols].astype(BF16)
            vl_scr[kv] = v_ref[:, cols].astype(BF16)
            kc_scr[kv] = kc_ref[:, cols].astype(BF16)
            vc_scr[kv] = vc_ref[:, cols].astype(BF16)

    q0 = pl.multiple_of(n * SWA_BLOCK, SWA_BLOCK)
    cq = cos_ref[pl.ds(q0, SWA_BLOCK), :]
    sq = sin_ref[pl.ds(q0, SWA_BLOCK), :]
    qs = [_rope(q_ref[:, g * LANE:(g + 1) * LANE], cq, sq, nf) for g in range(SWA_WIDTH // LANE)]

    qi = lax.broadcasted_iota(I32, (SWA_BLOCK, SWA_BLOCK), 0)
    r = lax.broadcasted_iota(I32, (SWA_BLOCK, SWA_BLOCK), 1)
    band, mb = [], []
    for off in (-1, 0, 1):
        blk = n + off
        valid = (blk >= 0) & (blk < n_blocks)
        st = pl.multiple_of(jnp.clip(blk, 0, n_blocks - 1) * SWA_BLOCK, SWA_BLOCK)
        band.append(pl.ds(st, SWA_BLOCK))
        rel = qi - r - off * SWA_BLOCK
        mb.append(jnp.where((jnp.abs(rel) <= SWA_WINDOW) & valid, 1.0, 0.0))
    mask = jnp.concatenate(mb, axis=1) > 0.5

    kl = [jnp.concatenate([kl_scr[kv, sl, :] for sl in band], axis=0) for kv in range(SWA_KV_HEADS)]
    vl = [jnp.concatenate([vl_scr[kv, sl, :] for sl in band], axis=0) for kv in range(SWA_KV_HEADS)]
    heads = range(SWA_HEADS)
    lanes = [slice((h * SWA_HEAD_DIM) % LANE, (h * SWA_HEAD_DIM) % LANE + SWA_HEAD_DIM) for h in heads]
    q = [qs[h * SWA_HEAD_DIM // LANE][:, lanes[h]].astype(BF16) for h in heads]
    s_ctx = [_dot_nt(q[h], kc_scr[h // SWA_GROUP]) * scale for h in heads]
    s_loc = [jnp.where(mask, _dot_nt(q[h], kl[h // SWA_GROUP]) * scale, NEG_BIG) for h in heads]
    m = [jnp.maximum(jnp.max(jnp.maximum(_lane_fold(s_ctx[h], jnp.maximum), _lane_fold(s_loc[h], jnp.maximum)),
                             axis=-1, keepdims=True), sink_ref[h]) for h in heads]
    p_ctx = [jnp.exp(s_ctx[h] - m[h]) for h in heads]
    p_loc = [jnp.exp(s_loc[h] - m[h]) for h in heads]
    den = [jnp.sum(_lane_fold(p_ctx[h], jnp.add) + _lane_fold(p_loc[h], jnp.add), axis=-1, keepdims=True)
           + jnp.exp(sink_ref[h] - m[h]) for h in heads]
    o = [_dot(p_ctx[h].astype(BF16), vc_scr[h // SWA_GROUP]) + _dot(p_loc[h].astype(BF16), vl[h // SWA_GROUP])
         for h in heads]
    for h in heads:
        o_ref[:, h * SWA_HEAD_DIM:(h + 1) * SWA_HEAD_DIM] = o[h] / den[h]


def _swa_lat(rows, proj, sink, k_ctx, v_ctx, cos, sin):
    t = rows.nl_t
    n_blocks = t // SWA_BLOCK
    q_base = rows.nc // SWA_BLOCK
    kv_base = rows.nc // t
    s_ctx = k_ctx.shape[1]
    return pl.pallas_call(
        functools.partial(_swa_lat_kernel, n_blocks=n_blocks),
        out_shape=jax.ShapeDtypeStruct((rows.nl, SWA_WIDTH), F32),
        grid=(rows.nl_b, n_blocks),
        in_specs=[
            pl.BlockSpec(memory_space=pltpu.SMEM),
            pl.BlockSpec((SWA_BLOCK, SWA_WIDTH), lambda b, n: (q_base + b * n_blocks + n, COL_Q)),
            pl.BlockSpec((t, LANE), lambda b, n: (kv_base + b, COL_K)),
            pl.BlockSpec((t, LANE), lambda b, n: (kv_base + b, COL_V)),
            pl.BlockSpec((None, s_ctx, LANE), lambda b, n: (b, 0, 0)),
            pl.BlockSpec((None, s_ctx, LANE), lambda b, n: (b, 0, 0)),
            pl.BlockSpec((t, LANE), lambda b, n: (0, 0)),
            pl.BlockSpec((t, LANE), lambda b, n: (0, 0)),
        ],
        out_specs=pl.BlockSpec((SWA_BLOCK, SWA_WIDTH), lambda b, n: (b * n_blocks + n, 0)),
        scratch_shapes=[pltpu.VMEM((SWA_KV_HEADS, t, SWA_HEAD_DIM), BF16),
                        pltpu.VMEM((SWA_KV_HEADS, t, SWA_HEAD_DIM), BF16),
                        pltpu.VMEM((SWA_KV_HEADS, s_ctx, SWA_HEAD_DIM), BF16),
                        pltpu.VMEM((SWA_KV_HEADS, s_ctx, SWA_HEAD_DIM), BF16)],
        compiler_params=_params("parallel", "arbitrary"),
        name="swa_lat",
    )(sink, proj, proj, proj, k_ctx, v_ctx, cos, sin)


def _hgrn_consts():
    c = HGRN_CHUNK
    t = np.arange(c)[:, None]
    u = np.arange(c)[None, :]
    tri_f = (u <= t).astype(np.float32)
    masks = [t == u]
    for m in HGRN_LEVELS:
        right = ((t // m) % 2) == 1
        masks.append(right & ((u // m) == (t // m) - 1))
    m_f = np.stack(masks).astype(np.float32)
    return (jnp.asarray(tri_f, BF16), jnp.asarray(m_f, F32),
            jnp.asarray(tri_f[::-1, ::-1], BF16), jnp.asarray(m_f[:, ::-1, ::-1], F32))


def _hgrn_intra(items):
    c = HGRN_CHUNK
    nsub = c // SUBLANES
    n = range(len(items))
    fwd = [it[7] for it in items]
    v = [it[2] for it in items]
    q = [_silu(it[0]) * (HGRN_KEY_DIM ** -0.5) for it in items]
    f = [it[3] + it[4] * jax.nn.sigmoid(it[1]) for it in items]
    lf = [jnp.log(f[j]) for j in n]
    k = [1.0 - f[j] for j in n]

    hi = [lf[j].astype(BF16) for j in n]
    r1 = [lf[j] - hi[j].astype(F32) for j in n]
    mid = [r1[j].astype(BF16) for j in n]
    lo = [(r1[j] - mid[j].astype(F32)).astype(BF16) for j in n]
    tri = [items[j][5][...] for j in n]
    b = [_dot(tri[j], hi[j]) + (_dot(tri[j], mid[j]) + _dot(tri[j], lo[j])) for j in n]
    total = [jnp.sum(lf[j], axis=0, keepdims=True) for j in n]

    sub_io = lax.broadcasted_iota(I32, (1, SUBLANES, HGRN_KEY_DIM), 1)
    row_io = lax.broadcasted_iota(I32, (c, HGRN_KEY_DIM), 0)
    a = [items[j][6][0] * _dot_nt(q[j].astype(BF16), k[j].astype(BF16)) for j in n]
    for li, m in enumerate(HGRN_LEVELS, start=1):
        second = (row_io & m) != 0
        qk = [jnp.where(second if fwd[j] else jnp.logical_not(second), q[j], k[j]) for j in n]
        fac = []
        for j in n:
            first = m - 1 if fwd[j] else m
            if 2 * m <= SUBLANES:
                b3 = b[j].reshape(nsub, SUBLANES, HGRN_KEY_DIM)
                bnd = None
                for pair in reversed(range(SUBLANES // (2 * m))):
                    r = 2 * pair * m + first
                    piece = jnp.broadcast_to(b3[:, r:r + 1, :], b3.shape)
                    bnd = piece if bnd is None else jnp.where(sub_io < 2 * m * (pair + 1), piece, bnd)
                bnd = bnd.reshape(c, HGRN_KEY_DIM)
            else:
                pieces = [jnp.broadcast_to(b[j][2 * pair * m + first:2 * pair * m + first + 1, :],
                                           (2 * m, HGRN_KEY_DIM)) for pair in range(c // (2 * m))]
                bnd = pieces[0] if len(pieces) == 1 else jnp.concatenate(pieces, axis=0)
            fac.append(jnp.exp(-jnp.abs(b[j] - bnd)))
        x = [(qk[j] * fac[j]).astype(BF16) for j in n]
        prod = [_dot_nt(x[j], x[j]) for j in n]
        a = [jnp.where(items[j][6][li] > 0.5, prod[j], a[j]) for j in n]
    o = [_dot(a[j].astype(BF16), v[j].astype(BF16)) for j in n]
    return o, q, k, b, total


def _hgrn_carry(o, q, k, v, b, total, st):
    o = o + _dot_nt((q * jnp.exp(b)).astype(BF16), st.astype(BF16))
    kc = (k * jnp.exp(total - b)).astype(BF16)
    return o, st * jnp.exp(total) + _dot(v.T.astype(BF16), kc)


def _hgrn_kernel(*refs, n_chunks, has_s0, emit_state):
    (qb_ref, ff_ref, fb_ref, ib_ref, gb_ref, lbp_ref, gw_ref,
     ef_ref, mf_ref, eb_ref, mb_ref) = refs[:11]
    rest = list(refs[11:])
    s0_ref = rest.pop(0) if has_s0 else None
    r_ref = rest.pop(0)
    sout_ref = rest.pop(0) if emit_state else None
    of_scr, ob_scr, stf_scr, stb_scr = rest
    c = HGRN_CHUNK
    gw = gw_ref[...]

    for d, st_scr in enumerate((stf_scr, stb_scr)):
        if has_s0:
            st_scr[...] = s0_ref[d].T
        else:
            st_scr[...] = jnp.zeros((HGRN_VAL_DIM, HGRN_KEY_DIM), F32)

    group = min(HGRN_GROUP, n_chunks)

    def sweep(i, carry):
        sls, items = [], []
        for u in range(group):
            sls.append(pl.ds(pl.multiple_of((i * group + u) * c, c), c))
            items.append((qb_ref[sls[-1], :], ff_ref[sls[-1], :], ib_ref[sls[-1], :], lbp_ref[0:1, :],
                          lbp_ref[1:2, :], ef_ref, mf_ref, True))
        for u in range(group):
            sls.append(pl.ds(pl.multiple_of((n_chunks - 1 - (i * group + u)) * c, c), c))
            items.append((qb_ref[sls[-1], :], fb_ref[sls[-1], :], ib_ref[sls[-1], :], lbp_ref[2:3, :],
                          lbp_ref[3:4, :], eb_ref, mb_ref, False))
        o, q, k, b, total = _hgrn_intra(items)
        for d, (st_scr, o_scr) in enumerate(((stf_scr, of_scr), (stb_scr, ob_scr))):
            st = st_scr[...]
            for u in range(group):
                j = d * group + u
                o_j, st = _hgrn_carry(o[j], q[j], k[j], items[j][2], b[j], total[j], st)
                o_scr[sls[j], :] = o_j
            st_scr[...] = st
        return carry

    lax.fori_loop(0, n_chunks // group, sweep, 0)
    if emit_state:
        sout_ref[0] = stf_scr[...].T
        sout_ref[1] = stb_scr[...].T

    def readout(ci, carry):
        sl = pl.ds(pl.multiple_of(ci * c, c), c)
        tot = of_scr[sl, :] + ob_scr[sl, :]
        ms = jnp.mean(tot * tot, axis=-1, keepdims=True)
        r_ref[sl, :] = (tot * lax.rsqrt(ms + NORM_EPS) * gw) * _silu(gb_ref[sl, :])
        return carry

    lax.fori_loop(0, n_chunks, readout, 0, unroll=2)


def _hgrn(proj, lbp, gw, consts, n_b, t_len, row_block0, n_rows, s0=None, s0_layer=0, emit_state=False):
    n_chunks = t_len // HGRN_CHUNK
    e_f, m_f, e_b, m_b = consts

    def col(off):
        return pl.BlockSpec((t_len, LANE), lambda b, h: (row_block0 + b, COL_HGRN + off * HGRN_HEADS + h))

    def whole(a):
        nd = a.ndim
        return pl.BlockSpec(a.shape, lambda b, h: (0,) * nd)

    in_specs = [col(0), col(1), col(2), col(3), col(4),
                pl.BlockSpec((None, 4, LANE), lambda b, h: (h, 0, 0)),
                pl.BlockSpec((1, HGRN_VAL_DIM), lambda b, h: (0, 0)),
                whole(e_f), whole(m_f), whole(e_b), whole(m_b)]
    args = [proj, proj, proj, proj, proj, lbp, gw.reshape(1, HGRN_VAL_DIM), e_f, m_f, e_b, m_b]
    if s0 is not None:
        in_specs.append(pl.BlockSpec((None, None, 2, None, HGRN_KEY_DIM, HGRN_VAL_DIM),
                                     lambda b, h: (b, s0_layer, 0, h, 0, 0)))
        args.append(s0)
    out_shape = [jax.ShapeDtypeStruct((n_rows, HGRN_WIDTH), F32)]
    out_specs = [pl.BlockSpec((t_len, LANE), lambda b, h: (b, h))]
    if emit_state:
        out_shape.append(jax.ShapeDtypeStruct((n_b, 2, HGRN_HEADS, HGRN_KEY_DIM, HGRN_VAL_DIM), F32))
        out_specs.append(pl.BlockSpec((None, 2, None, HGRN_KEY_DIM, HGRN_VAL_DIM),
                                      lambda b, h: (b, 0, h, 0, 0)))
    return pl.pallas_call(
        functools.partial(_hgrn_kernel, n_chunks=n_chunks, has_s0=s0 is not None, emit_state=emit_state),
        out_shape=out_shape,
        grid=(n_b, HGRN_HEADS),
        in_specs=in_specs,
        out_specs=out_specs,
        scratch_shapes=[pltpu.VMEM((t_len, HGRN_VAL_DIM), F32), pltpu.VMEM((t_len, HGRN_VAL_DIM), F32),
                        pltpu.VMEM((HGRN_VAL_DIM, HGRN_KEY_DIM), F32), pltpu.VMEM((HGRN_VAL_DIM, HGRN_KEY_DIM), F32)],
        compiler_params=_params("parallel", "parallel"),
        name="hgrn_lat" if s0 is not None else "hgrn_ctx",
    )(*args)


def _outproj_kernel(*refs, n_parts, n_x, rows):
    pair_refs = refs[:2 * n_parts]
    w_refs = refs[2 * n_parts:3 * n_parts]
    x_refs = refs[3 * n_parts:3 * n_parts + n_x]
    mod_ref, nw_ref, wr_ref, br_ref, tri_ref, o_ref, xn_ref, meta_ref, cnt_ref, base_scr = refs[3 * n_parts + n_x:]
    acc = None
    for p in range(n_parts):
        a = rows.stream_tile(pair_refs[2 * p:2 * p + 2]).astype(BF16)
        d = _dot(a, w_refs[p][...])
        acc = d if acc is None else acc + d
    x_new = rows.stream_tile(x_refs) + mod_ref[2:3, :] * acc
    o_ref[...] = x_new
    _route_tile(x_new, mod_ref, nw_ref, wr_ref, br_ref, tri_ref, xn_ref, meta_ref, cnt_ref, base_scr)


def _outproj_route(rows, pairs, weights, x, mods, nw2, wr, br, tri):
    in_specs, args = [], []
    for pair in pairs:
        specs, arrs = rows.stream_specs(pair)
        in_specs += specs
        args += arrs
    in_specs += [pl.BlockSpec(w.shape, lambda i: (0, 0)) for w in weights]
    x_specs, x_args = rows.stream_specs(x)
    in_specs += x_specs + [
        rows.mod_spec(),
        pl.BlockSpec((1, D_MODEL), lambda i: (0, 0)),
        pl.BlockSpec((D_MODEL, LANE), lambda i: (0, 0)),
        pl.BlockSpec((1, LANE), lambda i: (0, 0)),
        pl.BlockSpec((TM, TM), lambda i: (0, 0)),
    ]
    return pl.pallas_call(
        functools.partial(_outproj_kernel, n_parts=len(pairs), n_x=len(x_args), rows=rows),
        out_shape=[jax.ShapeDtypeStruct((rows.n, D_MODEL), F32),
                   jax.ShapeDtypeStruct((rows.n, D_MODEL), F32),
                   jax.ShapeDtypeStruct((rows.n, LANE), F32),
                   jax.ShapeDtypeStruct((8, LANE), F32)],
        grid=(rows.n_tiles,),
        in_specs=in_specs,
        out_specs=[pl.BlockSpec((TM, D_MODEL), lambda i: (i, 0)),
                   pl.BlockSpec((TM, D_MODEL), lambda i: (i, 0)),
                   pl.BlockSpec((TM, LANE), lambda i: (i, 0)),
                   pl.BlockSpec((8, LANE), lambda i: (0, 0))],
        scratch_shapes=[pltpu.VMEM((8, LANE), F32)],
        compiler_params=_params("arbitrary"),
        name="outproj_route",
    )(*args, *weights, *x_args, mods, nw2.reshape(1, D_MODEL), wr, br, tri)


def _mla_proj_kernel(x_ref, mod_ref, nw_ref, wd_ref, qnw_ref, kvnw_ref, wuq_ref, wuqp_ref, cos_ref, sin_ref,
                     q_ref, ckv_ref, krb_ref):
    nf = MLA_ROPE_DIM // 4
    h = _normmod(x_ref[...], nw_ref[...], mod_ref[0:1, :], mod_ref[1:2, :]).astype(BF16)
    t1 = _dot(h, wd_ref[...])
    qd = t1[:, :MLA_Q_RANK]
    kvd = t1[:, MLA_Q_RANK:MLA_Q_RANK + MLA_KV_RANK]
    cos = cos_ref[...]
    sin = sin_ref[...]
    qn = qd * lax.rsqrt(jnp.mean(qd * qd, axis=-1, keepdims=True) + NORM_EPS) * qnw_ref[...]
    ckv_ref[...] = kvd * lax.rsqrt(jnp.mean(kvd * kvd, axis=-1, keepdims=True) + NORM_EPS) * kvnw_ref[...]
    krb_ref[...] = _rope(t1[:, MLA_Q_RANK + MLA_KV_RANK:], cos, sin, nf)
    qb = qn.astype(BF16)
    scale = MLA_QK_DIM ** -0.5
    cos2 = jnp.concatenate([cos, cos], axis=1)
    sin2 = jnp.concatenate([sin, sin], axis=1)
    for hp in range(MLA_HEADS // 2):
        cols = slice(hp * 2 * LANE, (hp + 1) * 2 * LANE)
        qh = _dot(qb, wuq_ref[:, cols])
        qp = _dot(qb, wuqp_ref[:, cols])
        q_ref[:, cols] = ((qh * cos2 + qp * sin2) * scale).astype(BF16)


def _mla_proj(rows, x, mods, nw, wd, qnw, kvnw, wuq, wuq_partner, cos, sin):
    return pl.pallas_call(
        _mla_proj_kernel,
        out_shape=[jax.ShapeDtypeStruct((rows.n, MLA_HEADS * LANE), BF16),
                   jax.ShapeDtypeStruct((rows.n, MLA_KV_RANK), F32),
                   jax.ShapeDtypeStruct((rows.n, LANE), F32)],
        grid=(rows.n_tiles,),
        in_specs=[
            pl.BlockSpec((TM, D_MODEL), lambda i: (i, 0)),
            rows.mod_spec(),
            pl.BlockSpec((1, D_MODEL), lambda i: (0, 0)),
            pl.BlockSpec((D_MODEL, MLA_DOWN_WIDTH), lambda i: (0, 0)),
            pl.BlockSpec((1, MLA_Q_RANK), lambda i: (0, 0)),
            pl.BlockSpec((1, MLA_KV_RANK), lambda i: (0, 0)),
            pl.BlockSpec((MLA_Q_RANK, MLA_HEADS * LANE), lambda i: (0, 0)),
            pl.BlockSpec((MLA_Q_RANK, MLA_HEADS * LANE), lambda i: (0, 0)),
            pl.BlockSpec((TM, LANE), lambda i: (rows.pos_block(i), 0)),
            pl.BlockSpec((TM, LANE), lambda i: (rows.pos_block(i), 0)),
        ],
        out_specs=[pl.BlockSpec((TM, MLA_HEADS * LANE), lambda i: (i, 0)),
                   pl.BlockSpec((TM, MLA_KV_RANK), lambda i: (i, 0)),
                   pl.BlockSpec((TM, LANE), lambda i: (i, 0))],
        compiler_params=_params("parallel"),
        name="mla_proj",
    )(x, mods, nw.reshape(1, D_MODEL), wd, qnw.reshape(1, MLA_Q_RANK), kvnw.reshape(1, MLA_KV_RANK), wuq, wuq_partner, cos, sin)


MLA_TQ = 256
MLA_HPS = 4


def _mla_attn_kernel(q_ref, ckv_ref, krb_ref, wkv_ref, o_ref, k_scr, v_scr, *, t_len, hps):
    tq = min(MLA_TQ, t_len)
    ckv = ckv_ref[...].astype(BF16)
    krb = krb_ref[...]
    for hh in range(hps):
        kvh = _dot(ckv, wkv_ref[hh])
        k_scr[hh] = (kvh[:, :LANE] + krb).astype(BF16)
        v_scr[hh] = kvh[:, LANE:].astype(BF16)

    def body(ti, carry):
        sl = pl.ds(pl.multiple_of(ti * tq, tq), tq)
        heads = range(hps)
        s = [_dot_nt(q_ref[sl, hh * LANE:(hh + 1) * LANE], k_scr[hh]) for hh in heads]
        m = [jnp.max(s[hh], axis=-1, keepdims=True) for hh in heads]
        p = [jnp.exp(s[hh] - m[hh]) for hh in heads]
        den = [jnp.sum(p[hh], axis=-1, keepdims=True) for hh in heads]
        o = [_dot(p[hh].astype(BF16), v_scr[hh]) / den[hh] for hh in heads]
        for pair in range(hps // 2):
            o_ref[sl, pair * LANE:(pair + 1) * LANE] = o[2 * pair] + o[2 * pair + 1]
        return carry

    lax.fori_loop(0, t_len // tq, body, 0)


def _mla_attn(q, ckv_all, krb_all, wkv, n_b, t_len, q_row_block0, n_rows):
    s_len = ckv_all.shape[1]
    hps = MLA_HEADS if t_len <= MLA_TQ else MLA_HPS
    return pl.pallas_call(
        functools.partial(_mla_attn_kernel, t_len=t_len, hps=hps),
        out_shape=jax.ShapeDtypeStruct((n_rows, MLA_HEADS * MLA_V_DIM), F32),
        grid=(n_b, MLA_HEADS // hps),
        in_specs=[
            pl.BlockSpec((t_len, hps * LANE), lambda b, hp: (q_row_block0 + b, hp)),
            pl.BlockSpec((None, s_len, MLA_KV_RANK), lambda b, hp: (b, 0, 0)),
            pl.BlockSpec((None, s_len, LANE), lambda b, hp: (b, 0, 0)),
            pl.BlockSpec((hps, MLA_KV_RANK, 2 * LANE), lambda b, hp: (hp, 0, 0)),
        ],
        out_specs=pl.BlockSpec((t_len, hps // 2 * LANE), lambda b, hp: (b, hp)),
        scratch_shapes=[pltpu.VMEM((hps, s_len, LANE), BF16), pltpu.VMEM((hps, s_len, LANE), BF16)],
        compiler_params=_params("parallel", "arbitrary"),
        name="mla_attn",
    )(q, ckv_all, krb_all, wkv)


META_E1, META_E2, META_W1, META_W2, META_R1, META_R2 = range(6)


def _route_tile(x, mod_ref, nw_ref, wr_ref, br_ref, tri_ref, xn_ref, meta_ref, cnt_ref, base_scr):
    @pl.when(pl.program_id(0) == 0)
    def _():
        base_scr[...] = jnp.zeros(base_scr.shape, F32)

    xn = _normmod(x, nw_ref[...], mod_ref[3:4, :], mod_ref[4:5, :])
    xn_ref[...] = xn
    logits = _dot_f32ish(xn, wr_ref[...]) + br_ref[...]
    lane = lax.broadcasted_iota(I32, logits.shape, 1).astype(F32)
    far = float(LANE)

    def first_argmax(vals, vmax):
        return jnp.min(jnp.where(vals == vmax, lane, far), axis=-1, keepdims=True)

    gl = jnp.where(lane < MOE_GROUPS, logits, NEG_BIG)
    gmax = jnp.max(gl, axis=-1, keepdims=True)
    g_w = 1.0 / jnp.sum(jnp.exp(gl - gmax), axis=-1, keepdims=True)
    g_idx = first_argmax(gl, gmax)
    e_lo = MOE_GROUPS + MOE_EPG * g_idx
    el = jnp.where((lane >= e_lo) & (lane < e_lo + MOE_EPG), logits, NEG_BIG)
    m1 = jnp.max(el, axis=-1, keepdims=True)
    i1 = first_argmax(el, m1)
    el2 = jnp.where(lane == i1, NEG_BIG, el)
    m2 = jnp.max(el2, axis=-1, keepdims=True)
    i2 = first_argmax(el2, m2)
    esum = jnp.sum(jnp.exp(el - m1), axis=-1, keepdims=True)
    p1 = 1.0 / esum
    p2 = jnp.exp(m2 - m1) / esum
    w1 = g_w * (p1 / (p1 + p2))
    w2 = g_w * (p2 / (p1 + p2))
    e1 = i1 - MOE_GROUPS
    e2 = i2 - MOE_GROUPS

    oh1 = lane == e1
    oh2 = lane == e2
    oh = jnp.where(oh1 | oh2, 1.0, 0.0)
    before = _dot(tri_ref[...], oh.astype(BF16)) + base_scr[0:1, :]
    r1 = jnp.sum(jnp.where(oh1, before, 0.0), axis=-1, keepdims=True)
    r2 = jnp.sum(jnp.where(oh2, before, 0.0), axis=-1, keepdims=True)
    base_scr[...] = base_scr[...] + jnp.sum(oh, axis=0, keepdims=True)
    cnt_ref[...] = base_scr[...]

    meta = jnp.zeros(logits.shape, F32)
    for slot, val in ((META_E1, e1), (META_E2, e2), (META_W1, w1), (META_W2, w2), (META_R1, r1), (META_R2, r2)):
        meta = jnp.where(lane == slot, val, meta)
    meta_ref[...] = meta


def _row_copy(src, src_row, dst, dst_row, sem):
    return pltpu.make_async_copy(src.at[pl.ds(src_row, 1), :], dst.at[pl.ds(dst_row, 1), :], sem)


def _dispatch_kernel(fill_ref, pos_ref, xn_ref, xs_hbm, zero_scr, src_scr, sems, fill_sem, *, n_tiles):
    @pl.when(pl.program_id(0) == 0)
    def _():
        zero_scr[...] = jnp.zeros(zero_scr.shape, F32)

        def fill_copy(t):
            return pltpu.make_async_copy(zero_scr, xs_hbm.at[pl.ds(pl.multiple_of(t * TME, TME), TME), :], fill_sem)

        def fill_start(t, carry):
            @pl.when(fill_ref[t] != 0)
            def _():
                fill_copy(t).start()
            return carry

        def fill_wait(t, carry):
            @pl.when(fill_ref[t] != 0)
            def _():
                fill_copy(t).wait()
            return carry

        lax.fori_loop(0, n_tiles, fill_start, 0)
        lax.fori_loop(0, n_tiles, fill_wait, 0)

    i = pl.program_id(0)
    slot = lax.rem(i, 2)

    def tile_wait(s):
        for _ in range(2):
            pltpu.make_async_copy(src_scr.at[s], xs_hbm.at[pl.ds(0, TM), :], sems.at[s]).wait()

    for s in range(2):
        @pl.when(slot == s)
        def _(s=s):
            src_scr[s] = xn_ref[...]

            def start(r, carry):
                for k in range(2):
                    _row_copy(src_scr.at[s], r, xs_hbm, pos_ref[0, 2 * r + k], sems.at[s]).start(priority=k)
                return carry

            lax.fori_loop(0, TM, start, 0, unroll=DMA_UNROLL)

    @pl.when(i > 0)
    def _():
        tile_wait(1 - slot)

    @pl.when(i == pl.num_programs(0) - 1)
    def _():
        tile_wait(slot)


def _dispatch(rows, tile_fill, pos, xn, n_tiles):
    return pl.pallas_call(
        functools.partial(_dispatch_kernel, n_tiles=n_tiles),
        out_shape=jax.ShapeDtypeStruct((n_tiles * TME, D_MODEL), F32),
        grid_spec=pltpu.PrefetchScalarGridSpec(
            num_scalar_prefetch=1,
            grid=(rows.n_tiles,),
            in_specs=[
                pl.BlockSpec((None, 1, 2 * TM), lambda i, fill: (i, 0, 0), memory_space=pltpu.SMEM),
                pl.BlockSpec((TM, D_MODEL), lambda i, fill: (i, 0)),
            ],
            out_specs=pl.BlockSpec(memory_space=pl.ANY),
            scratch_shapes=[pltpu.VMEM((TME, D_MODEL), F32), pltpu.VMEM((2, TM, D_MODEL), F32),
                            pltpu.SemaphoreType.DMA((2,)), pltpu.SemaphoreType.DMA(())],
        ),
        compiler_params=_params("arbitrary"),
        name="moe_dispatch",
    )(tile_fill, pos, xn)


def _ffn_kernel(te_ref, nv_ref, x_ref, wg_ref, wu_ref, wd_ref, y_ref, wg_b, wu_b, wd_b):
    t = pl.program_id(0)
    valid = t < nv_ref[0]
    new_expert = (t == 0) | (te_ref[t] != te_ref[jnp.maximum(t - 1, 0)])

    @pl.when(valid & new_expert)
    def _():
        wg_b[...] = wg_ref[...].astype(BF16)
        wu_b[...] = wu_ref[...].astype(BF16)
        wd_b[...] = wd_ref[...].astype(BF16)

    @pl.when(valid)
    def _():
        x = x_ref[...].astype(BF16)
        a = _silu(_dot(x, wg_b[...])) * _dot(x, wu_b[...])
        y_ref[...] = _dot(a.astype(BF16), wd_b[...])

    @pl.when(jnp.logical_not(valid))
    def _():
        y_ref[...] = jnp.zeros(y_ref.shape, F32)


def _ffn(tile_expert, n_valid, xs, w_gate, w_up, w_down, layer, n_tiles):
    def xmap(t, te, nv):
        return (jnp.minimum(t, nv[0] - 1), 0)

    def wmap(t, te, nv):
        return (layer, te[t], 0, 0)

    return pl.pallas_call(
        _ffn_kernel,
        out_shape=jax.ShapeDtypeStruct((n_tiles * TME, D_MODEL), F32),
        grid_spec=pltpu.PrefetchScalarGridSpec(
            num_scalar_prefetch=2,
            grid=(n_tiles,),
            in_specs=[
                pl.BlockSpec((TME, D_MODEL), xmap),
                pl.BlockSpec((None, None, D_MODEL, MOE_HIDDEN), wmap),
                pl.BlockSpec((None, None, D_MODEL, MOE_HIDDEN), wmap),
                pl.BlockSpec((None, None, MOE_HIDDEN, D_MODEL), wmap),
            ],
            out_specs=pl.BlockSpec((TME, D_MODEL), lambda t, te, nv: (t, 0)),
            scratch_shapes=[pltpu.VMEM((D_MODEL, MOE_HIDDEN), BF16),
                            pltpu.VMEM((D_MODEL, MOE_HIDDEN), BF16),
                            pltpu.VMEM((MOE_HIDDEN, D_MODEL), BF16)],
        ),
        compiler_params=_params("arbitrary"),
        name="moe_ffn",
    )(tile_expert, n_valid, xs, w_gate, w_up, w_down)


def _combine_kernel(pos_ref, pos_next_ref, x_ref, meta_ref, mod_ref, fnw_ref, ys_hbm, *rest, final, ctx_tiles,
                    n_tiles):
    if final:
        o_ctx_ref, o_lat_ref, bufs, sems = rest
    else:
        o_ref, bufs, sems = rest
    i = pl.program_id(0)
    slot = lax.rem(i, 2)

    def gather(p_ref, s):
        def start(r, carry):
            for k in range(2):
                _row_copy(ys_hbm, p_ref[0, 2 * r + k], bufs.at[s, k], r, sems.at[s]).start(priority=k)
            return carry

        lax.fori_loop(0, TM, start, 0, unroll=DMA_UNROLL)

    @pl.when(i == 0)
    def _():
        gather(pos_ref, 0)

    for s in range(2):
        @pl.when((i + 1 < n_tiles) & (slot == 1 - s))
        def _(s=s):
            gather(pos_next_ref, s)

    for k in range(2):
        pltpu.make_async_copy(ys_hbm.at[pl.ds(0, TM), :], bufs.at[slot, k], sems.at[slot]).wait()
    meta = meta_ref[...]
    y = meta[:, META_W1:META_W1 + 1] * bufs[slot, 0] + meta[:, META_W2:META_W2 + 1] * bufs[slot, 1]
    xo = x_ref[...] + mod_ref[5:6, :] * y
    if not final:
        o_ref[...] = xo
        return
    xo = xo * lax.rsqrt(jnp.mean(xo * xo, axis=-1, keepdims=True) + NORM_EPS) * fnw_ref[...]
    is_ctx = pl.program_id(0) < ctx_tiles

    @pl.when(is_ctx)
    def _():
        o_ctx_ref[...] = xo

    @pl.when(jnp.logical_not(is_ctx))
    def _():
        o_lat_ref[...] = xo


def _combine(rows, pos, x, meta, mods, fnw, ys, final):
    ct = rows.ctx_tiles
    if final:
        out_shape = [jax.ShapeDtypeStruct((rows.nc, D_MODEL), F32), jax.ShapeDtypeStruct((rows.nl, D_MODEL), F32)]
        out_specs = [pl.BlockSpec((TM, D_MODEL), lambda i: (jnp.minimum(i, ct - 1), 0)),
                     pl.BlockSpec((TM, D_MODEL), lambda i: (jnp.maximum(i - ct, 0), 0))]
    else:
        out_shape = jax.ShapeDtypeStruct((rows.n, D_MODEL), F32)
        out_specs = pl.BlockSpec((TM, D_MODEL), lambda i: (i, 0))
    return pl.pallas_call(
        functools.partial(_combine_kernel, final=final, ctx_tiles=ct, n_tiles=rows.n_tiles),
        out_shape=out_shape,
        grid=(rows.n_tiles,),
        in_specs=[
            pl.BlockSpec((None, 1, 2 * TM), lambda i: (i, 0, 0), memory_space=pltpu.SMEM),
            pl.BlockSpec((None, 1, 2 * TM), lambda i: (jnp.minimum(i + 1, rows.n_tiles - 1), 0, 0),
                         memory_space=pltpu.SMEM),
            pl.BlockSpec((TM, D_MODEL), lambda i: (i, 0)),
            pl.BlockSpec((TM, LANE), lambda i: (i, 0)),
            rows.mod_spec(),
            pl.BlockSpec((1, D_MODEL), lambda i: (0, 0)),
            pl.BlockSpec(memory_space=pl.ANY),
        ],
        out_specs=out_specs,
        scratch_shapes=[pltpu.VMEM((2, 2, TM, D_MODEL), F32), pltpu.SemaphoreType.DMA((2,))],
        compiler_params=_params("arbitrary"),
        name="moe_combine",
    )(pos, pos, x, meta, mods, fnw.reshape(1, D_MODEL), ys)


def _moe(rows, x, xn, meta, cnt, mods, w_gate, w_up, w_down, layer, fnw, final):
    n_assign = 2 * rows.n
    n_tiles = n_assign // TME + MOE_EXPERTS

    counts = cnt[0, :MOE_EXPERTS].astype(I32)
    padded = ((counts + TME - 1) // TME) * TME
    ends = jnp.cumsum(padded)
    starts = ends - padded
    experts = jnp.arange(MOE_EXPERTS, dtype=I32)
    e = meta[:, META_E1:META_E2 + 1].astype(I32)
    rank = meta[:, META_R1:META_R2 + 1].astype(I32)
    start_of = jnp.sum(jnp.where(e[..., None] == experts, starts, 0), axis=-1)
    pos = (start_of + rank).reshape(rows.n_tiles, 1, 2 * TM)
    n_valid = ends[-1] // TME
    tile_first = jnp.arange(n_tiles, dtype=I32) * TME
    tile_start = jnp.minimum(tile_first, ends[-1] - TME)
    tile_expert = jnp.sum((ends[None, :] <= tile_start[:, None]).astype(I32), axis=1)
    tile_expert = jnp.minimum(tile_expert, MOE_EXPERTS - 1)
    tile_oh = tile_expert[:, None] == experts
    tile_rows = jnp.sum(jnp.where(tile_oh, counts + starts, 0), axis=1) - tile_start
    tile_fill = ((tile_first >= ends[-1]) | (tile_rows < TME)).astype(I32)

    xs = _dispatch(rows, tile_fill, pos, xn, n_tiles)
    ys = _ffn(tile_expert, n_valid.reshape(1).astype(I32), xs, w_gate, w_up, w_down, layer, n_tiles)
    return _combine(rows, pos, x, meta, mods, fnw, ys, final)


def _lower_bound_params(p):
    pr = jax.nn.softmax(p.astype(F32), axis=0)
    lb = jnp.cumsum(pr, axis=0) - pr[0:1]
    lb = jnp.clip(lb, 0.0, 1.0 - 1e-6)
    return jnp.maximum(lb, LOG_TINY), 1.0 - lb


def kernel(x_prompt, x_sample, c, cache_swa_k, cache_swa_v, state_hgrn, cache_mla_ckv, cache_mla_krope, c_ctx, mod_w, mod_b, norm1_w, norm2_w, final_norm_w, even_w_in, even_w_out, swa_sink, hgrn_lb_fwd, hgrn_lb_bwd, hgrn_gnorm_w, mla_w_dq, mla_qnorm_w, mla_w_uq, mla_w_dkv, mla_kvnorm_w, mla_w_ukv, mla_w_o, moe_router_group_w, moe_router_group_b, moe_router_expert_w, moe_router_expert_b, moe_w_gate, moe_w_up, moe_w_down):
    nc_b, nc_t, _ = x_prompt.shape
    nl_b, nl_t, _ = x_sample.shape
    rows = _Rows(nc_b, nc_t, nl_b, nl_t)
    past = cache_swa_k.shape[2]

    x = (x_prompt.reshape(rows.nc, D_MODEL), x_sample.reshape(rows.nl, D_MODEL))
    mod_rows = 16
    cvec = jnp.concatenate([c_ctx[None, :], c, jnp.zeros((mod_rows - 1 - nl_b, D_MODEL), F32)], axis=0)
    mods_all = _modulation(cvec, mod_w, mod_b).reshape(DEPTH, mod_rows, 6, D_MODEL)

    hconsts = _hgrn_consts()
    la_f, l1_f = _lower_bound_params(hgrn_lb_fwd)
    la_b, l1_b = _lower_bound_params(hgrn_lb_bwd)
    lbp_all = jnp.stack([la_f, l1_f, la_b, l1_b], axis=1).reshape(N_EVEN, 4, HGRN_HEADS, LANE).transpose(0, 2, 1, 3)
    swa_cos, swa_sin = _rope_tables(nl_t, SWA_HEAD_DIM, 0, LANE, 0)
    mla_cos, mla_sin = _rope_tables(nl_t, MLA_ROPE_DIM, MLA_NOPE_DIM, MLA_QK_DIM, TM)
    tri =jnp.asarray(np.tril(np.ones((TM, TM), np.float32), -1), BF16)

    new_k, new_v, new_s, new_ckv, new_kr = [], [], [], [], []
    for l in range(DEPTH):
        j = l // 2
        mods = mods_all[l]
        wr = jnp.zeros((D_MODEL, LANE), F32)
        wr = wr.at[:, :MOE_GROUPS].set(moe_router_group_w[l])
        wr = wr.at[:, MOE_GROUPS:MOE_GROUPS + MOE_EXPERTS].set(
            moe_router_expert_w[l].transpose(1, 0, 2).reshape(D_MODEL, MOE_EXPERTS))
        br = jnp.zeros((1, LANE), F32)
        br = br.at[0, :MOE_GROUPS].set(moe_router_group_b[l])
        br = br.at[0, MOE_GROUPS:MOE_GROUPS + MOE_EXPERTS].set(moe_router_expert_b[l].reshape(MOE_EXPERTS))
        if l % 2 == 0:
            proj = _even_proj(rows, x, mods, norm1_w[l], even_w_in[j].astype(BF16))
            a_ctx = _swa_ctx(rows, proj, swa_sink[j])
            a_lat = _swa_lat(rows, proj, swa_sink[j],
                             cache_swa_k[:, j].reshape(nl_b, past, SWA_KV_WIDTH),
                             cache_swa_v[:, j].reshape(nl_b, past, SWA_KV_WIDTH), swa_cos, swa_sin)
            r_ctx, s_ctx = _hgrn(proj, lbp_all[j], hgrn_gnorm_w[j], hconsts, nc_b, nc_t, 0, rows.nc,
                                 emit_state=True)
            (r_lat,) = _hgrn(proj, lbp_all[j], hgrn_gnorm_w[j], hconsts, nl_b, nl_t, rows.nc // nl_t, rows.nl,
                             s0=state_hgrn, s0_layer=j)
            w_out = even_w_out[j].astype(BF16)
            x, xn, meta, cnt = _outproj_route(rows, [(a_ctx, a_lat), (r_ctx, r_lat)],
                                              [w_out[:SWA_WIDTH], w_out[SWA_WIDTH:]], x, mods, norm2_w[l], wr, br, tri)
            kv = proj[:rows.nc, SWA_WIDTH:SWA_WIDTH + 2 * SWA_KV_WIDTH]
            new_k.append(kv[:, :SWA_KV_WIDTH].reshape(nc_b, nc_t, SWA_KV_HEADS, SWA_HEAD_DIM))
            new_v.append(kv[:, SWA_KV_WIDTH:].reshape(nc_b, nc_t, SWA_KV_HEADS, SWA_HEAD_DIM))
            new_s.append(s_ctx)
        else:
            wd = jnp.zeros((D_MODEL, MLA_DOWN_WIDTH), F32)
            wd = wd.at[:, :MLA_Q_RANK].set(mla_w_dq[j])
            wd = wd.at[:, MLA_Q_RANK:MLA_Q_RANK + MLA_KV_RANK].set(mla_w_dkv[j][:, :MLA_KV_RANK])
            kr_lo = MLA_Q_RANK + MLA_KV_RANK + MLA_NOPE_DIM
            wd = wd.at[:, kr_lo:kr_lo + MLA_ROPE_DIM].set(mla_w_dkv[j][:, MLA_KV_RANK:])
            wuq = jnp.pad(mla_w_uq[j].reshape(MLA_Q_RANK, MLA_HEADS, MLA_QK_DIM),
                          ((0, 0), (0, 0), (0, LANE - MLA_QK_DIM))).reshape(MLA_Q_RANK, MLA_HEADS * LANE)
            wukv = mla_w_ukv[j].reshape(MLA_KV_RANK, MLA_HEADS, MLA_NOPE_DIM + MLA_V_DIM).transpose(1, 0, 2)
            wk = jnp.pad(wukv[..., :MLA_NOPE_DIM], ((0, 0), (0, 0), (0, LANE - MLA_NOPE_DIM)))
            wv_e = jnp.pad(wukv[..., MLA_NOPE_DIM:], ((0, 0), (0, 0), (0, LANE - MLA_V_DIM)))
            wv_o = jnp.pad(wukv[..., MLA_NOPE_DIM:], ((0, 0), (0, 0), (LANE - MLA_V_DIM, 0)))
            odd = (jnp.arange(MLA_HEADS) % 2 == 1)[:, None, None]
            wv = jnp.where(odd, wv_o, wv_e)
            wuq = wuq.astype(BF16)
            w3 = wuq.reshape(MLA_Q_RANK, MLA_HEADS, LANE)
            nf = MLA_ROPE_DIM // 4
            rot = w3[..., MLA_NOPE_DIM:MLA_QK_DIM].reshape(MLA_Q_RANK, MLA_HEADS, 2, 2, nf)[..., ::-1, :]
            wuq_partner = jnp.concatenate([w3[..., :MLA_NOPE_DIM], rot.reshape(MLA_Q_RANK, MLA_HEADS, MLA_ROPE_DIM),
                                           w3[..., MLA_QK_DIM:]], axis=-1).reshape(MLA_Q_RANK, MLA_HEADS * LANE)
            q, ckv, krb = _mla_proj(rows, x, mods, norm1_w[l], wd.astype(BF16), mla_qnorm_w[j], mla_kvnorm_w[j],
                                    wuq, wuq_partner, mla_cos, mla_sin)
            wkv = jnp.concatenate([wk, wv], axis=-1).astype(BF16)
            ckv_c = ckv[:rows.nc].reshape(nc_b, nc_t, MLA_KV_RANK)
            krb_c = krb[:rows.nc].reshape(nc_b, nc_t, LANE)
            o_ctx = _mla_attn(q, ckv_c, krb_c, wkv, nc_b, nc_t, 0, rows.nc)
            cache_kr = jnp.pad(cache_mla_krope[:, j], ((0, 0), (0, 0), (MLA_NOPE_DIM, LANE - MLA_QK_DIM)))
            ckv_l = jnp.concatenate([cache_mla_ckv[:, j], ckv[rows.nc:].reshape(nl_b, nl_t, MLA_KV_RANK)], axis=1)
            krb_l = jnp.concatenate([cache_kr, krb[rows.nc:].reshape(nl_b, nl_t, LANE)], axis=1)
            o_lat = _mla_attn(q, ckv_l, krb_l, wkv, nl_b, nl_t, rows.nc // nl_t, rows.nl)
            x, xn, meta, cnt = _outproj_route(rows, [(o_ctx, o_lat)], [mla_w_o[j].astype(BF16)], x, mods,
                                              norm2_w[l], wr, br, tri)
            new_ckv.append(ckv_c)
            new_kr.append(krb_c[..., MLA_NOPE_DIM:MLA_QK_DIM])

        x = _moe(rows, x, xn, meta, cnt, mods, moe_w_gate, moe_w_up, moe_w_down, l,
                 final_norm_w, final=(l == DEPTH - 1))

    y_prompt = x[0].reshape(nc_b, nc_t, D_MODEL)
    y_sample = x[1].reshape(nl_b, nl_t, D_MODEL)
    return (y_prompt, y_sample, jnp.stack(new_k, axis=1), jnp.stack(new_v, axis=1), jnp.stack(new_s, axis=1),
            jnp.stack(new_ckv, axis=1), jnp.stack(new_kr, axis=1))
```
